```python
import math
import jax
import jax.numpy as jnp
from jax import lax
import numpy as np

D_MODEL = 2048
BATCH = 2
SEQ = 4096
DEPTH = 4

CTX_LEN = 256
GRID_W = 64
HEAD_DIM = 128
N_GROUPS = 4
GROUP_W = D_MODEL // N_GROUPS
MIX_W = N_GROUPS * GROUP_W

NA_HEADS = GROUP_W // HEAD_DIM
NA_WIN_ROWS = 8
NA_WIN_COLS = 16
NA_QBLOCK_COLS = 16
DN_HEADS = GROUP_W // HEAD_DIM
DN_CONV_W = 5
DN_CHUNK = 64
ML_HEADS = GROUP_W // HEAD_DIM
ML_QK = HEAD_DIM // 2
ML_V = GROUP_W // ML_HEADS
ML_CHUNK = 64
WA_HEADS = GROUP_W // HEAD_DIM
WA_KV_HEADS = WA_HEADS // 2
WA_WINDOW = 128
WA_BLOCK = 128
ROPE_THETA = 10000.0
FFN_HIDDEN = ((8 * D_MODEL // 3 + 255) // 256) * 256
EPS = 1e-6

NA_IN = 3 * GROUP_W
DN_IN = 4 * GROUP_W + 4 * DN_HEADS
ML_IN = 2 * ML_HEADS * ML_QK + 2 * GROUP_W + 4 * ML_HEADS
WA_IN = GROUP_W + 2 * WA_KV_HEADS * HEAD_DIM
IN_COLS = NA_IN + DN_IN + ML_IN + WA_IN

kernel_name = "hybrid_parallel_group_flow_backbone"


def rmsnorm(x, gain):
    xf = x.astype(jnp.float32)
    xf = xf * lax.rsqrt(jnp.mean(xf * xf, axis=-1, keepdims=True) + EPS)
    return (xf * gain.astype(jnp.float32)).astype(x.dtype)


def l2norm(x):
    return x * lax.rsqrt(jnp.sum(x * x, axis=-1, keepdims=True) + EPS)


def modulate(h, shift, scale):
    return h * (1 + scale) + shift


def split_heads(t, n_heads):
    b, l, w = t.shape
    return t.reshape(b, l, n_heads, w // n_heads).transpose(0, 2, 1, 3)


def merge_heads(t):
    b, h, l, d = t.shape
    return t.transpose(0, 2, 1, 3).reshape(b, l, h * d)


def softmax32(s):
    return jax.nn.softmax(s.astype(jnp.float32), axis=-1)


def flip_seq(t, direction):
    return jnp.flip(t, axis=2) if direction == 1 else t


def swiglu(h, w_in, w_out):
    gate, up = jnp.split(h @ w_in, 2, axis=-1)
    return (jax.nn.silu(gate) * up) @ w_out


def axial_rope_tables(n):
    t = jnp.arange(n)
    n_freq = HEAD_DIM // 4
    inv_freq = ROPE_THETA ** (-jnp.arange(n_freq, dtype=jnp.float32) / n_freq)
    pos = jnp.stack([t // GRID_W, t % GRID_W], axis=-1).astype(jnp.float32)
    ang = pos[:, :, None] * inv_freq
    return jnp.cos(ang), jnp.sin(ang)


def apply_axial_rope(x, cos, sin):
    b, h, n, d = x.shape
    xr = x.reshape(b, h, n, 2, 2, d // 4)
    x1, x2 = xr[..., 0, :], xr[..., 1, :]
    cos, sin = cos.astype(x.dtype), sin.astype(x.dtype)
    return jnp.stack([x1 * cos - x2 * sin, x2 * cos + x1 * sin], axis=-2).reshape(b, h, n, d)


def centred_conv_silu(t, w):
    pad = w.shape[0] // 2
    out = lax.conv_general_dilated(t, w[:, None, :], window_strides=(1,), padding=[(pad, pad)],
                                   dimension_numbers=('NWC', 'WIO', 'NWC'), feature_group_count=t.shape[-1])
    return jax.nn.silu(out)


def neighbourhood_mixer(p, pc, qk_gain, rpb, need_ctx):
    q, k, v = (split_heads(t, NA_HEADS) for t in jnp.split(p, 3, axis=-1))
    qc, kc, vc = (split_heads(t, NA_HEADS) for t in jnp.split(pc, 3, axis=-1))
    q, k, kc = rmsnorm(q, qk_gain[0]), rmsnorm(k, qk_gain[1]), rmsnorm(kc, qk_gain[1])
    bsz, h, n, d = q.shape
    scale = d ** -0.5
    rows = n // GRID_W
    kh = min(NA_WIN_ROWS, rows)
    kw = NA_WIN_COLS
    qbw = NA_QBLOCK_COLS
    kbw = qbw + kw
    ncb = GRID_W // qbw
    r = jnp.arange(rows)
    key_rows = jnp.clip(r - kh // 2, 0, rows - kh)[:, None] + jnp.arange(kh)
    qcol = jnp.arange(GRID_W).reshape(ncb, qbw)
    key_cols = jnp.clip(qcol[:, 0] - kw // 2, 0, GRID_W - kbw)[:, None] + jnp.arange(kbw)
    win_start = jnp.clip(qcol - kw // 2, 0, GRID_W - kw)
    idx = (key_rows[:, None, :, None] * GRID_W + key_cols[None, :, None, :]).reshape(rows, ncb, kh * kbw)
    kg = jnp.take(k, idx, axis=2)
    vg = jnp.take(v, idx, axis=2)
    qb = q.reshape(bsz, h, rows, ncb, qbw, d)
    drow = key_rows - r[:, None]
    dcol = key_cols[:, None, :] - qcol[:, :, None]
    in_win = (key_cols[:, None, :] >= win_start[:, :, None]) & (key_cols[:, None, :] < win_start[:, :, None] + kw)
    bias = rpb[:, drow[:, None, None, :, None] + NA_WIN_ROWS - 1,
               jnp.clip(dcol, 1 - kw, kw - 1)[None, :, :, None, :] + NA_WIN_COLS - 1]
    bias = jnp.where(in_win[None, None, :, :, None, :], bias.astype(jnp.float32), -jnp.inf)
    bias = bias.reshape(h, rows, ncb, qbw, kh * kbw)
    s_loc = jnp.einsum('bhrjqd,bhrjkd->bhrjqk', qb, kg).astype(jnp.float32) * scale + bias
    s_ctx = jnp.einsum('bhrjqd,bhkd->bhrjqk', qb, kc).astype(jnp.float32) * scale
    prob = softmax32(jnp.concatenate([s_loc, s_ctx], axis=-1)).astype(v.dtype)
    n_loc = kh * kbw
    o = (jnp.einsum('bhrjqk,bhrjkd->bhrjqd', prob[..., :n_loc], vg)
         + jnp.einsum('bhrjqk,bhkd->bhrjqd', prob[..., n_loc:], vc))
    y = merge_heads(o.reshape(bsz, h, n, d))
    yc = None
    if need_ctx:
        qc = rmsnorm(qc, qk_gain[0])
        pcx = softmax32(jnp.einsum('bhqd,bhkd->bhqk', qc, kc).astype(jnp.float32) * scale).astype(vc.dtype)
        yc = merge_heads(jnp.einsum('bhqk,bhkd->bhqd', pcx, vc))
    return y, yc


def gated_delta_chunked(q, k, v, g, beta, s0, with_output):
    bsz, h, n, dk = q.shape
    dv = v.shape[-1]
    cs = DN_CHUNK
    nc = n // cs
    q = q.reshape(bsz, h, nc, cs, dk)
    k = k.reshape(bsz, h, nc, cs, dk)
    v = v.reshape(bsz, h, nc, cs, dv)
    g = jnp.cumsum(g.reshape(bsz, h, nc, cs), axis=-1)
    beta = beta.reshape(bsz, h, nc, cs)
    lower = jnp.tril(jnp.ones((cs, cs), bool))
    strict = jnp.tril(jnp.ones((cs, cs), bool), -1)
    decay = jnp.exp(jnp.where(lower, g[..., :, None] - g[..., None, :], -jnp.inf))
    kb = k * beta[..., None]
    a = jnp.where(strict, jnp.einsum('bhncd,bhnsd->bhncs', kb, k) * decay, 0.0) + jnp.eye(cs, dtype=q.dtype)
    u = lax.linalg.triangular_solve(a, v * beta[..., None], left_side=True, lower=True, unit_diagonal=True)
    w = lax.linalg.triangular_solve(a, kb * jnp.exp(g)[..., None], left_side=True, lower=True, unit_diagonal=True)
    g_last = g[..., -1]
    k_end = k * jnp.exp(g_last[..., None] - g)[..., None]
    xs = [u, w, k_end, g_last]
    if with_output:
        xs += [q * jnp.exp(g)[..., None], jnp.einsum('bhncd,bhnsd->bhncs', q, k) * decay]
    xs = tuple(jnp.moveaxis(t, 2, 0) for t in xs)

    def step(s, inp):
        u_i, w_i, ke_i, gl_i = inp[:4]
        v_new = u_i - jnp.einsum('bhcd,bhde->bhce', w_i, s)
        s_new = s * jnp.exp(gl_i)[..., None, None] + jnp.einsum('bhcd,bhce->bhde', ke_i, v_new)
        if with_output:
            qd_i, qk_i = inp[4:]
            o_i = jnp.einsum('bhcd,bhde->bhce', qd_i, s) + jnp.einsum('bhcs,bhse->bhce', qk_i, v_new)
            return s_new, o_i
        return s_new, None

    s_fin, o = lax.scan(step, s0, xs)
    if not with_output:
        return None, s_fin
    return jnp.moveaxis(o, 0, 2).reshape(bsz, h, n, dv), s_fin


def deltanet_mixer(p, pc, conv_w, a_log, dt_bias, norm_g, need_ctx):
    a = jnp.exp(a_log.astype(jnp.float32))[:, None, :, None]
    dtb = dt_bias.astype(jnp.float32)[:, None, :, None]

    def prep(t):
        b, l, _ = t.shape
        qkv = centred_conv_silu(t[..., :3 * GROUP_W], conv_w).astype(jnp.float32)
        q, k, v = (split_heads(u, DN_HEADS) for u in jnp.split(qkv, 3, axis=-1))
        q = l2norm(q) * HEAD_DIM ** -0.5
        k = l2norm(k)
        gb = t[..., 4 * GROUP_W:].astype(jnp.float32).reshape(b, l, 2, 2, DN_HEADS).transpose(2, 3, 0, 4, 1)
        beta = jax.nn.sigmoid(gb[0])
        g = -a * jax.nn.softplus(gb[1] + dtb)
        return q, k, v, beta, g

    q, k, v, beta, g = prep(p)
    qc, kc, vc, beta_c, g_c = prep(pc)
    s0 = jnp.zeros(kc.shape[:2] + (HEAD_DIM, HEAD_DIM), jnp.float32)
    y, yc = 0.0, 0.0
    for d in range(2):
        oc, s_ctx = gated_delta_chunked(flip_seq(qc, d), flip_seq(kc, d), flip_seq(vc, d),
                                        flip_seq(g_c[d], d), flip_seq(beta_c[d], d), s0, need_ctx)
        o, _ = gated_delta_chunked(flip_seq(q, d), flip_seq(k, d), flip_seq(v, d),
                                   flip_seq(g[d], d), flip_seq(beta[d], d), s_ctx, True)
        y = y + flip_seq(o, d)
        if need_ctx:
            yc = yc + flip_seq(oc, d)

    def finish(o, t):
        gate = split_heads(t[..., 3 * GROUP_W:4 * GROUP_W].astype(jnp.float32), DN_HEADS)
        return merge_heads(rmsnorm(o, norm_g) * jax.nn.silu(gate)).astype(t.dtype)

    return finish(y, p), (finish(yc, pc) if need_ctx else None)


def mlstm_chunked(q, k, v, ig, fg, state, with_output):
    bsz, h, n, dqk = q.shape
    dv = v.shape[-1]
    cs = ML_CHUNK
    nc = n // cs
    q = q.reshape(bsz, h, nc, cs, dqk)
    k = k.reshape(bsz, h, nc, cs, dqk)
    v = v.reshape(bsz, h, nc, cs, dv)
    ig = ig.reshape(bsz, h, nc, cs)
    b = jnp.cumsum(jax.nn.log_sigmoid(fg).reshape(bsz, h, nc, cs), axis=-1)
    b_last = b[..., -1]
    g_end = b_last[..., None] - b + ig
    m_loc = jnp.max(g_end, axis=-1)
    wgt = jnp.exp(g_end - m_loc[..., None])
    c_loc = jnp.einsum('bhnld,bhnle->bhnde', k * wgt[..., None], v)
    n_loc = jnp.einsum('bhnl,bhnld->bhnd', wgt, k)

    def step(carry, inp):
        c_prev, n_prev, m_prev = carry
        bl, ml, cl, nl = inp
        m_new = jnp.maximum(bl + m_prev, ml)
        a = jnp.exp(bl + m_prev - m_new)
        e = jnp.exp(ml - m_new)
        new = (a[..., None, None] * c_prev + e[..., None, None] * cl, a[..., None] * n_prev + e[..., None] * nl, m_new)
        return new, (carry if with_output else None)

    xs = tuple(jnp.moveaxis(t, 2, 0) for t in (b_last, m_loc, c_loc, n_loc))
    final, starts = lax.scan(step, state, xs)
    if not with_output:
        return None, final
    c_st, n_st, m_st = (jnp.moveaxis(t, 0, 2) for t in starts)
    lower = jnp.tril(jnp.ones((cs, cs), bool))
    log_d = jnp.where(lower, b[..., :, None] - b[..., None, :] + ig[..., None, :], -jnp.inf)
    m_inter = b + m_st[..., None]
    m_t = jnp.maximum(jnp.max(log_d, axis=-1), m_inter)
    s = jnp.einsum('bhnld,bhnsd->bhnls', q, k) * jnp.exp(log_d - m_t[..., None])
    inter = jnp.exp(m_inter - m_t)
    num = jnp.einsum('bhnls,bhnse->bhnle', s, v) + inter[..., None] * jnp.einsum('bhnld,bhnde->bhnle', q, c_st)
    den = jnp.sum(s, axis=-1) + inter * jnp.einsum('bhnld,bhnd->bhnl', q, n_st)
    hh = num / jnp.maximum(jnp.abs(den), jnp.exp(-m_t))[..., None]
    return hh.reshape(bsz, h, n, dv), final


def mlstm_mixer(p, pc, i_bias, f_bias, norm_g, need_ctx):
    qkw = ML_HEADS * ML_QK
    ib = i_bias.astype(jnp.float32)[:, None, :, None]
    fb = f_bias.astype(jnp.float32)[:, None, :, None]

    def prep(t):
        b, l, _ = t.shape
        tf = t.astype(jnp.float32)
        q = split_heads(tf[..., :qkw], ML_HEADS)
        k = split_heads(tf[..., qkw:2 * qkw], ML_HEADS) * ML_QK ** -0.5
        v = split_heads(tf[..., 2 * qkw:2 * qkw + GROUP_W], ML_HEADS)
        gates = tf[..., 2 * qkw + 2 * GROUP_W:].reshape(b, l, 2, 2, ML_HEADS).transpose(2, 3, 0, 4, 1)
        return q, k, v, gates[0] + ib, gates[1] + fb

    q, k, v, ig, fg = prep(p)
    qc, kc, vc, ig_c, fg_c = prep(pc)
    bsz = kc.shape[0]
    state0 = (jnp.zeros((bsz, ML_HEADS, ML_QK, ML_V), jnp.float32),
              jnp.zeros((bsz, ML_HEADS, ML_QK), jnp.float32),
              jnp.zeros((bsz, ML_HEADS), jnp.float32))
    y, yc = 0.0, 0.0
    for d in range(2):
        hc, st = mlstm_chunked(flip_seq(qc, d), flip_seq(kc, d), flip_seq(vc, d),
                               flip_seq(ig_c[d], d), flip_seq(fg_c[d], d), state0, need_ctx)
        hl, _ = mlstm_chunked(flip_seq(q, d), flip_seq(k, d), flip_seq(v, d),
                              flip_seq(ig[d], d), flip_seq(fg[d], d), st, True)
        y = y + flip_seq(hl, d)
        if need_ctx:
            yc = yc + flip_seq(hc, d)

    def finish(hh, t):
        o_gate = jax.nn.sigmoid(t[..., 2 * qkw + GROUP_W:2 * qkw + 2 * GROUP_W].astype(jnp.float32))
        return (o_gate * merge_heads(rmsnorm(hh, norm_g[:, None, :]))).astype(t.dtype)

    return finish(y, p), (finish(yc, pc) if need_ctx else None)


def window_mixer(p, pc, qk_gain, sink, cos, sin, need_ctx):
    kvw = WA_KV_HEADS * HEAD_DIM
    q = split_heads(p[..., :GROUP_W], WA_HEADS)
    k = split_heads(p[..., GROUP_W:GROUP_W + kvw], WA_KV_HEADS)
    v = split_heads(p[..., GROUP_W + kvw:], WA_KV_HEADS)
    kc = rmsnorm(split_heads(pc[..., GROUP_W:GROUP_W + kvw], WA_KV_HEADS), qk_gain[1])
    vc = split_heads(pc[..., GROUP_W + kvw:], WA_KV_HEADS)
    q = apply_axial_rope(rmsnorm(q, qk_gain[0]), cos, sin)
    k = apply_axial_rope(rmsnorm(k, qk_gain[1]), cos, sin)
    bsz, _, n, d = q.shape
    nb = n // WA_BLOCK
    rep = WA_HEADS // WA_KV_HEADS
    lc = kc.shape[2]
    scale = d ** -0.5
    sink32 = sink.astype(jnp.float32).reshape(1, WA_KV_HEADS, rep, 1, 1)
    qg = q.reshape(bsz, WA_KV_HEADS, rep, nb, WA_BLOCK, d)

    def band(t):
        tp = jnp.pad(t, ((0, 0), (0, 0), (WA_BLOCK, WA_BLOCK), (0, 0))).reshape(bsz, WA_KV_HEADS, nb + 2, WA_BLOCK, d)
        return jnp.concatenate([tp[:, :, :nb], tp[:, :, 1:nb + 1], tp[:, :, 2:]], axis=3)

    kb, vb = band(k), band(v)
    qpos = jnp.arange(n).reshape(nb, WA_BLOCK)
    kpos = (jnp.arange(nb)[:, None] - 1) * WA_BLOCK + jnp.arange(3 * WA_BLOCK)
    ok = ((kpos[:, None, :] >= 0) & (kpos[:, None, :] < n)
          & (jnp.abs(qpos[:, :, None] - kpos[:, None, :]) <= WA_WINDOW))
    s_loc = jnp.where(ok, jnp.einsum('bgrnqd,bgnkd->bgrnqk', qg, kb).astype(jnp.float32) * scale, -jnp.inf)
    s_ctx = jnp.einsum('bgrnqd,bgkd->bgrnqk', qg, kc).astype(jnp.float32) * scale
    sink_col = jnp.broadcast_to(sink32[..., None], s_ctx.shape[:-1] + (1,))
    prob = softmax32(jnp.concatenate([s_loc, s_ctx, sink_col], axis=-1)).astype(v.dtype)
    n_loc = 3 * WA_BLOCK
    o = (jnp.einsum('bgrnqk,bgnkd->bgrnqd', prob[..., :n_loc], vb)
         + jnp.einsum('bgrnqk,bgkd->bgrnqd', prob[..., n_loc:n_loc + lc], vc))
    y = merge_heads(o.reshape(bsz, WA_HEADS, n, d))
    yc = None
    if need_ctx:
        qc = rmsnorm(split_heads(pc[..., :GROUP_W], WA_HEADS), qk_gain[0]).reshape(bsz, WA_KV_HEADS, rep, lc, d)
        sc = jnp.einsum('bgrqd,bgkd->bgrqk', qc, kc).astype(jnp.float32) * scale
        sc = jnp.concatenate([sc, jnp.broadcast_to(sink32, sc.shape[:-1] + (1,))], axis=-1)
        pcx = softmax32(sc).astype(vc.dtype)
        oc = jnp.einsum('bgrqk,bgkd->bgrqd', pcx[..., :lc], vc)
        yc = merge_heads(oc.reshape(bsz, WA_HEADS, lc, d))
    return y, yc


def setup_inputs(seed: int = 0) -> dict:
    key = jax.random.key(seed)
    ks = jax.random.split(key, 24)
    f32 = jnp.float32

    def nrm(k, shape, scale):
        return jax.random.normal(k, shape, f32) * scale

    dt = jnp.exp(jax.random.uniform(ks[14], (DEPTH, 2, DN_HEADS), f32, math.log(1e-3), math.log(1e-1)))
    return {
        "x": nrm(ks[0], (BATCH, SEQ, D_MODEL), 1.0),
        "c": nrm(ks[1], (BATCH, D_MODEL), 1.0),
        "ctx": nrm(ks[2], (BATCH, CTX_LEN, D_MODEL), 1.0),
        "c_ctx": nrm(ks[3], (D_MODEL,), 1.0),
        "w_ada": nrm(ks[4], (DEPTH, D_MODEL, 6 * D_MODEL), 0.5 * D_MODEL ** -0.5),
        "b_ada": nrm(ks[5], (DEPTH, 6 * D_MODEL), 0.02),
        "norm_mix": 1.0 + nrm(ks[6], (DEPTH, D_MODEL), 0.02),
        "norm_ffn": 1.0 + nrm(ks[7], (DEPTH, D_MODEL), 0.02),
        "w_in": nrm(ks[8], (DEPTH, D_MODEL, IN_COLS), D_MODEL ** -0.5),
        "w_out": nrm(ks[9], (DEPTH, MIX_W, D_MODEL), MIX_W ** -0.5),
        "na_qk_gain": 1.0 + nrm(ks[10], (DEPTH, 2, HEAD_DIM), 0.02),
        "na_rpb": nrm(ks[11], (DEPTH, NA_HEADS, 2 * NA_WIN_ROWS - 1, 2 * NA_WIN_COLS - 1), 0.5),
        "dn_conv": nrm(ks[12], (DEPTH, DN_CONV_W, 3 * GROUP_W), DN_CONV_W ** -0.5),
        "dn_a_log": jnp.log(jax.random.uniform(ks[13], (DEPTH, 2, DN_HEADS), f32, 1.0, 16.0)),
        "dn_dt_bias": dt + jnp.log(-jnp.expm1(-dt)),
        "dn_norm": 1.0 + nrm(ks[15], (DEPTH, HEAD_DIM), 0.02),
        "ml_i_bias": nrm(ks[16], (DEPTH, 2, ML_HEADS), 0.1),
        "ml_f_bias": jax.random.uniform(ks[17], (DEPTH, 2, ML_HEADS), f32, 3.0, 6.0),
        "ml_norm": 1.0 + nrm(ks[18], (DEPTH, ML_HEADS, ML_V), 0.02),
        "wa_qk_gain": 1.0 + nrm(ks[19], (DEPTH, 2, HEAD_DIM), 0.02),
        "wa_sink": nrm(ks[20], (DEPTH, WA_HEADS), 0.5),
        "w_ffn_in": nrm(ks[21], (DEPTH, D_MODEL, 2 * FFN_HIDDEN), D_MODEL ** -0.5),
        "w_ffn_out": nrm(ks[22], (DEPTH, FFN_HIDDEN, D_MODEL), FFN_HIDDEN ** -0.5),
    }


def reference(x, c, ctx, c_ctx, w_ada, b_ada, norm_mix, norm_ffn, w_in, w_out,
              na_qk_gain, na_rpb, dn_conv, dn_a_log, dn_dt_bias, dn_norm,
              ml_i_bias, ml_f_bias, ml_norm, wa_qk_gain, wa_sink, w_ffn_in, w_ffn_out):
    cos, sin = axial_rope_tables(x.shape[1])
    silu_c = jax.nn.silu(c)
    silu_cc = jax.nn.silu(c_ctx)
    cuts = [NA_IN, NA_IN + DN_IN, NA_IN + DN_IN + ML_IN]
    for l in range(DEPTH):
        need_ctx = l < DEPTH - 1
        mod = jnp.split((silu_c @ w_ada[l] + b_ada[l])[:, None, :], 6, axis=-1)
        modc = jnp.split(silu_cc @ w_ada[l] + b_ada[l], 6, axis=-1)
        h = modulate(rmsnorm(x, norm_mix[l]), mod[0], mod[1])
        hc = modulate(rmsnorm(ctx, norm_mix[l]), modc[0], modc[1])
        pa, pb, pm, pw = jnp.split(h @ w_in[l], cuts, axis=-1)
        pca, pcb, pcm, pcw = jnp.split(hc @ w_in[l], cuts, axis=-1)
        ya, yca = neighbourhood_mixer(pa, pca, na_qk_gain[l], na_rpb[l], need_ctx)
        yb, ycb = deltanet_mixer(pb, pcb, dn_conv[l], dn_a_log[l], dn_dt_bias[l], dn_norm[l], need_ctx)
        ym, ycm = mlstm_mixer(pm, pcm, ml_i_bias[l], ml_f_bias[l], ml_norm[l], need_ctx)
        yw, ycw = window_mixer(pw, pcw, wa_qk_gain[l], wa_sink[l], cos, sin, need_ctx)
        x = x + mod[2] * (jnp.concatenate([ya, yb, ym, yw], axis=-1) @ w_out[l])
        x = x + mod[5] * swiglu(modulate(rmsnorm(x, norm_ffn[l]), mod[3], mod[4]), w_ffn_in[l], w_ffn_out[l])
        if need_ctx:
            ctx = ctx + modc[2] * (jnp.concatenate([yca, ycb, ycm, ycw], axis=-1) @ w_out[l])
            ctx = ctx + modc[5] * swiglu(modulate(rmsnorm(ctx, norm_ffn[l]), modc[3], modc[4]),
                                         w_ffn_in[l], w_ffn_out[l])
    return x
```

```python
import functools
import math

import jax
import jax.numpy as jnp
from jax import lax
from jax.experimental import pallas as pl
from jax.experimental.pallas import tpu as pltpu

F32 = jnp.float32
BF16 = jnp.bfloat16

HEAD_DIM = 128
GRID_W = 64
CHUNK = 64
NA_WIN_ROWS = 8
NA_WIN_COLS = 16
NA_QROWS = 4
NA_KROWS = NA_QROWS + NA_WIN_ROWS
WA_BLOCK = 128
ML_QK = 64
ROPE_THETA = 10000.0
EPS = 1e-6
NEG_INF = float("-inf")
VMEM_LIMIT = 48 * 1024 * 1024
DN_VMEM_LIMIT = 56 * 1024 * 1024

NA_Q, NA_K, NA_V = 0, 4, 8
DN_QKV, DN_OG = 12, 24
ML_Q, ML_K, ML_V, ML_OG = 28, 30, 32, 36
WA_Q, WA_K, WA_V = 40, 44, 46
MAIN_COLS = 48 * 128
LANE_BETA, LANE_ALPHA, LANE_IG, LANE_FG = 0, 8, 16, 24


def _cparams(sem, vmem_limit=VMEM_LIMIT):
    return pltpu.CompilerParams(dimension_semantics=sem, vmem_limit_bytes=vmem_limit)


def _rms(x, gain):
    return x * lax.rsqrt(jnp.mean(x * x, axis=-1, keepdims=True) + EPS) * gain


def _silu(x):
    return x * jax.nn.sigmoid(x)


def _softplus(x):
    return jnp.maximum(x, 0.0) + jnp.log1p(jnp.exp(-jnp.abs(x)))


def _log_sigmoid(x):
    return jnp.minimum(x, 0.0) - jnp.log1p(jnp.exp(-jnp.abs(x)))


def _dot(a, b, precision=None):
    return jnp.dot(a, b, preferred_element_type=F32, precision=precision)


def _dot_nt(a, b):
    return lax.dot_general(a, b, (((1,), (1,)), ((), ())), preferred_element_type=F32)


def _dot_tn(a, b):
    return lax.dot_general(a, b, (((0,), (0,)), ((), ())), preferred_element_type=F32)


def _iota(shape, dim):
    return lax.broadcasted_iota(jnp.int32, shape, dim)


def _lane_pick(x, lane):
    return jnp.sum(jnp.where(_iota(x.shape, 1) == lane, x, 0.0), axis=-1, keepdims=True)


def _to_row(col):
    n = col.shape[0]
    eye = _iota((n, n), 0) == _iota((n, n), 1)
    return jnp.sum(jnp.where(eye, col, 0.0), axis=0, keepdims=True)


def _ada_kernel(c_ref, w_ref, b_ref, o_ref):
    a = _silu(c_ref[...]).astype(BF16)
    o_ref[0] = _dot(a, w_ref[0].astype(BF16)) + b_ref[0]


def _ada_call(cs, w_ada, b_ada):
    depth, d, n = w_ada.shape
    tn = 1024
    return pl.pallas_call(
        _ada_kernel,
        grid=(depth, n // tn),
        in_specs=[pl.BlockSpec((8, d), lambda l, j: (0, 0)),
                  pl.BlockSpec((1, d, tn), lambda l, j: (l, 0, j)),
                  pl.BlockSpec((1, 1, tn), lambda l, j: (l, 0, j))],
        out_specs=pl.BlockSpec((1, 8, tn), lambda l, j: (l, 0, j)),
        out_shape=jax.ShapeDtypeStruct((depth, 8, n), F32),
        compiler_params=_cparams(("arbitrary", "arbitrary")),
        name="ada",
    )(cs, w_ada, b_ada.reshape(depth, 1, n))


def _norm_kernel(x_ref, mod_ref, g_ref, h_ref):
    mod = mod_ref[0]
    h = _rms(x_ref[...], g_ref[...])
    h_ref[...] = (h * (1.0 + mod[1:2]) + mod[0:1]).astype(BF16)


def _norm_call(x, mod, gain, rows_per_mod, tm):
    m, d = x.shape
    per = rows_per_mod // tm
    return pl.pallas_call(
        _norm_kernel,
        grid=(m // tm,),
        in_specs=[pl.BlockSpec((tm, d), lambda i: (i, 0)),
                  pl.BlockSpec((1, 8, d), lambda i: (i // per, 0, 0)),
                  pl.BlockSpec((1, d), lambda i: (0, 0))],
        out_specs=pl.BlockSpec((tm, d), lambda i: (i, 0)),
        out_shape=jax.ShapeDtypeStruct((m, d), BF16),
        compiler_params=_cparams(("arbitrary",)),
        name="norm_mod",
    )(x, mod, gain)


def _matmul_kernel(a_ref, w_ref, o_ref):
    o_ref[...] = _dot(a_ref[...], w_ref[...])


def _matmul_call(a, w, tm, tn, name):
    m, k = a.shape
    n = w.shape[1]
    return pl.pallas_call(
        _matmul_kernel,
        grid=(n // tn, m // tm),
        in_specs=[pl.BlockSpec((tm, k), lambda j, i: (i, 0)),
                  pl.BlockSpec((k, tn), lambda j, i: (0, j))],
        out_specs=pl.BlockSpec((tm, tn), lambda j, i: (i, j)),
        out_shape=jax.ShapeDtypeStruct((m, n), F32),
        compiler_params=_cparams(("arbitrary", "arbitrary")),
        name=name,
    )(a, w)


def _outproj_kernel(ya_ref, yb_ref, ym_ref, yw_ref, w_ref, x_ref, mod_ref, g_ref, xo_ref, h_ref):
    gw = ya_ref.shape[1]
    acc = _dot(ya_ref[...], w_ref[0:gw, :])
    acc += _dot(yb_ref[...], w_ref[gw:2 * gw, :])
    acc += _dot(ym_ref[...], w_ref[2 * gw:3 * gw, :])
    acc += _dot(yw_ref[...], w_ref[3 * gw:4 * gw, :])
    mod = mod_ref[0]
    xn = x_ref[...] + mod[2:3] * acc
    xo_ref[...] = xn
    h = _rms(xn, g_ref[...])
    h_ref[...] = (h * (1.0 + mod[4:5]) + mod[3:4]).astype(BF16)


def _outproj_call(ys, w, x, mod, gain, rows_per_mod, tm):
    m, d = x.shape
    gw = ys[0].shape[1]
    per = rows_per_mod // tm
    yspec = pl.BlockSpec((tm, gw), lambda i: (i, 0))
    return pl.pallas_call(
        _outproj_kernel,
        grid=(m // tm,),
        in_specs=[yspec, yspec, yspec, yspec,
                  pl.BlockSpec((4 * gw, d), lambda i: (0, 0)),
                  pl.BlockSpec((tm, d), lambda i: (i, 0)),
                  pl.BlockSpec((1, 8, d), lambda i: (i // per, 0, 0)),
                  pl.BlockSpec((1, d), lambda i: (0, 0))],
        out_specs=[pl.BlockSpec((tm, d), lambda i: (i, 0)),
                   pl.BlockSpec((tm, d), lambda i: (i, 0))],
        out_shape=[jax.ShapeDtypeStruct((m, d), F32), jax.ShapeDtypeStruct((m, d), BF16)],
        input_output_aliases={5: 0},
        compiler_params=_cparams(("arbitrary",)),
        name="outproj",
    )(*ys, w, x, mod, gain)


def _ffn_kernel(h_ref, wg_ref, wu_ref, w2_ref, x_ref, mod_ref, modn_ref, gn_ref, xo_ref, hn_ref, acc_ref,
                *, with_next):
    j = pl.program_id(1)

    @pl.when(j == 0)
    def _():
        acc_ref[...] = jnp.zeros_like(acc_ref)

    h = h_ref[...]
    act = _silu(_dot(h, wg_ref[...])) * _dot(h, wu_ref[...])
    acc_ref[...] += _dot(act.astype(BF16), w2_ref[...])

    @pl.when(j == pl.num_programs(1) - 1)
    def _():
        xn = x_ref[...] + mod_ref[0][5:6] * acc_ref[...]
        xo_ref[...] = xn
        if with_next:
            modn = modn_ref[0]
            hn = _rms(xn, gn_ref[...])
            hn_ref[...] = (hn * (1.0 + modn[1:2]) + modn[0:1]).astype(BF16)
        else:
            hn_ref[...] = jnp.zeros_like(hn_ref)


def _ffn_call(h, w_in, w_out, x, mod, mod_next, gain_next, rows_per_mod, tm, tn, with_next):
    m, d = x.shape
    hidden = w_out.shape[0]
    nj = hidden // tn
    per = rows_per_mod // tm
    return pl.pallas_call(
        functools.partial(_ffn_kernel, with_next=with_next),
        grid=(m // tm, nj),
        in_specs=[pl.BlockSpec((tm, d), lambda i, j: (i, 0)),
                  pl.BlockSpec((d, tn), lambda i, j: (0, j)),
                  pl.BlockSpec((d, tn), lambda i, j: (0, nj + j)),
                  pl.BlockSpec((tn, d), lambda i, j: (j, 0)),
                  pl.BlockSpec((tm, d), lambda i, j: (i, 0)),
                  pl.BlockSpec((1, 8, d), lambda i, j: (i // per, 0, 0)),
                  pl.BlockSpec((1, 8, d), lambda i, j: (i // per, 0, 0)),
                  pl.BlockSpec((1, d), lambda i, j: (0, 0))],
        out_specs=[pl.BlockSpec((tm, d), lambda i, j: (i, 0)),
                   pl.BlockSpec((tm, d), lambda i, j: (i, 0))],
        out_shape=[jax.ShapeDtypeStruct((m, d), F32), jax.ShapeDtypeStruct((m, d), BF16)],
        scratch_shapes=[pltpu.VMEM((tm, d), F32)],
        input_output_aliases={4: 0},
        compiler_params=_cparams(("arbitrary", "arbitrary")),
        name="ffn",
    )(h, w_in, w_in, w_out, x, mod, mod_next, gain_next)


def _na_kernel(q_ref, k_ref, v_ref, kc_ref, vc_ref, gain_ref, bias_ref, o_ref,
               kn_scr, v_scr, kcn_scr, vc_scr, *, rows):
    rb = pl.program_id(2)
    n_rb = pl.num_programs(2)

    @pl.when(rb == 0)
    def _():
        g1 = gain_ref[1:2, :]
        kn_scr[...] = _rms(k_ref[...], g1).astype(BF16)
        v_scr[...] = v_ref[...].astype(BF16)
        kcn_scr[...] = _rms(kc_ref[...], g1).astype(BF16)
        vc_scr[...] = vc_ref[...].astype(BF16)

    scale = HEAD_DIM ** -0.5
    nk = NA_KROWS * GRID_W
    qn = _rms(q_ref[...], gain_ref[0:1, :]).astype(BF16)
    ws = jnp.clip(NA_QROWS * rb - NA_WIN_ROWS // 2, 0, rows - NA_KROWS)
    start = pl.multiple_of(ws * GRID_W, GRID_W)
    kw = kn_scr[pl.ds(start, nk), :]
    vw = v_scr[pl.ds(start, nk), :]
    typ = jnp.where(rb == 0, 0, jnp.where(rb == n_rb - 1, 2, 1))
    bias = bias_ref[0, pl.ds(typ, 1)][0]
    s_loc = _dot_nt(qn, kw) * scale + bias
    s_ctx = _dot_nt(qn, kcn_scr[...]) * scale
    m = jnp.maximum(jnp.max(s_loc, axis=-1, keepdims=True), jnp.max(s_ctx, axis=-1, keepdims=True))
    p_loc = jnp.exp(s_loc - m)
    p_ctx = jnp.exp(s_ctx - m)
    den = jnp.sum(p_loc, axis=-1, keepdims=True) + jnp.sum(p_ctx, axis=-1, keepdims=True)
    o = _dot(p_loc.astype(BF16), vw) + _dot(p_ctx.astype(BF16), vc_scr[...])
    o_ref[...] = (o / den).astype(BF16)


def _na_bias_tables(rpb, rows):
    n_rb = rows // NA_QROWS
    kh, kw = NA_WIN_ROWS, NA_WIN_COLS
    c = jnp.arange(GRID_W)
    cs = jnp.clip(c - kw // 2, 0, GRID_W - kw)
    in_col = (c[None, :] >= cs[:, None]) & (c[None, :] < cs[:, None] + kw)
    dc = jnp.clip(c[None, :] - c[:, None], 1 - kw, kw - 1) + kw - 1
    tabs = []
    for rb in (0, 1, n_rb - 1):
        ws = min(max(NA_QROWS * rb - kh // 2, 0), rows - NA_KROWS)
        r = NA_QROWS * rb + jnp.arange(NA_QROWS)
        kr = ws + jnp.arange(NA_KROWS)
        rs = jnp.clip(r - kh // 2, 0, rows - kh)
        in_row = (kr[None, :] >= rs[:, None]) & (kr[None, :] < rs[:, None] + kh)
        dr = jnp.clip(kr[None, :] - r[:, None], 1 - kh, kh - 1) + kh - 1
        val = rpb[:, dr[:, None, :, None], dc[None, :, None, :]]
        ok = in_row[:, None, :, None] & in_col[None, :, None, :]
        tab = jnp.where(ok[None], val.astype(F32), NEG_INF)
        tabs.append(tab.reshape(rpb.shape[0], NA_QROWS * GRID_W, NA_KROWS * GRID_W))
    return jnp.stack(tabs, axis=1)


def _na_call(p_lat, p_ctx, gain, bias, batch, seq, lc):
    rows = seq // GRID_W
    n_rb = rows // NA_QROWS
    tq = NA_QROWS * GRID_W
    nk = NA_KROWS * GRID_W
    heads = bias.shape[0]
    hd = HEAD_DIM
    return pl.pallas_call(
        functools.partial(_na_kernel, rows=rows),
        grid=(batch, heads, n_rb),
        in_specs=[pl.BlockSpec((tq, hd), lambda b, h, r: (b * n_rb + r, NA_Q + h)),
                  pl.BlockSpec((seq, hd), lambda b, h, r: (b, NA_K + h)),
                  pl.BlockSpec((seq, hd), lambda b, h, r: (b, NA_V + h)),
                  pl.BlockSpec((lc, hd), lambda b, h, r: (b, NA_K + h)),
                  pl.BlockSpec((lc, hd), lambda b, h, r: (b, NA_V + h)),
                  pl.BlockSpec((2, hd), lambda b, h, r: (0, 0)),
                  pl.BlockSpec((1, 3, tq, nk), lambda b, h, r: (h, 0, 0, 0))],
        out_specs=pl.BlockSpec((tq, hd), lambda b, h, r: (b * n_rb + r, h)),
        out_shape=jax.ShapeDtypeStruct((batch * seq, heads * hd), BF16),
        scratch_shapes=[pltpu.VMEM((seq, hd), BF16), pltpu.VMEM((seq, hd), BF16),
                        pltpu.VMEM((lc, hd), BF16), pltpu.VMEM((lc, hd), BF16)],
        compiler_params=_cparams(("arbitrary", "arbitrary", "arbitrary")),
        name="na_attn",
    )(p_lat, p_lat, p_lat, p_ctx, p_ctx, gain, bias)


def _ctx_attn_kernel(sink_ref, q_ref, k_ref, v_ref, gain_ref, o_ref, *, use_sink):
    scale = HEAD_DIM ** -0.5
    qn = _rms(q_ref[...], gain_ref[0:1, :]).astype(BF16)
    kn = _rms(k_ref[...], gain_ref[1:2, :]).astype(BF16)
    s = _dot_nt(qn, kn) * scale
    m = jnp.max(s, axis=-1, keepdims=True)
    if use_sink:
        sk = sink_ref[pl.program_id(1)]
        m = jnp.maximum(m, sk)
    p = jnp.exp(s - m)
    den = jnp.sum(p, axis=-1, keepdims=True)
    if use_sink:
        den = den + jnp.exp(sk - m)
    o = _dot(p.astype(BF16), v_ref[...].astype(BF16))
    o_ref[...] = (o / den).astype(BF16)


def _ctx_attn_call(p_ctx, gain, sink, batch, lc, heads, rep, qoff, koff, voff, use_sink, name):
    hd = HEAD_DIM
    return pl.pallas_call(
        functools.partial(_ctx_attn_kernel, use_sink=use_sink),
        grid=(batch, heads),
        in_specs=[pl.BlockSpec(memory_space=pltpu.SMEM),
                  pl.BlockSpec((lc, hd), lambda b, h: (b, qoff + h)),
                  pl.BlockSpec((lc, hd), lambda b, h: (b, koff + h // rep)),
                  pl.BlockSpec((lc, hd), lambda b, h: (b, voff + h // rep)),
                  pl.BlockSpec((2, hd), lambda b, h: (0, 0))],
        out_specs=pl.BlockSpec((lc, hd), lambda b, h: (b, h)),
        out_shape=jax.ShapeDtypeStruct((batch * lc, heads * hd), BF16),
        compiler_params=_cparams(("arbitrary", "arbitrary")),
        name=name,
    )(sink, p_ctx, p_ctx, p_ctx, gain)


def _rope(x, cos, sin_signed):
    lane = _iota(x.shape, 1)
    swapped = jnp.where(lane % 64 < 32, pltpu.roll(x, 96, axis=1), pltpu.roll(x, 32, axis=1))
    return x * cos + swapped * sin_signed


def _wa_kernel(sink_ref, q_ref, k_ref, v_ref, kc_ref, vc_ref, gain_ref, cq_ref, sq_ref, ck_ref, sk_ref, o_ref,
               kpad_scr, vpad_scr, kcn_scr, vc_scr, *, seq):
    g = pl.program_id(1)
    qb = pl.program_id(2)
    wb = WA_BLOCK
    hd = HEAD_DIM

    @pl.when(qb == 0)
    def _():
        g1 = gain_ref[1:2, :]
        zeros = jnp.zeros((wb, hd), BF16)
        kpad_scr[0:wb, :] = zeros
        kpad_scr[seq + wb:seq + 2 * wb, :] = zeros
        vpad_scr[0:wb, :] = zeros
        vpad_scr[seq + wb:seq + 2 * wb, :] = zeros
        kpad_scr[wb:seq + wb, :] = _rope(_rms(k_ref[...], g1), ck_ref[...], sk_ref[...]).astype(BF16)
        vpad_scr[wb:seq + wb, :] = v_ref[...].astype(BF16)
        kcn_scr[...] = _rms(kc_ref[...], g1).astype(BF16)
        vc_scr[...] = vc_ref[...].astype(BF16)

    scale = hd ** -0.5
    g0 = gain_ref[0:1, :]
    cq = cq_ref[...]
    sq = sq_ref[...]
    q2 = jnp.concatenate([_rope(_rms(q_ref[:, 0:hd], g0), cq, sq),
                          _rope(_rms(q_ref[:, hd:2 * hd], g0), cq, sq)], axis=0).astype(BF16)
    start = pl.multiple_of(qb * wb, wb)
    kw = kpad_scr[pl.ds(start, 3 * wb), :]
    vw = vpad_scr[pl.ds(start, 3 * wb), :]
    shp = (2 * wb, 3 * wb)
    i = _iota(shp, 0) % wb
    j = _iota(shp, 1)
    kpos = (qb - 1) * wb + j
    ok = (j - i >= 0) & (j - i <= 2 * wb) & (kpos >= 0) & (kpos < seq)
    s_loc = jnp.where(ok, _dot_nt(q2, kw) * scale, NEG_INF)
    s_ctx = _dot_nt(q2, kcn_scr[...]) * scale
    sink = jnp.where(_iota((2 * wb, 1), 0) < wb, sink_ref[2 * g], sink_ref[2 * g + 1])
    m = jnp.maximum(jnp.maximum(jnp.max(s_loc, axis=-1, keepdims=True), jnp.max(s_ctx, axis=-1, keepdims=True)),
                    sink)
    p_loc = jnp.exp(s_loc - m)
    p_ctx = jnp.exp(s_ctx - m)
    den = jnp.sum(p_loc, axis=-1, keepdims=True) + jnp.sum(p_ctx, axis=-1, keepdims=True) + jnp.exp(sink - m)
    o = (_dot(p_loc.astype(BF16), vw) + _dot(p_ctx.astype(BF16), vc_scr[...])) / den
    o_ref[:, 0:hd] = o[0:wb].astype(BF16)
    o_ref[:, hd:2 * hd] = o[wb:2 * wb].astype(BF16)


def _wa_call(p_lat, p_ctx, gain, sink, cos_t, sin_t, batch, seq, lc):
    wb = WA_BLOCK
    hd = HEAD_DIM
    nb = seq // wb
    kvh = 2
    return pl.pallas_call(
        functools.partial(_wa_kernel, seq=seq),
        grid=(batch, kvh, nb),
        in_specs=[pl.BlockSpec(memory_space=pltpu.SMEM),
                  pl.BlockSpec((wb, 2 * hd), lambda b, g, n: (b * nb + n, WA_Q // 2 + g)),
                  pl.BlockSpec((seq, hd), lambda b, g, n: (b, WA_K + g)),
                  pl.BlockSpec((seq, hd), lambda b, g, n: (b, WA_V + g)),
                  pl.BlockSpec((lc, hd), lambda b, g, n: (b, WA_K + g)),
                  pl.BlockSpec((lc, hd), lambda b, g, n: (b, WA_V + g)),
                  pl.BlockSpec((2, hd), lambda b, g, n: (0, 0)),
                  pl.BlockSpec((wb, hd), lambda b, g, n: (n, 0)),
                  pl.BlockSpec((wb, hd), lambda b, g, n: (n, 0)),
                  pl.BlockSpec((seq, hd), lambda b, g, n: (0, 0)),
                  pl.BlockSpec((seq, hd), lambda b, g, n: (0, 0))],
        out_specs=pl.BlockSpec((wb, 2 * hd), lambda b, g, n: (b * nb + n, g)),
        out_shape=jax.ShapeDtypeStruct((batch * seq, 2 * kvh * hd), BF16),
        scratch_shapes=[pltpu.VMEM((seq + 2 * wb, hd), BF16), pltpu.VMEM((seq + 2 * wb, hd), BF16),
                        pltpu.VMEM((lc, hd), BF16), pltpu.VMEM((lc, hd), BF16)],
        compiler_params=_cparams(("arbitrary", "arbitrary", "arbitrary")),
        name="wa_attn",
    )(sink, p_lat, p_lat, p_lat, p_ctx, p_ctx, gain, cos_t, sin_t, cos_t, sin_t)


def _dn_prep_kernel(x_ref, prev_ref, next_ref, w_ref, o_ref, *, n_tiles):
    t = pl.program_id(1)
    tl = x_ref.shape[0]
    x = x_ref[...]
    prev = jnp.where(t > 0, prev_ref[...], 0.0)
    nxt = jnp.where(t < n_tiles - 1, next_ref[...], 0.0)
    xe = jnp.concatenate([prev, x, nxt], axis=0)
    n = tl + 16
    taps = w_ref.shape[0]
    acc = jnp.zeros_like(x)
    for j in range(taps):
        off = 8 - taps // 2 + j
        shifted = xe if off % n == 0 else pltpu.roll(xe, n - off, axis=0)
        acc += shifted[0:tl] * w_ref[j:j + 1, :]
    y = _silu(acc)
    hd = HEAD_DIM
    nh = y.shape[1] // (3 * hd)
    for blk in range(3 * nh):
        yb = y[:, blk * hd:(blk + 1) * hd]
        if blk < 2 * nh:
            yb = yb * lax.rsqrt(jnp.sum(yb * yb, axis=-1, keepdims=True) + EPS)
            if blk < nh:
                yb = yb * hd ** -0.5
        o_ref[:, blk * hd:(blk + 1) * hd] = yb


def _dn_prep_call(p, conv_w, batch, length, tl):
    width = conv_w.shape[1]
    n_tiles = length // tl
    per8 = tl // 8
    last8 = batch * length // 8 - 1
    return pl.pallas_call(
        functools.partial(_dn_prep_kernel, n_tiles=n_tiles),
        grid=(batch, n_tiles),
        in_specs=[pl.BlockSpec((tl, width), lambda b, t: (b * n_tiles + t, 1)),
                  pl.BlockSpec((8, width), lambda b, t: (jnp.maximum((b * n_tiles + t) * per8 - 1, 0), 1)),
                  pl.BlockSpec((8, width), lambda b, t: (jnp.minimum((b * n_tiles + t + 1) * per8, last8), 1)),
                  pl.BlockSpec((conv_w.shape[0], width), lambda b, t: (0, 0))],
        out_specs=pl.BlockSpec((tl, width), lambda b, t: (b * n_tiles + t, 0)),
        out_shape=jax.ShapeDtypeStruct((batch * length, width), F32),
        compiler_params=_cparams(("arbitrary", "arbitrary")),
        name="dn_prep",
    )(p, p, p, conv_w)


def _unit_lower_inverse(lm, precision):
    n = lm.shape[0]
    eye = (_iota((n, n), 0) == _iota((n, n), 1)).astype(F32)
    x = eye - lm
    p = lm
    for _ in range(5):
        p = _dot(p, p, precision)
        x = x + _dot(x, p, precision)
    return x


def _dn_pass(q_ref, k_ref, v_ref, og_ref, gates_ref, gadd, gmul, norm_g, y_ref, length, h, state,
             u_scr, w_scr, qk_scr, qd_scr, ke_scr, egl_scr, acc_scr, solve_precision):
    pair = 2 * CHUNK
    n_pairs = length // pair
    n_chunks = length // CHUNK

    acc_scr[0:length, :] = jnp.zeros((length, HEAD_DIM), F32)

    shp = (pair, pair)
    ri = _iota(shp, 0)
    ci = _iota(shp, 1)
    same = (ri // CHUNK) == (ci // CHUNK)
    masks = ((same & (ci <= ri), same & (ci < ri)), (same & (ci >= ri), same & (ci > ri)))
    half = _iota((pair, 1), 0) < CHUNK

    def prep(pi, carry):
        r0 = pl.multiple_of(pi * pair, pair)
        q = q_ref[pl.ds(r0, pair), :]
        k = k_ref[pl.ds(r0, pair), :]
        v = v_ref[pl.ds(r0, pair), :]
        raw = gates_ref[pl.ds(r0, pair), :]
        gb = jnp.where(_iota(raw.shape, 1) < LANE_ALPHA, jax.nn.sigmoid(raw), -gmul * _softplus(raw + gadd))
        k16 = k.astype(BF16)
        kk = _dot_nt(k16, k16)
        qk = _dot_nt(q.astype(BF16), k16)
        for d in range(2):
            incl, strict = masks[d]
            beta_col = _lane_pick(gb, LANE_BETA + 4 * d + h)
            g_col = _lane_pick(gb, LANE_ALPHA + 4 * d + h)
            beta_row = _to_row(beta_col)
            g_row = _to_row(g_col)
            cum_col = jnp.sum(jnp.where(incl, g_row, 0.0), axis=-1, keepdims=True)
            cum_row = _to_row(cum_col)
            tot0 = jnp.sum(jnp.where(half, g_col, 0.0), axis=0, keepdims=True)
            tot1 = jnp.sum(jnp.where(half, 0.0, g_col), axis=0, keepdims=True)
            tot_col = jnp.where(half, tot0, tot1)
            decay = jnp.exp(jnp.where(incl, cum_col - cum_row, NEG_INF))
            lm = jnp.where(strict, kk * beta_col * decay, 0.0)
            ainv = _unit_lower_inverse(lm, solve_precision)
            u = _dot(ainv * beta_row, v, solve_precision)
            w = _dot(ainv * (beta_row * jnp.exp(cum_row)), k, solve_precision)
            u_scr[d, pl.ds(r0, pair), :] = u
            w_scr[d, pl.ds(r0, pair), :] = w.astype(BF16)
            qk_scr[d, pl.ds(r0, pair), :] = (qk * decay).astype(BF16)
            qd_scr[d, pl.ds(r0, pair), :] = (q * jnp.exp(cum_col)).astype(BF16)
            ke_scr[d, pl.ds(r0, pair), :] = (k * jnp.exp(tot_col - cum_col)).astype(BF16)
            egl = jnp.exp(jnp.where(_iota((16, 1), 0) < 8, tot0, tot1))
            egl_scr[d, pl.ds(pl.multiple_of(pi * 16, 16), 16), :] = jnp.broadcast_to(egl, (16, HEAD_DIM))
        return carry

    lax.fori_loop(0, n_pairs, prep, 0)

    def scan(s, st):
        new = []
        for d in range(2):
            c = s if d == 0 else n_chunks - 1 - s
            r0 = pl.multiple_of(c * CHUNK, CHUNK)
            sm = st[d]
            s16 = sm.astype(BF16)
            v_new = u_scr[d, pl.ds(r0, CHUNK), :] - _dot(w_scr[d, pl.ds(r0, CHUNK), :], s16)
            vn16 = v_new.astype(BF16)
            vn_pair = jnp.concatenate([vn16, vn16], axis=0)
            o = _dot(qd_scr[d, pl.ds(r0, CHUNK), :], s16) + _dot(qk_scr[d, pl.ds(r0, CHUNK), :], vn_pair)
            acc_scr[pl.ds(r0, CHUNK), :] += o
            egl = egl_scr[d, pl.ds(pl.multiple_of(c * 8, 8), 8), :][0:1, :]
            new.append(sm * egl + _dot_tn(ke_scr[d, pl.ds(r0, CHUNK), :], vn16))
        return tuple(new)

    state = lax.fori_loop(0, n_chunks, scan, state)

    y = _rms(acc_scr[0:length, :], norm_g) * _silu(og_ref[...])
    y_ref[...] = y.astype(BF16)
    return state


def _dn_kernel(q_ref, k_ref, v_ref, og_ref, gates_ref, qc_ref, kc_ref, vc_ref, ogc_ref, gatesc_ref,
               gadd_ref, gmul_ref, norm_ref, y_ref, yc_ref,
               u_scr, w_scr, qk_scr, qd_scr, ke_scr, egl_scr, acc_scr, *, seq, lc, solve_precision):
    h = pl.program_id(1)
    gadd = gadd_ref[...]
    gmul = gmul_ref[...]
    norm_g = norm_ref[...]
    scr = (u_scr, w_scr, qk_scr, qd_scr, ke_scr, egl_scr, acc_scr)
    zero = jnp.zeros((HEAD_DIM, HEAD_DIM), F32)
    state = _dn_pass(qc_ref, kc_ref, vc_ref, ogc_ref, gatesc_ref, gadd, gmul, norm_g, yc_ref, lc, h, (zero, zero),
                     *scr, solve_precision)
    _dn_pass(q_ref, k_ref, v_ref, og_ref, gates_ref, gadd, gmul, norm_g, y_ref, seq, h, state,
             *scr, solve_precision)


def _dn_call(dq_lat, dq_ctx, p_lat, p_ctx, pg_lat, pg_ctx, gadd, gmul, norm_g, batch, seq, lc, solve_precision):
    hd = HEAD_DIM
    heads = 4

    def blk(n, col):
        return pl.BlockSpec((n, hd), lambda b, h: (b, col + h))

    def gate_blk(n):
        return pl.BlockSpec((n, hd), lambda b, h: (b, 0))

    vec = pl.BlockSpec((1, hd), lambda b, h: (0, 0))
    return pl.pallas_call(
        functools.partial(_dn_kernel, seq=seq, lc=lc, solve_precision=solve_precision),
        grid=(batch, heads),
        in_specs=[blk(seq, 0), blk(seq, heads), blk(seq, 2 * heads), blk(seq, DN_OG), gate_blk(seq),
                  blk(lc, 0), blk(lc, heads), blk(lc, 2 * heads), blk(lc, DN_OG), gate_blk(lc),
                  vec, vec, vec],
        out_specs=[pl.BlockSpec((seq, hd), lambda b, h: (b, h)),
                   pl.BlockSpec((lc, hd), lambda b, h: (b, h))],
        out_shape=[jax.ShapeDtypeStruct((batch * seq, heads * hd), BF16),
                   jax.ShapeDtypeStruct((batch * lc, heads * hd), BF16)],
        scratch_shapes=[pltpu.VMEM((2, seq, hd), F32),
                        pltpu.VMEM((2, seq, hd), BF16),
                        pltpu.VMEM((2, seq, hd), BF16),
                        pltpu.VMEM((2, seq, hd), BF16),
                        pltpu.VMEM((2, seq, hd), BF16),
                        pltpu.VMEM((2, seq // CHUNK * 8, hd), F32),
                        pltpu.VMEM((seq, hd), F32)],
        compiler_params=_cparams(("arbitrary", "arbitrary"), DN_VMEM_LIMIT),
        name="deltanet",
    )(dq_lat, dq_lat, dq_lat, p_lat, pg_lat, dq_ctx, dq_ctx, dq_ctx, p_ctx, pg_ctx, gadd, gmul, norm_g)


def _ml_pass(q_ref, k_ref, v_ref, og_ref, gates_ref, gadd, norm_g, y_ref, length, h, state, acc_scr):
    n_chunks = length // CHUNK
    acc_scr[0:length, :] = jnp.zeros((length, HEAD_DIM), F32)
    shp = (CHUNK, CHUNK)
    ri = _iota(shp, 0)
    ci = _iota(shp, 1)
    masks = (ci <= ri, ci >= ri)
    head_lanes = (_iota((1, HEAD_DIM), 1) // ML_QK) == (h % 2)

    def step(s, st):
        new = []
        for d in range(2):
            c = s if d == 0 else n_chunks - 1 - s
            r0 = pl.multiple_of(c * CHUNK, CHUNK)
            incl = masks[d]
            c_st, n_st, m_st = st[d]
            q = jnp.where(head_lanes, q_ref[pl.ds(r0, CHUNK), :], 0.0)
            k = jnp.where(head_lanes, k_ref[pl.ds(r0, CHUNK), :], 0.0) * ML_QK ** -0.5
            v = v_ref[pl.ds(r0, CHUNK), :]
            gb = gates_ref[pl.ds(r0, CHUNK), :] + gadd
            ig_col = _lane_pick(gb, LANE_IG + 4 * d + h)
            lf_col = _log_sigmoid(_lane_pick(gb, LANE_FG + 4 * d + h))
            ig_row = _to_row(ig_col)
            lf_row = _to_row(lf_col)
            b_col = jnp.sum(jnp.where(incl, lf_row, 0.0), axis=-1, keepdims=True)
            b_row = _to_row(b_col)
            b_last = jnp.sum(lf_col, axis=0, keepdims=True)
            g_end = b_last - b_col + ig_col
            m_loc = jnp.max(g_end, axis=0, keepdims=True)
            kw = k * jnp.exp(g_end - m_loc)
            q16 = q.astype(BF16)
            v16 = v.astype(BF16)
            c_loc = _dot_tn(kw.astype(BF16), v16)
            n_loc = jnp.sum(kw, axis=0, keepdims=True)
            log_d = jnp.where(incl, b_col - b_row + ig_row, NEG_INF)
            m_inter = b_col + m_st
            m_t = jnp.maximum(jnp.max(log_d, axis=-1, keepdims=True), m_inter)
            sm = _dot_nt(q16, k.astype(BF16)) * jnp.exp(log_d - m_t)
            inter = jnp.exp(m_inter - m_t)
            num = _dot(sm.astype(BF16), v16) + inter * _dot(q16, c_st.astype(BF16))
            den = jnp.sum(sm, axis=-1, keepdims=True) + inter * jnp.sum(q * n_st, axis=-1, keepdims=True)
            acc_scr[pl.ds(r0, CHUNK), :] += num / jnp.maximum(jnp.abs(den), jnp.exp(-m_t))
            m_new = jnp.maximum(b_last + m_st, m_loc)
            a = jnp.exp(b_last + m_st - m_new)
            e = jnp.exp(m_loc - m_new)
            new.append((a * c_st + e * c_loc, a * n_st + e * n_loc, m_new))
        return tuple(new)

    state = lax.fori_loop(0, n_chunks, step, state)
    y = jax.nn.sigmoid(og_ref[...]) * _rms(acc_scr[0:length, :], norm_g)
    y_ref[...] = y.astype(BF16)
    return state


def _ml_kernel(q_ref, k_ref, v_ref, og_ref, gates_ref, qc_ref, kc_ref, vc_ref, ogc_ref, gatesc_ref,
               gadd_ref, norm_ref, y_ref, yc_ref, acc_scr, *, seq, lc):
    h = pl.program_id(1)
    gadd = gadd_ref[...]
    norm_g = norm_ref[0]
    zero = (jnp.zeros((HEAD_DIM, HEAD_DIM), F32), jnp.zeros((1, HEAD_DIM), F32), jnp.zeros((1, 1), F32))
    state = _ml_pass(qc_ref, kc_ref, vc_ref, ogc_ref, gatesc_ref, gadd, norm_g, yc_ref, lc, h, (zero, zero), acc_scr)
    _ml_pass(q_ref, k_ref, v_ref, og_ref, gates_ref, gadd, norm_g, y_ref, seq, h, state, acc_scr)


def _ml_call(p_lat, p_ctx, pg_lat, pg_ctx, gadd, norm_g, batch, seq, lc):
    hd = HEAD_DIM
    heads = 4

    def specs(n):
        return [pl.BlockSpec((n, hd), lambda b, h: (b, ML_Q + h // 2)),
                pl.BlockSpec((n, hd), lambda b, h: (b, ML_K + h // 2)),
                pl.BlockSpec((n, hd), lambda b, h: (b, ML_V + h)),
                pl.BlockSpec((n, hd), lambda b, h: (b, ML_OG + h)),
                pl.BlockSpec((n, hd), lambda b, h: (b, 0))]

    return pl.pallas_call(
        functools.partial(_ml_kernel, seq=seq, lc=lc),
        grid=(batch, heads),
        in_specs=specs(seq) + specs(lc) + [pl.BlockSpec((1, hd), lambda b, h: (0, 0)),
                                            pl.BlockSpec((1, 1, hd), lambda b, h: (h, 0, 0))],
        out_specs=[pl.BlockSpec((seq, hd), lambda b, h: (b, h)),
                   pl.BlockSpec((lc, hd), lambda b, h: (b, h))],
        out_shape=[jax.ShapeDtypeStruct((batch * seq, heads * hd), BF16),
                   jax.ShapeDtypeStruct((batch * lc, heads * hd), BF16)],
        scratch_shapes=[pltpu.VMEM((seq, hd), F32)],
        compiler_params=_cparams(("arbitrary", "arbitrary")),
        name="mlstm",
    )(p_lat, p_lat, p_lat, p_lat, pg_lat, p_ctx, p_ctx, p_ctx, p_ctx, pg_ctx, gadd, norm_g.reshape(heads, 1, hd))


def _regroup_w_in(w):
    gw = 512
    na_in = 3 * gw
    dn_in = 4 * gw + 16
    ml_main = 512 + 2 * gw
    dn0 = na_in
    ml0 = dn0 + dn_in
    wa0 = ml0 + ml_main + 16
    main = jnp.concatenate([w[:, :dn0 + 4 * gw], w[:, ml0:ml0 + ml_main], w[:, wa0:]], axis=1)
    gates = jnp.concatenate([w[:, dn0 + 4 * gw:ml0], w[:, ml0 + ml_main:wa0],
                             jnp.zeros((w.shape[0], 128 - 32), w.dtype)], axis=1)
    return main.astype(BF16), gates.astype(BF16)


def _rope_tables(seq):
    t = jnp.arange(seq)
    n_freq = HEAD_DIM // 4
    inv_freq = ROPE_THETA ** (-jnp.arange(n_freq, dtype=F32) / n_freq)
    pos = jnp.stack([t // GRID_W, t % GRID_W], axis=-1).astype(F32)
    ang = pos[:, :, None] * inv_freq
    cos, sin = jnp.cos(ang), jnp.sin(ang)
    cos_t = jnp.concatenate([cos, cos], axis=-1).reshape(seq, HEAD_DIM)
    sin_t = jnp.concatenate([-sin, sin], axis=-1).reshape(seq, HEAD_DIM)
    return cos_t, sin_t


def _gate_vectors(dt_bias, a_log, i_bias, f_bias):
    zeros8 = jnp.zeros((8,), F32)
    gadd = jnp.concatenate([zeros8, dt_bias.reshape(-1), i_bias.reshape(-1), f_bias.reshape(-1),
                            jnp.zeros((96,), F32)]).astype(F32)
    gmul = jnp.concatenate([zeros8, jnp.exp(a_log.astype(F32)).reshape(-1), jnp.zeros((112,), F32)])
    return gadd.reshape(1, 128), gmul.reshape(1, 128)


def _forward(x, c, ctx, c_ctx, w_ada, b_ada, norm_mix, norm_ffn, w_in, w_out, na_qk_gain, na_rpb, dn_conv,
             dn_a_log, dn_dt_bias, dn_norm, ml_i_bias, ml_f_bias, ml_norm, wa_qk_gain, wa_sink, w_ffn_in,
             w_ffn_out, solve_precision):
    batch, seq, d = x.shape
    lc = ctx.shape[1]
    depth = w_ada.shape[0]
    rows = seq // GRID_W
    tm = 512
    tn_ffn = 512

    cos_t, sin_t = _rope_tables(seq)
    cs = jnp.concatenate([c, c_ctx[None, :], jnp.zeros((8 - batch - 1, d), F32)], axis=0)
    mods = _ada_call(cs, w_ada, b_ada).reshape(depth, 8, 6, d)
    mods = jnp.pad(mods, ((0, 0), (0, 0), (0, 2), (0, 0)))

    xl = x.reshape(batch * seq, d)
    xc = ctx.reshape(batch * lc, d)
    hl = _norm_call(xl, mods[0, :batch], norm_mix[0:1], seq, tm)
    hc = _norm_call(xc, mods[0, batch:batch + 1], norm_mix[0:1], batch * lc, tm)

    for l in range(depth):
        need_ctx = l < depth - 1
        mod_l, mod_c = mods[l, :batch], mods[l, batch:batch + 1]
        w_main, w_gate = _regroup_w_in(w_in[l])
        p_lat = _matmul_call(hl, w_main, tm, 1024, "inproj")
        pg_lat = _matmul_call(hl, w_gate, tm, 128, "inproj_gates")
        p_ctx = _matmul_call(hc, w_main, tm, 1024, "inproj_ctx")
        pg_ctx = _matmul_call(hc, w_gate, tm, 128, "inproj_gates_ctx")

        bias = _na_bias_tables(na_rpb[l], rows)
        ya = _na_call(p_lat, p_ctx, na_qk_gain[l], bias, batch, seq, lc)
        gadd, gmul = _gate_vectors(dn_dt_bias[l], dn_a_log[l], ml_i_bias[l], ml_f_bias[l])
        dq_lat = _dn_prep_call(p_lat, dn_conv[l], batch, seq, 256)
        dq_ctx = _dn_prep_call(p_ctx, dn_conv[l], batch, lc, 256)
        yb, ycb = _dn_call(dq_lat, dq_ctx, p_lat, p_ctx, pg_lat, pg_ctx, gadd, gmul, dn_norm[l].reshape(1, -1),
                           batch, seq, lc, solve_precision)
        ym, ycm = _ml_call(p_lat, p_ctx, pg_lat, pg_ctx, gadd, ml_norm[l], batch, seq, lc)
        yw = _wa_call(p_lat, p_ctx, wa_qk_gain[l], wa_sink[l], cos_t, sin_t, batch, seq, lc)

        w_o = w_out[l].astype(BF16)
        w_f1 = w_ffn_in[l].astype(BF16)
        w_f2 = w_ffn_out[l].astype(BF16)
        nl = min(l + 1, depth - 1)
        xl, h2 = _outproj_call((ya, yb, ym, yw), w_o, xl, mod_l, norm_ffn[l:l + 1], seq, tm)
        xl, hl = _ffn_call(h2, w_f1, w_f2, xl, mod_l, mods[nl, :batch], norm_mix[nl:nl + 1], seq, tm, tn_ffn,
                           need_ctx)
        if need_ctx:
            yca = _ctx_attn_call(p_ctx, na_qk_gain[l], wa_sink[l], batch, lc, 4, 1, NA_Q, NA_K, NA_V, False,
                                 "na_ctx_attn")
            ycw = _ctx_attn_call(p_ctx, wa_qk_gain[l], wa_sink[l], batch, lc, 4, 2, WA_Q, WA_K, WA_V, True,
                                 "wa_ctx_attn")
            xc, h2c = _outproj_call((yca, ycb, ycm, ycw), w_o, xc, mod_c, norm_ffn[l:l + 1], batch * lc, tm)
            xc, hc = _ffn_call(h2c, w_f1, w_f2, xc, mod_c, mods[nl, batch:batch + 1], norm_mix[nl:nl + 1],
                               batch * lc, tm, tn_ffn, True)
    return xl.reshape(batch, seq, d)


def kernel(x, c, ctx, c_ctx, w_ada, b_ada, norm_mix, norm_ffn, w_in, w_out, na_qk_gain, na_rpb, dn_conv, dn_a_log,
           dn_dt_bias, dn_norm, ml_i_bias, ml_f_bias, ml_norm, wa_qk_gain, wa_sink, w_ffn_in, w_ffn_out):
    return _forward(x, c, ctx, c_ctx, w_ada, b_ada, norm_mix, norm_ffn, w_in, w_out, na_qk_gain, na_rpb, dn_conv,
                    dn_a_log, dn_dt_bias, dn_norm, ml_i_bias, ml_f_bias, ml_norm, wa_qk_gain, wa_sink, w_ffn_in,
                    w_ffn_out, lax.Precision.HIGHEST)
```

```python
import functools
import math

import jax
import jax.numpy as jnp
from jax import lax
from jax.experimental import pallas as pl
from jax.experimental.pallas import tpu as pltpu

F32 = jnp.float32
BF16 = jnp.bfloat16

HEAD_DIM = 128
GRID_W = 64
CHUNK = 64
PREP_GROUP = 4
SCAN_UNROLL = 2
NA_WIN_ROWS = 8
NA_WIN_COLS = 16
NA_QROWS = 4
NA_KROWS = NA_QROWS + NA_WIN_ROWS
WA_BLOCK = 128
ML_QK = 64
ROPE_THETA = 10000.0
EPS = 1e-6
NEG_INF = float("-inf")
VMEM_LIMIT = 48 * 1024 * 1024
DN_VMEM_LIMIT = 56 * 1024 * 1024

NA_Q, NA_K, NA_V = 0, 4, 8
DN_QKV, DN_OG = 12, 24
ML_Q, ML_K, ML_V, ML_OG = 28, 30, 32, 36
WA_Q, WA_K, WA_V = 40, 44, 46
MAIN_COLS = 48 * 128
LANE_BETA, LANE_ALPHA, LANE_IG, LANE_FG = 0, 8, 16, 24


def _cparams(sem, vmem_limit=VMEM_LIMIT):
    return pltpu.CompilerParams(dimension_semantics=sem, vmem_limit_bytes=vmem_limit)


def _rms(x, gain):
    return x * lax.rsqrt(jnp.mean(x * x, axis=-1, keepdims=True) + EPS) * gain


def _silu(x):
    return x * jax.nn.sigmoid(x)


def _softplus(x):
    return jnp.maximum(x, 0.0) + jnp.log1p(jnp.exp(-jnp.abs(x)))


def _log_sigmoid(x):
    return jnp.minimum(x, 0.0) - jnp.log1p(jnp.exp(-jnp.abs(x)))


def _dot(a, b, precision=None):
    return jnp.dot(a, b, preferred_element_type=F32, precision=precision)


def _dot_nt(a, b):
    return lax.dot_general(a, b, (((1,), (1,)), ((), ())), preferred_element_type=F32)


def _dot_tn(a, b):
    return lax.dot_general(a, b, (((0,), (0,)), ((), ())), preferred_element_type=F32)


def _iota(shape, dim):
    return lax.broadcasted_iota(jnp.int32, shape, dim)


def _lane_pick(x, lane):
    return jnp.sum(jnp.where(_iota(x.shape, 1) == lane, x, 0.0), axis=-1, keepdims=True)


def _to_row(col):
    n = col.shape[0]
    eye = _iota((n, n), 0) == _iota((n, n), 1)
    return jnp.sum(jnp.where(eye, col, 0.0), axis=0, keepdims=True)


def _ada_kernel(c_ref, w_ref, b_ref, o_ref):
    a = _silu(c_ref[...]).astype(BF16)
    o_ref[0] = _dot(a, w_ref[0].astype(BF16)) + b_ref[0]


def _ada_call(cs, w_ada, b_ada):
    depth, d, n = w_ada.shape
    tn = 1024
    return pl.pallas_call(
        _ada_kernel,
        grid=(depth, n // tn),
        in_specs=[pl.BlockSpec((8, d), lambda l, j: (0, 0)),
                  pl.BlockSpec((1, d, tn), lambda l, j: (l, 0, j)),
                  pl.BlockSpec((1, 1, tn), lambda l, j: (l, 0, j))],
        out_specs=pl.BlockSpec((1, 8, tn), lambda l, j: (l, 0, j)),
        out_shape=jax.ShapeDtypeStruct((depth, 8, n), F32),
        compiler_params=_cparams(("arbitrary", "arbitrary")),
        name="ada",
    )(cs, w_ada, b_ada.reshape(depth, 1, n))


def _norm_kernel(x_ref, mod_ref, g_ref, h_ref):
    mod = mod_ref[0]
    h = _rms(x_ref[...], g_ref[...])
    h_ref[...] = (h * (1.0 + mod[1:2]) + mod[0:1]).astype(BF16)


def _norm_call(x, mod, gain, rows_per_mod, tm):
    m, d = x.shape
    per = rows_per_mod // tm
    return pl.pallas_call(
        _norm_kernel,
        grid=(m // tm,),
        in_specs=[pl.BlockSpec((tm, d), lambda i: (i, 0)),
                  pl.BlockSpec((1, 8, d), lambda i: (i // per, 0, 0)),
                  pl.BlockSpec((1, d), lambda i: (0, 0))],
        out_specs=pl.BlockSpec((tm, d), lambda i: (i, 0)),
        out_shape=jax.ShapeDtypeStruct((m, d), BF16),
        compiler_params=_cparams(("arbitrary",)),
        name="norm_mod",
    )(x, mod, gain)


def _matmul_kernel(a_ref, w_ref, o_ref):
    o_ref[...] = _dot(a_ref[...], w_ref[...])


def _matmul_call(a, w, tm, tn, name):
    m, k = a.shape
    n = w.shape[1]
    return pl.pallas_call(
        _matmul_kernel,
        grid=(n // tn, m // tm),
        in_specs=[pl.BlockSpec((tm, k), lambda j, i: (i, 0)),
                  pl.BlockSpec((k, tn), lambda j, i: (0, j))],
        out_specs=pl.BlockSpec((tm, tn), lambda j, i: (i, j)),
        out_shape=jax.ShapeDtypeStruct((m, n), F32),
        compiler_params=_cparams(("arbitrary", "arbitrary")),
        name=name,
    )(a, w)


def _outproj_kernel(ya_ref, yb_ref, ym_ref, yw_ref, w_ref, x_ref, mod_ref, g_ref, xo_ref, h_ref):
    gw = ya_ref.shape[1]
    acc = _dot(ya_ref[...], w_ref[0:gw, :])
    acc += _dot(yb_ref[...], w_ref[gw:2 * gw, :])
    acc += _dot(ym_ref[...], w_ref[2 * gw:3 * gw, :])
    acc += _dot(yw_ref[...], w_ref[3 * gw:4 * gw, :])
    mod = mod_ref[0]
    xn = x_ref[...] + mod[2:3] * acc
    xo_ref[...] = xn
    h = _rms(xn, g_ref[...])
    h_ref[...] = (h * (1.0 + mod[4:5]) + mod[3:4]).astype(BF16)


def _outproj_call(ys, w, x, mod, gain, rows_per_mod, tm):
    m, d = x.shape
    gw = ys[0].shape[1]
    per = rows_per_mod // tm
    yspec = pl.BlockSpec((tm, gw), lambda i: (i, 0))
    return pl.pallas_call(
        _outproj_kernel,
        grid=(m // tm,),
        in_specs=[yspec, yspec, yspec, yspec,
                  pl.BlockSpec((4 * gw, d), lambda i: (0, 0)),
                  pl.BlockSpec((tm, d), lambda i: (i, 0)),
                  pl.BlockSpec((1, 8, d), lambda i: (i // per, 0, 0)),
                  pl.BlockSpec((1, d), lambda i: (0, 0))],
        out_specs=[pl.BlockSpec((tm, d), lambda i: (i, 0)),
                   pl.BlockSpec((tm, d), lambda i: (i, 0))],
        out_shape=[jax.ShapeDtypeStruct((m, d), F32), jax.ShapeDtypeStruct((m, d), BF16)],
        input_output_aliases={5: 0},
        compiler_params=_cparams(("arbitrary",)),
        name="outproj",
    )(*ys, w, x, mod, gain)


def _ffn_kernel(h_ref, wg_ref, wu_ref, w2_ref, x_ref, mod_ref, modn_ref, gn_ref, xo_ref, hn_ref, acc_ref,
                *, with_next):
    j = pl.program_id(1)

    @pl.when(j == 0)
    def _():
        acc_ref[...] = jnp.zeros_like(acc_ref)

    h = h_ref[...]
    act = _silu(_dot(h, wg_ref[...])) * _dot(h, wu_ref[...])
    acc_ref[...] += _dot(act.astype(BF16), w2_ref[...])

    @pl.when(j == pl.num_programs(1) - 1)
    def _():
        xn = x_ref[...] + mod_ref[0][5:6] * acc_ref[...]
        xo_ref[...] = xn
        if with_next:
            modn = modn_ref[0]
            hn = _rms(xn, gn_ref[...])
            hn_ref[...] = (hn * (1.0 + modn[1:2]) + modn[0:1]).astype(BF16)
        else:
            hn_ref[...] = jnp.zeros_like(hn_ref)


def _ffn_call(h, w_in, w_out, x, mod, mod_next, gain_next, rows_per_mod, tm, tn, with_next):
    m, d = x.shape
    hidden = w_out.shape[0]
    nj = hidden // tn
    per = rows_per_mod // tm
    return pl.pallas_call(
        functools.partial(_ffn_kernel, with_next=with_next),
        grid=(m // tm, nj),
        in_specs=[pl.BlockSpec((tm, d), lambda i, j: (i, 0)),
                  pl.BlockSpec((d, tn), lambda i, j: (0, j)),
                  pl.BlockSpec((d, tn), lambda i, j: (0, nj + j)),
                  pl.BlockSpec((tn, d), lambda i, j: (j, 0)),
                  pl.BlockSpec((tm, d), lambda i, j: (i, 0)),
                  pl.BlockSpec((1, 8, d), lambda i, j: (i // per, 0, 0)),
                  pl.BlockSpec((1, 8, d), lambda i, j: (i // per, 0, 0)),
                  pl.BlockSpec((1, d), lambda i, j: (0, 0))],
        out_specs=[pl.BlockSpec((tm, d), lambda i, j: (i, 0)),
                   pl.BlockSpec((tm, d), lambda i, j: (i, 0))],
        out_shape=[jax.ShapeDtypeStruct((m, d), F32), jax.ShapeDtypeStruct((m, d), BF16)],
        scratch_shapes=[pltpu.VMEM((tm, d), F32)],
        input_output_aliases={4: 0},
        compiler_params=_cparams(("arbitrary", "arbitrary")),
        name="ffn",
    )(h, w_in, w_in, w_out, x, mod, mod_next, gain_next)


def _na_kernel(q_ref, k_ref, v_ref, kc_ref, vc_ref, gain_ref, bias_ref, o_ref,
               kn_scr, v_scr, kcn_scr, vc_scr, *, rows):
    rb = pl.program_id(2)
    n_rb = pl.num_programs(2)

    @pl.when(rb == 0)
    def _():
        g1 = gain_ref[1:2, :]
        kn_scr[...] = _rms(k_ref[...], g1).astype(BF16)
        v_scr[...] = v_ref[...].astype(BF16)
        kcn_scr[...] = _rms(kc_ref[...], g1).astype(BF16)
        vc_scr[...] = vc_ref[...].astype(BF16)

    scale = HEAD_DIM ** -0.5
    nk = NA_KROWS * GRID_W
    qn = _rms(q_ref[...], gain_ref[0:1, :]).astype(BF16)
    ws = jnp.clip(NA_QROWS * rb - NA_WIN_ROWS // 2, 0, rows - NA_KROWS)
    start = pl.multiple_of(ws * GRID_W, GRID_W)
    kw = kn_scr[pl.ds(start, nk), :]
    vw = v_scr[pl.ds(start, nk), :]
    typ = jnp.where(rb == 0, 0, jnp.where(rb == n_rb - 1, 2, 1))
    bias = bias_ref[0, 0, pl.ds(typ, 1)][0]
    s_loc = _dot_nt(qn, kw) * scale + bias
    s_ctx = _dot_nt(qn, kcn_scr[...]) * scale
    m = jnp.maximum(jnp.max(s_loc, axis=-1, keepdims=True), jnp.max(s_ctx, axis=-1, keepdims=True))
    p_loc = jnp.exp(s_loc - m)
    p_ctx = jnp.exp(s_ctx - m)
    den = jnp.sum(p_loc, axis=-1, keepdims=True) + jnp.sum(p_ctx, axis=-1, keepdims=True)
    o = _dot(p_loc.astype(BF16), vw) + _dot(p_ctx.astype(BF16), vc_scr[...])
    o_ref[...] = (o / den).astype(BF16)


def _na_bias_tables(rpb, rows):
    n_rb = rows // NA_QROWS
    kh, kw = NA_WIN_ROWS, NA_WIN_COLS
    c = jnp.arange(GRID_W)
    cs = jnp.clip(c - kw // 2, 0, GRID_W - kw)
    in_col = (c[None, :] >= cs[:, None]) & (c[None, :] < cs[:, None] + kw)
    dc = jnp.clip(c[None, :] - c[:, None], 1 - kw, kw - 1) + kw - 1
    dc_hot = jax.nn.one_hot(dc, 2 * kw - 1, dtype=F32)
    tabs = []
    for rb in (0, 1, n_rb - 1):
        ws = min(max(NA_QROWS * rb - kh // 2, 0), rows - NA_KROWS)
        r = NA_QROWS * rb + jnp.arange(NA_QROWS)
        kr = ws + jnp.arange(NA_KROWS)
        rs = jnp.clip(r - kh // 2, 0, rows - kh)
        in_row = (kr[None, :] >= rs[:, None]) & (kr[None, :] < rs[:, None] + kh)
        dr = jnp.clip(kr[None, :] - r[:, None], 1 - kh, kh - 1) + kh - 1
        dr_hot = jax.nn.one_hot(dr, 2 * kh - 1, dtype=F32)
        val = jnp.einsum("ija,lhab,ckb->lhicjk", dr_hot, rpb.astype(F32), dc_hot,
                         precision=lax.Precision.HIGHEST)
        ok = in_row[:, None, :, None] & in_col[None, :, None, :]
        tab = jnp.where(ok[None, None], val, NEG_INF)
        tabs.append(tab.reshape(rpb.shape[0], rpb.shape[1], NA_QROWS * GRID_W, NA_KROWS * GRID_W))
    return jnp.stack(tabs, axis=2)


def _na_call(p_lat, p_ctx, gain, bias, layer, batch, seq, lc):
    rows = seq // GRID_W
    n_rb = rows // NA_QROWS
    tq = NA_QROWS * GRID_W
    nk = NA_KROWS * GRID_W
    heads = bias.shape[1]
    hd = HEAD_DIM
    return pl.pallas_call(
        functools.partial(_na_kernel, rows=rows),
        grid=(batch, heads, n_rb),
        in_specs=[pl.BlockSpec((tq, hd), lambda b, h, r: (b * n_rb + r, NA_Q + h)),
                  pl.BlockSpec((seq, hd), lambda b, h, r: (b, NA_K + h)),
                  pl.BlockSpec((seq, hd), lambda b, h, r: (b, NA_V + h)),
                  pl.BlockSpec((lc, hd), lambda b, h, r: (b, NA_K + h)),
                  pl.BlockSpec((lc, hd), lambda b, h, r: (b, NA_V + h)),
                  pl.BlockSpec((2, hd), lambda b, h, r: (0, 0)),
                  pl.BlockSpec((1, 1, 3, tq, nk), lambda b, h, r: (layer, h, 0, 0, 0))],
        out_specs=pl.BlockSpec((tq, hd), lambda b, h, r: (b * n_rb + r, h)),
        out_shape=jax.ShapeDtypeStruct((batch * seq, heads * hd), BF16),
        scratch_shapes=[pltpu.VMEM((seq, hd), BF16), pltpu.VMEM((seq, hd), BF16),
                        pltpu.VMEM((lc, hd), BF16), pltpu.VMEM((lc, hd), BF16)],
        compiler_params=_cparams(("arbitrary", "arbitrary", "arbitrary")),
        name="na_attn",
    )(p_lat, p_lat, p_lat, p_ctx, p_ctx, gain, bias)


def _ctx_attn_kernel(sink_ref, q_ref, k_ref, v_ref, gain_ref, o_ref, *, use_sink):
    scale = HEAD_DIM ** -0.5
    qn = _rms(q_ref[...], gain_ref[0:1, :]).astype(BF16)
    kn = _rms(k_ref[...], gain_ref[1:2, :]).astype(BF16)
    s = _dot_nt(qn, kn) * scale
    m = jnp.max(s, axis=-1, keepdims=True)
    if use_sink:
        sk = sink_ref[pl.program_id(1)]
        m = jnp.maximum(m, sk)
    p = jnp.exp(s - m)
    den = jnp.sum(p, axis=-1, keepdims=True)
    if use_sink:
        den = den + jnp.exp(sk - m)
    o = _dot(p.astype(BF16), v_ref[...].astype(BF16))
    o_ref[...] = (o / den).astype(BF16)


def _ctx_attn_call(p_ctx, gain, sink, batch, lc, heads, rep, qoff, koff, voff, use_sink, name):
    hd = HEAD_DIM
    return pl.pallas_call(
        functools.partial(_ctx_attn_kernel, use_sink=use_sink),
        grid=(batch, heads),
        in_specs=[pl.BlockSpec(memory_space=pltpu.SMEM),
                  pl.BlockSpec((lc, hd), lambda b, h: (b, qoff + h)),
                  pl.BlockSpec((lc, hd), lambda b, h: (b, koff + h // rep)),
                  pl.BlockSpec((lc, hd), lambda b, h: (b, voff + h // rep)),
                  pl.BlockSpec((2, hd), lambda b, h: (0, 0))],
        out_specs=pl.BlockSpec((lc, hd), lambda b, h: (b, h)),
        out_shape=jax.ShapeDtypeStruct((batch * lc, heads * hd), BF16),
        compiler_params=_cparams(("arbitrary", "arbitrary")),
        name=name,
    )(sink, p_ctx, p_ctx, p_ctx, gain)


def _rope(x, cos, sin_signed):
    lane = _iota(x.shape, 1)
    swapped = jnp.where(lane % 64 < 32, pltpu.roll(x, 96, axis=1), pltpu.roll(x, 32, axis=1))
    return x * cos + swapped * sin_signed


def _wa_kernel(sink_ref, q_ref, k_ref, v_ref, kc_ref, vc_ref, gain_ref, cq_ref, sq_ref, ck_ref, sk_ref, o_ref,
               kpad_scr, vpad_scr, kcn_scr, vc_scr, *, seq):
    g = pl.program_id(1)
    qb = pl.program_id(2)
    wb = WA_BLOCK
    hd = HEAD_DIM

    @pl.when(qb == 0)
    def _():
        g1 = gain_ref[1:2, :]
        zeros = jnp.zeros((wb, hd), BF16)
        kpad_scr[0:wb, :] = zeros
        kpad_scr[seq + wb:seq + 2 * wb, :] = zeros
        vpad_scr[0:wb, :] = zeros
        vpad_scr[seq + wb:seq + 2 * wb, :] = zeros
        kpad_scr[wb:seq + wb, :] = _rope(_rms(k_ref[...], g1), ck_ref[...], sk_ref[...]).astype(BF16)
        vpad_scr[wb:seq + wb, :] = v_ref[...].astype(BF16)
        kcn_scr[...] = _rms(kc_ref[...], g1).astype(BF16)
        vc_scr[...] = vc_ref[...].astype(BF16)

    scale = hd ** -0.5
    g0 = gain_ref[0:1, :]
    cq = cq_ref[...]
    sq = sq_ref[...]
    q2 = jnp.concatenate([_rope(_rms(q_ref[:, 0:hd], g0), cq, sq),
                          _rope(_rms(q_ref[:, hd:2 * hd], g0), cq, sq)], axis=0).astype(BF16)
    start = pl.multiple_of(qb * wb, wb)
    kw = kpad_scr[pl.ds(start, 3 * wb), :]
    vw = vpad_scr[pl.ds(start, 3 * wb), :]
    shp = (2 * wb, 3 * wb)
    i = _iota(shp, 0) % wb
    j = _iota(shp, 1)
    kpos = (qb - 1) * wb + j
    ok = (j - i >= 0) & (j - i <= 2 * wb) & (kpos >= 0) & (kpos < seq)
    s_loc = jnp.where(ok, _dot_nt(q2, kw) * scale, NEG_INF)
    s_ctx = _dot_nt(q2, kcn_scr[...]) * scale
    sink = jnp.where(_iota((2 * wb, 1), 0) < wb, sink_ref[2 * g], sink_ref[2 * g + 1])
    m = jnp.maximum(jnp.maximum(jnp.max(s_loc, axis=-1, keepdims=True), jnp.max(s_ctx, axis=-1, keepdims=True)),
                    sink)
    p_loc = jnp.exp(s_loc - m)
    p_ctx = jnp.exp(s_ctx - m)
    den = jnp.sum(p_loc, axis=-1, keepdims=True) + jnp.sum(p_ctx, axis=-1, keepdims=True) + jnp.exp(sink - m)
    o = (_dot(p_loc.astype(BF16), vw) + _dot(p_ctx.astype(BF16), vc_scr[...])) / den
    o_ref[:, 0:hd] = o[0:wb].astype(BF16)
    o_ref[:, hd:2 * hd] = o[wb:2 * wb].astype(BF16)


def _wa_call(p_lat, p_ctx, gain, sink, cos_t, sin_t, batch, seq, lc):
    wb = WA_BLOCK
    hd = HEAD_DIM
    nb = seq // wb
    kvh = 2
    return pl.pallas_call(
        functools.partial(_wa_kernel, seq=seq),
        grid=(batch, kvh, nb),
        in_specs=[pl.BlockSpec(memory_space=pltpu.SMEM),
                  pl.BlockSpec((wb, 2 * hd), lambda b, g, n: (b * nb + n, WA_Q // 2 + g)),
                  pl.BlockSpec((seq, hd), lambda b, g, n: (b, WA_K + g)),
                  pl.BlockSpec((seq, hd), lambda b, g, n: (b, WA_V + g)),
                  pl.BlockSpec((lc, hd), lambda b, g, n: (b, WA_K + g)),
                  pl.BlockSpec((lc, hd), lambda b, g, n: (b, WA_V + g)),
                  pl.BlockSpec((2, hd), lambda b, g, n: (0, 0)),
                  pl.BlockSpec((wb, hd), lambda b, g, n: (n, 0)),
                  pl.BlockSpec((wb, hd), lambda b, g, n: (n, 0)),
                  pl.BlockSpec((seq, hd), lambda b, g, n: (0, 0)),
                  pl.BlockSpec((seq, hd), lambda b, g, n: (0, 0))],
        out_specs=pl.BlockSpec((wb, 2 * hd), lambda b, g, n: (b * nb + n, g)),
        out_shape=jax.ShapeDtypeStruct((batch * seq, 2 * kvh * hd), BF16),
        scratch_shapes=[pltpu.VMEM((seq + 2 * wb, hd), BF16), pltpu.VMEM((seq + 2 * wb, hd), BF16),
                        pltpu.VMEM((lc, hd), BF16), pltpu.VMEM((lc, hd), BF16)],
        compiler_params=_cparams(("arbitrary", "arbitrary", "arbitrary")),
        name="wa_attn",
    )(sink, p_lat, p_lat, p_lat, p_ctx, p_ctx, gain, cos_t, sin_t, cos_t, sin_t)


def _dn_prep_kernel(x_ref, prev_ref, next_ref, w_ref, o_ref, *, n_tiles):
    t = pl.program_id(1)
    tl = x_ref.shape[0]
    x = x_ref[...]
    prev = jnp.where(t > 0, prev_ref[...], 0.0)
    nxt = jnp.where(t < n_tiles - 1, next_ref[...], 0.0)
    xe = jnp.concatenate([prev, x, nxt], axis=0)
    n = tl + 16
    taps = w_ref.shape[0]
    acc = jnp.zeros_like(x)
    for j in range(taps):
        off = 8 - taps // 2 + j
        shifted = xe if off % n == 0 else pltpu.roll(xe, n - off, axis=0)
        acc += shifted[0:tl] * w_ref[j:j + 1, :]
    y = _silu(acc)
    hd = HEAD_DIM
    nh = y.shape[1] // (3 * hd)
    for blk in range(3 * nh):
        yb = y[:, blk * hd:(blk + 1) * hd]
        if blk < 2 * nh:
            yb = yb * lax.rsqrt(jnp.sum(yb * yb, axis=-1, keepdims=True) + EPS)
            if blk < nh:
                yb = yb * hd ** -0.5
        o_ref[:, blk * hd:(blk + 1) * hd] = yb


def _dn_prep_call(p, conv_w, batch, length, tl):
    width = conv_w.shape[1]
    n_tiles = length // tl
    per8 = tl // 8
    last8 = batch * length // 8 - 1
    return pl.pallas_call(
        functools.partial(_dn_prep_kernel, n_tiles=n_tiles),
        grid=(batch, n_tiles),
        in_specs=[pl.BlockSpec((tl, width), lambda b, t: (b * n_tiles + t, 1)),
                  pl.BlockSpec((8, width), lambda b, t: (jnp.maximum((b * n_tiles + t) * per8 - 1, 0), 1)),
                  pl.BlockSpec((8, width), lambda b, t: (jnp.minimum((b * n_tiles + t + 1) * per8, last8), 1)),
                  pl.BlockSpec((conv_w.shape[0], width), lambda b, t: (0, 0))],
        out_specs=pl.BlockSpec((tl, width), lambda b, t: (b * n_tiles + t, 0)),
        out_shape=jax.ShapeDtypeStruct((batch * length, width), F32),
        compiler_params=_cparams(("arbitrary", "arbitrary")),
        name="dn_prep",
    )(p, p, p, conv_w)


def _split_bf16(a):
    hi = a.astype(BF16)
    return hi, (a - hi.astype(F32)).astype(BF16)


def _solve_dot(a, b, mode):
    if mode == "highest":
        return _dot(a, b, lax.Precision.HIGHEST)
    if mode == "bf16":
        return _dot(a.astype(BF16), b.astype(BF16))
    a_hi, a_lo = _split_bf16(a)
    b_hi, b_lo = _split_bf16(b)
    return _dot(a_hi, b_hi) + (_dot(a_hi, b_lo) + _dot(a_lo, b_hi))


def _unit_lower_inverse(lm, mode):
    n = lm.shape[0]
    eye = (_iota((n, n), 0) == _iota((n, n), 1)).astype(F32)
    x = eye - lm
    p = lm
    for _ in range(5):
        p = _solve_dot(p, p, mode)
        x = x + _solve_dot(x, p, mode)
    return x


def _dn_pass(q_ref, k_ref, v_ref, og_ref, gates_ref, gadd, gmul, norm_g, y_ref, length, h, state,
             u_scr, w_scr, qk_scr, qd_scr, ke_scr, egl_scr, acc_scr, solve_precision):
    pair = 2 * CHUNK
    n_pairs = length // pair
    n_chunks = length // CHUNK

    acc_scr[0:length, :] = jnp.zeros((length, HEAD_DIM), F32)

    shp = (pair, pair)
    ri = _iota(shp, 0)
    ci = _iota(shp, 1)
    same = (ri // CHUNK) == (ci // CHUNK)
    masks = ((same & (ci <= ri), same & (ci < ri)), (same & (ci >= ri), same & (ci > ri)))
    half = _iota((pair, 1), 0) < CHUNK

    group = min(n_pairs, PREP_GROUP)

    def prep(gi, carry):
        chains = []
        for g in range(group):
            pi = gi * group + g
            r0 = pl.multiple_of(pi * pair, pair)
            q = q_ref[pl.ds(r0, pair), :]
            k = k_ref[pl.ds(r0, pair), :]
            raw = gates_ref[pl.ds(r0, pair), :]
            gb = jnp.where(_iota(raw.shape, 1) < LANE_ALPHA, jax.nn.sigmoid(raw), -gmul * _softplus(raw + gadd))
            k16 = k.astype(BF16)
            kk = _dot_nt(k16, k16)
            qk = _dot_nt(q.astype(BF16), k16)
            for d in range(2):
                incl, strict = masks[d]
                beta_col = _lane_pick(gb, LANE_BETA + 4 * d + h)
                g_col = _lane_pick(gb, LANE_ALPHA + 4 * d + h)
                beta_row = _to_row(beta_col)
                g_row = _to_row(g_col)
                cum_col = jnp.sum(jnp.where(incl, g_row, 0.0), axis=-1, keepdims=True)
                cum_row = _to_row(cum_col)
                tot0 = jnp.sum(jnp.where(half, g_col, 0.0), axis=0, keepdims=True)
                tot1 = jnp.sum(jnp.where(half, 0.0, g_col), axis=0, keepdims=True)
                tot_col = jnp.where(half, tot0, tot1)
                decay = jnp.exp(jnp.where(incl, cum_col - cum_row, NEG_INF))
                lm = jnp.where(strict, kk * beta_col * decay, 0.0)
                qk_scr[d, pl.ds(r0, pair), :] = (qk * decay).astype(BF16)
                qd_scr[d, pl.ds(r0, pair), :] = (q * jnp.exp(cum_col)).astype(BF16)
                ke_scr[d, pl.ds(r0, pair), :] = (k * jnp.exp(tot_col - cum_col)).astype(BF16)
                egl = jnp.exp(jnp.where(_iota((16, 1), 0) < 8, tot0, tot1))
                egl_scr[d, pl.ds(pl.multiple_of(pi * 16, 16), 16), :] = jnp.broadcast_to(egl, (16, HEAD_DIM))
                chains.append((d, r0, lm, beta_row, beta_row * jnp.exp(cum_row)))
        eye = (ri == ci).astype(F32)
        ps = [c[2] for c in chains]
        xs = [eye - p for p in ps]
        for _ in range(5):
            ps = [_solve_dot(p, p, solve_precision) for p in ps]
            xs = [x + _solve_dot(x, p, solve_precision) for x, p in zip(xs, ps)]
        for (d, r0, _, beta_row, bexp_row), ainv in zip(chains, xs):
            u_scr[d, pl.ds(r0, pair), :] = _solve_dot(ainv * beta_row, v_ref[pl.ds(r0, pair), :], solve_precision)
            w_scr[d, pl.ds(r0, pair), :] = _solve_dot(ainv * bexp_row, k_ref[pl.ds(r0, pair), :],
                                                      solve_precision).astype(BF16)
        return carry

    lax.fori_loop(0, n_pairs // group, prep, 0)

    def scan(s, st):
        new = []
        for d in range(2):
            c = s if d == 0 else n_chunks - 1 - s
            r0 = pl.multiple_of(c * CHUNK, CHUNK)
            sm = st[d]
            s16 = sm.astype(BF16)
            v_new = u_scr[d, pl.ds(r0, CHUNK), :] - _dot(w_scr[d, pl.ds(r0, CHUNK), :], s16)
            vn16 = v_new.astype(BF16)
            vn_pair = jnp.concatenate([vn16, vn16], axis=0)
            o = _dot(qd_scr[d, pl.ds(r0, CHUNK), :], s16) + _dot(qk_scr[d, pl.ds(r0, CHUNK), :], vn_pair)
            acc_scr[pl.ds(r0, CHUNK), :] += o
            egl = egl_scr[d, pl.ds(pl.multiple_of(c * 8, 8), 8), :][0:1, :]
            new.append(sm * egl + _dot_tn(ke_scr[d, pl.ds(r0, CHUNK), :], vn16))
        return tuple(new)

    state = lax.fori_loop(0, n_chunks, scan, state, unroll=SCAN_UNROLL)

    y = _rms(acc_scr[0:length, :], norm_g) * _silu(og_ref[...])
    y_ref[...] = y.astype(BF16)
    return state


def _dn_kernel(q_ref, k_ref, v_ref, og_ref, gates_ref, qc_ref, kc_ref, vc_ref, ogc_ref, gatesc_ref,
               gadd_ref, gmul_ref, norm_ref, y_ref, yc_ref,
               u_scr, w_scr, qk_scr, qd_scr, ke_scr, egl_scr, acc_scr, *, seq, lc, solve_precision):
    h = pl.program_id(1)
    gadd = gadd_ref[...]
    gmul = gmul_ref[...]
    norm_g = norm_ref[...]
    scr = (u_scr, w_scr, qk_scr, qd_scr, ke_scr, egl_scr, acc_scr)
    zero = jnp.zeros((HEAD_DIM, HEAD_DIM), F32)
    state = _dn_pass(qc_ref, kc_ref, vc_ref, ogc_ref, gatesc_ref, gadd, gmul, norm_g, yc_ref, lc, h, (zero, zero),
                     *scr, solve_precision)
    _dn_pass(q_ref, k_ref, v_ref, og_ref, gates_ref, gadd, gmul, norm_g, y_ref, seq, h, state,
             *scr, solve_precision)


def _dn_call(dq_lat, dq_ctx, p_lat, p_ctx, pg_lat, pg_ctx, gadd, gmul, norm_g, batch, seq, lc, solve_precision):
    hd = HEAD_DIM
    heads = 4

    def blk(n, col):
        return pl.BlockSpec((n, hd), lambda b, h: (b, col + h))

    def gate_blk(n):
        return pl.BlockSpec((n, hd), lambda b, h: (b, 0))

    vec = pl.BlockSpec((1, hd), lambda b, h: (0, 0))
    return pl.pallas_call(
        functools.partial(_dn_kernel, seq=seq, lc=lc, solve_precision=solve_precision),
        grid=(batch, heads),
        in_specs=[blk(seq, 0), blk(seq, heads), blk(seq, 2 * heads), blk(seq, DN_OG), gate_blk(seq),
                  blk(lc, 0), blk(lc, heads), blk(lc, 2 * heads), blk(lc, DN_OG), gate_blk(lc),
                  vec, vec, vec],
        out_specs=[pl.BlockSpec((seq, hd), lambda b, h: (b, h)),
                   pl.BlockSpec((lc, hd), lambda b, h: (b, h))],
        out_shape=[jax.ShapeDtypeStruct((batch * seq, heads * hd), BF16),
                   jax.ShapeDtypeStruct((batch * lc, heads * hd), BF16)],
        scratch_shapes=[pltpu.VMEM((2, seq, hd), F32),
                        pltpu.VMEM((2, seq, hd), BF16),
                        pltpu.VMEM((2, seq, hd), BF16),
                        pltpu.VMEM((2, seq, hd), BF16),
                        pltpu.VMEM((2, seq, hd), BF16),
                        pltpu.VMEM((2, seq // CHUNK * 8, hd), F32),
                        pltpu.VMEM((seq, hd), F32)],
        compiler_params=_cparams(("arbitrary", "arbitrary"), DN_VMEM_LIMIT),
        name="deltanet",
    )(dq_lat, dq_lat, dq_lat, p_lat, pg_lat, dq_ctx, dq_ctx, dq_ctx, p_ctx, pg_ctx, gadd, gmul, norm_g)


def _ml_pass(q_ref, k_ref, v_ref, og_ref, gates_ref, gadd, norm_g, y_ref, length, h, state, acc_scr):
    n_chunks = length // CHUNK
    acc_scr[0:length, :] = jnp.zeros((length, HEAD_DIM), F32)
    shp = (CHUNK, CHUNK)
    ri = _iota(shp, 0)
    ci = _iota(shp, 1)
    masks = (ci <= ri, ci >= ri)
    head_lanes = (_iota((1, HEAD_DIM), 1) // ML_QK) == (h % 2)

    def step(s, st):
        new = []
        for d in range(2):
            c = s if d == 0 else n_chunks - 1 - s
            r0 = pl.multiple_of(c * CHUNK, CHUNK)
            incl = masks[d]
            c_st, n_st, m_st = st[d]
            q = jnp.where(head_lanes, q_ref[pl.ds(r0, CHUNK), :], 0.0)
            k = jnp.where(head_lanes, k_ref[pl.ds(r0, CHUNK), :], 0.0) * ML_QK ** -0.5
            v = v_ref[pl.ds(r0, CHUNK), :]
            gb = gates_ref[pl.ds(r0, CHUNK), :] + gadd
            ig_col = _lane_pick(gb, LANE_IG + 4 * d + h)
            lf_col = _log_sigmoid(_lane_pick(gb, LANE_FG + 4 * d + h))
            ig_row = _to_row(ig_col)
            lf_row = _to_row(lf_col)
            b_col = jnp.sum(jnp.where(incl, lf_row, 0.0), axis=-1, keepdims=True)
            b_row = _to_row(b_col)
            b_last = jnp.sum(lf_col, axis=0, keepdims=True)
            g_end = b_last - b_col + ig_col
            m_loc = jnp.max(g_end, axis=0, keepdims=True)
            kw = k * jnp.exp(g_end - m_loc)
            q16 = q.astype(BF16)
            v16 = v.astype(BF16)
            c_loc = _dot_tn(kw.astype(BF16), v16)
            n_loc = jnp.sum(kw, axis=0, keepdims=True)
            log_d = jnp.where(incl, b_col - b_row + ig_row, NEG_INF)
            m_inter = b_col + m_st
            m_t = jnp.maximum(jnp.max(log_d, axis=-1, keepdims=True), m_inter)
            sm = _dot_nt(q16, k.astype(BF16)) * jnp.exp(log_d - m_t)
            inter = jnp.exp(m_inter - m_t)
            num = _dot(sm.astype(BF16), v16) + inter * _dot(q16, c_st.astype(BF16))
            den = jnp.sum(sm, axis=-1, keepdims=True) + inter * jnp.sum(q * n_st, axis=-1, keepdims=True)
            acc_scr[pl.ds(r0, CHUNK), :] += num / jnp.maximum(jnp.abs(den), jnp.exp(-m_t))
            m_new = jnp.maximum(b_last + m_st, m_loc)
            a = jnp.exp(b_last + m_st - m_new)
            e = jnp.exp(m_loc - m_new)
            new.append((a * c_st + e * c_loc, a * n_st + e * n_loc, m_new))
        return tuple(new)

    state = lax.fori_loop(0, n_chunks, step, state, unroll=SCAN_UNROLL)
    y = jax.nn.sigmoid(og_ref[...]) * _rms(acc_scr[0:length, :], norm_g)
    y_ref[...] = y.astype(BF16)
    return state


def _ml_kernel(q_ref, k_ref, v_ref, og_ref, gates_ref, qc_ref, kc_ref, vc_ref, ogc_ref, gatesc_ref,
               gadd_ref, norm_ref, y_ref, yc_ref, acc_scr, *, seq, lc):
    h = pl.program_id(1)
    gadd = gadd_ref[...]
    norm_g = norm_ref[0]
    zero = (jnp.zeros((HEAD_DIM, HEAD_DIM), F32), jnp.zeros((1, HEAD_DIM), F32), jnp.zeros((1, 1), F32))
    state = _ml_pass(qc_ref, kc_ref, vc_ref, ogc_ref, gatesc_ref, gadd, norm_g, yc_ref, lc, h, (zero, zero), acc_scr)
    _ml_pass(q_ref, k_ref, v_ref, og_ref, gates_ref, gadd, norm_g, y_ref, seq, h, state, acc_scr)


def _ml_call(p_lat, p_ctx, pg_lat, pg_ctx, gadd, norm_g, batch, seq, lc):
    hd = HEAD_DIM
    heads = 4

    def specs(n):
        return [pl.BlockSpec((n, hd), lambda b, h: (b, ML_Q + h // 2)),
                pl.BlockSpec((n, hd), lambda b, h: (b, ML_K + h // 2)),
                pl.BlockSpec((n, hd), lambda b, h: (b, ML_V + h)),
                pl.BlockSpec((n, hd), lambda b, h: (b, ML_OG + h)),
                pl.BlockSpec((n, hd), lambda b, h: (b, 0))]

    return pl.pallas_call(
        functools.partial(_ml_kernel, seq=seq, lc=lc),
        grid=(batch, heads),
        in_specs=specs(seq) + specs(lc) + [pl.BlockSpec((1, hd), lambda b, h: (0, 0)),
                                            pl.BlockSpec((1, 1, hd), lambda b, h: (h, 0, 0))],
        out_specs=[pl.BlockSpec((seq, hd), lambda b, h: (b, h)),
                   pl.BlockSpec((lc, hd), lambda b, h: (b, h))],
        out_shape=[jax.ShapeDtypeStruct((batch * seq, heads * hd), BF16),
                   jax.ShapeDtypeStruct((batch * lc, heads * hd), BF16)],
        scratch_shapes=[pltpu.VMEM((seq, hd), F32)],
        compiler_params=_cparams(("arbitrary", "arbitrary")),
        name="mlstm",
    )(p_lat, p_lat, p_lat, p_lat, pg_lat, p_ctx, p_ctx, p_ctx, p_ctx, pg_ctx, gadd, norm_g.reshape(heads, 1, hd))


def _regroup_w_in(w):
    gw = 512
    na_in = 3 * gw
    dn_in = 4 * gw + 16
    ml_main = 512 + 2 * gw
    dn0 = na_in
    ml0 = dn0 + dn_in
    wa0 = ml0 + ml_main + 16
    main = jnp.concatenate([w[:, :dn0 + 4 * gw], w[:, ml0:ml0 + ml_main], w[:, wa0:]], axis=1)
    gates = jnp.concatenate([w[:, dn0 + 4 * gw:ml0], w[:, ml0 + ml_main:wa0],
                             jnp.zeros((w.shape[0], 128 - 32), w.dtype)], axis=1)
    return main.astype(BF16), gates.astype(BF16)


def _rope_tables(seq):
    t = jnp.arange(seq)
    n_freq = HEAD_DIM // 4
    inv_freq = ROPE_THETA ** (-jnp.arange(n_freq, dtype=F32) / n_freq)
    pos = jnp.stack([t // GRID_W, t % GRID_W], axis=-1).astype(F32)
    ang = pos[:, :, None] * inv_freq
    cos, sin = jnp.cos(ang), jnp.sin(ang)
    cos_t = jnp.concatenate([cos, cos], axis=-1).reshape(seq, HEAD_DIM)
    sin_t = jnp.concatenate([-sin, sin], axis=-1).reshape(seq, HEAD_DIM)
    return cos_t, sin_t


def _gate_vectors(dt_bias, a_log, i_bias, f_bias):
    zeros8 = jnp.zeros((8,), F32)
    gadd = jnp.concatenate([zeros8, dt_bias.reshape(-1), i_bias.reshape(-1), f_bias.reshape(-1),
                            jnp.zeros((96,), F32)]).astype(F32)
    gmul = jnp.concatenate([zeros8, jnp.exp(a_log.astype(F32)).reshape(-1), jnp.zeros((112,), F32)])
    return gadd.reshape(1, 128), gmul.reshape(1, 128)


def _forward(x, c, ctx, c_ctx, w_ada, b_ada, norm_mix, norm_ffn, w_in, w_out, na_qk_gain, na_rpb, dn_conv,
             dn_a_log, dn_dt_bias, dn_norm, ml_i_bias, ml_f_bias, ml_norm, wa_qk_gain, wa_sink, w_ffn_in,
             w_ffn_out, solve_precision):
    batch, seq, d = x.shape
    lc = ctx.shape[1]
    depth = w_ada.shape[0]
    rows = seq // GRID_W
    tm = 512
    tn_ffn = 512

    cos_t, sin_t = _rope_tables(seq)
    na_bias = _na_bias_tables(na_rpb, rows)
    cs = jnp.concatenate([c, c_ctx[None, :], jnp.zeros((8 - batch - 1, d), F32)], axis=0)
    mods = _ada_call(cs, w_ada, b_ada).reshape(depth, 8, 6, d)
    mods = jnp.pad(mods, ((0, 0), (0, 0), (0, 2), (0, 0)))

    xl = x.reshape(batch * seq, d)
    xc = ctx.reshape(batch * lc, d)
    hl = _norm_call(xl, mods[0, :batch], norm_mix[0:1], seq, tm)
    hc = _norm_call(xc, mods[0, batch:batch + 1], norm_mix[0:1], batch * lc, tm)

    for l in range(depth):
        need_ctx = l < depth - 1
        mod_l, mod_c = mods[l, :batch], mods[l, batch:batch + 1]
        w_main, w_gate = _regroup_w_in(w_in[l])
        p_lat = _matmul_call(hl, w_main, tm, 1024, "inproj")
        pg_lat = _matmul_call(hl, w_gate, tm, 128, "inproj_gates")
        p_ctx = _matmul_call(hc, w_main, tm, 1024, "inproj_ctx")
        pg_ctx = _matmul_call(hc, w_gate, tm, 128, "inproj_gates_ctx")

        ya = _na_call(p_lat, p_ctx, na_qk_gain[l], na_bias, l, batch, seq, lc)
        gadd, gmul = _gate_vectors(dn_dt_bias[l], dn_a_log[l], ml_i_bias[l], ml_f_bias[l])
        dq_lat = _dn_prep_call(p_lat, dn_conv[l], batch, seq, 256)
        dq_ctx = _dn_prep_call(p_ctx, dn_conv[l], batch, lc, 256)
        yb, ycb = _dn_call(dq_lat, dq_ctx, p_lat, p_ctx, pg_lat, pg_ctx, gadd, gmul, dn_norm[l].reshape(1, -1),
                           batch, seq, lc, solve_precision)
        ym, ycm = _ml_call(p_lat, p_ctx, pg_lat, pg_ctx, gadd, ml_norm[l], batch, seq, lc)
        yw = _wa_call(p_lat, p_ctx, wa_qk_gain[l], wa_sink[l], cos_t, sin_t, batch, seq, lc)

        w_o = w_out[l].astype(BF16)
        w_f1 = w_ffn_in[l].astype(BF16)
        w_f2 = w_ffn_out[l].astype(BF16)
        nl = min(l + 1, depth - 1)
        xl, h2 = _outproj_call((ya, yb, ym, yw), w_o, xl, mod_l, norm_ffn[l:l + 1], seq, tm)
        xl, hl = _ffn_call(h2, w_f1, w_f2, xl, mod_l, mods[nl, :batch], norm_mix[nl:nl + 1], seq, tm, tn_ffn,
                           need_ctx)
        if need_ctx:
            yca = _ctx_attn_call(p_ctx, na_qk_gain[l], wa_sink[l], batch, lc, 4, 1, NA_Q, NA_K, NA_V, False,
                                 "na_ctx_attn")
            ycw = _ctx_attn_call(p_ctx, wa_qk_gain[l], wa_sink[l], batch, lc, 4, 2, WA_Q, WA_K, WA_V, True,
                                 "wa_ctx_attn")
            xc, h2c = _outproj_call((yca, ycb, ycm, ycw), w_o, xc, mod_c, norm_ffn[l:l + 1], batch * lc, tm)
            xc, hc = _ffn_call(h2c, w_f1, w_f2, xc, mod_c, mods[nl, batch:batch + 1], norm_mix[nl:nl + 1],
                               batch * lc, tm, tn_ffn, True)
    return xl.reshape(batch, seq, d)


def kernel(x, c, ctx, c_ctx, w_ada, b_ada, norm_mix, norm_ffn, w_in, w_out, na_qk_gain, na_rpb, dn_conv, dn_a_log,
           dn_dt_bias, dn_norm, ml_i_bias, ml_f_bias, ml_norm, wa_qk_gain, wa_sink, w_ffn_in, w_ffn_out):
    return _forward(x, c, ctx, c_ctx, w_ada, b_ada, norm_mix, norm_ffn, w_in, w_out, na_qk_gain, na_rpb, dn_conv,
                    dn_a_log, dn_dt_bias, dn_norm, ml_i_bias, ml_f_bias, ml_norm, wa_qk_gain, wa_sink, w_ffn_in,
                    w_ffn_out, "bf16")
```

```python
import functools
import math

import jax
import jax.numpy as jnp
from jax import lax
from jax.experimental import pallas as pl
from jax.experimental.pallas import tpu as pltpu

F32 = jnp.float32
BF16 = jnp.bfloat16

HEAD_DIM = 128
GRID_W = 64
CHUNK = 64
PREP_GROUP = 4
ML_GROUP = 4
SCAN_UNROLL = 2
NA_WIN_ROWS = 8
NA_WIN_COLS = 16
NA_QROWS = 4
NA_KROWS = NA_QROWS + NA_WIN_ROWS
WA_BLOCK = 128
ML_QK = 64
ROPE_THETA = 10000.0
EPS = 1e-6
NEG_INF = float("-inf")
VMEM_LIMIT = 48 * 1024 * 1024
DN_VMEM_LIMIT = 56 * 1024 * 1024

NA_Q, NA_K, NA_V = 0, 4, 8
DN_QKV, DN_OG = 12, 24
ML_Q, ML_K, ML_V, ML_OG = 28, 30, 32, 36
WA_Q, WA_K, WA_V = 40, 44, 46
MAIN_COLS = 48 * 128
LANE_BETA, LANE_ALPHA, LANE_IG, LANE_FG = 0, 8, 16, 24


def _cparams(sem, vmem_limit=VMEM_LIMIT):
    return pltpu.CompilerParams(dimension_semantics=sem, vmem_limit_bytes=vmem_limit)


def _rms(x, gain):
    return x * lax.rsqrt(jnp.mean(x * x, axis=-1, keepdims=True) + EPS) * gain


def _silu(x):
    return x * jax.nn.sigmoid(x)


def _softplus(x):
    return jnp.maximum(x, 0.0) + jnp.log1p(jnp.exp(-jnp.abs(x)))


def _log_sigmoid(x):
    return jnp.minimum(x, 0.0) - jnp.log1p(jnp.exp(-jnp.abs(x)))


def _dot(a, b, precision=None):
    return jnp.dot(a, b, preferred_element_type=F32, precision=precision)


def _dot_nt(a, b):
    return lax.dot_general(a, b, (((1,), (1,)), ((), ())), preferred_element_type=F32)


def _dot_tn(a, b):
    return lax.dot_general(a, b, (((0,), (0,)), ((), ())), preferred_element_type=F32)


def _iota(shape, dim):
    return lax.broadcasted_iota(jnp.int32, shape, dim)


def _lane_pick(x, lane):
    return jnp.sum(jnp.where(_iota(x.shape, 1) == lane, x, 0.0), axis=-1, keepdims=True)


def _to_row(col):
    n = col.shape[0]
    eye = _iota((n, n), 0) == _iota((n, n), 1)
    return jnp.sum(jnp.where(eye, col, 0.0), axis=0, keepdims=True)


def _ada_kernel(c_ref, w_ref, b_ref, o_ref):
    a = _silu(c_ref[...]).astype(BF16)
    o_ref[0] = _dot(a, w_ref[0].astype(BF16)) + b_ref[0]


def _ada_call(cs, w_ada, b_ada):
    depth, d, n = w_ada.shape
    tn = 1024
    return pl.pallas_call(
        _ada_kernel,
        grid=(depth, n // tn),
        in_specs=[pl.BlockSpec((8, d), lambda l, j: (0, 0)),
                  pl.BlockSpec((1, d, tn), lambda l, j: (l, 0, j)),
                  pl.BlockSpec((1, 1, tn), lambda l, j: (l, 0, j))],
        out_specs=pl.BlockSpec((1, 8, tn), lambda l, j: (l, 0, j)),
        out_shape=jax.ShapeDtypeStruct((depth, 8, n), F32),
        compiler_params=_cparams(("arbitrary", "arbitrary")),
        name="ada",
    )(cs, w_ada, b_ada.reshape(depth, 1, n))


def _norm_kernel(x_ref, mod_ref, g_ref, h_ref):
    mod = mod_ref[0]
    h = _rms(x_ref[...], g_ref[...])
    h_ref[...] = (h * (1.0 + mod[1:2]) + mod[0:1]).astype(BF16)


def _norm_call(x, mod, gain, rows_per_mod, tm):
    m, d = x.shape
    per = rows_per_mod // tm
    return pl.pallas_call(
        _norm_kernel,
        grid=(m // tm,),
        in_specs=[pl.BlockSpec((tm, d), lambda i: (i, 0)),
                  pl.BlockSpec((1, 8, d), lambda i: (i // per, 0, 0)),
                  pl.BlockSpec((1, d), lambda i: (0, 0))],
        out_specs=pl.BlockSpec((tm, d), lambda i: (i, 0)),
        out_shape=jax.ShapeDtypeStruct((m, d), BF16),
        compiler_params=_cparams(("arbitrary",)),
        name="norm_mod",
    )(x, mod, gain)


def _matmul_kernel(a_ref, w_ref, o_ref):
    o_ref[...] = _dot(a_ref[...], w_ref[...])


def _matmul_call(a, w, tm, tn, name):
    m, k = a.shape
    n = w.shape[1]
    return pl.pallas_call(
        _matmul_kernel,
        grid=(n // tn, m // tm),
        in_specs=[pl.BlockSpec((tm, k), lambda j, i: (i, 0)),
                  pl.BlockSpec((k, tn), lambda j, i: (0, j))],
        out_specs=pl.BlockSpec((tm, tn), lambda j, i: (i, j)),
        out_shape=jax.ShapeDtypeStruct((m, n), F32),
        compiler_params=_cparams(("arbitrary", "arbitrary")),
        name=name,
    )(a, w)


def _outproj_kernel(ya_ref, yb_ref, ym_ref, yw_ref, w_ref, x_ref, mod_ref, g_ref, xo_ref, h_ref):
    gw = ya_ref.shape[1]
    acc = _dot(ya_ref[...], w_ref[0, 0:gw, :])
    acc += _dot(yb_ref[...], w_ref[0, gw:2 * gw, :])
    acc += _dot(ym_ref[...], w_ref[0, 2 * gw:3 * gw, :])
    acc += _dot(yw_ref[...], w_ref[0, 3 * gw:4 * gw, :])
    mod = mod_ref[0]
    xn = x_ref[...] + mod[2:3] * acc
    xo_ref[...] = xn
    h = _rms(xn, g_ref[...])
    h_ref[...] = (h * (1.0 + mod[4:5]) + mod[3:4]).astype(BF16)


def _outproj_call(ys, w, layer, x, mod, gain, rows_per_mod, tm):
    m, d = x.shape
    gw = ys[0].shape[1]
    per = rows_per_mod // tm
    yspec = pl.BlockSpec((tm, gw), lambda i: (i, 0))
    return pl.pallas_call(
        _outproj_kernel,
        grid=(m // tm,),
        in_specs=[yspec, yspec, yspec, yspec,
                  pl.BlockSpec((1, 4 * gw, d), lambda i: (layer, 0, 0)),
                  pl.BlockSpec((tm, d), lambda i: (i, 0)),
                  pl.BlockSpec((1, 8, d), lambda i: (i // per, 0, 0)),
                  pl.BlockSpec((1, d), lambda i: (0, 0))],
        out_specs=[pl.BlockSpec((tm, d), lambda i: (i, 0)),
                   pl.BlockSpec((tm, d), lambda i: (i, 0))],
        out_shape=[jax.ShapeDtypeStruct((m, d), F32), jax.ShapeDtypeStruct((m, d), BF16)],
        input_output_aliases={5: 0},
        compiler_params=_cparams(("arbitrary",)),
        name="outproj",
    )(*ys, w, x, mod, gain)


def _ffn_kernel(h_ref, wg_ref, wu_ref, w2_ref, x_ref, mod_ref, modn_ref, gn_ref, xo_ref, hn_ref, acc_ref,
                *, with_next):
    j = pl.program_id(1)

    @pl.when(j == 0)
    def _():
        acc_ref[...] = jnp.zeros_like(acc_ref)

    h = h_ref[...]
    act = _silu(_dot(h, wg_ref[0])) * _dot(h, wu_ref[0])
    acc_ref[...] += _dot(act.astype(BF16), w2_ref[0])

    @pl.when(j == pl.num_programs(1) - 1)
    def _():
        xn = x_ref[...] + mod_ref[0][5:6] * acc_ref[...]
        xo_ref[...] = xn
        if with_next:
            modn = modn_ref[0]
            hn = _rms(xn, gn_ref[...])
            hn_ref[...] = (hn * (1.0 + modn[1:2]) + modn[0:1]).astype(BF16)
        else:
            hn_ref[...] = jnp.zeros_like(hn_ref)


def _ffn_call(h, w_in, w_out, layer, x, mod, mod_next, gain_next, rows_per_mod, tm, tn, with_next):
    m, d = x.shape
    hidden = w_out.shape[1]
    nj = hidden // tn
    per = rows_per_mod // tm
    return pl.pallas_call(
        functools.partial(_ffn_kernel, with_next=with_next),
        grid=(m // tm, nj),
        in_specs=[pl.BlockSpec((tm, d), lambda i, j: (i, 0)),
                  pl.BlockSpec((1, d, tn), lambda i, j: (layer, 0, j)),
                  pl.BlockSpec((1, d, tn), lambda i, j: (layer, 0, nj + j)),
                  pl.BlockSpec((1, tn, d), lambda i, j: (layer, j, 0)),
                  pl.BlockSpec((tm, d), lambda i, j: (i, 0)),
                  pl.BlockSpec((1, 8, d), lambda i, j: (i // per, 0, 0)),
                  pl.BlockSpec((1, 8, d), lambda i, j: (i // per, 0, 0)),
                  pl.BlockSpec((1, d), lambda i, j: (0, 0))],
        out_specs=[pl.BlockSpec((tm, d), lambda i, j: (i, 0)),
                   pl.BlockSpec((tm, d), lambda i, j: (i, 0))],
        out_shape=[jax.ShapeDtypeStruct((m, d), F32), jax.ShapeDtypeStruct((m, d), BF16)],
        scratch_shapes=[pltpu.VMEM((tm, d), F32)],
        input_output_aliases={4: 0},
        compiler_params=_cparams(("arbitrary", "arbitrary")),
        name="ffn",
    )(h, w_in, w_in, w_out, x, mod, mod_next, gain_next)


def _na_kernel(q_ref, k_ref, v_ref, kc_ref, vc_ref, gain_ref, bias_ref, o_ref,
               kn_scr, v_scr, kcn_scr, vc_scr, *, rows):
    rb = pl.program_id(2)
    n_rb = pl.num_programs(2)

    @pl.when(rb == 0)
    def _():
        g1 = gain_ref[1:2, :]
        kn_scr[...] = _rms(k_ref[...], g1).astype(BF16)
        v_scr[...] = v_ref[...].astype(BF16)
        kcn_scr[...] = _rms(kc_ref[...], g1).astype(BF16)
        vc_scr[...] = vc_ref[...].astype(BF16)

    scale = HEAD_DIM ** -0.5
    nk = NA_KROWS * GRID_W
    qn = _rms(q_ref[...], gain_ref[0:1, :]).astype(BF16)
    ws = jnp.clip(NA_QROWS * rb - NA_WIN_ROWS // 2, 0, rows - NA_KROWS)
    start = pl.multiple_of(ws * GRID_W, GRID_W)
    kw = kn_scr[pl.ds(start, nk), :]
    vw = v_scr[pl.ds(start, nk), :]
    typ = jnp.where(rb == 0, 0, jnp.where(rb == n_rb - 1, 2, 1))
    bias = bias_ref[0, 0, pl.ds(typ, 1)][0]
    s_loc = _dot_nt(qn, kw) * scale + bias
    s_ctx = _dot_nt(qn, kcn_scr[...]) * scale
    m = jnp.maximum(jnp.max(s_loc, axis=-1, keepdims=True), jnp.max(s_ctx, axis=-1, keepdims=True))
    p_loc = jnp.exp(s_loc - m)
    p_ctx = jnp.exp(s_ctx - m)
    den = jnp.sum(p_loc, axis=-1, keepdims=True) + jnp.sum(p_ctx, axis=-1, keepdims=True)
    o = _dot(p_loc.astype(BF16), vw) + _dot(p_ctx.astype(BF16), vc_scr[...])
    o_ref[...] = (o / den).astype(BF16)


def _na_bias_tables(rpb, rows):
    n_rb = rows // NA_QROWS
    kh, kw = NA_WIN_ROWS, NA_WIN_COLS
    c = jnp.arange(GRID_W)
    cs = jnp.clip(c - kw // 2, 0, GRID_W - kw)
    in_col = (c[None, :] >= cs[:, None]) & (c[None, :] < cs[:, None] + kw)
    dc = jnp.clip(c[None, :] - c[:, None], 1 - kw, kw - 1) + kw - 1
    dc_hot = jax.nn.one_hot(dc, 2 * kw - 1, dtype=F32)
    tabs = []
    for rb in (0, 1, n_rb - 1):
        ws = min(max(NA_QROWS * rb - kh // 2, 0), rows - NA_KROWS)
        r = NA_QROWS * rb + jnp.arange(NA_QROWS)
        kr = ws + jnp.arange(NA_KROWS)
        rs = jnp.clip(r - kh // 2, 0, rows - kh)
        in_row = (kr[None, :] >= rs[:, None]) & (kr[None, :] < rs[:, None] + kh)
        dr = jnp.clip(kr[None, :] - r[:, None], 1 - kh, kh - 1) + kh - 1
        dr_hot = jax.nn.one_hot(dr, 2 * kh - 1, dtype=F32)
        val = jnp.einsum("ija,lhab,ckb->lhicjk", dr_hot, rpb.astype(F32), dc_hot,
                         precision=lax.Precision.HIGHEST)
        ok = in_row[:, None, :, None] & in_col[None, :, None, :]
        tab = jnp.where(ok[None, None], val, NEG_INF)
        tabs.append(tab.reshape(rpb.shape[0], rpb.shape[1], NA_QROWS * GRID_W, NA_KROWS * GRID_W))
    return jnp.stack(tabs, axis=2)


def _na_call(p_lat, p_ctx, gain, bias, layer, batch, seq, lc):
    rows = seq // GRID_W
    n_rb = rows // NA_QROWS
    tq = NA_QROWS * GRID_W
    nk = NA_KROWS * GRID_W
    heads = bias.shape[1]
    hd = HEAD_DIM
    return pl.pallas_call(
        functools.partial(_na_kernel, rows=rows),
        grid=(batch, heads, n_rb),
        in_specs=[pl.BlockSpec((tq, hd), lambda b, h, r: (b * n_rb + r, NA_Q + h)),
                  pl.BlockSpec((seq, hd), lambda b, h, r: (b, NA_K + h)),
                  pl.BlockSpec((seq, hd), lambda b, h, r: (b, NA_V + h)),
                  pl.BlockSpec((lc, hd), lambda b, h, r: (b, NA_K + h)),
                  pl.BlockSpec((lc, hd), lambda b, h, r: (b, NA_V + h)),
                  pl.BlockSpec((2, hd), lambda b, h, r: (0, 0)),
                  pl.BlockSpec((1, 1, 3, tq, nk), lambda b, h, r: (layer, h, 0, 0, 0))],
        out_specs=pl.BlockSpec((tq, hd), lambda b, h, r: (b * n_rb + r, h)),
        out_shape=jax.ShapeDtypeStruct((batch * seq, heads * hd), BF16),
        scratch_shapes=[pltpu.VMEM((seq, hd), BF16), pltpu.VMEM((seq, hd), BF16),
                        pltpu.VMEM((lc, hd), BF16), pltpu.VMEM((lc, hd), BF16)],
        compiler_params=_cparams(("arbitrary", "arbitrary", "arbitrary")),
        name="na_attn",
    )(p_lat, p_lat, p_lat, p_ctx, p_ctx, gain, bias)


def _ctx_attn_kernel(sink_ref, q_ref, k_ref, v_ref, gain_ref, o_ref, *, use_sink):
    scale = HEAD_DIM ** -0.5
    qn = _rms(q_ref[...], gain_ref[0:1, :]).astype(BF16)
    kn = _rms(k_ref[...], gain_ref[1:2, :]).astype(BF16)
    s = _dot_nt(qn, kn) * scale
    m = jnp.max(s, axis=-1, keepdims=True)
    if use_sink:
        sk = sink_ref[pl.program_id(1)]
        m = jnp.maximum(m, sk)
    p = jnp.exp(s - m)
    den = jnp.sum(p, axis=-1, keepdims=True)
    if use_sink:
        den = den + jnp.exp(sk - m)
    o = _dot(p.astype(BF16), v_ref[...].astype(BF16))
    o_ref[...] = (o / den).astype(BF16)


def _ctx_attn_call(p_ctx, gain, sink, batch, lc, heads, rep, qoff, koff, voff, use_sink, name):
    hd = HEAD_DIM
    return pl.pallas_call(
        functools.partial(_ctx_attn_kernel, use_sink=use_sink),
        grid=(batch, heads),
        in_specs=[pl.BlockSpec(memory_space=pltpu.SMEM),
                  pl.BlockSpec((lc, hd), lambda b, h: (b, qoff + h)),
                  pl.BlockSpec((lc, hd), lambda b, h: (b, koff + h // rep)),
                  pl.BlockSpec((lc, hd), lambda b, h: (b, voff + h // rep)),
                  pl.BlockSpec((2, hd), lambda b, h: (0, 0))],
        out_specs=pl.BlockSpec((lc, hd), lambda b, h: (b, h)),
        out_shape=jax.ShapeDtypeStruct((batch * lc, heads * hd), BF16),
        compiler_params=_cparams(("arbitrary", "arbitrary")),
        name=name,
    )(sink, p_ctx, p_ctx, p_ctx, gain)


def _rope(x, cos, sin_signed):
    lane = _iota(x.shape, 1)
    swapped = jnp.where(lane % 64 < 32, pltpu.roll(x, 96, axis=1), pltpu.roll(x, 32, axis=1))
    return x * cos + swapped * sin_signed


def _wa_kernel(sink_ref, q_ref, k_ref, v_ref, kc_ref, vc_ref, gain_ref, cq_ref, sq_ref, ck_ref, sk_ref, o_ref,
               kpad_scr, vpad_scr, kcn_scr, vc_scr, *, seq):
    g = pl.program_id(1)
    qb = pl.program_id(2)
    wb = WA_BLOCK
    hd = HEAD_DIM

    @pl.when(qb == 0)
    def _():
        g1 = gain_ref[1:2, :]
        zeros = jnp.zeros((wb, hd), BF16)
        kpad_scr[0:wb, :] = zeros
        kpad_scr[seq + wb:seq + 2 * wb, :] = zeros
        vpad_scr[0:wb, :] = zeros
        vpad_scr[seq + wb:seq + 2 * wb, :] = zeros
        kpad_scr[wb:seq + wb, :] = _rope(_rms(k_ref[...], g1), ck_ref[...], sk_ref[...]).astype(BF16)
        vpad_scr[wb:seq + wb, :] = v_ref[...].astype(BF16)
        kcn_scr[...] = _rms(kc_ref[...], g1).astype(BF16)
        vc_scr[...] = vc_ref[...].astype(BF16)

    scale = hd ** -0.5
    g0 = gain_ref[0:1, :]
    cq = cq_ref[...]
    sq = sq_ref[...]
    q2 = jnp.concatenate([_rope(_rms(q_ref[:, 0:hd], g0), cq, sq),
                          _rope(_rms(q_ref[:, hd:2 * hd], g0), cq, sq)], axis=0).astype(BF16)
    start = pl.multiple_of(qb * wb, wb)
    kw = kpad_scr[pl.ds(start, 3 * wb), :]
    vw = vpad_scr[pl.ds(start, 3 * wb), :]
    shp = (2 * wb, 3 * wb)
    i = _iota(shp, 0) % wb
    j = _iota(shp, 1)
    kpos = (qb - 1) * wb + j
    ok = (j - i >= 0) & (j - i <= 2 * wb) & (kpos >= 0) & (kpos < seq)
    s_loc = jnp.where(ok, _dot_nt(q2, kw) * scale, NEG_INF)
    s_ctx = _dot_nt(q2, kcn_scr[...]) * scale
    sink = jnp.where(_iota((2 * wb, 1), 0) < wb, sink_ref[2 * g], sink_ref[2 * g + 1])
    m = jnp.maximum(jnp.maximum(jnp.max(s_loc, axis=-1, keepdims=True), jnp.max(s_ctx, axis=-1, keepdims=True)),
                    sink)
    p_loc = jnp.exp(s_loc - m)
    p_ctx = jnp.exp(s_ctx - m)
    den = jnp.sum(p_loc, axis=-1, keepdims=True) + jnp.sum(p_ctx, axis=-1, keepdims=True) + jnp.exp(sink - m)
    o = (_dot(p_loc.astype(BF16), vw) + _dot(p_ctx.astype(BF16), vc_scr[...])) / den
    o_ref[:, 0:hd] = o[0:wb].astype(BF16)
    o_ref[:, hd:2 * hd] = o[wb:2 * wb].astype(BF16)


def _wa_call(p_lat, p_ctx, gain, sink, cos_t, sin_t, batch, seq, lc):
    wb = WA_BLOCK
    hd = HEAD_DIM
    nb = seq // wb
    kvh = 2
    return pl.pallas_call(
        functools.partial(_wa_kernel, seq=seq),
        grid=(batch, kvh, nb),
        in_specs=[pl.BlockSpec(memory_space=pltpu.SMEM),
                  pl.BlockSpec((wb, 2 * hd), lambda b, g, n: (b * nb + n, WA_Q // 2 + g)),
                  pl.BlockSpec((seq, hd), lambda b, g, n: (b, WA_K + g)),
                  pl.BlockSpec((seq, hd), lambda b, g, n: (b, WA_V + g)),
                  pl.BlockSpec((lc, hd), lambda b, g, n: (b, WA_K + g)),
                  pl.BlockSpec((lc, hd), lambda b, g, n: (b, WA_V + g)),
                  pl.BlockSpec((2, hd), lambda b, g, n: (0, 0)),
                  pl.BlockSpec((wb, hd), lambda b, g, n: (n, 0)),
                  pl.BlockSpec((wb, hd), lambda b, g, n: (n, 0)),
                  pl.BlockSpec((seq, hd), lambda b, g, n: (0, 0)),
                  pl.BlockSpec((seq, hd), lambda b, g, n: (0, 0))],
        out_specs=pl.BlockSpec((wb, 2 * hd), lambda b, g, n: (b * nb + n, g)),
        out_shape=jax.ShapeDtypeStruct((batch * seq, 2 * kvh * hd), BF16),
        scratch_shapes=[pltpu.VMEM((seq + 2 * wb, hd), BF16), pltpu.VMEM((seq + 2 * wb, hd), BF16),
                        pltpu.VMEM((lc, hd), BF16), pltpu.VMEM((lc, hd), BF16)],
        compiler_params=_cparams(("arbitrary", "arbitrary", "arbitrary")),
        name="wa_attn",
    )(sink, p_lat, p_lat, p_lat, p_ctx, p_ctx, gain, cos_t, sin_t, cos_t, sin_t)


def _dn_prep_kernel(x_ref, prev_ref, next_ref, w_ref, o_ref, *, n_tiles):
    t = pl.program_id(1)
    tl = x_ref.shape[0]
    x = x_ref[...]
    prev = jnp.where(t > 0, prev_ref[...], 0.0)
    nxt = jnp.where(t < n_tiles - 1, next_ref[...], 0.0)
    xe = jnp.concatenate([prev, x, nxt], axis=0)
    n = tl + 16
    taps = w_ref.shape[0]
    acc = jnp.zeros_like(x)
    for j in range(taps):
        off = 8 - taps // 2 + j
        shifted = xe if off % n == 0 else pltpu.roll(xe, n - off, axis=0)
        acc += shifted[0:tl] * w_ref[j:j + 1, :]
    y = _silu(acc)
    hd = HEAD_DIM
    nh = y.shape[1] // (3 * hd)
    for blk in range(3 * nh):
        yb = y[:, blk * hd:(blk + 1) * hd]
        if blk < 2 * nh:
            yb = yb * lax.rsqrt(jnp.sum(yb * yb, axis=-1, keepdims=True) + EPS)
            if blk < nh:
                yb = yb * hd ** -0.5
        o_ref[:, blk * hd:(blk + 1) * hd] = yb


def _dn_prep_call(p, conv_w, batch, length, tl):
    width = conv_w.shape[1]
    n_tiles = length // tl
    per8 = tl // 8
    last8 = batch * length // 8 - 1
    return pl.pallas_call(
        functools.partial(_dn_prep_kernel, n_tiles=n_tiles),
        grid=(batch, n_tiles),
        in_specs=[pl.BlockSpec((tl, width), lambda b, t: (b * n_tiles + t, 1)),
                  pl.BlockSpec((8, width), lambda b, t: (jnp.maximum((b * n_tiles + t) * per8 - 1, 0), 1)),
                  pl.BlockSpec((8, width), lambda b, t: (jnp.minimum((b * n_tiles + t + 1) * per8, last8), 1)),
                  pl.BlockSpec((conv_w.shape[0], width), lambda b, t: (0, 0))],
        out_specs=pl.BlockSpec((tl, width), lambda b, t: (b * n_tiles + t, 0)),
        out_shape=jax.ShapeDtypeStruct((batch * length, width), F32),
        compiler_params=_cparams(("arbitrary", "arbitrary")),
        name="dn_prep",
    )(p, p, p, conv_w)


def _split_bf16(a):
    hi = a.astype(BF16)
    return hi, (a - hi.astype(F32)).astype(BF16)


def _solve_dot(a, b, mode):
    if mode == "highest":
        return _dot(a, b, lax.Precision.HIGHEST)
    if mode == "bf16":
        return _dot(a.astype(BF16), b.astype(BF16))
    a_hi, a_lo = _split_bf16(a)
    b_hi, b_lo = _split_bf16(b)
    return _dot(a_hi, b_hi) + (_dot(a_hi, b_lo) + _dot(a_lo, b_hi))


def _unit_lower_inverse(lm, mode):
    n = lm.shape[0]
    eye = (_iota((n, n), 0) == _iota((n, n), 1)).astype(F32)
    x = eye - lm
    p = lm
    for _ in range(5):
        p = _solve_dot(p, p, mode)
        x = x + _solve_dot(x, p, mode)
    return x


def _dn_pass(q_ref, k_ref, v_ref, og_ref, gates_ref, gadd, gmul, norm_g, y_ref, length, h, state,
             u_scr, w_scr, qk_scr, qd_scr, ke_scr, egl_scr, acc_scr, solve_precision):
    pair = 2 * CHUNK
    n_pairs = length // pair
    n_chunks = length // CHUNK

    acc_scr[0:length, :] = jnp.zeros((length, HEAD_DIM), F32)

    shp = (pair, pair)
    ri = _iota(shp, 0)
    ci = _iota(shp, 1)
    same = (ri // CHUNK) == (ci // CHUNK)
    masks = ((same & (ci <= ri), same & (ci < ri)), (same & (ci >= ri), same & (ci > ri)))
    half = _iota((pair, 1), 0) < CHUNK

    group = min(n_pairs, PREP_GROUP)

    def prep(gi, carry):
        chains = []
        for g in range(group):
            pi = gi * group + g
            r0 = pl.multiple_of(pi * pair, pair)
            q = q_ref[pl.ds(r0, pair), :]
            k = k_ref[pl.ds(r0, pair), :]
            raw = gates_ref[pl.ds(r0, pair), :]
            gb = jnp.where(_iota(raw.shape, 1) < LANE_ALPHA, jax.nn.sigmoid(raw), -gmul * _softplus(raw + gadd))
            k16 = k.astype(BF16)
            kk = _dot_nt(k16, k16)
            qk = _dot_nt(q.astype(BF16), k16)
            for d in range(2):
                incl, strict = masks[d]
                beta_col = _lane_pick(gb, LANE_BETA + 4 * d + h)
                g_col = _lane_pick(gb, LANE_ALPHA + 4 * d + h)
                beta_row = _to_row(beta_col)
                g_row = _to_row(g_col)
                cum_col = jnp.sum(jnp.where(incl, g_row, 0.0), axis=-1, keepdims=True)
                cum_row = _to_row(cum_col)
                tot0 = jnp.sum(jnp.where(half, g_col, 0.0), axis=0, keepdims=True)
                tot1 = jnp.sum(jnp.where(half, 0.0, g_col), axis=0, keepdims=True)
                tot_col = jnp.where(half, tot0, tot1)
                decay = jnp.exp(jnp.where(incl, cum_col - cum_row, NEG_INF))
                lm = jnp.where(strict, kk * beta_col * decay, 0.0)
                qk_scr[d, pl.ds(r0, pair), :] = (qk * decay).astype(BF16)
                qd_scr[d, pl.ds(r0, pair), :] = (q * jnp.exp(cum_col)).astype(BF16)
                ke_scr[d, pl.ds(r0, pair), :] = (k * jnp.exp(tot_col - cum_col)).astype(BF16)
                egl = jnp.exp(jnp.where(_iota((16, 1), 0) < 8, tot0, tot1))
                egl_scr[d, pl.ds(pl.multiple_of(pi * 16, 16), 16), :] = jnp.broadcast_to(egl, (16, HEAD_DIM))
                chains.append((d, r0, lm, beta_row, beta_row * jnp.exp(cum_row)))
        eye = (ri == ci).astype(F32)
        ps = [c[2] for c in chains]
        xs = [eye - p for p in ps]
        for _ in range(5):
            ps = [_solve_dot(p, p, solve_precision) for p in ps]
            xs = [x + _solve_dot(x, p, solve_precision) for x, p in zip(xs, ps)]
        for (d, r0, _, beta_row, bexp_row), ainv in zip(chains, xs):
            u_scr[d, pl.ds(r0, pair), :] = _solve_dot(ainv * beta_row, v_ref[pl.ds(r0, pair), :], solve_precision)
            w_scr[d, pl.ds(r0, pair), :] = _solve_dot(ainv * bexp_row, k_ref[pl.ds(r0, pair), :],
                                                      solve_precision).astype(BF16)
        return carry

    lax.fori_loop(0, n_pairs // group, prep, 0)

    def scan(s, st):
        new = []
        for d in range(2):
            c = s if d == 0 else n_chunks - 1 - s
            r0 = pl.multiple_of(c * CHUNK, CHUNK)
            sm = st[d]
            s16 = sm.astype(BF16)
            v_new = u_scr[d, pl.ds(r0, CHUNK), :] - _dot(w_scr[d, pl.ds(r0, CHUNK), :], s16)
            vn16 = v_new.astype(BF16)
            vn_pair = jnp.concatenate([vn16, vn16], axis=0)
            o = _dot(qd_scr[d, pl.ds(r0, CHUNK), :], s16) + _dot(qk_scr[d, pl.ds(r0, CHUNK), :], vn_pair)
            acc_scr[pl.ds(r0, CHUNK), :] += o
            egl = egl_scr[d, pl.ds(pl.multiple_of(c * 8, 8), 8), :][0:1, :]
            new.append(sm * egl + _dot_tn(ke_scr[d, pl.ds(r0, CHUNK), :], vn16))
        return tuple(new)

    state = lax.fori_loop(0, n_chunks, scan, state, unroll=SCAN_UNROLL)

    y = _rms(acc_scr[0:length, :], norm_g) * _silu(og_ref[...])
    y_ref[...] = y.astype(BF16)
    return state


def _dn_kernel(q_ref, k_ref, v_ref, og_ref, gates_ref, qc_ref, kc_ref, vc_ref, ogc_ref, gatesc_ref,
               gadd_ref, gmul_ref, norm_ref, y_ref, yc_ref,
               u_scr, w_scr, qk_scr, qd_scr, ke_scr, egl_scr, acc_scr, *, seq, lc, solve_precision):
    h = pl.program_id(1)
    gadd = gadd_ref[...]
    gmul = gmul_ref[...]
    norm_g = norm_ref[...]
    scr = (u_scr, w_scr, qk_scr, qd_scr, ke_scr, egl_scr, acc_scr)
    zero = jnp.zeros((HEAD_DIM, HEAD_DIM), F32)
    state = _dn_pass(qc_ref, kc_ref, vc_ref, ogc_ref, gatesc_ref, gadd, gmul, norm_g, yc_ref, lc, h, (zero, zero),
                     *scr, solve_precision)
    _dn_pass(q_ref, k_ref, v_ref, og_ref, gates_ref, gadd, gmul, norm_g, y_ref, seq, h, state,
             *scr, solve_precision)


def _dn_call(dq_lat, dq_ctx, p_lat, p_ctx, pg_lat, pg_ctx, gadd, gmul, norm_g, batch, seq, lc, solve_precision):
    hd = HEAD_DIM
    heads = 4

    def blk(n, col):
        return pl.BlockSpec((n, hd), lambda b, h: (b, col + h))

    def gate_blk(n):
        return pl.BlockSpec((n, hd), lambda b, h: (b, 0))

    vec = pl.BlockSpec((1, hd), lambda b, h: (0, 0))
    return pl.pallas_call(
        functools.partial(_dn_kernel, seq=seq, lc=lc, solve_precision=solve_precision),
        grid=(batch, heads),
        in_specs=[blk(seq, 0), blk(seq, heads), blk(seq, 2 * heads), blk(seq, DN_OG), gate_blk(seq),
                  blk(lc, 0), blk(lc, heads), blk(lc, 2 * heads), blk(lc, DN_OG), gate_blk(lc),
                  vec, vec, vec],
        out_specs=[pl.BlockSpec((seq, hd), lambda b, h: (b, h)),
                   pl.BlockSpec((lc, hd), lambda b, h: (b, h))],
        out_shape=[jax.ShapeDtypeStruct((batch * seq, heads * hd), BF16),
                   jax.ShapeDtypeStruct((batch * lc, heads * hd), BF16)],
        scratch_shapes=[pltpu.VMEM((2, seq, hd), F32),
                        pltpu.VMEM((2, seq, hd), BF16),
                        pltpu.VMEM((2, seq, hd), BF16),
                        pltpu.VMEM((2, seq, hd), BF16),
                        pltpu.VMEM((2, seq, hd), BF16),
                        pltpu.VMEM((2, seq // CHUNK * 8, hd), F32),
                        pltpu.VMEM((seq, hd), F32)],
        compiler_params=_cparams(("arbitrary", "arbitrary"), DN_VMEM_LIMIT),
        name="deltanet",
    )(dq_lat, dq_lat, dq_lat, p_lat, pg_lat, dq_ctx, dq_ctx, dq_ctx, p_ctx, pg_ctx, gadd, gmul, norm_g)


def _ml_pass(q_ref, k_ref, v_ref, og_ref, gates_ref, gadd, norm_g, y_ref, length, h, state, acc_scr):
    n_chunks = length // CHUNK
    acc_scr[0:length, :] = jnp.zeros((length, HEAD_DIM), F32)
    shp = (CHUNK, CHUNK)
    ri = _iota(shp, 0)
    ci = _iota(shp, 1)
    masks = (ci <= ri, ci >= ri)
    head_lanes = (_iota((1, HEAD_DIM), 1) // ML_QK) == (h % 2)

    group = min(n_chunks, ML_GROUP)

    def each(f, *lists):
        return [f(*a) for a in zip(*lists)]

    def step(gi, st):
        dirs, r0s = [], []
        for j in range(group):
            s = gi * group + j
            for d in range(2):
                c = s if d == 0 else n_chunks - 1 - s
                dirs.append(d)
                r0s.append(pl.multiple_of(c * CHUNK, CHUNK))
        incl = [masks[d] for d in dirs]
        q = [jnp.where(head_lanes, q_ref[pl.ds(r0, CHUNK), :], 0.0) for r0 in r0s]
        k = [jnp.where(head_lanes, k_ref[pl.ds(r0, CHUNK), :], 0.0) * ML_QK ** -0.5 for r0 in r0s]
        v16 = [v_ref[pl.ds(r0, CHUNK), :].astype(BF16) for r0 in r0s]
        gb = [gates_ref[pl.ds(r0, CHUNK), :] + gadd for r0 in r0s]
        ig_col = [_lane_pick(g, LANE_IG + 4 * d + h) for g, d in zip(gb, dirs)]
        lf_col = [_log_sigmoid(_lane_pick(g, LANE_FG + 4 * d + h)) for g, d in zip(gb, dirs)]
        ig_row = each(_to_row, ig_col)
        lf_row = each(_to_row, lf_col)
        b_col = each(lambda m, r: jnp.sum(jnp.where(m, r, 0.0), axis=-1, keepdims=True), incl, lf_row)
        b_row = each(_to_row, b_col)
        b_last = each(lambda x: jnp.sum(x, axis=0, keepdims=True), lf_col)
        g_end = each(lambda bl, b, i: bl - b + i, b_last, b_col, ig_col)
        m_loc = each(lambda x: jnp.max(x, axis=0, keepdims=True), g_end)
        kw = each(lambda kk, g, m: kk * jnp.exp(g - m), k, g_end, m_loc)
        q16 = each(lambda x: x.astype(BF16), q)
        k16 = each(lambda x: x.astype(BF16), k)
        c_loc = each(lambda a, b: _dot_tn(a.astype(BF16), b), kw, v16)
        n_loc = each(lambda x: jnp.sum(x, axis=0, keepdims=True), kw)
        log_d = each(lambda m, bc, br, ir: jnp.where(m, bc - br + ir, NEG_INF), incl, b_col, b_row, ig_row)
        mx_col = each(lambda x: jnp.max(x, axis=-1, keepdims=True), log_d)
        s0 = each(lambda a, b, ld, mx: _dot_nt(a, b) * jnp.exp(ld - mx), q16, k16, log_d, mx_col)
        sv0 = each(lambda s, vv: _dot(s.astype(BF16), vv), s0, v16)
        rs0 = each(lambda s: jnp.sum(s, axis=-1, keepdims=True), s0)
        st = list(st)
        for i, (d, r0) in enumerate(zip(dirs, r0s)):
            c_st, n_st, m_st = st[d]
            m_inter = b_col[i] + m_st
            m_t = jnp.maximum(mx_col[i], m_inter)
            intra = jnp.exp(mx_col[i] - m_t)
            inter = jnp.exp(m_inter - m_t)
            num = intra * sv0[i] + inter * _dot(q16[i], c_st.astype(BF16))
            den = intra * rs0[i] + inter * jnp.sum(q[i] * n_st, axis=-1, keepdims=True)
            acc_scr[pl.ds(r0, CHUNK), :] += num / jnp.maximum(jnp.abs(den), jnp.exp(-m_t))
            m_new = jnp.maximum(b_last[i] + m_st, m_loc[i])
            a = jnp.exp(b_last[i] + m_st - m_new)
            e = jnp.exp(m_loc[i] - m_new)
            st[d] = (a * c_st + e * c_loc[i], a * n_st + e * n_loc[i], m_new)
        return tuple(st)

    state = lax.fori_loop(0, n_chunks // group, step, state)
    y = jax.nn.sigmoid(og_ref[...]) * _rms(acc_scr[0:length, :], norm_g)
    y_ref[...] = y.astype(BF16)
    return state


def _ml_kernel(q_ref, k_ref, v_ref, og_ref, gates_ref, qc_ref, kc_ref, vc_ref, ogc_ref, gatesc_ref,
               gadd_ref, norm_ref, y_ref, yc_ref, acc_scr, *, seq, lc):
    h = pl.program_id(1)
    gadd = gadd_ref[...]
    norm_g = norm_ref[0]
    zero = (jnp.zeros((HEAD_DIM, HEAD_DIM), F32), jnp.zeros((1, HEAD_DIM), F32), jnp.zeros((1, 1), F32))
    state = _ml_pass(qc_ref, kc_ref, vc_ref, ogc_ref, gatesc_ref, gadd, norm_g, yc_ref, lc, h, (zero, zero), acc_scr)
    _ml_pass(q_ref, k_ref, v_ref, og_ref, gates_ref, gadd, norm_g, y_ref, seq, h, state, acc_scr)


def _ml_call(p_lat, p_ctx, pg_lat, pg_ctx, gadd, norm_g, batch, seq, lc):
    hd = HEAD_DIM
    heads = 4

    def specs(n):
        return [pl.BlockSpec((n, hd), lambda b, h: (b, ML_Q + h // 2)),
                pl.BlockSpec((n, hd), lambda b, h: (b, ML_K + h // 2)),
                pl.BlockSpec((n, hd), lambda b, h: (b, ML_V + h)),
                pl.BlockSpec((n, hd), lambda b, h: (b, ML_OG + h)),
                pl.BlockSpec((n, hd), lambda b, h: (b, 0))]

    return pl.pallas_call(
        functools.partial(_ml_kernel, seq=seq, lc=lc),
        grid=(batch, heads),
        in_specs=specs(seq) + specs(lc) + [pl.BlockSpec((1, hd), lambda b, h: (0, 0)),
                                            pl.BlockSpec((1, 1, hd), lambda b, h: (h, 0, 0))],
        out_specs=[pl.BlockSpec((seq, hd), lambda b, h: (b, h)),
                   pl.BlockSpec((lc, hd), lambda b, h: (b, h))],
        out_shape=[jax.ShapeDtypeStruct((batch * seq, heads * hd), BF16),
                   jax.ShapeDtypeStruct((batch * lc, heads * hd), BF16)],
        scratch_shapes=[pltpu.VMEM((seq, hd), F32)],
        compiler_params=_cparams(("arbitrary", "arbitrary")),
        name="mlstm",
    )(p_lat, p_lat, p_lat, p_lat, pg_lat, p_ctx, p_ctx, p_ctx, p_ctx, pg_ctx, gadd, norm_g.reshape(heads, 1, hd))


def _regroup_w_in(w):
    gw = 512
    na_in = 3 * gw
    dn_in = 4 * gw + 16
    ml_main = 512 + 2 * gw
    dn0 = na_in
    ml0 = dn0 + dn_in
    wa0 = ml0 + ml_main + 16
    main = jnp.concatenate([w[:, :dn0 + 4 * gw], w[:, ml0:ml0 + ml_main], w[:, wa0:]], axis=1)
    gates = jnp.concatenate([w[:, dn0 + 4 * gw:ml0], w[:, ml0 + ml_main:wa0],
                             jnp.zeros((w.shape[0], 128 - 32), w.dtype)], axis=1)
    return main.astype(BF16), gates.astype(BF16)


def _rope_tables(seq):
    t = jnp.arange(seq)
    n_freq = HEAD_DIM // 4
    inv_freq = ROPE_THETA ** (-jnp.arange(n_freq, dtype=F32) / n_freq)
    pos = jnp.stack([t // GRID_W, t % GRID_W], axis=-1).astype(F32)
    ang = pos[:, :, None] * inv_freq
    cos, sin = jnp.cos(ang), jnp.sin(ang)
    cos_t = jnp.concatenate([cos, cos], axis=-1).reshape(seq, HEAD_DIM)
    sin_t = jnp.concatenate([-sin, sin], axis=-1).reshape(seq, HEAD_DIM)
    return cos_t, sin_t


def _gate_vectors(dt_bias, a_log, i_bias, f_bias):
    zeros8 = jnp.zeros((8,), F32)
    gadd = jnp.concatenate([zeros8, dt_bias.reshape(-1), i_bias.reshape(-1), f_bias.reshape(-1),
                            jnp.zeros((96,), F32)]).astype(F32)
    gmul = jnp.concatenate([zeros8, jnp.exp(a_log.astype(F32)).reshape(-1), jnp.zeros((112,), F32)])
    return gadd.reshape(1, 128), gmul.reshape(1, 128)


def _forward(x, c, ctx, c_ctx, w_ada, b_ada, norm_mix, norm_ffn, w_in, w_out, na_qk_gain, na_rpb, dn_conv,
             dn_a_log, dn_dt_bias, dn_norm, ml_i_bias, ml_f_bias, ml_norm, wa_qk_gain, wa_sink, w_ffn_in,
             w_ffn_out, solve_precision):
    batch, seq, d = x.shape
    lc = ctx.shape[1]
    depth = w_ada.shape[0]
    rows = seq // GRID_W
    tm = 512
    tn_ffn = 512

    cos_t, sin_t = _rope_tables(seq)
    na_bias = _na_bias_tables(na_rpb, rows)
    cs = jnp.concatenate([c, c_ctx[None, :], jnp.zeros((8 - batch - 1, d), F32)], axis=0)
    mods = _ada_call(cs, w_ada, b_ada).reshape(depth, 8, 6, d)
    mods = jnp.pad(mods, ((0, 0), (0, 0), (0, 2), (0, 0)))

    w_o = w_out.astype(BF16)
    w_f1 = w_ffn_in.astype(BF16)
    w_f2 = w_ffn_out.astype(BF16)

    xl = x.reshape(batch * seq, d)
    xc = ctx.reshape(batch * lc, d)
    hl = _norm_call(xl, mods[0, :batch], norm_mix[0:1], seq, tm)
    hc = _norm_call(xc, mods[0, batch:batch + 1], norm_mix[0:1], batch * lc, tm)

    for l in range(depth):
        need_ctx = l < depth - 1
        mod_l, mod_c = mods[l, :batch], mods[l, batch:batch + 1]
        w_main, w_gate = _regroup_w_in(w_in[l])
        p_lat = _matmul_call(hl, w_main, tm, 1024, "inproj")
        pg_lat = _matmul_call(hl, w_gate, tm, 128, "inproj_gates")
        p_ctx = _matmul_call(hc, w_main, tm, 1024, "inproj_ctx")
        pg_ctx = _matmul_call(hc, w_gate, tm, 128, "inproj_gates_ctx")

        ya = _na_call(p_lat, p_ctx, na_qk_gain[l], na_bias, l, batch, seq, lc)
        gadd, gmul = _gate_vectors(dn_dt_bias[l], dn_a_log[l], ml_i_bias[l], ml_f_bias[l])
        dq_lat = _dn_prep_call(p_lat, dn_conv[l], batch, seq, 256)
        dq_ctx = _dn_prep_call(p_ctx, dn_conv[l], batch, lc, 256)
        yb, ycb = _dn_call(dq_lat, dq_ctx, p_lat, p_ctx, pg_lat, pg_ctx, gadd, gmul, dn_norm[l].reshape(1, -1),
                           batch, seq, lc, solve_precision)
        ym, ycm = _ml_call(p_lat, p_ctx, pg_lat, pg_ctx, gadd, ml_norm[l], batch, seq, lc)
        yw = _wa_call(p_lat, p_ctx, wa_qk_gain[l], wa_sink[l], cos_t, sin_t, batch, seq, lc)

        nl = min(l + 1, depth - 1)
        xl, h2 = _outproj_call((ya, yb, ym, yw), w_o, l, xl, mod_l, norm_ffn[l:l + 1], seq, tm)
        xl, hl = _ffn_call(h2, w_f1, w_f2, l, xl, mod_l, mods[nl, :batch], norm_mix[nl:nl + 1], seq, tm, tn_ffn,
                           need_ctx)
        if need_ctx:
            yca = _ctx_attn_call(p_ctx, na_qk_gain[l], wa_sink[l], batch, lc, 4, 1, NA_Q, NA_K, NA_V, False,
                                 "na_ctx_attn")
            ycw = _ctx_attn_call(p_ctx, wa_qk_gain[l], wa_sink[l], batch, lc, 4, 2, WA_Q, WA_K, WA_V, True,
                                 "wa_ctx_attn")
            xc, h2c = _outproj_call((yca, ycb, ycm, ycw), w_o, l, xc, mod_c, norm_ffn[l:l + 1], batch * lc, tm)
            xc, hc = _ffn_call(h2c, w_f1, w_f2, l, xc, mod_c, mods[nl, batch:batch + 1], norm_mix[nl:nl + 1],
                               batch * lc, tm, tn_ffn, True)
    return xl.reshape(batch, seq, d)


def kernel(x, c, ctx, c_ctx, w_ada, b_ada, norm_mix, norm_ffn, w_in, w_out, na_qk_gain, na_rpb, dn_conv, dn_a_log,
           dn_dt_bias, dn_norm, ml_i_bias, ml_f_bias, ml_norm, wa_qk_gain, wa_sink, w_ffn_in, w_ffn_out):
    return _forward(x, c, ctx, c_ctx, w_ada, b_ada, norm_mix, norm_ffn, w_in, w_out, na_qk_gain, na_rpb, dn_conv,
                    dn_a_log, dn_dt_bias, dn_norm, ml_i_bias, ml_f_bias, ml_norm, wa_qk_gain, wa_sink, w_ffn_in,
                    w_ffn_out, "bf16")
```

```python
import functools
import math

import jax
import jax.numpy as jnp
from jax import lax
from jax.experimental import pallas as pl
from jax.experimental.pallas import tpu as pltpu

F32 = jnp.float32
BF16 = jnp.bfloat16

HEAD_DIM = 128
GRID_W = 64
CHUNK = 128
PREP_GROUP = 4
ML_GROUP = 4
SCAN_UNROLL = 2
NA_WIN_ROWS = 8
NA_WIN_COLS = 16
NA_QROWS = 4
NA_KROWS = NA_QROWS + NA_WIN_ROWS
WA_BLOCK = 128
ML_QK = 64
ROPE_THETA = 10000.0
EPS = 1e-6
NEG_INF = float("-inf")
VMEM_LIMIT = 48 * 1024 * 1024
DN_VMEM_LIMIT = 56 * 1024 * 1024

NA_Q, NA_K, NA_V = 0, 4, 8
DN_QKV, DN_OG = 12, 24
ML_Q, ML_K, ML_V, ML_OG = 28, 30, 32, 36
WA_Q, WA_K, WA_V = 40, 44, 46
MAIN_COLS = 48 * 128
LANE_BETA, LANE_ALPHA, LANE_IG, LANE_FG = 0, 8, 16, 24


def _cparams(sem, vmem_limit=VMEM_LIMIT):
    return pltpu.CompilerParams(dimension_semantics=sem, vmem_limit_bytes=vmem_limit)


def _rms(x, gain):
    return x * lax.rsqrt(jnp.mean(x * x, axis=-1, keepdims=True) + EPS) * gain


def _silu(x):
    return x * jax.nn.sigmoid(x)


def _softplus(x):
    return jnp.maximum(x, 0.0) + jnp.log1p(jnp.exp(-jnp.abs(x)))


def _log_sigmoid(x):
    return jnp.minimum(x, 0.0) - jnp.log1p(jnp.exp(-jnp.abs(x)))


def _dot(a, b, precision=None):
    return jnp.dot(a, b, preferred_element_type=F32, precision=precision)


def _dot_nt(a, b):
    return lax.dot_general(a, b, (((1,), (1,)), ((), ())), preferred_element_type=F32)


def _dot_tn(a, b):
    return lax.dot_general(a, b, (((0,), (0,)), ((), ())), preferred_element_type=F32)


def _iota(shape, dim):
    return lax.broadcasted_iota(jnp.int32, shape, dim)


def _lane_pick(x, lane):
    return jnp.sum(jnp.where(_iota(x.shape, 1) == lane, x, 0.0), axis=-1, keepdims=True)


def _to_row(col):
    n = col.shape[0]
    eye = _iota((n, n), 0) == _iota((n, n), 1)
    return jnp.sum(jnp.where(eye, col, 0.0), axis=0, keepdims=True)


def _ada_kernel(c_ref, w_ref, b_ref, o_ref):
    a = _silu(c_ref[...]).astype(BF16)
    o_ref[0] = _dot(a, w_ref[0].astype(BF16)) + b_ref[0]


def _ada_call(cs, w_ada, b_ada):
    depth, d, n = w_ada.shape
    tn = 1024
    return pl.pallas_call(
        _ada_kernel,
        grid=(depth, n // tn),
        in_specs=[pl.BlockSpec((8, d), lambda l, j: (0, 0)),
                  pl.BlockSpec((1, d, tn), lambda l, j: (l, 0, j)),
                  pl.BlockSpec((1, 1, tn), lambda l, j: (l, 0, j))],
        out_specs=pl.BlockSpec((1, 8, tn), lambda l, j: (l, 0, j)),
        out_shape=jax.ShapeDtypeStruct((depth, 8, n), F32),
        compiler_params=_cparams(("arbitrary", "arbitrary")),
        name="ada",
    )(cs, w_ada, b_ada.reshape(depth, 1, n))


def _norm_kernel(x_ref, mod_ref, g_ref, h_ref):
    mod = mod_ref[0]
    h = _rms(x_ref[...], g_ref[...])
    h_ref[...] = (h * (1.0 + mod[1:2]) + mod[0:1]).astype(BF16)


def _norm_call(x, mod, gain, rows_per_mod, tm):
    m, d = x.shape
    per = rows_per_mod // tm
    return pl.pallas_call(
        _norm_kernel,
        grid=(m // tm,),
        in_specs=[pl.BlockSpec((tm, d), lambda i: (i, 0)),
                  pl.BlockSpec((1, 8, d), lambda i: (i // per, 0, 0)),
                  pl.BlockSpec((1, d), lambda i: (0, 0))],
        out_specs=pl.BlockSpec((tm, d), lambda i: (i, 0)),
        out_shape=jax.ShapeDtypeStruct((m, d), BF16),
        compiler_params=_cparams(("arbitrary",)),
        name="norm_mod",
    )(x, mod, gain)


def _matmul_kernel(a_ref, w_ref, o_ref):
    o_ref[...] = _dot(a_ref[...], w_ref[0])


def _matmul_call(a, w, layer, tm, tn, name):
    m, k = a.shape
    n = w.shape[2]
    return pl.pallas_call(
        _matmul_kernel,
        grid=(n // tn, m // tm),
        in_specs=[pl.BlockSpec((tm, k), lambda j, i: (i, 0)),
                  pl.BlockSpec((1, k, tn), lambda j, i: (layer, 0, j))],
        out_specs=pl.BlockSpec((tm, tn), lambda j, i: (i, j)),
        out_shape=jax.ShapeDtypeStruct((m, n), F32),
        compiler_params=_cparams(("arbitrary", "arbitrary")),
        name=name,
    )(a, w)


def _outproj_kernel(ya_ref, yb_ref, ym_ref, yw_ref, w_ref, x_ref, mod_ref, g_ref, xo_ref, h_ref):
    gw = ya_ref.shape[1]
    acc = _dot(ya_ref[...], w_ref[0, 0:gw, :])
    acc += _dot(yb_ref[...], w_ref[0, gw:2 * gw, :])
    acc += _dot(ym_ref[...], w_ref[0, 2 * gw:3 * gw, :])
    acc += _dot(yw_ref[...], w_ref[0, 3 * gw:4 * gw, :])
    mod = mod_ref[0]
    xn = x_ref[...] + mod[2:3] * acc
    xo_ref[...] = xn
    h = _rms(xn, g_ref[...])
    h_ref[...] = (h * (1.0 + mod[4:5]) + mod[3:4]).astype(BF16)


def _outproj_call(ys, w, layer, x, mod, gain, rows_per_mod, tm):
    m, d = x.shape
    gw = ys[0].shape[1]
    per = rows_per_mod // tm
    yspec = pl.BlockSpec((tm, gw), lambda i: (i, 0))
    return pl.pallas_call(
        _outproj_kernel,
        grid=(m // tm,),
        in_specs=[yspec, yspec, yspec, yspec,
                  pl.BlockSpec((1, 4 * gw, d), lambda i: (layer, 0, 0)),
                  pl.BlockSpec((tm, d), lambda i: (i, 0)),
                  pl.BlockSpec((1, 8, d), lambda i: (i // per, 0, 0)),
                  pl.BlockSpec((1, d), lambda i: (0, 0))],
        out_specs=[pl.BlockSpec((tm, d), lambda i: (i, 0)),
                   pl.BlockSpec((tm, d), lambda i: (i, 0))],
        out_shape=[jax.ShapeDtypeStruct((m, d), F32), jax.ShapeDtypeStruct((m, d), BF16)],
        input_output_aliases={5: 0},
        compiler_params=_cparams(("arbitrary",)),
        name="outproj",
    )(*ys, w, x, mod, gain)


def _ffn_kernel(h_ref, wg_ref, wu_ref, w2_ref, x_ref, mod_ref, modn_ref, gn_ref, xo_ref, hn_ref, acc_ref,
                *, with_next):
    j = pl.program_id(1)

    @pl.when(j == 0)
    def _():
        acc_ref[...] = jnp.zeros_like(acc_ref)

    h = h_ref[...]
    act = _silu(_dot(h, wg_ref[0])) * _dot(h, wu_ref[0])
    acc_ref[...] += _dot(act.astype(BF16), w2_ref[0])

    @pl.when(j == pl.num_programs(1) - 1)
    def _():
        xn = x_ref[...] + mod_ref[0][5:6] * acc_ref[...]
        xo_ref[...] = xn
        if with_next:
            modn = modn_ref[0]
            hn = _rms(xn, gn_ref[...])
            hn_ref[...] = (hn * (1.0 + modn[1:2]) + modn[0:1]).astype(BF16)
        else:
            hn_ref[...] = jnp.zeros_like(hn_ref)


def _ffn_call(h, w_in, w_out, layer, x, mod, mod_next, gain_next, rows_per_mod, tm, tn, with_next):
    m, d = x.shape
    hidden = w_out.shape[1]
    nj = hidden // tn
    per = rows_per_mod // tm
    return pl.pallas_call(
        functools.partial(_ffn_kernel, with_next=with_next),
        grid=(m // tm, nj),
        in_specs=[pl.BlockSpec((tm, d), lambda i, j: (i, 0)),
                  pl.BlockSpec((1, d, tn), lambda i, j: (layer, 0, j)),
                  pl.BlockSpec((1, d, tn), lambda i, j: (layer, 0, nj + j)),
                  pl.BlockSpec((1, tn, d), lambda i, j: (layer, j, 0)),
                  pl.BlockSpec((tm, d), lambda i, j: (i, 0)),
                  pl.BlockSpec((1, 8, d), lambda i, j: (i // per, 0, 0)),
                  pl.BlockSpec((1, 8, d), lambda i, j: (i // per, 0, 0)),
                  pl.BlockSpec((1, d), lambda i, j: (0, 0))],
        out_specs=[pl.BlockSpec((tm, d), lambda i, j: (i, 0)),
                   pl.BlockSpec((tm, d), lambda i, j: (i, 0))],
        out_shape=[jax.ShapeDtypeStruct((m, d), F32), jax.ShapeDtypeStruct((m, d), BF16)],
        scratch_shapes=[pltpu.VMEM((tm, d), F32)],
        input_output_aliases={4: 0},
        compiler_params=_cparams(("arbitrary", "arbitrary")),
        name="ffn",
    )(h, w_in, w_in, w_out, x, mod, mod_next, gain_next)


def _na_kernel(q_ref, k_ref, v_ref, kc_ref, vc_ref, gain_ref, colbias_ref, o_ref,
               kn_scr, v_scr, kcn_scr, vc_scr, bias_scr, *, rows):
    rb = pl.program_id(2)
    n_rb = pl.num_programs(2)

    @pl.when(rb == 0)
    def _():
        g1 = gain_ref[1:2, :]
        kn_scr[...] = _rms(k_ref[...], g1).astype(BF16)
        v_scr[...] = v_ref[...].astype(BF16)
        kcn_scr[...] = _rms(kc_ref[...], g1).astype(BF16)
        vc_scr[...] = vc_ref[...].astype(BF16)
        closed = jnp.full((GRID_W, GRID_W), NEG_INF, F32)
        for typ, (offs, inside) in enumerate(_na_row_plan(rows)):
            for i in range(NA_QROWS):
                for jp in range(NA_KROWS // 2):
                    halves = [colbias_ref[0, 0, offs[i][j]] if inside[i][j] else closed for j in (2 * jp, 2 * jp + 1)]
                    bias_scr[typ, i * GRID_W:(i + 1) * GRID_W, 2 * jp * GRID_W:(2 * jp + 2) * GRID_W] = (
                        jnp.concatenate(halves, axis=1))

    scale = HEAD_DIM ** -0.5
    nk = NA_KROWS * GRID_W
    qn = _rms(q_ref[...], gain_ref[0:1, :]).astype(BF16)
    ws = jnp.clip(NA_QROWS * rb - NA_WIN_ROWS // 2, 0, rows - NA_KROWS)
    start = pl.multiple_of(ws * GRID_W, GRID_W)
    kw = kn_scr[pl.ds(start, nk), :]
    vw = v_scr[pl.ds(start, nk), :]
    typ = jnp.where(rb == 0, 0, jnp.where(rb == n_rb - 1, 2, 1))
    bias = bias_scr[pl.ds(typ, 1)][0]
    s_loc = _dot_nt(qn, kw) * scale + bias
    s_ctx = _dot_nt(qn, kcn_scr[...]) * scale
    m = jnp.maximum(jnp.max(s_loc, axis=-1, keepdims=True), jnp.max(s_ctx, axis=-1, keepdims=True))
    p_loc = jnp.exp(s_loc - m)
    p_ctx = jnp.exp(s_ctx - m)
    den = jnp.sum(p_loc, axis=-1, keepdims=True) + jnp.sum(p_ctx, axis=-1, keepdims=True)
    o = _dot(p_loc.astype(BF16), vw) + _dot(p_ctx.astype(BF16), vc_scr[...])
    o_ref[...] = (o / den).astype(BF16)


def _na_row_plan(rows):
    n_rb = rows // NA_QROWS
    kh = NA_WIN_ROWS
    plan = []
    for rb in (0, 1, n_rb - 1):
        ws = min(max(NA_QROWS * rb - kh // 2, 0), rows - NA_KROWS)
        offs, inside = [], []
        for i in range(NA_QROWS):
            r = NA_QROWS * rb + i
            rs = min(max(r - kh // 2, 0), rows - kh)
            offs.append([min(max(ws + j - r, 1 - kh), kh - 1) + kh - 1 for j in range(NA_KROWS)])
            inside.append([rs <= ws + j < rs + kh for j in range(NA_KROWS)])
        plan.append((offs, inside))
    return plan


def _na_col_tables(rpb):
    kw = NA_WIN_COLS
    c = jnp.arange(GRID_W)
    cs = jnp.clip(c - kw // 2, 0, GRID_W - kw)
    in_col = (c[None, :] >= cs[:, None]) & (c[None, :] < cs[:, None] + kw)
    dc = jnp.clip(c[None, :] - c[:, None], 1 - kw, kw - 1) + kw - 1
    dc_hot = jax.nn.one_hot(dc, 2 * kw - 1, dtype=F32)
    val = jnp.einsum("lhab,ckb->lhack", rpb.astype(F32), dc_hot, precision=lax.Precision.HIGHEST)
    return jnp.where(in_col, val, NEG_INF)


def _na_call(p_lat, p_ctx, gain, colbias, layer, batch, seq, lc):
    rows = seq // GRID_W
    n_rb = rows // NA_QROWS
    tq = NA_QROWS * GRID_W
    nk = NA_KROWS * GRID_W
    heads = colbias.shape[1]
    n_off = colbias.shape[2]
    hd = HEAD_DIM
    return pl.pallas_call(
        functools.partial(_na_kernel, rows=rows),
        grid=(batch, heads, n_rb),
        in_specs=[pl.BlockSpec((tq, hd), lambda b, h, r: (b * n_rb + r, NA_Q + h)),
                  pl.BlockSpec((seq, hd), lambda b, h, r: (b, NA_K + h)),
                  pl.BlockSpec((seq, hd), lambda b, h, r: (b, NA_V + h)),
                  pl.BlockSpec((lc, hd), lambda b, h, r: (b, NA_K + h)),
                  pl.BlockSpec((lc, hd), lambda b, h, r: (b, NA_V + h)),
                  pl.BlockSpec((2, hd), lambda b, h, r: (0, 0)),
                  pl.BlockSpec((1, 1, n_off, GRID_W, GRID_W), lambda b, h, r: (layer, h, 0, 0, 0))],
        out_specs=pl.BlockSpec((tq, hd), lambda b, h, r: (b * n_rb + r, h)),
        out_shape=jax.ShapeDtypeStruct((batch * seq, heads * hd), BF16),
        scratch_shapes=[pltpu.VMEM((seq, hd), BF16), pltpu.VMEM((seq, hd), BF16),
                        pltpu.VMEM((lc, hd), BF16), pltpu.VMEM((lc, hd), BF16),
                        pltpu.VMEM((3, tq, nk), F32)],
        compiler_params=_cparams(("arbitrary", "arbitrary", "arbitrary")),
        name="na_attn",
    )(p_lat, p_lat, p_lat, p_ctx, p_ctx, gain, colbias)


def _ctx_attn_kernel(sink_ref, q_ref, k_ref, v_ref, gain_ref, o_ref, *, use_sink):
    scale = HEAD_DIM ** -0.5
    qn = _rms(q_ref[...], gain_ref[0:1, :]).astype(BF16)
    kn = _rms(k_ref[...], gain_ref[1:2, :]).astype(BF16)
    s = _dot_nt(qn, kn) * scale
    m = jnp.max(s, axis=-1, keepdims=True)
    if use_sink:
        sk = sink_ref[pl.program_id(1)]
        m = jnp.maximum(m, sk)
    p = jnp.exp(s - m)
    den = jnp.sum(p, axis=-1, keepdims=True)
    if use_sink:
        den = den + jnp.exp(sk - m)
    o = _dot(p.astype(BF16), v_ref[...].astype(BF16))
    o_ref[...] = (o / den).astype(BF16)


def _ctx_attn_call(p_ctx, gain, sink, batch, lc, heads, rep, qoff, koff, voff, use_sink, name):
    hd = HEAD_DIM
    return pl.pallas_call(
        functools.partial(_ctx_attn_kernel, use_sink=use_sink),
        grid=(batch, heads),
        in_specs=[pl.BlockSpec(memory_space=pltpu.SMEM),
                  pl.BlockSpec((lc, hd), lambda b, h: (b, qoff + h)),
                  pl.BlockSpec((lc, hd), lambda b, h: (b, koff + h // rep)),
                  pl.BlockSpec((lc, hd), lambda b, h: (b, voff + h // rep)),
                  pl.BlockSpec((2, hd), lambda b, h: (0, 0))],
        out_specs=pl.BlockSpec((lc, hd), lambda b, h: (b, h)),
        out_shape=jax.ShapeDtypeStruct((batch * lc, heads * hd), BF16),
        compiler_params=_cparams(("arbitrary", "arbitrary")),
        name=name,
    )(sink, p_ctx, p_ctx, p_ctx, gain)


def _rope(x, cos, sin_signed):
    lane = _iota(x.shape, 1)
    swapped = jnp.where(lane % 64 < 32, pltpu.roll(x, 96, axis=1), pltpu.roll(x, 32, axis=1))
    return x * cos + swapped * sin_signed


def _wa_kernel(sink_ref, q_ref, k_ref, v_ref, kc_ref, vc_ref, gain_ref, cq_ref, sq_ref, ck_ref, sk_ref, o_ref,
               kpad_scr, vpad_scr, kcn_scr, vc_scr, *, seq):
    g = pl.program_id(1)
    qb = pl.program_id(2)
    wb = WA_BLOCK
    hd = HEAD_DIM

    @pl.when(qb == 0)
    def _():
        g1 = gain_ref[1:2, :]
        zeros = jnp.zeros((wb, hd), BF16)
        kpad_scr[0:wb, :] = zeros
        kpad_scr[seq + wb:seq + 2 * wb, :] = zeros
        vpad_scr[0:wb, :] = zeros
        vpad_scr[seq + wb:seq + 2 * wb, :] = zeros
        kpad_scr[wb:seq + wb, :] = _rope(_rms(k_ref[...], g1), ck_ref[...], sk_ref[...]).astype(BF16)
        vpad_scr[wb:seq + wb, :] = v_ref[...].astype(BF16)
        kcn_scr[...] = _rms(kc_ref[...], g1).astype(BF16)
        vc_scr[...] = vc_ref[...].astype(BF16)

    scale = hd ** -0.5
    g0 = gain_ref[0:1, :]
    cq = cq_ref[...]
    sq = sq_ref[...]
    q2 = jnp.concatenate([_rope(_rms(q_ref[:, 0:hd], g0), cq, sq),
                          _rope(_rms(q_ref[:, hd:2 * hd], g0), cq, sq)], axis=0).astype(BF16)
    start = pl.multiple_of(qb * wb, wb)
    kw = kpad_scr[pl.ds(start, 3 * wb), :]
    vw = vpad_scr[pl.ds(start, 3 * wb), :]
    shp = (2 * wb, 3 * wb)
    i = _iota(shp, 0) % wb
    j = _iota(shp, 1)
    kpos = (qb - 1) * wb + j
    ok = (j - i >= 0) & (j - i <= 2 * wb) & (kpos >= 0) & (kpos < seq)
    s_loc = jnp.where(ok, _dot_nt(q2, kw) * scale, NEG_INF)
    s_ctx = _dot_nt(q2, kcn_scr[...]) * scale
    sink = jnp.where(_iota((2 * wb, 1), 0) < wb, sink_ref[2 * g], sink_ref[2 * g + 1])
    m = jnp.maximum(jnp.maximum(jnp.max(s_loc, axis=-1, keepdims=True), jnp.max(s_ctx, axis=-1, keepdims=True)),
                    sink)
    p_loc = jnp.exp(s_loc - m)
    p_ctx = jnp.exp(s_ctx - m)
    den = jnp.sum(p_loc, axis=-1, keepdims=True) + jnp.sum(p_ctx, axis=-1, keepdims=True) + jnp.exp(sink - m)
    o = (_dot(p_loc.astype(BF16), vw) + _dot(p_ctx.astype(BF16), vc_scr[...])) / den
    o_ref[:, 0:hd] = o[0:wb].astype(BF16)
    o_ref[:, hd:2 * hd] = o[wb:2 * wb].astype(BF16)


def _wa_call(p_lat, p_ctx, gain, sink, cos_t, sin_t, batch, seq, lc):
    wb = WA_BLOCK
    hd = HEAD_DIM
    nb = seq // wb
    kvh = 2
    return pl.pallas_call(
        functools.partial(_wa_kernel, seq=seq),
        grid=(batch, kvh, nb),
        in_specs=[pl.BlockSpec(memory_space=pltpu.SMEM),
                  pl.BlockSpec((wb, 2 * hd), lambda b, g, n: (b * nb + n, WA_Q // 2 + g)),
                  pl.BlockSpec((seq, hd), lambda b, g, n: (b, WA_K + g)),
                  pl.BlockSpec((seq, hd), lambda b, g, n: (b, WA_V + g)),
                  pl.BlockSpec((lc, hd), lambda b, g, n: (b, WA_K + g)),
                  pl.BlockSpec((lc, hd), lambda b, g, n: (b, WA_V + g)),
                  pl.BlockSpec((2, hd), lambda b, g, n: (0, 0)),
                  pl.BlockSpec((wb, hd), lambda b, g, n: (n, 0)),
                  pl.BlockSpec((wb, hd), lambda b, g, n: (n, 0)),
                  pl.BlockSpec((seq, hd), lambda b, g, n: (0, 0)),
                  pl.BlockSpec((seq, hd), lambda b, g, n: (0, 0))],
        out_specs=pl.BlockSpec((wb, 2 * hd), lambda b, g, n: (b * nb + n, g)),
        out_shape=jax.ShapeDtypeStruct((batch * seq, 2 * kvh * hd), BF16),
        scratch_shapes=[pltpu.VMEM((seq + 2 * wb, hd), BF16), pltpu.VMEM((seq + 2 * wb, hd), BF16),
                        pltpu.VMEM((lc, hd), BF16), pltpu.VMEM((lc, hd), BF16)],
        compiler_params=_cparams(("arbitrary", "arbitrary", "arbitrary")),
        name="wa_attn",
    )(sink, p_lat, p_lat, p_lat, p_ctx, p_ctx, gain, cos_t, sin_t, cos_t, sin_t)


def _dn_prep_kernel(x_ref, prev_ref, next_ref, w_ref, o_ref, *, n_tiles):
    t = pl.program_id(1)
    tl = x_ref.shape[0]
    x = x_ref[...]
    prev = jnp.where(t > 0, prev_ref[...], 0.0)
    nxt = jnp.where(t < n_tiles - 1, next_ref[...], 0.0)
    xe = jnp.concatenate([prev, x, nxt], axis=0)
    n = tl + 16
    taps = w_ref.shape[0]
    acc = jnp.zeros_like(x)
    for j in range(taps):
        off = 8 - taps // 2 + j
        shifted = xe if off % n == 0 else pltpu.roll(xe, n - off, axis=0)
        acc += shifted[0:tl] * w_ref[j:j + 1, :]
    y = _silu(acc)
    hd = HEAD_DIM
    nh = y.shape[1] // (3 * hd)
    for blk in range(3 * nh):
        yb = y[:, blk * hd:(blk + 1) * hd]
        if blk < 2 * nh:
            yb = yb * lax.rsqrt(jnp.sum(yb * yb, axis=-1, keepdims=True) + EPS)
            if blk < nh:
                yb = yb * hd ** -0.5
        o_ref[:, blk * hd:(blk + 1) * hd] = yb


def _dn_prep_call(p, conv_w, batch, length, tl):
    width = conv_w.shape[1]
    n_tiles = length // tl
    per8 = tl // 8
    last8 = batch * length // 8 - 1
    return pl.pallas_call(
        functools.partial(_dn_prep_kernel, n_tiles=n_tiles),
        grid=(batch, n_tiles),
        in_specs=[pl.BlockSpec((tl, width), lambda b, t: (b * n_tiles + t, 1)),
                  pl.BlockSpec((8, width), lambda b, t: (jnp.maximum((b * n_tiles + t) * per8 - 1, 0), 1)),
                  pl.BlockSpec((8, width), lambda b, t: (jnp.minimum((b * n_tiles + t + 1) * per8, last8), 1)),
                  pl.BlockSpec((conv_w.shape[0], width), lambda b, t: (0, 0))],
        out_specs=pl.BlockSpec((tl, width), lambda b, t: (b * n_tiles + t, 0)),
        out_shape=jax.ShapeDtypeStruct((batch * length, width), F32),
        compiler_params=_cparams(("arbitrary", "arbitrary")),
        name="dn_prep",
    )(p, p, p, conv_w)


def _split_bf16(a):
    hi = a.astype(BF16)
    return hi, (a - hi.astype(F32)).astype(BF16)


def _solve_dot(a, b, mode):
    if mode == "highest":
        return _dot(a, b, lax.Precision.HIGHEST)
    if mode == "bf16":
        return _dot(a.astype(BF16), b.astype(BF16))
    a_hi, a_lo = _split_bf16(a)
    b_hi, b_lo = _split_bf16(b)
    return _dot(a_hi, b_hi) + (_dot(a_hi, b_lo) + _dot(a_lo, b_hi))


def _dn_pass(q_ref, k_ref, v_ref, og_ref, gates_ref, gadd, gmul, norm_g, y_ref, length, h, state,
             lin_scr, add_scr, egl_scr, acc_scr, solve_precision):
    assert CHUNK == HEAD_DIM
    n_chunks = length // CHUNK

    acc_scr[0:length, :] = jnp.zeros((length, HEAD_DIM), F32)

    shp = (CHUNK, CHUNK)
    ri = _iota(shp, 0)
    ci = _iota(shp, 1)
    masks = ((ci <= ri, ci < ri), (ci >= ri, ci > ri))
    couple = ([], [])
    s = 1
    while s < CHUNK:
        same = (ri // (2 * s)) == (ci // (2 * s))
        couple[0].append(same & ((ri // s) % 2 == 1) & ((ci // s) % 2 == 0))
        couple[1].append(same & ((ri // s) % 2 == 0) & ((ci // s) % 2 == 1))
        s *= 2
    group = min(n_chunks, PREP_GROUP)

    def prep(gi, carry):
        chains = []
        for g in range(group):
            c = gi * group + g
            r0 = pl.multiple_of(c * CHUNK, CHUNK)
            q = q_ref[pl.ds(r0, CHUNK), :]
            k = k_ref[pl.ds(r0, CHUNK), :]
            raw = gates_ref[pl.ds(r0, CHUNK), :]
            gb = jnp.where(_iota(raw.shape, 1) < LANE_ALPHA, jax.nn.sigmoid(raw), -gmul * _softplus(raw + gadd))
            k16 = k.astype(BF16)
            kk = _dot_nt(k16, k16)
            qk = _dot_nt(q.astype(BF16), k16)
            for d in range(2):
                incl, strict = masks[d]
                beta_col = _lane_pick(gb, LANE_BETA + 4 * d + h)
                g_col = _lane_pick(gb, LANE_ALPHA + 4 * d + h)
                beta_row = _to_row(beta_col)
                g_row = _to_row(g_col)
                cum_col = jnp.sum(jnp.where(incl, g_row, 0.0), axis=-1, keepdims=True)
                cum_row = _to_row(cum_col)
                tot = jnp.sum(g_col, axis=0, keepdims=True)
                decay = jnp.exp(jnp.where(incl, cum_col - cum_row, NEG_INF))
                lm = jnp.where(strict, kk * beta_col * decay, 0.0)
                egl_scr[d, pl.ds(pl.multiple_of(c * 8, 8), 8), :] = jnp.broadcast_to(jnp.exp(tot), (8, HEAD_DIM))
                chains.append((d, c, r0, lm, beta_row, beta_row * jnp.exp(cum_row), (qk * decay).astype(BF16),
                               q * jnp.exp(cum_col), k * jnp.exp(tot - cum_col)))
        eye = (ri == ci).astype(F32)
        xs = [eye - jnp.where(couple[ch[0]][0], ch[3], 0.0) for ch in chains]
        for lvl in range(1, len(couple[0])):
            cs = [jnp.where(couple[ch[0]][lvl], ch[3], 0.0) for ch in chains]
            xc = [_solve_dot(x, c, solve_precision) for x, c in zip(xs, cs)]
            xs = [x - _solve_dot(t, x, solve_precision) for x, t in zip(xs, xc)]
        u16 = [_solve_dot(ainv * ch[4], v_ref[pl.ds(ch[2], CHUNK), :], solve_precision).astype(BF16)
               for ch, ainv in zip(chains, xs)]
        w16 = [_solve_dot(ainv * ch[5], k_ref[pl.ds(ch[2], CHUNK), :], solve_precision).astype(BF16)
               for ch, ainv in zip(chains, xs)]
        ket = [ch[8].T.astype(BF16) for ch in chains]
        s_lin = [_dot(kt, w) for kt, w in zip(ket, w16)]
        s_add = [_dot(kt, u) for kt, u in zip(ket, u16)]
        o_lin = [ch[7] - _dot(ch[6], w) for ch, w in zip(chains, w16)]
        o_add = [_dot(ch[6], u) for ch, u in zip(chains, u16)]
        for ch, sl, sa, ol, oa in zip(chains, s_lin, s_add, o_lin, o_add):
            d = ch[0]
            base = pl.multiple_of(ch[1] * 2 * CHUNK, 2 * CHUNK)
            lin_scr[d, pl.ds(base, CHUNK), :] = sl.astype(BF16)
            lin_scr[d, pl.ds(base + CHUNK, CHUNK), :] = ol.astype(BF16)
            add_scr[d, pl.ds(base, CHUNK), :] = sa
            add_scr[d, pl.ds(base + CHUNK, CHUNK), :] = oa
        return carry

    lax.fori_loop(0, n_chunks // group, prep, 0)

    def scan(s, st):
        new = []
        for d in range(2):
            c = s if d == 0 else n_chunks - 1 - s
            base = pl.multiple_of(c * 2 * CHUNK, 2 * CHUNK)
            sm = st[d]
            prod = _dot(lin_scr[d, pl.ds(base, 2 * CHUNK), :], sm.astype(BF16))
            add = add_scr[d, pl.ds(base, 2 * CHUNK), :]
            acc_scr[pl.ds(pl.multiple_of(c * CHUNK, CHUNK), CHUNK), :] += prod[CHUNK:] + add[CHUNK:]
            egl = egl_scr[d, pl.ds(pl.multiple_of(c * 8, 8), 8), :][0:1, :]
            new.append(sm * egl - prod[:CHUNK] + add[:CHUNK])
        return tuple(new)

    state = lax.fori_loop(0, n_chunks, scan, state, unroll=SCAN_UNROLL)

    y = _rms(acc_scr[0:length, :], norm_g) * _silu(og_ref[...])
    y_ref[...] = y.astype(BF16)
    return state


def _dn_kernel(q_ref, k_ref, v_ref, og_ref, gates_ref, qc_ref, kc_ref, vc_ref, ogc_ref, gatesc_ref,
               gadd_ref, gmul_ref, norm_ref, y_ref, yc_ref,
               lin_scr, add_scr, egl_scr, acc_scr, *, seq, lc, solve_precision):
    h = pl.program_id(1)
    gadd = gadd_ref[...]
    gmul = gmul_ref[...]
    norm_g = norm_ref[...]
    scr = (lin_scr, add_scr, egl_scr, acc_scr)
    zero = jnp.zeros((HEAD_DIM, HEAD_DIM), F32)
    state = _dn_pass(qc_ref, kc_ref, vc_ref, ogc_ref, gatesc_ref, gadd, gmul, norm_g, yc_ref, lc, h, (zero, zero),
                     *scr, solve_precision)
    _dn_pass(q_ref, k_ref, v_ref, og_ref, gates_ref, gadd, gmul, norm_g, y_ref, seq, h, state,
             *scr, solve_precision)


def _dn_call(dq_lat, dq_ctx, p_lat, p_ctx, pg_lat, pg_ctx, gadd, gmul, norm_g, batch, seq, lc, solve_precision):
    hd = HEAD_DIM
    heads = 4

    def blk(n, col):
        return pl.BlockSpec((n, hd), lambda b, h: (b, col + h))

    def gate_blk(n):
        return pl.BlockSpec((n, hd), lambda b, h: (b, 0))

    vec = pl.BlockSpec((1, hd), lambda b, h: (0, 0))
    return pl.pallas_call(
        functools.partial(_dn_kernel, seq=seq, lc=lc, solve_precision=solve_precision),
        grid=(batch, heads),
        in_specs=[blk(seq, 0), blk(seq, heads), blk(seq, 2 * heads), blk(seq, DN_OG), gate_blk(seq),
                  blk(lc, 0), blk(lc, heads), blk(lc, 2 * heads), blk(lc, DN_OG), gate_blk(lc),
                  vec, vec, vec],
        out_specs=[pl.BlockSpec((seq, hd), lambda b, h: (b, h)),
                   pl.BlockSpec((lc, hd), lambda b, h: (b, h))],
        out_shape=[jax.ShapeDtypeStruct((batch * seq, heads * hd), BF16),
                   jax.ShapeDtypeStruct((batch * lc, heads * hd), BF16)],
        scratch_shapes=[pltpu.VMEM((2, 2 * seq, hd), BF16),
                        pltpu.VMEM((2, 2 * seq, hd), F32),
                        pltpu.VMEM((2, seq // CHUNK * 8, hd), F32),
                        pltpu.VMEM((seq, hd), F32)],
        compiler_params=_cparams(("arbitrary", "arbitrary"), DN_VMEM_LIMIT),
        name="deltanet",
    )(dq_lat, dq_lat, dq_lat, p_lat, pg_lat, dq_ctx, dq_ctx, dq_ctx, p_ctx, pg_ctx, gadd, gmul, norm_g)


def _ml_pass(q_ref, k_ref, v_ref, og_ref, gates_ref, gadd, norm_g, y_ref, length, h, state, acc_scr):
    n_chunks = length // CHUNK
    acc_scr[0:length, :] = jnp.zeros((length, HEAD_DIM), F32)
    shp = (CHUNK, CHUNK)
    ri = _iota(shp, 0)
    ci = _iota(shp, 1)
    masks = (ci <= ri, ci >= ri)
    head_lanes = (_iota((1, HEAD_DIM), 1) // ML_QK) == (h % 2)

    group = min(n_chunks, ML_GROUP)

    def each(f, *lists):
        return [f(*a) for a in zip(*lists)]

    def step(gi, st):
        dirs, r0s = [], []
        for j in range(group):
            s = gi * group + j
            for d in range(2):
                c = s if d == 0 else n_chunks - 1 - s
                dirs.append(d)
                r0s.append(pl.multiple_of(c * CHUNK, CHUNK))
        incl = [masks[d] for d in dirs]
        q = [jnp.where(head_lanes, q_ref[pl.ds(r0, CHUNK), :], 0.0) for r0 in r0s]
        k = [jnp.where(head_lanes, k_ref[pl.ds(r0, CHUNK), :], 0.0) * ML_QK ** -0.5 for r0 in r0s]
        v16 = [v_ref[pl.ds(r0, CHUNK), :].astype(BF16) for r0 in r0s]
        gb = [gates_ref[pl.ds(r0, CHUNK), :] + gadd for r0 in r0s]
        ig_col = [_lane_pick(g, LANE_IG + 4 * d + h) for g, d in zip(gb, dirs)]
        lf_col = [_log_sigmoid(_lane_pick(g, LANE_FG + 4 * d + h)) for g, d in zip(gb, dirs)]
        ig_row = each(_to_row, ig_col)
        lf_row = each(_to_row, lf_col)
        b_col = each(lambda m, r: jnp.sum(jnp.where(m, r, 0.0), axis=-1, keepdims=True), incl, lf_row)
        b_row = each(_to_row, b_col)
        b_last = each(lambda x: jnp.sum(x, axis=0, keepdims=True), lf_col)
        g_end = each(lambda bl, b, i: bl - b + i, b_last, b_col, ig_col)
        m_loc = each(lambda x: jnp.max(x, axis=0, keepdims=True), g_end)
        kw = each(lambda kk, g, m: kk * jnp.exp(g - m), k, g_end, m_loc)
        q16 = each(lambda x: x.astype(BF16), q)
        k16 = each(lambda x: x.astype(BF16), k)
        c_loc = each(lambda a, b: _dot_tn(a.astype(BF16), b), kw, v16)
        n_loc = each(lambda x: jnp.sum(x, axis=0, keepdims=True), kw)
        log_d = each(lambda m, bc, br, ir: jnp.where(m, bc - br + ir, NEG_INF), incl, b_col, b_row, ig_row)
        mx_col = each(lambda x: jnp.max(x, axis=-1, keepdims=True), log_d)
        s0 = each(lambda a, b, ld, mx: _dot_nt(a, b) * jnp.exp(ld - mx), q16, k16, log_d, mx_col)
        sv0 = each(lambda s, vv: _dot(s.astype(BF16), vv), s0, v16)
        rs0 = each(lambda s: jnp.sum(s, axis=-1, keepdims=True), s0)
        st = list(st)
        for i, (d, r0) in enumerate(zip(dirs, r0s)):
            c_st, n_st, m_st = st[d]
            m_inter = b_col[i] + m_st
            m_t = jnp.maximum(mx_col[i], m_inter)
            intra = jnp.exp(mx_col[i] - m_t)
            inter = jnp.exp(m_inter - m_t)
            num = intra * sv0[i] + inter * _dot(q16[i], c_st.astype(BF16))
            den = intra * rs0[i] + inter * jnp.sum(q[i] * n_st, axis=-1, keepdims=True)
            acc_scr[pl.ds(r0, CHUNK), :] += num / jnp.maximum(jnp.abs(den), jnp.exp(-m_t))
            m_new = jnp.maximum(b_last[i] + m_st, m_loc[i])
            a = jnp.exp(b_last[i] + m_st - m_new)
            e = jnp.exp(m_loc[i] - m_new)
            st[d] = (a * c_st + e * c_loc[i], a * n_st + e * n_loc[i], m_new)
        return tuple(st)

    state = lax.fori_loop(0, n_chunks // group, step, state)
    y = jax.nn.sigmoid(og_ref[...]) * _rms(acc_scr[0:length, :], norm_g)
    y_ref[...] = y.astype(BF16)
    return state


def _ml_kernel(q_ref, k_ref, v_ref, og_ref, gates_ref, qc_ref, kc_ref, vc_ref, ogc_ref, gatesc_ref,
               gadd_ref, norm_ref, y_ref, yc_ref, acc_scr, *, seq, lc):
    h = pl.program_id(1)
    gadd = gadd_ref[...]
    norm_g = norm_ref[0]
    zero = (jnp.zeros((HEAD_DIM, HEAD_DIM), F32), jnp.zeros((1, HEAD_DIM), F32), jnp.zeros((1, 1), F32))
    state = _ml_pass(qc_ref, kc_ref, vc_ref, ogc_ref, gatesc_ref, gadd, norm_g, yc_ref, lc, h, (zero, zero), acc_scr)
    _ml_pass(q_ref, k_ref, v_ref, og_ref, gates_ref, gadd, norm_g, y_ref, seq, h, state, acc_scr)


def _ml_call(p_lat, p_ctx, pg_lat, pg_ctx, gadd, norm_g, batch, seq, lc):
    hd = HEAD_DIM
    heads = 4

    def specs(n):
        return [pl.BlockSpec((n, hd), lambda b, h: (b, ML_Q + h // 2)),
                pl.BlockSpec((n, hd), lambda b, h: (b, ML_K + h // 2)),
                pl.BlockSpec((n, hd), lambda b, h: (b, ML_V + h)),
                pl.BlockSpec((n, hd), lambda b, h: (b, ML_OG + h)),
                pl.BlockSpec((n, hd), lambda b, h: (b, 0))]

    return pl.pallas_call(
        functools.partial(_ml_kernel, seq=seq, lc=lc),
        grid=(batch, heads),
        in_specs=specs(seq) + specs(lc) + [pl.BlockSpec((1, hd), lambda b, h: (0, 0)),
                                            pl.BlockSpec((1, 1, hd), lambda b, h: (h, 0, 0))],
        out_specs=[pl.BlockSpec((seq, hd), lambda b, h: (b, h)),
                   pl.BlockSpec((lc, hd), lambda b, h: (b, h))],
        out_shape=[jax.ShapeDtypeStruct((batch * seq, heads * hd), BF16),
                   jax.ShapeDtypeStruct((batch * lc, heads * hd), BF16)],
        scratch_shapes=[pltpu.VMEM((seq, hd), F32)],
        compiler_params=_cparams(("arbitrary", "arbitrary")),
        name="mlstm",
    )(p_lat, p_lat, p_lat, p_lat, pg_lat, p_ctx, p_ctx, p_ctx, p_ctx, pg_ctx, gadd, norm_g.reshape(heads, 1, hd))


def _regroup_w_in(w):
    gw = 512
    na_in = 3 * gw
    dn_in = 4 * gw + 16
    ml_main = 512 + 2 * gw
    dn0 = na_in
    ml0 = dn0 + dn_in
    wa0 = ml0 + ml_main + 16
    w = w.astype(BF16)
    main = jnp.concatenate([w[..., :dn0 + 4 * gw], w[..., ml0:ml0 + ml_main], w[..., wa0:]], axis=-1)
    gates = jnp.concatenate([w[..., dn0 + 4 * gw:ml0], w[..., ml0 + ml_main:wa0],
                             jnp.zeros(w.shape[:-1] + (128 - 32,), BF16)], axis=-1)
    return main, gates


def _rope_tables(seq):
    t = jnp.arange(seq)
    n_freq = HEAD_DIM // 4
    inv_freq = ROPE_THETA ** (-jnp.arange(n_freq, dtype=F32) / n_freq)
    pos = jnp.stack([t // GRID_W, t % GRID_W], axis=-1).astype(F32)
    ang = pos[:, :, None] * inv_freq
    cos, sin = jnp.cos(ang), jnp.sin(ang)
    cos_t = jnp.concatenate([cos, cos], axis=-1).reshape(seq, HEAD_DIM)
    sin_t = jnp.concatenate([-sin, sin], axis=-1).reshape(seq, HEAD_DIM)
    return cos_t, sin_t


def _gate_vectors(dt_bias, a_log, i_bias, f_bias):
    zeros8 = jnp.zeros((8,), F32)
    gadd = jnp.concatenate([zeros8, dt_bias.reshape(-1), i_bias.reshape(-1), f_bias.reshape(-1),
                            jnp.zeros((96,), F32)]).astype(F32)
    gmul = jnp.concatenate([zeros8, jnp.exp(a_log.astype(F32)).reshape(-1), jnp.zeros((112,), F32)])
    return gadd.reshape(1, 128), gmul.reshape(1, 128)


def _forward(x, c, ctx, c_ctx, w_ada, b_ada, norm_mix, norm_ffn, w_in, w_out, na_qk_gain, na_rpb, dn_conv,
             dn_a_log, dn_dt_bias, dn_norm, ml_i_bias, ml_f_bias, ml_norm, wa_qk_gain, wa_sink, w_ffn_in,
             w_ffn_out, solve_precision):
    batch, seq, d = x.shape
    lc = ctx.shape[1]
    depth = w_ada.shape[0]
    rows = seq // GRID_W
    tm = 512
    tn_ffn = 512

    cos_t, sin_t = _rope_tables(seq)
    na_bias = _na_col_tables(na_rpb)
    cs = jnp.concatenate([c, c_ctx[None, :], jnp.zeros((8 - batch - 1, d), F32)], axis=0)
    mods = _ada_call(cs, w_ada, b_ada).reshape(depth, 8, 6, d)
    mods = jnp.pad(mods, ((0, 0), (0, 0), (0, 2), (0, 0)))

    w_main, w_gate = _regroup_w_in(w_in)
    w_o = w_out.astype(BF16)
    w_f1 = w_ffn_in.astype(BF16)
    w_f2 = w_ffn_out.astype(BF16)

    xl = x.reshape(batch * seq, d)
    xc = ctx.reshape(batch * lc, d)
    hl = _norm_call(xl, mods[0, :batch], norm_mix[0:1], seq, tm)
    hc = _norm_call(xc, mods[0, batch:batch + 1], norm_mix[0:1], batch * lc, tm)

    for l in range(depth):
        need_ctx = l < depth - 1
        mod_l, mod_c = mods[l, :batch], mods[l, batch:batch + 1]
        p_lat = _matmul_call(hl, w_main, l, tm, 1024, "inproj")
        pg_lat = _matmul_call(hl, w_gate, l, tm, 128, "inproj_gates")
        p_ctx = _matmul_call(hc, w_main, l, tm, 1024, "inproj_ctx")
        pg_ctx = _matmul_call(hc, w_gate, l, tm, 128, "inproj_gates_ctx")

        ya = _na_call(p_lat, p_ctx, na_qk_gain[l], na_bias, l, batch, seq, lc)
        gadd, gmul = _gate_vectors(dn_dt_bias[l], dn_a_log[l], ml_i_bias[l], ml_f_bias[l])
        dq_lat = _dn_prep_call(p_lat, dn_conv[l], batch, seq, 256)
        dq_ctx = _dn_prep_call(p_ctx, dn_conv[l], batch, lc, 256)
        yb, ycb = _dn_call(dq_lat, dq_ctx, p_lat, p_ctx, pg_lat, pg_ctx, gadd, gmul, dn_norm[l].reshape(1, -1),
                           batch, seq, lc, solve_precision)
        ym, ycm = _ml_call(p_lat, p_ctx, pg_lat, pg_ctx, gadd, ml_norm[l], batch, seq, lc)
        yw = _wa_call(p_lat, p_ctx, wa_qk_gain[l], wa_sink[l], cos_t, sin_t, batch, seq, lc)

        nl = min(l + 1, depth - 1)
        xl, h2 = _outproj_call((ya, yb, ym, yw), w_o, l, xl, mod_l, norm_ffn[l:l + 1], seq, tm)
        xl, hl = _ffn_call(h2, w_f1, w_f2, l, xl, mod_l, mods[nl, :batch], norm_mix[nl:nl + 1], seq, tm, tn_ffn,
                           need_ctx)
        if need_ctx:
            yca = _ctx_attn_call(p_ctx, na_qk_gain[l], wa_sink[l], batch, lc, 4, 1, NA_Q, NA_K, NA_V, False,
                                 "na_ctx_attn")
            ycw = _ctx_attn_call(p_ctx, wa_qk_gain[l], wa_sink[l], batch, lc, 4, 2, WA_Q, WA_K, WA_V, True,
                                 "wa_ctx_attn")
            xc, h2c = _outproj_call((yca, ycb, ycm, ycw), w_o, l, xc, mod_c, norm_ffn[l:l + 1], batch * lc, tm)
            xc, hc = _ffn_call(h2c, w_f1, w_f2, l, xc, mod_c, mods[nl, batch:batch + 1], norm_mix[nl:nl + 1],
                               batch * lc, tm, tn_ffn, True)
    return xl.reshape(batch, seq, d)


def kernel(x, c, ctx, c_ctx, w_ada, b_ada, norm_mix, norm_ffn, w_in, w_out, na_qk_gain, na_rpb, dn_conv, dn_a_log,
           dn_dt_bias, dn_norm, ml_i_bias, ml_f_bias, ml_norm, wa_qk_gain, wa_sink, w_ffn_in, w_ffn_out):
    return _forward(x, c, ctx, c_ctx, w_ada, b_ada, norm_mix, norm_ffn, w_in, w_out, na_qk_gain, na_rpb, dn_conv,
                    dn_a_log, dn_dt_bias, dn_norm, ml_i_bias, ml_f_bias, ml_norm, wa_qk_gain, wa_sink, w_ffn_in,
                    w_ffn_out, "bf16")
```

```python
import functools

import jax
import jax.numpy as jnp
from jax import lax
from jax.experimental import pallas as pl
from jax.experimental.pallas import tpu as pltpu

F32 = jnp.float32
BF16 = jnp.bfloat16

HEAD_DIM = 128
GRID_W = 64
CHUNK = 128
PREP_GROUP = 4
ML_GROUP = 4
SCAN_UNROLL = 2
NA_WIN_ROWS = 8
NA_WIN_COLS = 16
NA_QROWS = 4
NA_KROWS = NA_QROWS + NA_WIN_ROWS
WA_BLOCK = 128
ML_QK = 64
ROPE_THETA = 10000.0
EPS = 1e-6
NEG_INF = float("-inf")
VMEM_LIMIT = 48 * 1024 * 1024
DN_VMEM_LIMIT = 56 * 1024 * 1024

NA_Q, NA_K, NA_V = 0, 4, 8
DN_OG = 24
ML_Q, ML_K, ML_V, ML_OG = 28, 30, 32, 36
WA_Q, WA_K, WA_V = 40, 44, 46
MAIN_COLS = 48 * 128
LANE_BETA, LANE_ALPHA, LANE_IG, LANE_FG = 0, 8, 16, 24


def _cparams(sem, vmem_limit=VMEM_LIMIT):
    return pltpu.CompilerParams(dimension_semantics=sem, vmem_limit_bytes=vmem_limit)


def _rms(x, gain):
    return x * lax.rsqrt(jnp.mean(x * x, axis=-1, keepdims=True) + EPS) * gain


def _silu(x):
    return x * jax.nn.sigmoid(x)


def _softplus(x):
    return jnp.maximum(x, 0.0) + jnp.log1p(jnp.exp(-jnp.abs(x)))


def _log_sigmoid(x):
    return jnp.minimum(x, 0.0) - jnp.log1p(jnp.exp(-jnp.abs(x)))


def _dot(a, b, precision=None):
    return jnp.dot(a, b, preferred_element_type=F32, precision=precision)


def _dot_nt(a, b):
    return lax.dot_general(a, b, (((1,), (1,)), ((), ())), preferred_element_type=F32)


def _dot_tn(a, b):
    return lax.dot_general(a, b, (((0,), (0,)), ((), ())), preferred_element_type=F32)


def _iota(shape, dim):
    return lax.broadcasted_iota(jnp.int32, shape, dim)


def _lane_pick(x, lane):
    return jnp.sum(jnp.where(_iota(x.shape, 1) == lane, x, 0.0), axis=-1, keepdims=True)


def _to_row(col):
    n = col.shape[0]
    eye = _iota((n, n), 0) == _iota((n, n), 1)
    return jnp.sum(jnp.where(eye, col, 0.0), axis=0, keepdims=True)


def _ada_kernel(c_ref, w_ref, b_ref, o_ref):
    a = _silu(c_ref[...]).astype(BF16)
    o_ref[0] = _dot(a, w_ref[0].astype(BF16)) + b_ref[0]


def _ada_call(cs, w_ada, b_ada):
    depth, d, n = w_ada.shape
    tn = 1024
    return pl.pallas_call(
        _ada_kernel,
        grid=(depth, n // tn),
        in_specs=[pl.BlockSpec((8, d), lambda l, j: (0, 0)),
                  pl.BlockSpec((1, d, tn), lambda l, j: (l, 0, j)),
                  pl.BlockSpec((1, 1, tn), lambda l, j: (l, 0, j))],
        out_specs=pl.BlockSpec((1, 8, tn), lambda l, j: (l, 0, j)),
        out_shape=jax.ShapeDtypeStruct((depth, 8, n), F32),
        compiler_params=_cparams(("arbitrary", "arbitrary")),
        name="ada",
    )(cs, w_ada, b_ada.reshape(depth, 1, n))


def _norm_kernel(x_ref, mod_ref, g_ref, h_ref):
    mod = mod_ref[0]
    h = _rms(x_ref[...], g_ref[...])
    h_ref[...] = (h * (1.0 + mod[1:2]) + mod[0:1]).astype(BF16)


def _norm_call(x, mod, gain, rows_per_mod, tm):
    m, d = x.shape
    per = rows_per_mod // tm
    return pl.pallas_call(
        _norm_kernel,
        grid=(m // tm,),
        in_specs=[pl.BlockSpec((tm, d), lambda i: (i, 0)),
                  pl.BlockSpec((1, 8, d), lambda i: (i // per, 0, 0)),
                  pl.BlockSpec((1, d), lambda i: (0, 0))],
        out_specs=pl.BlockSpec((tm, d), lambda i: (i, 0)),
        out_shape=jax.ShapeDtypeStruct((m, d), BF16),
        compiler_params=_cparams(("arbitrary",)),
        name="norm_mod",
    )(x, mod, gain)


def _matmul_kernel(a_ref, w_ref, o_ref):
    o_ref[...] = _dot(a_ref[...], w_ref[0])


def _matmul_call(a, w, layer, tm, tn, name):
    m, k = a.shape
    n = w.shape[2]
    return pl.pallas_call(
        _matmul_kernel,
        grid=(n // tn, m // tm),
        in_specs=[pl.BlockSpec((tm, k), lambda j, i: (i, 0)),
                  pl.BlockSpec((1, k, tn), lambda j, i: (layer, 0, j))],
        out_specs=pl.BlockSpec((tm, tn), lambda j, i: (i, j)),
        out_shape=jax.ShapeDtypeStruct((m, n), F32),
        compiler_params=_cparams(("arbitrary", "arbitrary")),
        name=name,
    )(a, w)


def _outproj_kernel(ya_ref, yb_ref, ym_ref, yw_ref, w_ref, x_ref, mod_ref, g_ref, xo_ref, h_ref):
    gw = ya_ref.shape[1]
    acc = _dot(ya_ref[...], w_ref[0, 0:gw, :])
    acc += _dot(yb_ref[...], w_ref[0, gw:2 * gw, :])
    acc += _dot(ym_ref[...], w_ref[0, 2 * gw:3 * gw, :])
    acc += _dot(yw_ref[...], w_ref[0, 3 * gw:4 * gw, :])
    mod = mod_ref[0]
    xn = x_ref[...] + mod[2:3] * acc
    xo_ref[...] = xn
    h = _rms(xn, g_ref[...])
    h_ref[...] = (h * (1.0 + mod[4:5]) + mod[3:4]).astype(BF16)


def _outproj_call(ys, w, layer, x, mod, gain, rows_per_mod, tm):
    m, d = x.shape
    gw = ys[0].shape[1]
    per = rows_per_mod // tm
    yspec = pl.BlockSpec((tm, gw), lambda i: (i, 0))
    return pl.pallas_call(
        _outproj_kernel,
        grid=(m // tm,),
        in_specs=[yspec, yspec, yspec, yspec,
                  pl.BlockSpec((1, 4 * gw, d), lambda i: (layer, 0, 0)),
                  pl.BlockSpec((tm, d), lambda i: (i, 0)),
                  pl.BlockSpec((1, 8, d), lambda i: (i // per, 0, 0)),
                  pl.BlockSpec((1, d), lambda i: (0, 0))],
        out_specs=[pl.BlockSpec((tm, d), lambda i: (i, 0)),
                   pl.BlockSpec((tm, d), lambda i: (i, 0))],
        out_shape=[jax.ShapeDtypeStruct((m, d), F32), jax.ShapeDtypeStruct((m, d), BF16)],
        input_output_aliases={5: 0},
        compiler_params=_cparams(("arbitrary",)),
        name="outproj",
    )(*ys, w, x, mod, gain)


def _ffn_kernel(h_ref, wg_ref, wu_ref, w2_ref, x_ref, mod_ref, modn_ref, gn_ref, xo_ref, hn_ref, acc_ref,
                *, with_next):
    j = pl.program_id(1)

    @pl.when(j == 0)
    def _():
        acc_ref[...] = jnp.zeros_like(acc_ref)

    h = h_ref[...]
    act = _silu(_dot(h, wg_ref[0])) * _dot(h, wu_ref[0])
    acc_ref[...] += _dot(act.astype(BF16), w2_ref[0])

    @pl.when(j == pl.num_programs(1) - 1)
    def _():
        xn = x_ref[...] + mod_ref[0][5:6] * acc_ref[...]
        xo_ref[...] = xn
        if with_next:
            modn = modn_ref[0]
            hn = _rms(xn, gn_ref[...])
            hn_ref[...] = (hn * (1.0 + modn[1:2]) + modn[0:1]).astype(BF16)
        else:
            hn_ref[...] = jnp.zeros_like(hn_ref)


def _ffn_call(h, w_in, w_out, layer, x, mod, mod_next, gain_next, rows_per_mod, tm, tn, with_next):
    m, d = x.shape
    hidden = w_out.shape[1]
    nj = hidden // tn
    per = rows_per_mod // tm
    return pl.pallas_call(
        functools.partial(_ffn_kernel, with_next=with_next),
        grid=(m // tm, nj),
        in_specs=[pl.BlockSpec((tm, d), lambda i, j: (i, 0)),
                  pl.BlockSpec((1, d, tn), lambda i, j: (layer, 0, j)),
                  pl.BlockSpec((1, d, tn), lambda i, j: (layer, 0, nj + j)),
                  pl.BlockSpec((1, tn, d), lambda i, j: (layer, j, 0)),
                  pl.BlockSpec((tm, d), lambda i, j: (i, 0)),
                  pl.BlockSpec((1, 8, d), lambda i, j: (i // per, 0, 0)),
                  pl.BlockSpec((1, 8, d), lambda i, j: (i // per, 0, 0)),
                  pl.BlockSpec((1, d), lambda i, j: (0, 0))],
        out_specs=[pl.BlockSpec((tm, d), lambda i, j: (i, 0)),
                   pl.BlockSpec((tm, d), lambda i, j: (i, 0))],
        out_shape=[jax.ShapeDtypeStruct((m, d), F32), jax.ShapeDtypeStruct((m, d), BF16)],
        scratch_shapes=[pltpu.VMEM((tm, d), F32)],
        input_output_aliases={4: 0},
        compiler_params=_cparams(("arbitrary", "arbitrary")),
        name="ffn",
    )(h, w_in, w_in, w_out, x, mod, mod_next, gain_next)


def _na_kernel(q_ref, k_ref, v_ref, kc_ref, vc_ref, gain_ref, colbias_ref, o_ref,
               kn_scr, v_scr, kcn_scr, vc_scr, bias_scr, *, rows):
    rb = pl.program_id(2)
    n_rb = pl.num_programs(2)

    @pl.when(rb == 0)
    def _():
        g1 = gain_ref[1:2, :]
        kn_scr[...] = _rms(k_ref[...], g1).astype(BF16)
        v_scr[...] = v_ref[...].astype(BF16)
        kcn_scr[...] = _rms(kc_ref[...], g1).astype(BF16)
        vc_scr[...] = vc_ref[...].astype(BF16)
        closed = jnp.full((GRID_W, GRID_W), NEG_INF, F32)
        for typ, (offs, inside) in enumerate(_na_row_plan(rows)):
            for i in range(NA_QROWS):
                for jp in range(NA_KROWS // 2):
                    halves = [colbias_ref[0, 0, offs[i][j]] if inside[i][j] else closed for j in (2 * jp, 2 * jp + 1)]
                    bias_scr[typ, i * GRID_W:(i + 1) * GRID_W, 2 * jp * GRID_W:(2 * jp + 2) * GRID_W] = (
                        jnp.concatenate(halves, axis=1))

    scale = HEAD_DIM ** -0.5
    nk = NA_KROWS * GRID_W
    qn = _rms(q_ref[...], gain_ref[0:1, :]).astype(BF16)
    ws = jnp.clip(NA_QROWS * rb - NA_WIN_ROWS // 2, 0, rows - NA_KROWS)
    start = pl.multiple_of(ws * GRID_W, GRID_W)
    kw = kn_scr[pl.ds(start, nk), :]
    vw = v_scr[pl.ds(start, nk), :]
    typ = jnp.where(rb == 0, 0, jnp.where(rb == n_rb - 1, 2, 1))
    bias = bias_scr[pl.ds(typ, 1)][0]
    s_loc = _dot_nt(qn, kw) * scale + bias
    s_ctx = _dot_nt(qn, kcn_scr[...]) * scale
    m = jnp.maximum(jnp.max(s_loc, axis=-1, keepdims=True), jnp.max(s_ctx, axis=-1, keepdims=True))
    p_loc = jnp.exp(s_loc - m)
    p_ctx = jnp.exp(s_ctx - m)
    den = jnp.sum(p_loc, axis=-1, keepdims=True) + jnp.sum(p_ctx, axis=-1, keepdims=True)
    o = _dot(p_loc.astype(BF16), vw) + _dot(p_ctx.astype(BF16), vc_scr[...])
    o_ref[...] = (o / den).astype(BF16)


def _na_row_plan(rows):
    n_rb = rows // NA_QROWS
    kh = NA_WIN_ROWS
    plan = []
    for rb in (0, 1, n_rb - 1):
        ws = min(max(NA_QROWS * rb - kh // 2, 0), rows - NA_KROWS)
        offs, inside = [], []
        for i in range(NA_QROWS):
            r = NA_QROWS * rb + i
            rs = min(max(r - kh // 2, 0), rows - kh)
            offs.append([min(max(ws + j - r, 1 - kh), kh - 1) + kh - 1 for j in range(NA_KROWS)])
            inside.append([rs <= ws + j < rs + kh for j in range(NA_KROWS)])
        plan.append((offs, inside))
    return plan


def _na_col_tables(rpb):
    kw = NA_WIN_COLS
    c = jnp.arange(GRID_W)
    cs = jnp.clip(c - kw // 2, 0, GRID_W - kw)
    in_col = (c[None, :] >= cs[:, None]) & (c[None, :] < cs[:, None] + kw)
    dc = jnp.clip(c[None, :] - c[:, None], 1 - kw, kw - 1) + kw - 1
    dc_hot = jax.nn.one_hot(dc, 2 * kw - 1, dtype=F32)
    val = jnp.einsum("lhab,ckb->lhack", rpb.astype(F32), dc_hot, precision=lax.Precision.HIGHEST)
    return jnp.where(in_col, val, NEG_INF)


def _na_call(p_lat, p_ctx, gain, colbias, layer, batch, seq, lc):
    rows = seq // GRID_W
    n_rb = rows // NA_QROWS
    tq = NA_QROWS * GRID_W
    nk = NA_KROWS * GRID_W
    heads = colbias.shape[1]
    n_off = colbias.shape[2]
    hd = HEAD_DIM
    return pl.pallas_call(
        functools.partial(_na_kernel, rows=rows),
        grid=(batch, heads, n_rb),
        in_specs=[pl.BlockSpec((tq, hd), lambda b, h, r: (b * n_rb + r, NA_Q + h)),
                  pl.BlockSpec((seq, hd), lambda b, h, r: (b, NA_K + h)),
                  pl.BlockSpec((seq, hd), lambda b, h, r: (b, NA_V + h)),
                  pl.BlockSpec((lc, hd), lambda b, h, r: (b, NA_K + h)),
                  pl.BlockSpec((lc, hd), lambda b, h, r: (b, NA_V + h)),
                  pl.BlockSpec((2, hd), lambda b, h, r: (0, 0)),
                  pl.BlockSpec((1, 1, n_off, GRID_W, GRID_W), lambda b, h, r: (layer, h, 0, 0, 0))],
        out_specs=pl.BlockSpec((tq, hd), lambda b, h, r: (b * n_rb + r, h)),
        out_shape=jax.ShapeDtypeStruct((batch * seq, heads * hd), BF16),
        scratch_shapes=[pltpu.VMEM((seq, hd), BF16), pltpu.VMEM((seq, hd), BF16),
                        pltpu.VMEM((lc, hd), BF16), pltpu.VMEM((lc, hd), BF16),
                        pltpu.VMEM((3, tq, nk), F32)],
        compiler_params=_cparams(("arbitrary", "arbitrary", "arbitrary")),
        name="na_attn",
    )(p_lat, p_lat, p_lat, p_ctx, p_ctx, gain, colbias)


def _ctx_attn_kernel(sink_ref, q_ref, k_ref, v_ref, gain_ref, o_ref, *, use_sink):
    scale = HEAD_DIM ** -0.5
    qn = _rms(q_ref[...], gain_ref[0:1, :]).astype(BF16)
    kn = _rms(k_ref[...], gain_ref[1:2, :]).astype(BF16)
    s = _dot_nt(qn, kn) * scale
    m = jnp.max(s, axis=-1, keepdims=True)
    if use_sink:
        sk = sink_ref[pl.program_id(1)]
        m = jnp.maximum(m, sk)
    p = jnp.exp(s - m)
    den = jnp.sum(p, axis=-1, keepdims=True)
    if use_sink:
        den = den + jnp.exp(sk - m)
    o = _dot(p.astype(BF16), v_ref[...].astype(BF16))
    o_ref[...] = (o / den).astype(BF16)


def _ctx_attn_call(p_ctx, gain, sink, batch, lc, heads, rep, qoff, koff, voff, use_sink, name):
    hd = HEAD_DIM
    return pl.pallas_call(
        functools.partial(_ctx_attn_kernel, use_sink=use_sink),
        grid=(batch, heads),
        in_specs=[pl.BlockSpec(memory_space=pltpu.SMEM),
                  pl.BlockSpec((lc, hd), lambda b, h: (b, qoff + h)),
                  pl.BlockSpec((lc, hd), lambda b, h: (b, koff + h // rep)),
                  pl.BlockSpec((lc, hd), lambda b, h: (b, voff + h // rep)),
                  pl.BlockSpec((2, hd), lambda b, h: (0, 0))],
        out_specs=pl.BlockSpec((lc, hd), lambda b, h: (b, h)),
        out_shape=jax.ShapeDtypeStruct((batch * lc, heads * hd), BF16),
        compiler_params=_cparams(("arbitrary", "arbitrary")),
        name=name,
    )(sink, p_ctx, p_ctx, p_ctx, gain)


def _rope(x, cos, sin_signed):
    lane = _iota(x.shape, 1)
    swapped = jnp.where(lane % 64 < 32, pltpu.roll(x, 96, axis=1), pltpu.roll(x, 32, axis=1))
    return x * cos + swapped * sin_signed


def _wa_kernel(sink_ref, q_ref, k_ref, v_ref, kc_ref, vc_ref, gain_ref, cq_ref, sq_ref, ck_ref, sk_ref, o_ref,
               kpad_scr, vpad_scr, kcn_scr, vc_scr, *, seq):
    g = pl.program_id(1)
    qb = pl.program_id(2)
    wb = WA_BLOCK
    hd = HEAD_DIM

    @pl.when(qb == 0)
    def _():
        g1 = gain_ref[1:2, :]
        zeros = jnp.zeros((wb, hd), BF16)
        kpad_scr[0:wb, :] = zeros
        kpad_scr[seq + wb:seq + 2 * wb, :] = zeros
        vpad_scr[0:wb, :] = zeros
        vpad_scr[seq + wb:seq + 2 * wb, :] = zeros
        kpad_scr[wb:seq + wb, :] = _rope(_rms(k_ref[...], g1), ck_ref[...], sk_ref[...]).astype(BF16)
        vpad_scr[wb:seq + wb, :] = v_ref[...].astype(BF16)
        kcn_scr[...] = _rms(kc_ref[...], g1).astype(BF16)
        vc_scr[...] = vc_ref[...].astype(BF16)

    scale = hd ** -0.5
    g0 = gain_ref[0:1, :]
    cq = cq_ref[...]
    sq = sq_ref[...]
    q2 = jnp.concatenate([_rope(_rms(q_ref[:, 0:hd], g0), cq, sq),
                          _rope(_rms(q_ref[:, hd:2 * hd], g0), cq, sq)], axis=0).astype(BF16)
    start = pl.multiple_of(qb * wb, wb)
    kw = kpad_scr[pl.ds(start, 3 * wb), :]
    vw = vpad_scr[pl.ds(start, 3 * wb), :]
    shp = (2 * wb, 3 * wb)
    i = _iota(shp, 0) % wb
    j = _iota(shp, 1)
    kpos = (qb - 1) * wb + j
    ok = (j - i >= 0) & (j - i <= 2 * wb) & (kpos >= 0) & (kpos < seq)
    s_loc = jnp.where(ok, _dot_nt(q2, kw) * scale, NEG_INF)
    s_ctx = _dot_nt(q2, kcn_scr[...]) * scale
    sink = jnp.where(_iota((2 * wb, 1), 0) < wb, sink_ref[2 * g], sink_ref[2 * g + 1])
    m = jnp.maximum(jnp.maximum(jnp.max(s_loc, axis=-1, keepdims=True), jnp.max(s_ctx, axis=-1, keepdims=True)),
                    sink)
    p_loc = jnp.exp(s_loc - m)
    p_ctx = jnp.exp(s_ctx - m)
    den = jnp.sum(p_loc, axis=-1, keepdims=True) + jnp.sum(p_ctx, axis=-1, keepdims=True) + jnp.exp(sink - m)
    o = (_dot(p_loc.astype(BF16), vw) + _dot(p_ctx.astype(BF16), vc_scr[...])) / den
    o_ref[:, 0:hd] = o[0:wb].astype(BF16)
    o_ref[:, hd:2 * hd] = o[wb:2 * wb].astype(BF16)


def _wa_call(p_lat, p_ctx, gain, sink, cos_t, sin_t, batch, seq, lc):
    wb = WA_BLOCK
    hd = HEAD_DIM
    nb = seq // wb
    kvh = 2
    return pl.pallas_call(
        functools.partial(_wa_kernel, seq=seq),
        grid=(batch, kvh, nb),
        in_specs=[pl.BlockSpec(memory_space=pltpu.SMEM),
                  pl.BlockSpec((wb, 2 * hd), lambda b, g, n: (b * nb + n, WA_Q // 2 + g)),
                  pl.BlockSpec((seq, hd), lambda b, g, n: (b, WA_K + g)),
                  pl.BlockSpec((seq, hd), lambda b, g, n: (b, WA_V + g)),
                  pl.BlockSpec((lc, hd), lambda b, g, n: (b, WA_K + g)),
                  pl.BlockSpec((lc, hd), lambda b, g, n: (b, WA_V + g)),
                  pl.BlockSpec((2, hd), lambda b, g, n: (0, 0)),
                  pl.BlockSpec((wb, hd), lambda b, g, n: (n, 0)),
                  pl.BlockSpec((wb, hd), lambda b, g, n: (n, 0)),
                  pl.BlockSpec((seq, hd), lambda b, g, n: (0, 0)),
                  pl.BlockSpec((seq, hd), lambda b, g, n: (0, 0))],
        out_specs=pl.BlockSpec((wb, 2 * hd), lambda b, g, n: (b * nb + n, g)),
        out_shape=jax.ShapeDtypeStruct((batch * seq, 2 * kvh * hd), BF16),
        scratch_shapes=[pltpu.VMEM((seq + 2 * wb, hd), BF16), pltpu.VMEM((seq + 2 * wb, hd), BF16),
                        pltpu.VMEM((lc, hd), BF16), pltpu.VMEM((lc, hd), BF16)],
        compiler_params=_cparams(("arbitrary", "arbitrary", "arbitrary")),
        name="wa_attn",
    )(sink, p_lat, p_lat, p_lat, p_ctx, p_ctx, gain, cos_t, sin_t, cos_t, sin_t)


def _dn_prep_kernel(x_ref, prev_ref, next_ref, w_ref, o_ref, *, n_tiles):
    t = pl.program_id(1)
    tl = x_ref.shape[0]
    x = x_ref[...]
    prev = jnp.where(t > 0, prev_ref[...], 0.0)
    nxt = jnp.where(t < n_tiles - 1, next_ref[...], 0.0)
    xe = jnp.concatenate([prev, x, nxt], axis=0)
    n = tl + 16
    taps = w_ref.shape[0]
    acc = jnp.zeros_like(x)
    for j in range(taps):
        off = 8 - taps // 2 + j
        shifted = xe if off % n == 0 else pltpu.roll(xe, n - off, axis=0)
        acc += shifted[0:tl] * w_ref[j:j + 1, :]
    y = _silu(acc)
    hd = HEAD_DIM
    nh = y.shape[1] // (3 * hd)
    for blk in range(3 * nh):
        yb = y[:, blk * hd:(blk + 1) * hd]
        if blk < 2 * nh:
            yb = yb * lax.rsqrt(jnp.sum(yb * yb, axis=-1, keepdims=True) + EPS)
            if blk < nh:
                yb = yb * hd ** -0.5
        o_ref[:, blk * hd:(blk + 1) * hd] = yb


def _dn_prep_call(p, conv_w, batch, length, tl):
    width = conv_w.shape[1]
    n_tiles = length // tl
    per8 = tl // 8
    last8 = batch * length // 8 - 1
    return pl.pallas_call(
        functools.partial(_dn_prep_kernel, n_tiles=n_tiles),
        grid=(batch, n_tiles),
        in_specs=[pl.BlockSpec((tl, width), lambda b, t: (b * n_tiles + t, 1)),
                  pl.BlockSpec((8, width), lambda b, t: (jnp.maximum((b * n_tiles + t) * per8 - 1, 0), 1)),
                  pl.BlockSpec((8, width), lambda b, t: (jnp.minimum((b * n_tiles + t + 1) * per8, last8), 1)),
                  pl.BlockSpec((conv_w.shape[0], width), lambda b, t: (0, 0))],
        out_specs=pl.BlockSpec((tl, width), lambda b, t: (b * n_tiles + t, 0)),
        out_shape=jax.ShapeDtypeStruct((batch * length, width), F32),
        compiler_params=_cparams(("arbitrary", "arbitrary")),
        name="dn_prep",
    )(p, p, p, conv_w)


def _split_bf16(a):
    hi = a.astype(BF16)
    return hi, (a - hi.astype(F32)).astype(BF16)


def _solve_dot(a, b, mode):
    if mode == "highest":
        return _dot(a, b, lax.Precision.HIGHEST)
    if mode == "bf16":
        return _dot(a.astype(BF16), b.astype(BF16))
    a_hi, a_lo = _split_bf16(a)
    b_hi, b_lo = _split_bf16(b)
    return _dot(a_hi, b_hi) + (_dot(a_hi, b_lo) + _dot(a_lo, b_hi))


def _dn_pass(q_ref, k_ref, v_ref, og_ref, gates_ref, gadd, gmul, norm_g, y_ref, length, h, state,
             lin_scr, add_scr, egl_scr, acc_scr, solve_precision):
    assert CHUNK == HEAD_DIM
    n_chunks = length // CHUNK

    acc_scr[0:length, :] = jnp.zeros((length, HEAD_DIM), F32)

    shp = (CHUNK, CHUNK)
    ri = _iota(shp, 0)
    ci = _iota(shp, 1)
    masks = ((ci <= ri, ci < ri), (ci >= ri, ci > ri))
    couple = ([], [])
    s = 1
    while s < CHUNK:
        same = (ri // (2 * s)) == (ci // (2 * s))
        couple[0].append(same & ((ri // s) % 2 == 1) & ((ci // s) % 2 == 0))
        couple[1].append(same & ((ri // s) % 2 == 0) & ((ci // s) % 2 == 1))
        s *= 2
    group = min(n_chunks, PREP_GROUP)

    def prep(gi, carry):
        chains = []
        for g in range(group):
            c = gi * group + g
            r0 = pl.multiple_of(c * CHUNK, CHUNK)
            q = q_ref[pl.ds(r0, CHUNK), :]
            k = k_ref[pl.ds(r0, CHUNK), :]
            raw = gates_ref[pl.ds(r0, CHUNK), :]
            gb = jnp.where(_iota(raw.shape, 1) < LANE_ALPHA, jax.nn.sigmoid(raw), -gmul * _softplus(raw + gadd))
            k16 = k.astype(BF16)
            kk = _dot_nt(k16, k16)
            qk = _dot_nt(q.astype(BF16), k16)
            for d in range(2):
                incl, strict = masks[d]
                beta_col = _lane_pick(gb, LANE_BETA + 4 * d + h)
                g_col = _lane_pick(gb, LANE_ALPHA + 4 * d + h)
                beta_row = _to_row(beta_col)
                g_row = _to_row(g_col)
                cum_col = jnp.sum(jnp.where(incl, g_row, 0.0), axis=-1, keepdims=True)
                cum_row = _to_row(cum_col)
                tot = jnp.sum(g_col, axis=0, keepdims=True)
                decay = jnp.exp(jnp.where(incl, cum_col - cum_row, NEG_INF))
                lm = jnp.where(strict, kk * beta_col * decay, 0.0)
                egl_scr[d, pl.ds(pl.multiple_of(c * 8, 8), 8), :] = jnp.broadcast_to(jnp.exp(tot), (8, HEAD_DIM))
                chains.append((d, c, r0, lm, beta_row, beta_row * jnp.exp(cum_row), (qk * decay).astype(BF16),
                               q * jnp.exp(cum_col), k * jnp.exp(tot - cum_col)))
        eye = (ri == ci).astype(F32)
        xs = [eye - jnp.where(couple[ch[0]][0], ch[3], 0.0) for ch in chains]
        for lvl in range(1, len(couple[0])):
            cs = [jnp.where(couple[ch[0]][lvl], ch[3], 0.0) for ch in chains]
            xc = [_solve_dot(x, c, solve_precision) for x, c in zip(xs, cs)]
            xs = [x - _solve_dot(t, x, solve_precision) for x, t in zip(xs, xc)]
        u16 = [_solve_dot(ainv * ch[4], v_ref[pl.ds(ch[2], CHUNK), :], solve_precision).astype(BF16)
               for ch, ainv in zip(chains, xs)]
        w16 = [_solve_dot(ainv * ch[5], k_ref[pl.ds(ch[2], CHUNK), :], solve_precision).astype(BF16)
               for ch, ainv in zip(chains, xs)]
        ket = [ch[8].T.astype(BF16) for ch in chains]
        s_lin = [_dot(kt, w) for kt, w in zip(ket, w16)]
        s_add = [_dot(kt, u) for kt, u in zip(ket, u16)]
        o_lin = [ch[7] - _dot(ch[6], w) for ch, w in zip(chains, w16)]
        o_add = [_dot(ch[6], u) for ch, u in zip(chains, u16)]
        for ch, sl, sa, ol, oa in zip(chains, s_lin, s_add, o_lin, o_add):
            d = ch[0]
            base = pl.multiple_of(ch[1] * 2 * CHUNK, 2 * CHUNK)
            lin_scr[d, pl.ds(base, CHUNK), :] = sl.astype(BF16)
            lin_scr[d, pl.ds(base + CHUNK, CHUNK), :] = ol.astype(BF16)
            add_scr[d, pl.ds(base, CHUNK), :] = sa
            add_scr[d, pl.ds(base + CHUNK, CHUNK), :] = oa
        return carry

    lax.fori_loop(0, n_chunks // group, prep, 0)

    def scan(s, st):
        new = []
        for d in range(2):
            c = s if d == 0 else n_chunks - 1 - s
            base = pl.multiple_of(c * 2 * CHUNK, 2 * CHUNK)
            sm = st[d]
            prod = _dot(lin_scr[d, pl.ds(base, 2 * CHUNK), :], sm.astype(BF16))
            add = add_scr[d, pl.ds(base, 2 * CHUNK), :]
            acc_scr[pl.ds(pl.multiple_of(c * CHUNK, CHUNK), CHUNK), :] += prod[CHUNK:] + add[CHUNK:]
            egl = egl_scr[d, pl.ds(pl.multiple_of(c * 8, 8), 8), :][0:1, :]
            new.append(sm * egl - prod[:CHUNK] + add[:CHUNK])
        return tuple(new)

    state = lax.fori_loop(0, n_chunks, scan, state, unroll=SCAN_UNROLL)

    y = _rms(acc_scr[0:length, :], norm_g) * _silu(og_ref[...])
    y_ref[...] = y.astype(BF16)
    return state


def _dn_kernel(q_ref, k_ref, v_ref, og_ref, gates_ref, qc_ref, kc_ref, vc_ref, ogc_ref, gatesc_ref,
               gadd_ref, gmul_ref, norm_ref, y_ref, yc_ref,
               lin_scr, add_scr, egl_scr, acc_scr, *, seq, lc, solve_precision):
    h = pl.program_id(1)
    gadd = gadd_ref[...]
    gmul = gmul_ref[...]
    norm_g = norm_ref[...]
    scr = (lin_scr, add_scr, egl_scr, acc_scr)
    zero = jnp.zeros((HEAD_DIM, HEAD_DIM), F32)
    state = _dn_pass(qc_ref, kc_ref, vc_ref, ogc_ref, gatesc_ref, gadd, gmul, norm_g, yc_ref, lc, h, (zero, zero),
                     *scr, solve_precision)
    _dn_pass(q_ref, k_ref, v_ref, og_ref, gates_ref, gadd, gmul, norm_g, y_ref, seq, h, state,
             *scr, solve_precision)


def _dn_call(dq_lat, dq_ctx, p_lat, p_ctx, pg_lat, pg_ctx, gadd, gmul, norm_g, batch, seq, lc, solve_precision):
    hd = HEAD_DIM
    heads = 4

    def blk(n, col):
        return pl.BlockSpec((n, hd), lambda b, h: (b, col + h))

    def gate_blk(n):
        return pl.BlockSpec((n, hd), lambda b, h: (b, 0))

    vec = pl.BlockSpec((1, hd), lambda b, h: (0, 0))
    return pl.pallas_call(
        functools.partial(_dn_kernel, seq=seq, lc=lc, solve_precision=solve_precision),
        grid=(batch, heads),
        in_specs=[blk(seq, 0), blk(seq, heads), blk(seq, 2 * heads), blk(seq, DN_OG), gate_blk(seq),
                  blk(lc, 0), blk(lc, heads), blk(lc, 2 * heads), blk(lc, DN_OG), gate_blk(lc),
                  vec, vec, vec],
        out_specs=[pl.BlockSpec((seq, hd), lambda b, h: (b, h)),
                   pl.BlockSpec((lc, hd), lambda b, h: (b, h))],
        out_shape=[jax.ShapeDtypeStruct((batch * seq, heads * hd), BF16),
                   jax.ShapeDtypeStruct((batch * lc, heads * hd), BF16)],
        scratch_shapes=[pltpu.VMEM((2, 2 * seq, hd), BF16),
                        pltpu.VMEM((2, 2 * seq, hd), F32),
                        pltpu.VMEM((2, seq // CHUNK * 8, hd), F32),
                        pltpu.VMEM((seq, hd), F32)],
        compiler_params=_cparams(("arbitrary", "arbitrary"), DN_VMEM_LIMIT),
        name="deltanet",
    )(dq_lat, dq_lat, dq_lat, p_lat, pg_lat, dq_ctx, dq_ctx, dq_ctx, p_ctx, pg_ctx, gadd, gmul, norm_g)


def _ml_pass(q_ref, k_ref, v_ref, og_ref, gates_ref, gadd, norm_g, y_ref, length, h, state, acc_scr):
    n_chunks = length // CHUNK
    acc_scr[0:length, :] = jnp.zeros((length, HEAD_DIM), F32)
    shp = (CHUNK, CHUNK)
    ri = _iota(shp, 0)
    ci = _iota(shp, 1)
    masks = (ci <= ri, ci >= ri)
    head_lanes = (_iota((1, HEAD_DIM), 1) // ML_QK) == (h % 2)

    group = min(n_chunks, ML_GROUP)

    def each(f, *lists):
        return [f(*a) for a in zip(*lists)]

    def step(gi, st):
        dirs, r0s = [], []
        for j in range(group):
            s = gi * group + j
            for d in range(2):
                c = s if d == 0 else n_chunks - 1 - s
                dirs.append(d)
                r0s.append(pl.multiple_of(c * CHUNK, CHUNK))
        incl = [masks[d] for d in dirs]
        q = [jnp.where(head_lanes, q_ref[pl.ds(r0, CHUNK), :], 0.0) for r0 in r0s]
        k = [jnp.where(head_lanes, k_ref[pl.ds(r0, CHUNK), :], 0.0) * ML_QK ** -0.5 for r0 in r0s]
        v16 = [v_ref[pl.ds(r0, CHUNK), :].astype(BF16) for r0 in r0s]
        gb = [gates_ref[pl.ds(r0, CHUNK), :] + gadd for r0 in r0s]
        ig_col = [_lane_pick(g, LANE_IG + 4 * d + h) for g, d in zip(gb, dirs)]
        lf_col = [_log_sigmoid(_lane_pick(g, LANE_FG + 4 * d + h)) for g, d in zip(gb, dirs)]
        ig_row = each(_to_row, ig_col)
        lf_row = each(_to_row, lf_col)
        b_col = each(lambda m, r: jnp.sum(jnp.where(m, r, 0.0), axis=-1, keepdims=True), incl, lf_row)
        b_row = each(_to_row, b_col)
        b_last = each(lambda x: jnp.sum(x, axis=0, keepdims=True), lf_col)
        g_end = each(lambda bl, b, i: bl - b + i, b_last, b_col, ig_col)
        m_loc = each(lambda x: jnp.max(x, axis=0, keepdims=True), g_end)
        kw = each(lambda kk, g, m: kk * jnp.exp(g - m), k, g_end, m_loc)
        q16 = each(lambda x: x.astype(BF16), q)
        k16 = each(lambda x: x.astype(BF16), k)
        c_loc = each(lambda a, b: _dot_tn(a.astype(BF16), b), kw, v16)
        n_loc = each(lambda x: jnp.sum(x, axis=0, keepdims=True), kw)
        log_d = each(lambda m, bc, br, ir: jnp.where(m, bc - br + ir, NEG_INF), incl, b_col, b_row, ig_row)
        mx_col = each(lambda x: jnp.max(x, axis=-1, keepdims=True), log_d)
        s0 = each(lambda a, b, ld, mx: _dot_nt(a, b) * jnp.exp(ld - mx), q16, k16, log_d, mx_col)
        sv0 = each(lambda s, vv: _dot(s.astype(BF16), vv), s0, v16)
        rs0 = each(lambda s: jnp.sum(s, axis=-1, keepdims=True), s0)
        st = list(st)
        for i, (d, r0) in enumerate(zip(dirs, r0s)):
            c_st, n_st, m_st = st[d]
            m_inter = b_col[i] + m_st
            m_t = jnp.maximum(mx_col[i], m_inter)
            intra = jnp.exp(mx_col[i] - m_t)
            inter = jnp.exp(m_inter - m_t)
            num = intra * sv0[i] + inter * _dot(q16[i], c_st.astype(BF16))
            den = intra * rs0[i] + inter * jnp.sum(q[i] * n_st, axis=-1, keepdims=True)
            acc_scr[pl.ds(r0, CHUNK), :] += num / jnp.maximum(jnp.abs(den), jnp.exp(-m_t))
            m_new = jnp.maximum(b_last[i] + m_st, m_loc[i])
            a = jnp.exp(b_last[i] + m_st - m_new)
            e = jnp.exp(m_loc[i] - m_new)
            st[d] = (a * c_st + e * c_loc[i], a * n_st + e * n_loc[i], m_new)
        return tuple(st)

    state = lax.fori_loop(0, n_chunks // group, step, state)
    y = jax.nn.sigmoid(og_ref[...]) * _rms(acc_scr[0:length, :], norm_g)
    y_ref[...] = y.astype(BF16)
    return state


def _ml_kernel(q_ref, k_ref, v_ref, og_ref, gates_ref, qc_ref, kc_ref, vc_ref, ogc_ref, gatesc_ref,
               gadd_ref, norm_ref, y_ref, yc_ref, acc_scr, *, seq, lc):
    h = pl.program_id(1)
    gadd = gadd_ref[...]
    norm_g = norm_ref[0]
    zero = (jnp.zeros((HEAD_DIM, HEAD_DIM), F32), jnp.zeros((1, HEAD_DIM), F32), jnp.zeros((1, 1), F32))
    state = _ml_pass(qc_ref, kc_ref, vc_ref, ogc_ref, gatesc_ref, gadd, norm_g, yc_ref, lc, h, (zero, zero), acc_scr)
    _ml_pass(q_ref, k_ref, v_ref, og_ref, gates_ref, gadd, norm_g, y_ref, seq, h, state, acc_scr)


def _ml_call(p_lat, p_ctx, pg_lat, pg_ctx, gadd, norm_g, batch, seq, lc):
    hd = HEAD_DIM
    heads = 4

    def specs(n):
        return [pl.BlockSpec((n, hd), lambda b, h: (b, ML_Q + h // 2)),
                pl.BlockSpec((n, hd), lambda b, h: (b, ML_K + h // 2)),
                pl.BlockSpec((n, hd), lambda b, h: (b, ML_V + h)),
                pl.BlockSpec((n, hd), lambda b, h: (b, ML_OG + h)),
                pl.BlockSpec((n, hd), lambda b, h: (b, 0))]

    return pl.pallas_call(
        functools.partial(_ml_kernel, seq=seq, lc=lc),
        grid=(batch, heads),
        in_specs=specs(seq) + specs(lc) + [pl.BlockSpec((1, hd), lambda b, h: (0, 0)),
                                            pl.BlockSpec((1, 1, hd), lambda b, h: (h, 0, 0))],
        out_specs=[pl.BlockSpec((seq, hd), lambda b, h: (b, h)),
                   pl.BlockSpec((lc, hd), lambda b, h: (b, h))],
        out_shape=[jax.ShapeDtypeStruct((batch * seq, heads * hd), BF16),
                   jax.ShapeDtypeStruct((batch * lc, heads * hd), BF16)],
        scratch_shapes=[pltpu.VMEM((seq, hd), F32)],
        compiler_params=_cparams(("arbitrary", "arbitrary")),
        name="mlstm",
    )(p_lat, p_lat, p_lat, p_lat, pg_lat, p_ctx, p_ctx, p_ctx, p_ctx, pg_ctx, gadd, norm_g.reshape(heads, 1, hd))


def _regroup_kernel(w_ref, main_ref, gate_ref, *, cuts):
    x = w_ref[0]
    (a0, a1), (b0, b1), (c0, c1), (g0, g1), (h0, h1) = cuts
    main_ref[0, :, 0:a1 - a0] = x[:, a0:a1].astype(BF16)
    main_ref[0, :, a1 - a0:a1 - a0 + b1 - b0] = x[:, b0:b1].astype(BF16)
    main_ref[0, :, a1 - a0 + b1 - b0:] = x[:, c0:c1].astype(BF16)
    pad = jnp.zeros((x.shape[0], 128 - (g1 - g0) - (h1 - h0)), F32)
    gate_ref[0] = jnp.concatenate([x[:, g0:g1], x[:, h0:h1], pad], axis=1).astype(BF16)


def _regroup_call(w_in):
    depth, d, n = w_in.shape
    gw = 512
    dn_gate0 = 3 * gw + 4 * gw
    ml0 = dn_gate0 + 16
    ml_gate0 = ml0 + 3 * gw
    wa0 = ml_gate0 + 16
    cuts = ((0, dn_gate0), (ml0, ml_gate0), (wa0, n), (dn_gate0, ml0), (ml_gate0, wa0))
    tr = 256
    return pl.pallas_call(
        functools.partial(_regroup_kernel, cuts=cuts),
        grid=(depth, d // tr),
        in_specs=[pl.BlockSpec((1, tr, n), lambda l, i: (l, i, 0))],
        out_specs=[pl.BlockSpec((1, tr, MAIN_COLS), lambda l, i: (l, i, 0)),
                   pl.BlockSpec((1, tr, 128), lambda l, i: (l, i, 0))],
        out_shape=[jax.ShapeDtypeStruct((depth, d, MAIN_COLS), BF16), jax.ShapeDtypeStruct((depth, d, 128), BF16)],
        compiler_params=_cparams(("arbitrary", "arbitrary")),
        name="regroup_w_in",
    )(w_in)


def _rope_tables(seq):
    t = jnp.arange(seq)
    n_freq = HEAD_DIM // 4
    inv_freq = ROPE_THETA ** (-jnp.arange(n_freq, dtype=F32) / n_freq)
    pos = jnp.stack([t // GRID_W, t % GRID_W], axis=-1).astype(F32)
    ang = pos[:, :, None] * inv_freq
    cos, sin = jnp.cos(ang), jnp.sin(ang)
    cos_t = jnp.concatenate([cos, cos], axis=-1).reshape(seq, HEAD_DIM)
    sin_t = jnp.concatenate([-sin, sin], axis=-1).reshape(seq, HEAD_DIM)
    return cos_t, sin_t


def _gate_vectors(dt_bias, a_log, i_bias, f_bias):
    zeros8 = jnp.zeros((8,), F32)
    gadd = jnp.concatenate([zeros8, dt_bias.reshape(-1), i_bias.reshape(-1), f_bias.reshape(-1),
                            jnp.zeros((96,), F32)]).astype(F32)
    gmul = jnp.concatenate([zeros8, jnp.exp(a_log.astype(F32)).reshape(-1), jnp.zeros((112,), F32)])
    return gadd.reshape(1, 128), gmul.reshape(1, 128)


def _forward(x, c, ctx, c_ctx, w_ada, b_ada, norm_mix, norm_ffn, w_in, w_out, na_qk_gain, na_rpb, dn_conv,
             dn_a_log, dn_dt_bias, dn_norm, ml_i_bias, ml_f_bias, ml_norm, wa_qk_gain, wa_sink, w_ffn_in,
             w_ffn_out, solve_precision):
    batch, seq, d = x.shape
    lc = ctx.shape[1]
    depth = w_ada.shape[0]
    tm = 512
    tn_ffn = 512

    cos_t, sin_t = _rope_tables(seq)
    na_bias = _na_col_tables(na_rpb)
    cs = jnp.concatenate([c, c_ctx[None, :], jnp.zeros((8 - batch - 1, d), F32)], axis=0)
    mods = _ada_call(cs, w_ada, b_ada).reshape(depth, 8, 6, d)
    mods = jnp.pad(mods, ((0, 0), (0, 0), (0, 2), (0, 0)))

    w_main, w_gate = _regroup_call(w_in)
    w_o = w_out.astype(BF16)
    w_f1 = w_ffn_in.astype(BF16)
    w_f2 = w_ffn_out.astype(BF16)

    xl = x.reshape(batch * seq, d)
    xc = ctx.reshape(batch * lc, d)
    hl = _norm_call(xl, mods[0, :batch], norm_mix[0:1], seq, tm)
    hc = _norm_call(xc, mods[0, batch:batch + 1], norm_mix[0:1], batch * lc, tm)

    for l in range(depth):
        need_ctx = l < depth - 1
        mod_l, mod_c = mods[l, :batch], mods[l, batch:batch + 1]
        p_lat = _matmul_call(hl, w_main, l, tm, 1024, "inproj")
        pg_lat = _matmul_call(hl, w_gate, l, tm, 128, "inproj_gates")
        p_ctx = _matmul_call(hc, w_main, l, tm, 1024, "inproj_ctx")
        pg_ctx = _matmul_call(hc, w_gate, l, tm, 128, "inproj_gates_ctx")

        ya = _na_call(p_lat, p_ctx, na_qk_gain[l], na_bias, l, batch, seq, lc)
        gadd, gmul = _gate_vectors(dn_dt_bias[l], dn_a_log[l], ml_i_bias[l], ml_f_bias[l])
        dq_lat = _dn_prep_call(p_lat, dn_conv[l], batch, seq, 256)
        dq_ctx = _dn_prep_call(p_ctx, dn_conv[l], batch, lc, 256)
        yb, ycb = _dn_call(dq_lat, dq_ctx, p_lat, p_ctx, pg_lat, pg_ctx, gadd, gmul, dn_norm[l].reshape(1, -1),
                           batch, seq, lc, solve_precision)
        ym, ycm = _ml_call(p_lat, p_ctx, pg_lat, pg_ctx, gadd, ml_norm[l], batch, seq, lc)
        yw = _wa_call(p_lat, p_ctx, wa_qk_gain[l], wa_sink[l], cos_t, sin_t, batch, seq, lc)

        nl = min(l + 1, depth - 1)
        xl, h2 = _outproj_call((ya, yb, ym, yw), w_o, l, xl, mod_l, norm_ffn[l:l + 1], seq, tm)
        xl, hl = _ffn_call(h2, w_f1, w_f2, l, xl, mod_l, mods[nl, :batch], norm_mix[nl:nl + 1], seq, tm, tn_ffn,
                           need_ctx)
        if need_ctx:
            yca = _ctx_attn_call(p_ctx, na_qk_gain[l], wa_sink[l], batch, lc, 4, 1, NA_Q, NA_K, NA_V, False,
                                 "na_ctx_attn")
            ycw = _ctx_attn_call(p_ctx, wa_qk_gain[l], wa_sink[l], batch, lc, 4, 2, WA_Q, WA_K, WA_V, True,
                                 "wa_ctx_attn")
            xc, h2c = _outproj_call((yca, ycb, ycm, ycw), w_o, l, xc, mod_c, norm_ffn[l:l + 1], batch * lc, tm)
            xc, hc = _ffn_call(h2c, w_f1, w_f2, l, xc, mod_c, mods[nl, batch:batch + 1], norm_mix[nl:nl + 1],
                               batch * lc, tm, tn_ffn, True)
    return xl.reshape(batch, seq, d)


def kernel(x, c, ctx, c_ctx, w_ada, b_ada, norm_mix, norm_ffn, w_in, w_out, na_qk_gain, na_rpb, dn_conv, dn_a_log,
           dn_dt_bias, dn_norm, ml_i_bias, ml_f_bias, ml_norm, wa_qk_gain, wa_sink, w_ffn_in, w_ffn_out):
    return _forward(x, c, ctx, c_ctx, w_ada, b_ada, norm_mix, norm_ffn, w_in, w_out, na_qk_gain, na_rpb, dn_conv,
                    dn_a_log, dn_dt_bias, dn_norm, ml_i_bias, ml_f_bias, ml_norm, wa_qk_gain, wa_sink, w_ffn_in,
                    w_ffn_out, "bf16")
```

```python
import functools

import jax
import jax.numpy as jnp
from jax import lax
from jax.experimental import pallas as pl
from jax.experimental.pallas import tpu as pltpu

F32 = jnp.float32
BF16 = jnp.bfloat16

HEAD_DIM = 128
GRID_W = 64
CHUNK = 128
PREP_GROUP = 4
ML_GROUP = 4
SCAN_UNROLL = 2
NA_WIN_ROWS = 8
NA_WIN_COLS = 16
NA_QROWS = 4
NA_KROWS = NA_QROWS + NA_WIN_ROWS
WA_BLOCK = 128
ML_QK = 64
ROPE_THETA = 10000.0
EPS = 1e-6
NEG_INF = float("-inf")
VMEM_LIMIT = 48 * 1024 * 1024
DN_VMEM_LIMIT = 56 * 1024 * 1024

NA_Q, NA_K, NA_V = 0, 4, 8
DN_OG = 24
ML_Q, ML_K, ML_V, ML_OG = 28, 30, 32, 36
WA_Q, WA_K, WA_V = 40, 44, 46
MAIN_COLS = 48 * 128
LANE_BETA, LANE_ALPHA, LANE_IG, LANE_FG = 0, 8, 16, 24


def _cparams(sem, vmem_limit=VMEM_LIMIT):
    return pltpu.CompilerParams(dimension_semantics=sem, vmem_limit_bytes=vmem_limit)


def _rms(x, gain):
    return x * lax.rsqrt(jnp.mean(x * x, axis=-1, keepdims=True) + EPS) * gain


def _silu(x):
    return x * jax.nn.sigmoid(x)


def _softplus(x):
    return jnp.maximum(x, 0.0) + jnp.log1p(jnp.exp(-jnp.abs(x)))


def _log_sigmoid(x):
    return jnp.minimum(x, 0.0) - jnp.log1p(jnp.exp(-jnp.abs(x)))


def _dot(a, b, precision=None):
    return jnp.dot(a, b, preferred_element_type=F32, precision=precision)


def _dot_nt(a, b):
    return lax.dot_general(a, b, (((1,), (1,)), ((), ())), preferred_element_type=F32)


def _dot_tn(a, b):
    return lax.dot_general(a, b, (((0,), (0,)), ((), ())), preferred_element_type=F32)


def _iota(shape, dim):
    return lax.broadcasted_iota(jnp.int32, shape, dim)


def _lane_pick(x, lane):
    return jnp.sum(jnp.where(_iota(x.shape, 1) == lane, x, 0.0), axis=-1, keepdims=True)


def _to_row(col):
    n = col.shape[0]
    eye = _iota((n, n), 0) == _iota((n, n), 1)
    return jnp.sum(jnp.where(eye, col, 0.0), axis=0, keepdims=True)


def _ada_kernel(c_ref, w_ref, b_ref, o_ref):
    a = _silu(c_ref[...]).astype(BF16)
    o_ref[0] = _dot(a, w_ref[0].astype(BF16)) + b_ref[0]


def _ada_call(cs, w_ada, b_ada):
    depth, d, n = w_ada.shape
    tn = 1024
    return pl.pallas_call(
        _ada_kernel,
        grid=(depth, n // tn),
        in_specs=[pl.BlockSpec((8, d), lambda l, j: (0, 0)),
                  pl.BlockSpec((1, d, tn), lambda l, j: (l, 0, j)),
                  pl.BlockSpec((1, 1, tn), lambda l, j: (l, 0, j))],
        out_specs=pl.BlockSpec((1, 8, tn), lambda l, j: (l, 0, j)),
        out_shape=jax.ShapeDtypeStruct((depth, 8, n), F32),
        compiler_params=_cparams(("arbitrary", "arbitrary")),
        name="ada",
    )(cs, w_ada, b_ada.reshape(depth, 1, n))


def _norm_kernel(x_ref, mod_ref, g_ref, h_ref):
    mod = mod_ref[0]
    h = _rms(x_ref[...], g_ref[...])
    h_ref[...] = (h * (1.0 + mod[1:2]) + mod[0:1]).astype(BF16)


def _norm_call(x, mod, gain, rows_per_mod, tm):
    m, d = x.shape
    per = rows_per_mod // tm
    return pl.pallas_call(
        _norm_kernel,
        grid=(m // tm,),
        in_specs=[pl.BlockSpec((tm, d), lambda i: (i, 0)),
                  pl.BlockSpec((1, 8, d), lambda i: (i // per, 0, 0)),
                  pl.BlockSpec((1, d), lambda i: (0, 0))],
        out_specs=pl.BlockSpec((tm, d), lambda i: (i, 0)),
        out_shape=jax.ShapeDtypeStruct((m, d), BF16),
        compiler_params=_cparams(("arbitrary",)),
        name="norm_mod",
    )(x, mod, gain)


def _matmul_kernel(a_ref, wt_ref, o_ref):
    o_ref[...] = _dot_nt(a_ref[...], wt_ref[0])


def _matmul_call(a, wt, layer, tm, tn, name):
    m, k = a.shape
    n = wt.shape[1]
    return pl.pallas_call(
        _matmul_kernel,
        grid=(n // tn, m // tm),
        in_specs=[pl.BlockSpec((tm, k), lambda j, i: (i, 0)),
                  pl.BlockSpec((1, tn, k), lambda j, i: (layer, j, 0))],
        out_specs=pl.BlockSpec((tm, tn), lambda j, i: (i, j)),
        out_shape=jax.ShapeDtypeStruct((m, n), F32),
        compiler_params=_cparams(("arbitrary", "arbitrary")),
        name=name,
    )(a, wt)


def _outproj_kernel(ya_ref, yb_ref, ym_ref, yw_ref, w_ref, x_ref, mod_ref, g_ref, xo_ref, h_ref):
    gw = ya_ref.shape[1]
    acc = _dot(ya_ref[...], w_ref[0, 0:gw, :])
    acc += _dot(yb_ref[...], w_ref[0, gw:2 * gw, :])
    acc += _dot(ym_ref[...], w_ref[0, 2 * gw:3 * gw, :])
    acc += _dot(yw_ref[...], w_ref[0, 3 * gw:4 * gw, :])
    mod = mod_ref[0]
    xn = x_ref[...] + mod[2:3] * acc
    xo_ref[...] = xn
    h = _rms(xn, g_ref[...])
    h_ref[...] = (h * (1.0 + mod[4:5]) + mod[3:4]).astype(BF16)


def _outproj_call(ys, w, layer, x, mod, gain, rows_per_mod, tm):
    m, d = x.shape
    gw = ys[0].shape[1]
    per = rows_per_mod // tm
    yspec = pl.BlockSpec((tm, gw), lambda i: (i, 0))
    return pl.pallas_call(
        _outproj_kernel,
        grid=(m // tm,),
        in_specs=[yspec, yspec, yspec, yspec,
                  pl.BlockSpec((1, 4 * gw, d), lambda i: (layer, 0, 0)),
                  pl.BlockSpec((tm, d), lambda i: (i, 0)),
                  pl.BlockSpec((1, 8, d), lambda i: (i // per, 0, 0)),
                  pl.BlockSpec((1, d), lambda i: (0, 0))],
        out_specs=[pl.BlockSpec((tm, d), lambda i: (i, 0)),
                   pl.BlockSpec((tm, d), lambda i: (i, 0))],
        out_shape=[jax.ShapeDtypeStruct((m, d), F32), jax.ShapeDtypeStruct((m, d), BF16)],
        compiler_params=_cparams(("arbitrary",)),
        name="outproj",
    )(*ys, w, x, mod, gain)


def _ffn_kernel(h_ref, wg_ref, wu_ref, w2_ref, x_ref, mod_ref, modn_ref, gn_ref, xo_ref, hn_ref, acc_ref,
                *, with_next):
    j = pl.program_id(1)

    @pl.when(j == 0)
    def _():
        acc_ref[...] = jnp.zeros_like(acc_ref)

    h = h_ref[...]
    act = _silu(_dot(h, wg_ref[0])) * _dot(h, wu_ref[0])
    acc_ref[...] += _dot(act.astype(BF16), w2_ref[0])

    @pl.when(j == pl.num_programs(1) - 1)
    def _():
        xn = x_ref[...] + mod_ref[0][5:6] * acc_ref[...]
        xo_ref[...] = xn
        if with_next:
            modn = modn_ref[0]
            hn = _rms(xn, gn_ref[...])
            hn_ref[...] = (hn * (1.0 + modn[1:2]) + modn[0:1]).astype(BF16)
        else:
            hn_ref[...] = jnp.zeros_like(hn_ref)


def _ffn_call(h, w_in, w_out, layer, x, mod, mod_next, gain_next, rows_per_mod, tm, tn, with_next):
    m, d = x.shape
    hidden = w_out.shape[1]
    nj = hidden // tn
    per = rows_per_mod // tm
    return pl.pallas_call(
        functools.partial(_ffn_kernel, with_next=with_next),
        grid=(m // tm, nj),
        in_specs=[pl.BlockSpec((tm, d), lambda i, j: (i, 0)),
                  pl.BlockSpec((1, d, tn), lambda i, j: (layer, 0, j)),
                  pl.BlockSpec((1, d, tn), lambda i, j: (layer, 0, nj + j)),
                  pl.BlockSpec((1, tn, d), lambda i, j: (layer, j, 0)),
                  pl.BlockSpec((tm, d), lambda i, j: (i, 0)),
                  pl.BlockSpec((1, 8, d), lambda i, j: (i // per, 0, 0)),
                  pl.BlockSpec((1, 8, d), lambda i, j: (i // per, 0, 0)),
                  pl.BlockSpec((1, d), lambda i, j: (0, 0))],
        out_specs=[pl.BlockSpec((tm, d), lambda i, j: (i, 0)),
                   pl.BlockSpec((tm, d), lambda i, j: (i, 0))],
        out_shape=[jax.ShapeDtypeStruct((m, d), F32), jax.ShapeDtypeStruct((m, d), BF16)],
        scratch_shapes=[pltpu.VMEM((tm, d), F32)],
        compiler_params=_cparams(("arbitrary", "arbitrary")),
        name="ffn",
    )(h, w_in, w_in, w_out, x, mod, mod_next, gain_next)


def _na_kernel(q_ref, k_ref, v_ref, kc_ref, vc_ref, gain_ref, colbias_ref, o_ref,
               kn_scr, v_scr, kcn_scr, vc_scr, bias_scr, *, rows):
    rb = pl.program_id(2)
    n_rb = pl.num_programs(2)

    @pl.when(rb == 0)
    def _():
        g1 = gain_ref[1:2, :]
        kn_scr[...] = _rms(k_ref[...], g1).astype(BF16)
        v_scr[...] = v_ref[...].astype(BF16)
        kcn_scr[...] = _rms(kc_ref[...], g1).astype(BF16)
        vc_scr[...] = vc_ref[...].astype(BF16)
        closed = jnp.full((GRID_W, GRID_W), NEG_INF, F32)
        for typ, (offs, inside) in enumerate(_na_row_plan(rows)):
            for i in range(NA_QROWS):
                for jp in range(NA_KROWS // 2):
                    halves = [colbias_ref[0, 0, offs[i][j]] if inside[i][j] else closed for j in (2 * jp, 2 * jp + 1)]
                    bias_scr[typ, i * GRID_W:(i + 1) * GRID_W, 2 * jp * GRID_W:(2 * jp + 2) * GRID_W] = (
                        jnp.concatenate(halves, axis=1))

    scale = HEAD_DIM ** -0.5
    nk = NA_KROWS * GRID_W
    qn = _rms(q_ref[...], gain_ref[0:1, :]).astype(BF16)
    ws = jnp.clip(NA_QROWS * rb - NA_WIN_ROWS // 2, 0, rows - NA_KROWS)
    start = pl.multiple_of(ws * GRID_W, GRID_W)
    kw = kn_scr[pl.ds(start, nk), :]
    vw = v_scr[pl.ds(start, nk), :]
    typ = jnp.where(rb == 0, 0, jnp.where(rb == n_rb - 1, 2, 1))
    bias = bias_scr[pl.ds(typ, 1)][0]
    s_loc = _dot_nt(qn, kw) * scale + bias
    s_ctx = _dot_nt(qn, kcn_scr[...]) * scale
    m = jnp.maximum(jnp.max(s_loc, axis=-1, keepdims=True), jnp.max(s_ctx, axis=-1, keepdims=True))
    p_loc = jnp.exp(s_loc - m)
    p_ctx = jnp.exp(s_ctx - m)
    den = jnp.sum(p_loc, axis=-1, keepdims=True) + jnp.sum(p_ctx, axis=-1, keepdims=True)
    o = _dot(p_loc.astype(BF16), vw) + _dot(p_ctx.astype(BF16), vc_scr[...])
    o_ref[...] = (o / den).astype(BF16)


def _na_row_plan(rows):
    n_rb = rows // NA_QROWS
    kh = NA_WIN_ROWS
    plan = []
    for rb in (0, 1, n_rb - 1):
        ws = min(max(NA_QROWS * rb - kh // 2, 0), rows - NA_KROWS)
        offs, inside = [], []
        for i in range(NA_QROWS):
            r = NA_QROWS * rb + i
            rs = min(max(r - kh // 2, 0), rows - kh)
            offs.append([min(max(ws + j - r, 1 - kh), kh - 1) + kh - 1 for j in range(NA_KROWS)])
            inside.append([rs <= ws + j < rs + kh for j in range(NA_KROWS)])
        plan.append((offs, inside))
    return plan


def _na_col_tables(rpb):
    kw = NA_WIN_COLS
    c = jnp.arange(GRID_W)
    cs = jnp.clip(c - kw // 2, 0, GRID_W - kw)
    in_col = (c[None, :] >= cs[:, None]) & (c[None, :] < cs[:, None] + kw)
    dc = jnp.clip(c[None, :] - c[:, None], 1 - kw, kw - 1) + kw - 1
    dc_hot = jax.nn.one_hot(dc, 2 * kw - 1, dtype=F32)
    val = jnp.einsum("lhab,ckb->lhack", rpb.astype(F32), dc_hot, precision=lax.Precision.HIGHEST)
    return jnp.where(in_col, val, NEG_INF)


def _na_call(p_lat, p_ctx, gain, colbias, layer, batch, seq, lc):
    rows = seq // GRID_W
    n_rb = rows // NA_QROWS
    tq = NA_QROWS * GRID_W
    nk = NA_KROWS * GRID_W
    heads = colbias.shape[1]
    n_off = colbias.shape[2]
    hd = HEAD_DIM
    return pl.pallas_call(
        functools.partial(_na_kernel, rows=rows),
        grid=(batch, heads, n_rb),
        in_specs=[pl.BlockSpec((tq, hd), lambda b, h, r: (b * n_rb + r, NA_Q + h)),
                  pl.BlockSpec((seq, hd), lambda b, h, r: (b, NA_K + h)),
                  pl.BlockSpec((seq, hd), lambda b, h, r: (b, NA_V + h)),
                  pl.BlockSpec((lc, hd), lambda b, h, r: (b, NA_K + h)),
                  pl.BlockSpec((lc, hd), lambda b, h, r: (b, NA_V + h)),
                  pl.BlockSpec((2, hd), lambda b, h, r: (0, 0)),
                  pl.BlockSpec((1, 1, n_off, GRID_W, GRID_W), lambda b, h, r: (layer, h, 0, 0, 0))],
        out_specs=pl.BlockSpec((tq, hd), lambda b, h, r: (b * n_rb + r, h)),
        out_shape=jax.ShapeDtypeStruct((batch * seq, heads * hd), BF16),
        scratch_shapes=[pltpu.VMEM((seq, hd), BF16), pltpu.VMEM((seq, hd), BF16),
                        pltpu.VMEM((lc, hd), BF16), pltpu.VMEM((lc, hd), BF16),
                        pltpu.VMEM((3, tq, nk), F32)],
        compiler_params=_cparams(("arbitrary", "arbitrary", "arbitrary")),
        name="na_attn",
    )(p_lat, p_lat, p_lat, p_ctx, p_ctx, gain, colbias)


def _ctx_attn_kernel(sink_ref, q_ref, k_ref, v_ref, gain_ref, o_ref, *, use_sink):
    scale = HEAD_DIM ** -0.5
    qn = _rms(q_ref[...], gain_ref[0:1, :]).astype(BF16)
    kn = _rms(k_ref[...], gain_ref[1:2, :]).astype(BF16)
    s = _dot_nt(qn, kn) * scale
    m = jnp.max(s, axis=-1, keepdims=True)
    if use_sink:
        sk = sink_ref[pl.program_id(1)]
        m = jnp.maximum(m, sk)
    p = jnp.exp(s - m)
    den = jnp.sum(p, axis=-1, keepdims=True)
    if use_sink:
        den = den + jnp.exp(sk - m)
    o = _dot(p.astype(BF16), v_ref[...].astype(BF16))
    o_ref[...] = (o / den).astype(BF16)


def _ctx_attn_call(p_ctx, gain, sink, batch, lc, heads, rep, qoff, koff, voff, use_sink, name):
    hd = HEAD_DIM
    return pl.pallas_call(
        functools.partial(_ctx_attn_kernel, use_sink=use_sink),
        grid=(batch, heads),
        in_specs=[pl.BlockSpec(memory_space=pltpu.SMEM),
                  pl.BlockSpec((lc, hd), lambda b, h: (b, qoff + h)),
                  pl.BlockSpec((lc, hd), lambda b, h: (b, koff + h // rep)),
                  pl.BlockSpec((lc, hd), lambda b, h: (b, voff + h // rep)),
                  pl.BlockSpec((2, hd), lambda b, h: (0, 0))],
        out_specs=pl.BlockSpec((lc, hd), lambda b, h: (b, h)),
        out_shape=jax.ShapeDtypeStruct((batch * lc, heads * hd), BF16),
        compiler_params=_cparams(("arbitrary", "arbitrary")),
        name=name,
    )(sink, p_ctx, p_ctx, p_ctx, gain)


def _rope(x, cos, sin_signed):
    lane = _iota(x.shape, 1)
    swapped = jnp.where(lane % 64 < 32, pltpu.roll(x, 96, axis=1), pltpu.roll(x, 32, axis=1))
    return x * cos + swapped * sin_signed


def _wa_kernel(sink_ref, q_ref, k_ref, v_ref, kc_ref, vc_ref, gain_ref, cq_ref, sq_ref, ck_ref, sk_ref, o_ref,
               kpad_scr, vpad_scr, kcn_scr, vc_scr, *, seq):
    g = pl.program_id(1)
    qb = pl.program_id(2)
    wb = WA_BLOCK
    hd = HEAD_DIM

    @pl.when(qb == 0)
    def _():
        g1 = gain_ref[1:2, :]
        zeros = jnp.zeros((wb, hd), BF16)
        kpad_scr[0:wb, :] = zeros
        kpad_scr[seq + wb:seq + 2 * wb, :] = zeros
        vpad_scr[0:wb, :] = zeros
        vpad_scr[seq + wb:seq + 2 * wb, :] = zeros
        kpad_scr[wb:seq + wb, :] = _rope(_rms(k_ref[...], g1), ck_ref[...], sk_ref[...]).astype(BF16)
        vpad_scr[wb:seq + wb, :] = v_ref[...].astype(BF16)
        kcn_scr[...] = _rms(kc_ref[...], g1).astype(BF16)
        vc_scr[...] = vc_ref[...].astype(BF16)

    scale = hd ** -0.5
    g0 = gain_ref[0:1, :]
    cq = cq_ref[...]
    sq = sq_ref[...]
    q2 = jnp.concatenate([_rope(_rms(q_ref[:, 0:hd], g0), cq, sq),
                          _rope(_rms(q_ref[:, hd:2 * hd], g0), cq, sq)], axis=0).astype(BF16)
    start = pl.multiple_of(qb * wb, wb)
    kw = kpad_scr[pl.ds(start, 3 * wb), :]
    vw = vpad_scr[pl.ds(start, 3 * wb), :]
    shp = (2 * wb, 3 * wb)
    i = _iota(shp, 0) % wb
    j = _iota(shp, 1)
    kpos = (qb - 1) * wb + j
    ok = (j - i >= 0) & (j - i <= 2 * wb) & (kpos >= 0) & (kpos < seq)
    s_loc = jnp.where(ok, _dot_nt(q2, kw) * scale, NEG_INF)
    s_ctx = _dot_nt(q2, kcn_scr[...]) * scale
    sink = jnp.where(_iota((2 * wb, 1), 0) < wb, sink_ref[2 * g], sink_ref[2 * g + 1])
    m = jnp.maximum(jnp.maximum(jnp.max(s_loc, axis=-1, keepdims=True), jnp.max(s_ctx, axis=-1, keepdims=True)),
                    sink)
    p_loc = jnp.exp(s_loc - m)
    p_ctx = jnp.exp(s_ctx - m)
    den = jnp.sum(p_loc, axis=-1, keepdims=True) + jnp.sum(p_ctx, axis=-1, keepdims=True) + jnp.exp(sink - m)
    o = (_dot(p_loc.astype(BF16), vw) + _dot(p_ctx.astype(BF16), vc_scr[...])) / den
    o_ref[:, 0:hd] = o[0:wb].astype(BF16)
    o_ref[:, hd:2 * hd] = o[wb:2 * wb].astype(BF16)


def _wa_call(p_lat, p_ctx, gain, sink, cos_t, sin_t, batch, seq, lc):
    wb = WA_BLOCK
    hd = HEAD_DIM
    nb = seq // wb
    kvh = 2
    return pl.pallas_call(
        functools.partial(_wa_kernel, seq=seq),
        grid=(batch, kvh, nb),
        in_specs=[pl.BlockSpec(memory_space=pltpu.SMEM),
                  pl.BlockSpec((wb, 2 * hd), lambda b, g, n: (b * nb + n, WA_Q // 2 + g)),
                  pl.BlockSpec((seq, hd), lambda b, g, n: (b, WA_K + g)),
                  pl.BlockSpec((seq, hd), lambda b, g, n: (b, WA_V + g)),
                  pl.BlockSpec((lc, hd), lambda b, g, n: (b, WA_K + g)),
                  pl.BlockSpec((lc, hd), lambda b, g, n: (b, WA_V + g)),
                  pl.BlockSpec((2, hd), lambda b, g, n: (0, 0)),
                  pl.BlockSpec((wb, hd), lambda b, g, n: (n, 0)),
                  pl.BlockSpec((wb, hd), lambda b, g, n: (n, 0)),
                  pl.BlockSpec((seq, hd), lambda b, g, n: (0, 0)),
                  pl.BlockSpec((seq, hd), lambda b, g, n: (0, 0))],
        out_specs=pl.BlockSpec((wb, 2 * hd), lambda b, g, n: (b * nb + n, g)),
        out_shape=jax.ShapeDtypeStruct((batch * seq, 2 * kvh * hd), BF16),
        scratch_shapes=[pltpu.VMEM((seq + 2 * wb, hd), BF16), pltpu.VMEM((seq + 2 * wb, hd), BF16),
                        pltpu.VMEM((lc, hd), BF16), pltpu.VMEM((lc, hd), BF16)],
        compiler_params=_cparams(("arbitrary", "arbitrary", "arbitrary")),
        name="wa_attn",
    )(sink, p_lat, p_lat, p_lat, p_ctx, p_ctx, gain, cos_t, sin_t, cos_t, sin_t)


def _dn_prep_kernel(x_ref, prev_ref, next_ref, w_ref, o_ref, *, n_tiles):
    t = pl.program_id(1)
    tl = x_ref.shape[0]
    x = x_ref[...]
    prev = jnp.where(t > 0, prev_ref[...], 0.0)
    nxt = jnp.where(t < n_tiles - 1, next_ref[...], 0.0)
    xe = jnp.concatenate([prev, x, nxt], axis=0)
    n = tl + 16
    taps = w_ref.shape[0]
    acc = jnp.zeros_like(x)
    for j in range(taps):
        off = 8 - taps // 2 + j
        shifted = xe if off % n == 0 else pltpu.roll(xe, n - off, axis=0)
        acc += shifted[0:tl] * w_ref[j:j + 1, :]
    y = _silu(acc)
    hd = HEAD_DIM
    nh = y.shape[1] // (3 * hd)
    for blk in range(3 * nh):
        yb = y[:, blk * hd:(blk + 1) * hd]
        if blk < 2 * nh:
            yb = yb * lax.rsqrt(jnp.sum(yb * yb, axis=-1, keepdims=True) + EPS)
            if blk < nh:
                yb = yb * hd ** -0.5
        o_ref[:, blk * hd:(blk + 1) * hd] = yb


def _dn_prep_call(p, conv_w, batch, length, tl):
    width = conv_w.shape[1]
    n_tiles = length // tl
    per8 = tl // 8
    last8 = batch * length // 8 - 1
    return pl.pallas_call(
        functools.partial(_dn_prep_kernel, n_tiles=n_tiles),
        grid=(batch, n_tiles),
        in_specs=[pl.BlockSpec((tl, width), lambda b, t: (b * n_tiles + t, 1)),
                  pl.BlockSpec((8, width), lambda b, t: (jnp.maximum((b * n_tiles + t) * per8 - 1, 0), 1)),
                  pl.BlockSpec((8, width), lambda b, t: (jnp.minimum((b * n_tiles + t + 1) * per8, last8), 1)),
                  pl.BlockSpec((conv_w.shape[0], width), lambda b, t: (0, 0))],
        out_specs=pl.BlockSpec((tl, width), lambda b, t: (b * n_tiles + t, 0)),
        out_shape=jax.ShapeDtypeStruct((batch * length, width), F32),
        compiler_params=_cparams(("arbitrary", "arbitrary")),
        name="dn_prep",
    )(p, p, p, conv_w)


def _split_bf16(a):
    hi = a.astype(BF16)
    return hi, (a - hi.astype(F32)).astype(BF16)


def _solve_dot(a, b, mode):
    if mode == "highest":
        return _dot(a, b, lax.Precision.HIGHEST)
    if mode == "bf16":
        return _dot(a.astype(BF16), b.astype(BF16))
    a_hi, a_lo = _split_bf16(a)
    b_hi, b_lo = _split_bf16(b)
    return _dot(a_hi, b_hi) + (_dot(a_hi, b_lo) + _dot(a_lo, b_hi))


def _dn_pass(q_ref, k_ref, v_ref, og_ref, gates_ref, gadd, gmul, norm_g, y_ref, length, h, state,
             lin_scr, add_scr, egl_scr, acc_scr, solve_precision):
    assert CHUNK == HEAD_DIM
    n_chunks = length // CHUNK

    acc_scr[0:length, :] = jnp.zeros((length, HEAD_DIM), F32)

    shp = (CHUNK, CHUNK)
    ri = _iota(shp, 0)
    ci = _iota(shp, 1)
    masks = ((ci <= ri, ci < ri), (ci >= ri, ci > ri))
    couple = ([], [])
    s = 1
    while s < CHUNK:
        same = (ri // (2 * s)) == (ci // (2 * s))
        couple[0].append(same & ((ri // s) % 2 == 1) & ((ci // s) % 2 == 0))
        couple[1].append(same & ((ri // s) % 2 == 0) & ((ci // s) % 2 == 1))
        s *= 2
    group = min(n_chunks, PREP_GROUP)

    def prep(gi, carry):
        chains = []
        for g in range(group):
            c = gi * group + g
            r0 = pl.multiple_of(c * CHUNK, CHUNK)
            q = q_ref[pl.ds(r0, CHUNK), :]
            k = k_ref[pl.ds(r0, CHUNK), :]
            raw = gates_ref[pl.ds(r0, CHUNK), :]
            gb = jnp.where(_iota(raw.shape, 1) < LANE_ALPHA, jax.nn.sigmoid(raw), -gmul * _softplus(raw + gadd))
            k16 = k.astype(BF16)
            kk = _dot_nt(k16, k16)
            qk = _dot_nt(q.astype(BF16), k16)
            for d in range(2):
                incl, strict = masks[d]
                beta_col = _lane_pick(gb, LANE_BETA + 4 * d + h)
                g_col = _lane_pick(gb, LANE_ALPHA + 4 * d + h)
                beta_row = _to_row(beta_col)
                g_row = _to_row(g_col)
                cum_col = jnp.sum(jnp.where(incl, g_row, 0.0), axis=-1, keepdims=True)
                cum_row = _to_row(cum_col)
                tot = jnp.sum(g_col, axis=0, keepdims=True)
                decay = jnp.exp(jnp.where(incl, cum_col - cum_row, NEG_INF))
                lm = jnp.where(strict, kk * beta_col * decay, 0.0)
                egl_scr[d, pl.ds(pl.multiple_of(c * 8, 8), 8), :] = jnp.broadcast_to(jnp.exp(tot), (8, HEAD_DIM))
                chains.append((d, c, r0, lm, beta_row, beta_row * jnp.exp(cum_row), (qk * decay).astype(BF16),
                               q * jnp.exp(cum_col), k * jnp.exp(tot - cum_col)))
        eye = (ri == ci).astype(F32)
        xs = [eye - jnp.where(couple[ch[0]][0], ch[3], 0.0) for ch in chains]
        for lvl in range(1, len(couple[0])):
            cs = [jnp.where(couple[ch[0]][lvl], ch[3], 0.0) for ch in chains]
            xc = [_solve_dot(x, c, solve_precision) for x, c in zip(xs, cs)]
            xs = [x - _solve_dot(t, x, solve_precision) for x, t in zip(xs, xc)]
        u16 = [_solve_dot(ainv * ch[4], v_ref[pl.ds(ch[2], CHUNK), :], solve_precision).astype(BF16)
               for ch, ainv in zip(chains, xs)]
        w16 = [_solve_dot(ainv * ch[5], k_ref[pl.ds(ch[2], CHUNK), :], solve_precision).astype(BF16)
               for ch, ainv in zip(chains, xs)]
        ket = [ch[8].T.astype(BF16) for ch in chains]
        s_lin = [_dot(kt, w) for kt, w in zip(ket, w16)]
        s_add = [_dot(kt, u) for kt, u in zip(ket, u16)]
        o_lin = [ch[7] - _dot(ch[6], w) for ch, w in zip(chains, w16)]
        o_add = [_dot(ch[6], u) for ch, u in zip(chains, u16)]
        for ch, sl, sa, ol, oa in zip(chains, s_lin, s_add, o_lin, o_add):
            d = ch[0]
            base = pl.multiple_of(ch[1] * 2 * CHUNK, 2 * CHUNK)
            lin_scr[d, pl.ds(base, CHUNK), :] = sl.astype(BF16)
            lin_scr[d, pl.ds(base + CHUNK, CHUNK), :] = ol.astype(BF16)
            add_scr[d, pl.ds(base, CHUNK), :] = sa
            add_scr[d, pl.ds(base + CHUNK, CHUNK), :] = oa
        return carry

    lax.fori_loop(0, n_chunks // group, prep, 0)

    def scan(s, st):
        new = []
        for d in range(2):
            c = s if d == 0 else n_chunks - 1 - s
            base = pl.multiple_of(c * 2 * CHUNK, 2 * CHUNK)
            sm = st[d]
            prod = _dot(lin_scr[d, pl.ds(base, 2 * CHUNK), :], sm.astype(BF16))
            add = add_scr[d, pl.ds(base, 2 * CHUNK), :]
            acc_scr[pl.ds(pl.multiple_of(c * CHUNK, CHUNK), CHUNK), :] += prod[CHUNK:] + add[CHUNK:]
            egl = egl_scr[d, pl.ds(pl.multiple_of(c * 8, 8), 8), :][0:1, :]
            new.append(sm * egl - prod[:CHUNK] + add[:CHUNK])
        return tuple(new)

    state = lax.fori_loop(0, n_chunks, scan, state, unroll=SCAN_UNROLL)

    y = _rms(acc_scr[0:length, :], norm_g) * _silu(og_ref[...])
    y_ref[...] = y.astype(BF16)
    return state


def _dn_kernel(q_ref, k_ref, v_ref, og_ref, gates_ref, qc_ref, kc_ref, vc_ref, ogc_ref, gatesc_ref,
               gadd_ref, gmul_ref, norm_ref, y_ref, yc_ref,
               lin_scr, add_scr, egl_scr, acc_scr, *, seq, lc, solve_precision):
    h = pl.program_id(1)
    gadd = gadd_ref[...]
    gmul = gmul_ref[...]
    norm_g = norm_ref[...]
    scr = (lin_scr, add_scr, egl_scr, acc_scr)
    zero = jnp.zeros((HEAD_DIM, HEAD_DIM), F32)
    state = _dn_pass(qc_ref, kc_ref, vc_ref, ogc_ref, gatesc_ref, gadd, gmul, norm_g, yc_ref, lc, h, (zero, zero),
                     *scr, solve_precision)
    _dn_pass(q_ref, k_ref, v_ref, og_ref, gates_ref, gadd, gmul, norm_g, y_ref, seq, h, state,
             *scr, solve_precision)


def _dn_call(dq_lat, dq_ctx, p_lat, p_ctx, pg_lat, pg_ctx, gadd, gmul, norm_g, batch, seq, lc, solve_precision):
    hd = HEAD_DIM
    heads = 4

    def blk(n, col):
        return pl.BlockSpec((n, hd), lambda b, h: (b, col + h))

    def gate_blk(n):
        return pl.BlockSpec((n, hd), lambda b, h: (b, 0))

    vec = pl.BlockSpec((1, hd), lambda b, h: (0, 0))
    return pl.pallas_call(
        functools.partial(_dn_kernel, seq=seq, lc=lc, solve_precision=solve_precision),
        grid=(batch, heads),
        in_specs=[blk(seq, 0), blk(seq, heads), blk(seq, 2 * heads), blk(seq, DN_OG), gate_blk(seq),
                  blk(lc, 0), blk(lc, heads), blk(lc, 2 * heads), blk(lc, DN_OG), gate_blk(lc),
                  vec, vec, vec],
        out_specs=[pl.BlockSpec((seq, hd), lambda b, h: (b, h)),
                   pl.BlockSpec((lc, hd), lambda b, h: (b, h))],
        out_shape=[jax.ShapeDtypeStruct((batch * seq, heads * hd), BF16),
                   jax.ShapeDtypeStruct((batch * lc, heads * hd), BF16)],
        scratch_shapes=[pltpu.VMEM((2, 2 * seq, hd), BF16),
                        pltpu.VMEM((2, 2 * seq, hd), F32),
                        pltpu.VMEM((2, seq // CHUNK * 8, hd), F32),
                        pltpu.VMEM((seq, hd), F32)],
        compiler_params=_cparams(("arbitrary", "arbitrary"), DN_VMEM_LIMIT),
        name="deltanet",
    )(dq_lat, dq_lat, dq_lat, p_lat, pg_lat, dq_ctx, dq_ctx, dq_ctx, p_ctx, pg_ctx, gadd, gmul, norm_g)


def _ml_pass(q_ref, k_ref, v_ref, og_ref, gates_ref, gadd, norm_g, y_ref, length, h, state, acc_scr):
    n_chunks = length // CHUNK
    acc_scr[0:length, :] = jnp.zeros((length, HEAD_DIM), F32)
    shp = (CHUNK, CHUNK)
    ri = _iota(shp, 0)
    ci = _iota(shp, 1)
    masks = (ci <= ri, ci >= ri)
    head_lanes = (_iota((1, HEAD_DIM), 1) // ML_QK) == (h % 2)

    group = min(n_chunks, ML_GROUP)

    def each(f, *lists):
        return [f(*a) for a in zip(*lists)]

    def step(gi, st):
        dirs, r0s = [], []
        for j in range(group):
            s = gi * group + j
            for d in range(2):
                c = s if d == 0 else n_chunks - 1 - s
                dirs.append(d)
                r0s.append(pl.multiple_of(c * CHUNK, CHUNK))
        incl = [masks[d] for d in dirs]
        q = [jnp.where(head_lanes, q_ref[pl.ds(r0, CHUNK), :], 0.0) for r0 in r0s]
        k = [jnp.where(head_lanes, k_ref[pl.ds(r0, CHUNK), :], 0.0) * ML_QK ** -0.5 for r0 in r0s]
        v16 = [v_ref[pl.ds(r0, CHUNK), :].astype(BF16) for r0 in r0s]
        gb = [gates_ref[pl.ds(r0, CHUNK), :] + gadd for r0 in r0s]
        ig_col = [_lane_pick(g, LANE_IG + 4 * d + h) for g, d in zip(gb, dirs)]
        lf_col = [_log_sigmoid(_lane_pick(g, LANE_FG + 4 * d + h)) for g, d in zip(gb, dirs)]
        ig_row = each(_to_row, ig_col)
        lf_row = each(_to_row, lf_col)
        b_col = each(lambda m, r: jnp.sum(jnp.where(m, r, 0.0), axis=-1, keepdims=True), incl, lf_row)
        b_row = each(_to_row, b_col)
        b_last = each(lambda x: jnp.sum(x, axis=0, keepdims=True), lf_col)
        g_end = each(lambda bl, b, i: bl - b + i, b_last, b_col, ig_col)
        m_loc = each(lambda x: jnp.max(x, axis=0, keepdims=True), g_end)
        kw = each(lambda kk, g, m: kk * jnp.exp(g - m), k, g_end, m_loc)
        q16 = each(lambda x: x.astype(BF16), q)
        k16 = each(lambda x: x.astype(BF16), k)
        c_loc = each(lambda a, b: _dot_tn(a.astype(BF16), b), kw, v16)
        n_loc = each(lambda x: jnp.sum(x, axis=0, keepdims=True), kw)
        log_d = each(lambda m, bc, br, ir: jnp.where(m, bc - br + ir, NEG_INF), incl, b_col, b_row, ig_row)
        mx_col = each(lambda x: jnp.max(x, axis=-1, keepdims=True), log_d)
        s0 = each(lambda a, b, ld, mx: _dot_nt(a, b) * jnp.exp(ld - mx), q16, k16, log_d, mx_col)
        sv0 = each(lambda s, vv: _dot(s.astype(BF16), vv), s0, v16)
        rs0 = each(lambda s: jnp.sum(s, axis=-1, keepdims=True), s0)
        st = list(st)
        for i, (d, r0) in enumerate(zip(dirs, r0s)):
            c_st, n_st, m_st = st[d]
            m_inter = b_col[i] + m_st
            m_t = jnp.maximum(mx_col[i], m_inter)
            intra = jnp.exp(mx_col[i] - m_t)
            inter = jnp.exp(m_inter - m_t)
            num = intra * sv0[i] + inter * _dot(q16[i], c_st.astype(BF16))
            den = intra * rs0[i] + inter * jnp.sum(q[i] * n_st, axis=-1, keepdims=True)
            acc_scr[pl.ds(r0, CHUNK), :] += num / jnp.maximum(jnp.abs(den), jnp.exp(-m_t))
            m_new = jnp.maximum(b_last[i] + m_st, m_loc[i])
            a = jnp.exp(b_last[i] + m_st - m_new)
            e = jnp.exp(m_loc[i] - m_new)
            st[d] = (a * c_st + e * c_loc[i], a * n_st + e * n_loc[i], m_new)
        return tuple(st)

    state = lax.fori_loop(0, n_chunks // group, step, state)
    y = jax.nn.sigmoid(og_ref[...]) * _rms(acc_scr[0:length, :], norm_g)
    y_ref[...] = y.astype(BF16)
    return state


def _ml_kernel(q_ref, k_ref, v_ref, og_ref, gates_ref, qc_ref, kc_ref, vc_ref, ogc_ref, gatesc_ref,
               gadd_ref, norm_ref, y_ref, yc_ref, acc_scr, *, seq, lc):
    h = pl.program_id(1)
    gadd = gadd_ref[...]
    norm_g = norm_ref[0]
    zero = (jnp.zeros((HEAD_DIM, HEAD_DIM), F32), jnp.zeros((1, HEAD_DIM), F32), jnp.zeros((1, 1), F32))
    state = _ml_pass(qc_ref, kc_ref, vc_ref, ogc_ref, gatesc_ref, gadd, norm_g, yc_ref, lc, h, (zero, zero), acc_scr)
    _ml_pass(q_ref, k_ref, v_ref, og_ref, gates_ref, gadd, norm_g, y_ref, seq, h, state, acc_scr)


def _ml_call(p_lat, p_ctx, pg_lat, pg_ctx, gadd, norm_g, batch, seq, lc):
    hd = HEAD_DIM
    heads = 4

    def specs(n):
        return [pl.BlockSpec((n, hd), lambda b, h: (b, ML_Q + h // 2)),
                pl.BlockSpec((n, hd), lambda b, h: (b, ML_K + h // 2)),
                pl.BlockSpec((n, hd), lambda b, h: (b, ML_V + h)),
                pl.BlockSpec((n, hd), lambda b, h: (b, ML_OG + h)),
                pl.BlockSpec((n, hd), lambda b, h: (b, 0))]

    return pl.pallas_call(
        functools.partial(_ml_kernel, seq=seq, lc=lc),
        grid=(batch, heads),
        in_specs=specs(seq) + specs(lc) + [pl.BlockSpec((1, hd), lambda b, h: (0, 0)),
                                            pl.BlockSpec((1, 1, hd), lambda b, h: (h, 0, 0))],
        out_specs=[pl.BlockSpec((seq, hd), lambda b, h: (b, h)),
                   pl.BlockSpec((lc, hd), lambda b, h: (b, h))],
        out_shape=[jax.ShapeDtypeStruct((batch * seq, heads * hd), BF16),
                   jax.ShapeDtypeStruct((batch * lc, heads * hd), BF16)],
        scratch_shapes=[pltpu.VMEM((seq, hd), F32)],
        compiler_params=_cparams(("arbitrary", "arbitrary")),
        name="mlstm",
    )(p_lat, p_lat, p_lat, p_lat, pg_lat, p_ctx, p_ctx, p_ctx, p_ctx, pg_ctx, gadd, norm_g.reshape(heads, 1, hd))


def _regroup_kernel(wt_ref, main_ref, gate_ref, *, cuts):
    (a0, a1), (b0, b1), (c0, c1), (g0, g1), (h0, h1) = cuts
    main_ref[0, 0:a1 - a0, :] = wt_ref[0, a0:a1, :].astype(BF16)
    main_ref[0, a1 - a0:a1 - a0 + b1 - b0, :] = wt_ref[0, b0:b1, :].astype(BF16)
    main_ref[0, a1 - a0 + b1 - b0:, :] = wt_ref[0, c0:c1, :].astype(BF16)
    ng = (g1 - g0) + (h1 - h0)
    gate_ref[0, 0:g1 - g0, :] = wt_ref[0, g0:g1, :].astype(BF16)
    gate_ref[0, g1 - g0:ng, :] = wt_ref[0, h0:h1, :].astype(BF16)
    gate_ref[0, ng:, :] = jnp.zeros((gate_ref.shape[1] - ng, gate_ref.shape[2]), BF16)


def _regroup_call(w_in):
    depth, d, n = w_in.shape
    gw = 512
    dn_gate0 = 3 * gw + 4 * gw
    ml0 = dn_gate0 + 16
    ml_gate0 = ml0 + 3 * gw
    wa0 = ml_gate0 + 16
    cuts = ((0, dn_gate0), (ml0, ml_gate0), (wa0, n), (dn_gate0, ml0), (ml_gate0, wa0))
    tc = 256
    return pl.pallas_call(
        functools.partial(_regroup_kernel, cuts=cuts),
        grid=(depth, d // tc),
        in_specs=[pl.BlockSpec((1, n, tc), lambda l, i: (l, 0, i))],
        out_specs=[pl.BlockSpec((1, MAIN_COLS, tc), lambda l, i: (l, 0, i)),
                   pl.BlockSpec((1, 128, tc), lambda l, i: (l, 0, i))],
        out_shape=[jax.ShapeDtypeStruct((depth, MAIN_COLS, d), BF16), jax.ShapeDtypeStruct((depth, 128, d), BF16)],
        compiler_params=_cparams(("arbitrary", "arbitrary")),
        name="regroup_w_in",
    )(jnp.swapaxes(w_in, 1, 2))


def _rope_tables(seq):
    t = jnp.arange(seq)
    n_freq = HEAD_DIM // 4
    inv_freq = ROPE_THETA ** (-jnp.arange(n_freq, dtype=F32) / n_freq)
    pos = jnp.stack([t // GRID_W, t % GRID_W], axis=-1).astype(F32)
    ang = pos[:, :, None] * inv_freq
    cos, sin = jnp.cos(ang), jnp.sin(ang)
    cos_t = jnp.concatenate([cos, cos], axis=-1).reshape(seq, HEAD_DIM)
    sin_t = jnp.concatenate([-sin, sin], axis=-1).reshape(seq, HEAD_DIM)
    return cos_t, sin_t


def _gate_vectors(dt_bias, a_log, i_bias, f_bias):
    zeros8 = jnp.zeros((8,), F32)
    gadd = jnp.concatenate([zeros8, dt_bias.reshape(-1), i_bias.reshape(-1), f_bias.reshape(-1),
                            jnp.zeros((96,), F32)]).astype(F32)
    gmul = jnp.concatenate([zeros8, jnp.exp(a_log.astype(F32)).reshape(-1), jnp.zeros((112,), F32)])
    return gadd.reshape(1, 128), gmul.reshape(1, 128)


def _forward(x, c, ctx, c_ctx, w_ada, b_ada, norm_mix, norm_ffn, w_in, w_out, na_qk_gain, na_rpb, dn_conv,
             dn_a_log, dn_dt_bias, dn_norm, ml_i_bias, ml_f_bias, ml_norm, wa_qk_gain, wa_sink, w_ffn_in,
             w_ffn_out, solve_precision):
    batch, seq, d = x.shape
    lc = ctx.shape[1]
    depth = w_ada.shape[0]
    tm = 512
    tn_ffn = 512

    cos_t, sin_t = _rope_tables(seq)
    na_bias = _na_col_tables(na_rpb)
    cs = jnp.concatenate([c, c_ctx[None, :], jnp.zeros((8 - batch - 1, d), F32)], axis=0)
    mods = _ada_call(cs, w_ada, b_ada).reshape(depth, 8, 6, d)
    mods = jnp.pad(mods, ((0, 0), (0, 0), (0, 2), (0, 0)))

    w_main, w_gate = _regroup_call(w_in)
    w_o = w_out.astype(BF16)
    w_f1 = w_ffn_in.astype(BF16)
    w_f2 = w_ffn_out.astype(BF16)

    xl = x.reshape(batch * seq, d)
    xc = ctx.reshape(batch * lc, d)
    hl = _norm_call(xl, mods[0, :batch], norm_mix[0:1], seq, tm)
    hc = _norm_call(xc, mods[0, batch:batch + 1], norm_mix[0:1], batch * lc, tm)

    for l in range(depth):
        need_ctx = l < depth - 1
        mod_l, mod_c = mods[l, :batch], mods[l, batch:batch + 1]
        p_lat = _matmul_call(hl, w_main, l, tm, 1024, "inproj")
        pg_lat = _matmul_call(hl, w_gate, l, tm, 128, "inproj_gates")
        p_ctx = _matmul_call(hc, w_main, l, tm, 1024, "inproj_ctx")
        pg_ctx = _matmul_call(hc, w_gate, l, tm, 128, "inproj_gates_ctx")

        ya = _na_call(p_lat, p_ctx, na_qk_gain[l], na_bias, l, batch, seq, lc)
        gadd, gmul = _gate_vectors(dn_dt_bias[l], dn_a_log[l], ml_i_bias[l], ml_f_bias[l])
        dq_lat = _dn_prep_call(p_lat, dn_conv[l], batch, seq, 256)
        dq_ctx = _dn_prep_call(p_ctx, dn_conv[l], batch, lc, 256)
        yb, ycb = _dn_call(dq_lat, dq_ctx, p_lat, p_ctx, pg_lat, pg_ctx, gadd, gmul, dn_norm[l].reshape(1, -1),
                           batch, seq, lc, solve_precision)
        ym, ycm = _ml_call(p_lat, p_ctx, pg_lat, pg_ctx, gadd, ml_norm[l], batch, seq, lc)
        yw = _wa_call(p_lat, p_ctx, wa_qk_gain[l], wa_sink[l], cos_t, sin_t, batch, seq, lc)

        nl = min(l + 1, depth - 1)
        xl, h2 = _outproj_call((ya, yb, ym, yw), w_o, l, xl, mod_l, norm_ffn[l:l + 1], seq, tm)
        xl, hl = _ffn_call(h2, w_f1, w_f2, l, xl, mod_l, mods[nl, :batch], norm_mix[nl:nl + 1], seq, tm, tn_ffn,
                           need_ctx)
        if need_ctx:
            yca = _ctx_attn_call(p_ctx, na_qk_gain[l], wa_sink[l], batch, lc, 4, 1, NA_Q, NA_K, NA_V, False,
                                 "na_ctx_attn")
            ycw = _ctx_attn_call(p_ctx, wa_qk_gain[l], wa_sink[l], batch, lc, 4, 2, WA_Q, WA_K, WA_V, True,
                                 "wa_ctx_attn")
            xc, h2c = _outproj_call((yca, ycb, ycm, ycw), w_o, l, xc, mod_c, norm_ffn[l:l + 1], batch * lc, tm)
            xc, hc = _ffn_call(h2c, w_f1, w_f2, l, xc, mod_c, mods[nl, batch:batch + 1], norm_mix[nl:nl + 1],
                               batch * lc, tm, tn_ffn, True)
    return xl.reshape(batch, seq, d)


def kernel(x, c, ctx, c_ctx, w_ada, b_ada, norm_mix, norm_ffn, w_in, w_out, na_qk_gain, na_rpb, dn_conv, dn_a_log,
           dn_dt_bias, dn_norm, ml_i_bias, ml_f_bias, ml_norm, wa_qk_gain, wa_sink, w_ffn_in, w_ffn_out):
    return _forward(x, c, ctx, c_ctx, w_ada, b_ada, norm_mix, norm_ffn, w_in, w_out, na_qk_gain, na_rpb, dn_conv,
                    dn_a_log, dn_dt_bias, dn_norm, ml_i_bias, ml_f_bias, ml_norm, wa_qk_gain, wa_sink, w_ffn_in,
                    w_ffn_out, "bf16")
```

```python
import functools

import jax
import jax.numpy as jnp
from jax import lax
from jax.experimental import pallas as pl
from jax.experimental.pallas import tpu as pltpu

F32 = jnp.float32
BF16 = jnp.bfloat16

HEAD_DIM = 128
GRID_W = 64
CHUNK = 128
PREP_GROUP = 4
ML_GROUP = 4
SCAN_UNROLL = 2
NA_WIN_ROWS = 8
NA_WIN_COLS = 16
NA_QROWS = 4
NA_KROWS = NA_QROWS + NA_WIN_ROWS
WA_BLOCK = 128
ML_QK = 64
ROPE_THETA = 10000.0
EPS = 1e-6
NEG_INF = float("-inf")
VMEM_LIMIT = 48 * 1024 * 1024
DN_VMEM_LIMIT = 56 * 1024 * 1024

NA_Q, NA_K, NA_V = 0, 4, 8
DN_OG = 24
ML_Q, ML_K, ML_V, ML_OG = 28, 30, 32, 36
WA_Q, WA_K, WA_V = 40, 44, 46
MAIN_COLS = 48 * 128
LANE_BETA, LANE_ALPHA, LANE_IG, LANE_FG = 0, 8, 16, 24


def _cparams(sem, vmem_limit=VMEM_LIMIT):
    return pltpu.CompilerParams(dimension_semantics=sem, vmem_limit_bytes=vmem_limit)


def _rms(x, gain):
    return x * lax.rsqrt(jnp.mean(x * x, axis=-1, keepdims=True) + EPS) * gain


def _silu(x):
    return x * jax.nn.sigmoid(x)


def _softplus(x):
    return jnp.maximum(x, 0.0) + jnp.log1p(jnp.exp(-jnp.abs(x)))


def _log_sigmoid(x):
    return jnp.minimum(x, 0.0) - jnp.log1p(jnp.exp(-jnp.abs(x)))


def _dot(a, b, precision=None):
    return jnp.dot(a, b, preferred_element_type=F32, precision=precision)


def _dot_nt(a, b):
    return lax.dot_general(a, b, (((1,), (1,)), ((), ())), preferred_element_type=F32)


def _dot_tn(a, b):
    return lax.dot_general(a, b, (((0,), (0,)), ((), ())), preferred_element_type=F32)


def _iota(shape, dim):
    return lax.broadcasted_iota(jnp.int32, shape, dim)


def _lane_pick(x, lane):
    return jnp.sum(jnp.where(_iota(x.shape, 1) == lane, x, 0.0), axis=-1, keepdims=True)


def _to_row(col):
    n = col.shape[0]
    eye = _iota((n, n), 0) == _iota((n, n), 1)
    return jnp.sum(jnp.where(eye, col, 0.0), axis=0, keepdims=True)


def _ada_kernel(c_ref, w_ref, b_ref, o_ref):
    a = _silu(c_ref[...]).astype(BF16)
    o_ref[0] = _dot(a, w_ref[0].astype(BF16)) + b_ref[0]


def _ada_call(cs, w_ada, b_ada):
    depth, d, n = w_ada.shape
    tn = 1024
    return pl.pallas_call(
        _ada_kernel,
        grid=(depth, n // tn),
        in_specs=[pl.BlockSpec((8, d), lambda l, j: (0, 0)),
                  pl.BlockSpec((1, d, tn), lambda l, j: (l, 0, j)),
                  pl.BlockSpec((1, 1, tn), lambda l, j: (l, 0, j))],
        out_specs=pl.BlockSpec((1, 8, tn), lambda l, j: (l, 0, j)),
        out_shape=jax.ShapeDtypeStruct((depth, 8, n), F32),
        compiler_params=_cparams(("arbitrary", "arbitrary")),
        name="ada",
    )(cs, w_ada, b_ada.reshape(depth, 1, n))


def _norm_kernel(x_ref, mod_ref, g_ref, h_ref):
    mod = mod_ref[0]
    h = _rms(x_ref[...], g_ref[...])
    h_ref[...] = (h * (1.0 + mod[1:2]) + mod[0:1]).astype(BF16)


def _norm_call(x, mod, gain, rows_per_mod, tm):
    m, d = x.shape
    per = rows_per_mod // tm
    return pl.pallas_call(
        _norm_kernel,
        grid=(m // tm,),
        in_specs=[pl.BlockSpec((tm, d), lambda i: (i, 0)),
                  pl.BlockSpec((1, 8, d), lambda i: (i // per, 0, 0)),
                  pl.BlockSpec((1, d), lambda i: (0, 0))],
        out_specs=pl.BlockSpec((tm, d), lambda i: (i, 0)),
        out_shape=jax.ShapeDtypeStruct((m, d), BF16),
        compiler_params=_cparams(("arbitrary",)),
        name="norm_mod",
    )(x, mod, gain)


def _matmul_kernel(a_ref, wt_ref, o_ref):
    o_ref[...] = _dot_nt(a_ref[...], wt_ref[0])


def _matmul_call(a, wt, layer, tm, tn, name):
    m, k = a.shape
    n = wt.shape[1]
    return pl.pallas_call(
        _matmul_kernel,
        grid=(n // tn, m // tm),
        in_specs=[pl.BlockSpec((tm, k), lambda j, i: (i, 0)),
                  pl.BlockSpec((1, tn, k), lambda j, i: (layer, j, 0))],
        out_specs=pl.BlockSpec((tm, tn), lambda j, i: (i, j)),
        out_shape=jax.ShapeDtypeStruct((m, n), F32),
        compiler_params=_cparams(("arbitrary", "arbitrary")),
        name=name,
    )(a, wt)


def _outproj_kernel(ya_ref, yb_ref, ym_ref, yw_ref, w_ref, x_ref, mod_ref, g_ref, xo_ref, h_ref):
    gw = ya_ref.shape[1]
    acc = _dot(ya_ref[...], w_ref[0, 0:gw, :])
    acc += _dot(yb_ref[...], w_ref[0, gw:2 * gw, :])
    acc += _dot(ym_ref[...], w_ref[0, 2 * gw:3 * gw, :])
    acc += _dot(yw_ref[...], w_ref[0, 3 * gw:4 * gw, :])
    mod = mod_ref[0]
    xn = x_ref[...] + mod[2:3] * acc
    xo_ref[...] = xn
    h = _rms(xn, g_ref[...])
    h_ref[...] = (h * (1.0 + mod[4:5]) + mod[3:4]).astype(BF16)


def _outproj_call(ys, w, layer, x, mod, gain, rows_per_mod, tm):
    m, d = x.shape
    gw = ys[0].shape[1]
    per = rows_per_mod // tm
    yspec = pl.BlockSpec((tm, gw), lambda i: (i, 0))
    return pl.pallas_call(
        _outproj_kernel,
        grid=(m // tm,),
        in_specs=[yspec, yspec, yspec, yspec,
                  pl.BlockSpec((1, 4 * gw, d), lambda i: (layer, 0, 0)),
                  pl.BlockSpec((tm, d), lambda i: (i, 0)),
                  pl.BlockSpec((1, 8, d), lambda i: (i // per, 0, 0)),
                  pl.BlockSpec((1, d), lambda i: (0, 0))],
        out_specs=[pl.BlockSpec((tm, d), lambda i: (i, 0)),
                   pl.BlockSpec((tm, d), lambda i: (i, 0))],
        out_shape=[jax.ShapeDtypeStruct((m, d), F32), jax.ShapeDtypeStruct((m, d), BF16)],
        compiler_params=_cparams(("arbitrary",)),
        name="outproj",
    )(*ys, w, x, mod, gain)


def _ffn_kernel(h_ref, wg_ref, wu_ref, w2_ref, x_ref, mod_ref, modn_ref, gn_ref, xo_ref, hn_ref, acc_ref,
                *, with_next):
    j = pl.program_id(1)

    @pl.when(j == 0)
    def _():
        acc_ref[...] = jnp.zeros_like(acc_ref)

    h = h_ref[...]
    act = _silu(_dot(h, wg_ref[0])) * _dot(h, wu_ref[0])
    acc_ref[...] += _dot(act.astype(BF16), w2_ref[0])

    @pl.when(j == pl.num_programs(1) - 1)
    def _():
        xn = x_ref[...] + mod_ref[0][5:6] * acc_ref[...]
        xo_ref[...] = xn
        if with_next:
            modn = modn_ref[0]
            hn = _rms(xn, gn_ref[...])
            hn_ref[...] = (hn * (1.0 + modn[1:2]) + modn[0:1]).astype(BF16)
        else:
            hn_ref[...] = jnp.zeros_like(hn_ref)


def _ffn_call(h, w_in, w_out, layer, x, mod, mod_next, gain_next, rows_per_mod, tm, tn, with_next):
    m, d = x.shape
    hidden = w_out.shape[1]
    nj = hidden // tn
    per = rows_per_mod // tm
    return pl.pallas_call(
        functools.partial(_ffn_kernel, with_next=with_next),
        grid=(m // tm, nj),
        in_specs=[pl.BlockSpec((tm, d), lambda i, j: (i, 0)),
                  pl.BlockSpec((1, d, tn), lambda i, j: (layer, 0, j)),
                  pl.BlockSpec((1, d, tn), lambda i, j: (layer, 0, nj + j)),
                  pl.BlockSpec((1, tn, d), lambda i, j: (layer, j, 0)),
                  pl.BlockSpec((tm, d), lambda i, j: (i, 0)),
                  pl.BlockSpec((1, 8, d), lambda i, j: (i // per, 0, 0)),
                  pl.BlockSpec((1, 8, d), lambda i, j: (i // per, 0, 0)),
                  pl.BlockSpec((1, d), lambda i, j: (0, 0))],
        out_specs=[pl.BlockSpec((tm, d), lambda i, j: (i, 0)),
                   pl.BlockSpec((tm, d), lambda i, j: (i, 0))],
        out_shape=[jax.ShapeDtypeStruct((m, d), F32), jax.ShapeDtypeStruct((m, d), BF16)],
        scratch_shapes=[pltpu.VMEM((tm, d), F32)],
        compiler_params=_cparams(("arbitrary", "arbitrary")),
        name="ffn",
    )(h, w_in, w_in, w_out, x, mod, mod_next, gain_next)


def _na_kernel(q_ref, k_ref, v_ref, kc_ref, vc_ref, gain_ref, colbias_ref, o_ref,
               kn_scr, v_scr, kcn_scr, vc_scr, bias_scr, sl0_scr, sc0_scr, sl1_scr, sc1_scr, *, rows):
    step = pl.program_id(2)
    n_rb = pl.num_programs(2) - 1

    @pl.when(step == 0)
    def _():
        sl1_scr[...] = jnp.zeros_like(sl1_scr)
        sc1_scr[...] = jnp.zeros_like(sc1_scr)
        g1 = gain_ref[1:2, :]
        kn_scr[...] = _rms(k_ref[...], g1).astype(BF16)
        v_scr[...] = v_ref[...].astype(BF16)
        kcn_scr[...] = _rms(kc_ref[...], g1).astype(BF16)
        vc_scr[...] = vc_ref[...].astype(BF16)
        closed = jnp.full((GRID_W, GRID_W), NEG_INF, F32)
        for typ, (offs, inside) in enumerate(_na_row_plan(rows)):
            for i in range(NA_QROWS):
                for jp in range(NA_KROWS // 2):
                    halves = [colbias_ref[0, 0, offs[i][j]] if inside[i][j] else closed for j in (2 * jp, 2 * jp + 1)]
                    bias_scr[typ, i * GRID_W:(i + 1) * GRID_W, 2 * jp * GRID_W:(2 * jp + 2) * GRID_W] = (
                        jnp.concatenate(halves, axis=1))

    scale = HEAD_DIM ** -0.5
    nk = NA_KROWS * GRID_W

    def window_start(blk):
        ws = jnp.clip(NA_QROWS * blk - NA_WIN_ROWS // 2, 0, rows - NA_KROWS)
        return pl.multiple_of(ws * GRID_W, GRID_W)

    def stages(write_loc, write_ctx, read_loc, read_ctx):
        prev = jnp.maximum(step - 1, 0)
        s_loc = read_loc[...]
        s_ctx = read_ctx[...]
        m = jnp.maximum(jnp.max(s_loc, axis=-1, keepdims=True), jnp.max(s_ctx, axis=-1, keepdims=True))
        p_loc = jnp.exp(s_loc - m)
        p_ctx = jnp.exp(s_ctx - m)
        den = jnp.sum(p_loc, axis=-1, keepdims=True) + jnp.sum(p_ctx, axis=-1, keepdims=True)
        vw = v_scr[pl.ds(window_start(prev), nk), :]
        o = _dot(p_loc.astype(BF16), vw) + _dot(p_ctx.astype(BF16), vc_scr[...])
        o_ref[...] = (o / den).astype(BF16)
        cur = jnp.minimum(step, n_rb - 1)
        qn = _rms(q_ref[...], gain_ref[0:1, :]).astype(BF16)
        kw = kn_scr[pl.ds(window_start(cur), nk), :]
        typ = jnp.where(cur == 0, 0, jnp.where(cur == n_rb - 1, 2, 1))
        write_loc[...] = _dot_nt(qn, kw) * scale + bias_scr[pl.ds(typ, 1)][0]
        write_ctx[...] = _dot_nt(qn, kcn_scr[...]) * scale

    @pl.when(step % 2 == 0)
    def _():
        stages(sl0_scr, sc0_scr, sl1_scr, sc1_scr)

    @pl.when(step % 2 == 1)
    def _():
        stages(sl1_scr, sc1_scr, sl0_scr, sc0_scr)


def _na_row_plan(rows):
    n_rb = rows // NA_QROWS
    kh = NA_WIN_ROWS
    plan = []
    for rb in (0, 1, n_rb - 1):
        ws = min(max(NA_QROWS * rb - kh // 2, 0), rows - NA_KROWS)
        offs, inside = [], []
        for i in range(NA_QROWS):
            r = NA_QROWS * rb + i
            rs = min(max(r - kh // 2, 0), rows - kh)
            offs.append([min(max(ws + j - r, 1 - kh), kh - 1) + kh - 1 for j in range(NA_KROWS)])
            inside.append([rs <= ws + j < rs + kh for j in range(NA_KROWS)])
        plan.append((offs, inside))
    return plan


def _na_col_tables(rpb):
    kw = NA_WIN_COLS
    c = jnp.arange(GRID_W)
    cs = jnp.clip(c - kw // 2, 0, GRID_W - kw)
    in_col = (c[None, :] >= cs[:, None]) & (c[None, :] < cs[:, None] + kw)
    dc = jnp.clip(c[None, :] - c[:, None], 1 - kw, kw - 1) + kw - 1
    dc_hot = jax.nn.one_hot(dc, 2 * kw - 1, dtype=F32)
    val = jnp.einsum("lhab,ckb->lhack", rpb.astype(F32), dc_hot, precision=lax.Precision.HIGHEST)
    return jnp.where(in_col, val, NEG_INF)


def _na_call(p_lat, p_ctx, gain, colbias, layer, batch, seq, lc):
    rows = seq // GRID_W
    n_rb = rows // NA_QROWS
    tq = NA_QROWS * GRID_W
    nk = NA_KROWS * GRID_W
    heads = colbias.shape[1]
    n_off = colbias.shape[2]
    hd = HEAD_DIM
    return pl.pallas_call(
        functools.partial(_na_kernel, rows=rows),
        grid=(batch, heads, n_rb + 1),
        in_specs=[pl.BlockSpec((tq, hd), lambda b, h, r: (b * n_rb + jnp.minimum(r, n_rb - 1), NA_Q + h)),
                  pl.BlockSpec((seq, hd), lambda b, h, r: (b, NA_K + h)),
                  pl.BlockSpec((seq, hd), lambda b, h, r: (b, NA_V + h)),
                  pl.BlockSpec((lc, hd), lambda b, h, r: (b, NA_K + h)),
                  pl.BlockSpec((lc, hd), lambda b, h, r: (b, NA_V + h)),
                  pl.BlockSpec((2, hd), lambda b, h, r: (0, 0)),
                  pl.BlockSpec((1, 1, n_off, GRID_W, GRID_W), lambda b, h, r: (layer, h, 0, 0, 0))],
        out_specs=pl.BlockSpec((tq, hd), lambda b, h, r: (b * n_rb + jnp.maximum(r - 1, 0), h)),
        out_shape=jax.ShapeDtypeStruct((batch * seq, heads * hd), BF16),
        scratch_shapes=[pltpu.VMEM((seq, hd), BF16), pltpu.VMEM((seq, hd), BF16),
                        pltpu.VMEM((lc, hd), BF16), pltpu.VMEM((lc, hd), BF16),
                        pltpu.VMEM((3, tq, nk), F32),
                        pltpu.VMEM((tq, nk), F32), pltpu.VMEM((tq, lc), F32),
                        pltpu.VMEM((tq, nk), F32), pltpu.VMEM((tq, lc), F32)],
        compiler_params=_cparams(("arbitrary", "arbitrary", "arbitrary")),
        name="na_attn",
    )(p_lat, p_lat, p_lat, p_ctx, p_ctx, gain, colbias)


def _ctx_attn_kernel(sink_ref, q_ref, k_ref, v_ref, gain_ref, o_ref, *, use_sink):
    scale = HEAD_DIM ** -0.5
    qn = _rms(q_ref[...], gain_ref[0:1, :]).astype(BF16)
    kn = _rms(k_ref[...], gain_ref[1:2, :]).astype(BF16)
    s = _dot_nt(qn, kn) * scale
    m = jnp.max(s, axis=-1, keepdims=True)
    if use_sink:
        sk = sink_ref[pl.program_id(1)]
        m = jnp.maximum(m, sk)
    p = jnp.exp(s - m)
    den = jnp.sum(p, axis=-1, keepdims=True)
    if use_sink:
        den = den + jnp.exp(sk - m)
    o = _dot(p.astype(BF16), v_ref[...].astype(BF16))
    o_ref[...] = (o / den).astype(BF16)


def _ctx_attn_call(p_ctx, gain, sink, batch, lc, heads, rep, qoff, koff, voff, use_sink, name):
    hd = HEAD_DIM
    return pl.pallas_call(
        functools.partial(_ctx_attn_kernel, use_sink=use_sink),
        grid=(batch, heads),
        in_specs=[pl.BlockSpec(memory_space=pltpu.SMEM),
                  pl.BlockSpec((lc, hd), lambda b, h: (b, qoff + h)),
                  pl.BlockSpec((lc, hd), lambda b, h: (b, koff + h // rep)),
                  pl.BlockSpec((lc, hd), lambda b, h: (b, voff + h // rep)),
                  pl.BlockSpec((2, hd), lambda b, h: (0, 0))],
        out_specs=pl.BlockSpec((lc, hd), lambda b, h: (b, h)),
        out_shape=jax.ShapeDtypeStruct((batch * lc, heads * hd), BF16),
        compiler_params=_cparams(("arbitrary", "arbitrary")),
        name=name,
    )(sink, p_ctx, p_ctx, p_ctx, gain)


def _rope(x, cos, sin_signed):
    lane = _iota(x.shape, 1)
    swapped = jnp.where(lane % 64 < 32, pltpu.roll(x, 96, axis=1), pltpu.roll(x, 32, axis=1))
    return x * cos + swapped * sin_signed


def _wa_kernel(sink_ref, q_ref, k_ref, v_ref, kc_ref, vc_ref, gain_ref, cq_ref, sq_ref, ck_ref, sk_ref, o_ref,
               kpad_scr, vpad_scr, kcn_scr, vc_scr, sl0_scr, sc0_scr, sl1_scr, sc1_scr, *, seq):
    g = pl.program_id(1)
    step = pl.program_id(2)
    nb = pl.num_programs(2) - 1
    wb = WA_BLOCK
    hd = HEAD_DIM

    @pl.when(step == 0)
    def _():
        sl1_scr[...] = jnp.zeros_like(sl1_scr)
        sc1_scr[...] = jnp.zeros_like(sc1_scr)
        g1 = gain_ref[1:2, :]
        zeros = jnp.zeros((wb, hd), BF16)
        kpad_scr[0:wb, :] = zeros
        kpad_scr[seq + wb:seq + 2 * wb, :] = zeros
        vpad_scr[0:wb, :] = zeros
        vpad_scr[seq + wb:seq + 2 * wb, :] = zeros
        kpad_scr[wb:seq + wb, :] = _rope(_rms(k_ref[...], g1), ck_ref[...], sk_ref[...]).astype(BF16)
        vpad_scr[wb:seq + wb, :] = v_ref[...].astype(BF16)
        kcn_scr[...] = _rms(kc_ref[...], g1).astype(BF16)
        vc_scr[...] = vc_ref[...].astype(BF16)

    scale = hd ** -0.5

    def stages(write_loc, write_ctx, read_loc, read_ctx):
        prev = jnp.maximum(step - 1, 0)
        s_loc = read_loc[...]
        s_ctx = read_ctx[...]
        sink = jnp.where(_iota((2 * wb, 1), 0) < wb, sink_ref[2 * g], sink_ref[2 * g + 1])
        m = jnp.maximum(jnp.maximum(jnp.max(s_loc, axis=-1, keepdims=True), jnp.max(s_ctx, axis=-1, keepdims=True)),
                        sink)
        p_loc = jnp.exp(s_loc - m)
        p_ctx = jnp.exp(s_ctx - m)
        den = jnp.sum(p_loc, axis=-1, keepdims=True) + jnp.sum(p_ctx, axis=-1, keepdims=True) + jnp.exp(sink - m)
        vw = vpad_scr[pl.ds(pl.multiple_of(prev * wb, wb), 3 * wb), :]
        o = (_dot(p_loc.astype(BF16), vw) + _dot(p_ctx.astype(BF16), vc_scr[...])) / den
        o_ref[:, 0:hd] = o[0:wb].astype(BF16)
        o_ref[:, hd:2 * hd] = o[wb:2 * wb].astype(BF16)
        cur = jnp.minimum(step, nb - 1)
        g0 = gain_ref[0:1, :]
        cq = cq_ref[...]
        sq = sq_ref[...]
        q2 = jnp.concatenate([_rope(_rms(q_ref[:, 0:hd], g0), cq, sq),
                              _rope(_rms(q_ref[:, hd:2 * hd], g0), cq, sq)], axis=0).astype(BF16)
        kw = kpad_scr[pl.ds(pl.multiple_of(cur * wb, wb), 3 * wb), :]
        shp = (2 * wb, 3 * wb)
        i = _iota(shp, 0) % wb
        j = _iota(shp, 1)
        kpos = (cur - 1) * wb + j
        ok = (j - i >= 0) & (j - i <= 2 * wb) & (kpos >= 0) & (kpos < seq)
        write_loc[...] = jnp.where(ok, _dot_nt(q2, kw) * scale, NEG_INF)
        write_ctx[...] = _dot_nt(q2, kcn_scr[...]) * scale

    @pl.when(step % 2 == 0)
    def _():
        stages(sl0_scr, sc0_scr, sl1_scr, sc1_scr)

    @pl.when(step % 2 == 1)
    def _():
        stages(sl1_scr, sc1_scr, sl0_scr, sc0_scr)


def _wa_call(p_lat, p_ctx, gain, sink, cos_t, sin_t, batch, seq, lc):
    wb = WA_BLOCK
    hd = HEAD_DIM
    nb = seq // wb
    kvh = 2
    return pl.pallas_call(
        functools.partial(_wa_kernel, seq=seq),
        grid=(batch, kvh, nb + 1),
        in_specs=[pl.BlockSpec(memory_space=pltpu.SMEM),
                  pl.BlockSpec((wb, 2 * hd), lambda b, g, n: (b * nb + jnp.minimum(n, nb - 1), WA_Q // 2 + g)),
                  pl.BlockSpec((seq, hd), lambda b, g, n: (b, WA_K + g)),
                  pl.BlockSpec((seq, hd), lambda b, g, n: (b, WA_V + g)),
                  pl.BlockSpec((lc, hd), lambda b, g, n: (b, WA_K + g)),
                  pl.BlockSpec((lc, hd), lambda b, g, n: (b, WA_V + g)),
                  pl.BlockSpec((2, hd), lambda b, g, n: (0, 0)),
                  pl.BlockSpec((wb, hd), lambda b, g, n: (jnp.minimum(n, nb - 1), 0)),
                  pl.BlockSpec((wb, hd), lambda b, g, n: (jnp.minimum(n, nb - 1), 0)),
                  pl.BlockSpec((seq, hd), lambda b, g, n: (0, 0)),
                  pl.BlockSpec((seq, hd), lambda b, g, n: (0, 0))],
        out_specs=pl.BlockSpec((wb, 2 * hd), lambda b, g, n: (b * nb + jnp.maximum(n - 1, 0), g)),
        out_shape=jax.ShapeDtypeStruct((batch * seq, 2 * kvh * hd), BF16),
        scratch_shapes=[pltpu.VMEM((seq + 2 * wb, hd), BF16), pltpu.VMEM((seq + 2 * wb, hd), BF16),
                        pltpu.VMEM((lc, hd), BF16), pltpu.VMEM((lc, hd), BF16),
                        pltpu.VMEM((2 * wb, 3 * wb), F32), pltpu.VMEM((2 * wb, lc), F32),
                        pltpu.VMEM((2 * wb, 3 * wb), F32), pltpu.VMEM((2 * wb, lc), F32)],
        compiler_params=_cparams(("arbitrary", "arbitrary", "arbitrary")),
        name="wa_attn",
    )(sink, p_lat, p_lat, p_lat, p_ctx, p_ctx, gain, cos_t, sin_t, cos_t, sin_t)


def _dn_prep_kernel(x_ref, prev_ref, next_ref, w_ref, o_ref, *, n_tiles):
    t = pl.program_id(1)
    tl = x_ref.shape[0]
    x = x_ref[...]
    prev = jnp.where(t > 0, prev_ref[...], 0.0)
    nxt = jnp.where(t < n_tiles - 1, next_ref[...], 0.0)
    xe = jnp.concatenate([prev, x, nxt], axis=0)
    n = tl + 16
    taps = w_ref.shape[0]
    acc = jnp.zeros_like(x)
    for j in range(taps):
        off = 8 - taps // 2 + j
        shifted = xe if off % n == 0 else pltpu.roll(xe, n - off, axis=0)
        acc += shifted[0:tl] * w_ref[j:j + 1, :]
    y = _silu(acc)
    hd = HEAD_DIM
    nh = y.shape[1] // (3 * hd)
    for blk in range(3 * nh):
        yb = y[:, blk * hd:(blk + 1) * hd]
        if blk < 2 * nh:
            yb = yb * lax.rsqrt(jnp.sum(yb * yb, axis=-1, keepdims=True) + EPS)
            if blk < nh:
                yb = yb * hd ** -0.5
        o_ref[:, blk * hd:(blk + 1) * hd] = yb


def _dn_prep_call(p, conv_w, batch, length, tl):
    width = conv_w.shape[1]
    n_tiles = length // tl
    per8 = tl // 8
    last8 = batch * length // 8 - 1
    return pl.pallas_call(
        functools.partial(_dn_prep_kernel, n_tiles=n_tiles),
        grid=(batch, n_tiles),
        in_specs=[pl.BlockSpec((tl, width), lambda b, t: (b * n_tiles + t, 1)),
                  pl.BlockSpec((8, width), lambda b, t: (jnp.maximum((b * n_tiles + t) * per8 - 1, 0), 1)),
                  pl.BlockSpec((8, width), lambda b, t: (jnp.minimum((b * n_tiles + t + 1) * per8, last8), 1)),
                  pl.BlockSpec((conv_w.shape[0], width), lambda b, t: (0, 0))],
        out_specs=pl.BlockSpec((tl, width), lambda b, t: (b * n_tiles + t, 0)),
        out_shape=jax.ShapeDtypeStruct((batch * length, width), F32),
        compiler_params=_cparams(("arbitrary", "arbitrary")),
        name="dn_prep",
    )(p, p, p, conv_w)


def _split_bf16(a):
    hi = a.astype(BF16)
    return hi, (a - hi.astype(F32)).astype(BF16)


def _solve_dot(a, b, mode):
    if mode == "highest":
        return _dot(a, b, lax.Precision.HIGHEST)
    if mode == "bf16":
        return _dot(a.astype(BF16), b.astype(BF16))
    a_hi, a_lo = _split_bf16(a)
    b_hi, b_lo = _split_bf16(b)
    return _dot(a_hi, b_hi) + (_dot(a_hi, b_lo) + _dot(a_lo, b_hi))


def _dn_pass(q_ref, k_ref, v_ref, og_ref, gates_ref, gadd, gmul, norm_g, y_ref, length, h, state,
             lin_scr, add_scr, egl_scr, acc_scr, solve_precision):
    assert CHUNK == HEAD_DIM
    n_chunks = length // CHUNK

    acc_scr[0:length, :] = jnp.zeros((length, HEAD_DIM), F32)

    shp = (CHUNK, CHUNK)
    ri = _iota(shp, 0)
    ci = _iota(shp, 1)
    masks = ((ci <= ri, ci < ri), (ci >= ri, ci > ri))
    couple = ([], [])
    s = 1
    while s < CHUNK:
        same = (ri // (2 * s)) == (ci // (2 * s))
        couple[0].append(same & ((ri // s) % 2 == 1) & ((ci // s) % 2 == 0))
        couple[1].append(same & ((ri // s) % 2 == 0) & ((ci // s) % 2 == 1))
        s *= 2
    group = min(n_chunks, PREP_GROUP)

    def prep(gi, carry):
        chains = []
        for g in range(group):
            c = gi * group + g
            r0 = pl.multiple_of(c * CHUNK, CHUNK)
            q = q_ref[pl.ds(r0, CHUNK), :]
            k = k_ref[pl.ds(r0, CHUNK), :]
            raw = gates_ref[pl.ds(r0, CHUNK), :]
            gb = jnp.where(_iota(raw.shape, 1) < LANE_ALPHA, jax.nn.sigmoid(raw), -gmul * _softplus(raw + gadd))
            k16 = k.astype(BF16)
            kk = _dot_nt(k16, k16)
            qk = _dot_nt(q.astype(BF16), k16)
            for d in range(2):
                incl, strict = masks[d]
                beta_col = _lane_pick(gb, LANE_BETA + 4 * d + h)
                g_col = _lane_pick(gb, LANE_ALPHA + 4 * d + h)
                beta_row = _to_row(beta_col)
                g_row = _to_row(g_col)
                cum_col = jnp.sum(jnp.where(incl, g_row, 0.0), axis=-1, keepdims=True)
                cum_row = _to_row(cum_col)
                tot = jnp.sum(g_col, axis=0, keepdims=True)
                decay = jnp.exp(jnp.where(incl, cum_col - cum_row, NEG_INF))
                lm = jnp.where(strict, kk * beta_col * decay, 0.0)
                egl_scr[d, pl.ds(pl.multiple_of(c * 8, 8), 8), :] = jnp.broadcast_to(jnp.exp(tot), (8, HEAD_DIM))
                chains.append((d, c, r0, lm, beta_row, beta_row * jnp.exp(cum_row), (qk * decay).astype(BF16),
                               q * jnp.exp(cum_col), k * jnp.exp(tot - cum_col)))
        eye = (ri == ci).astype(F32)
        xs = [eye - jnp.where(couple[ch[0]][0], ch[3], 0.0) for ch in chains]
        for lvl in range(1, len(couple[0])):
            cs = [jnp.where(couple[ch[0]][lvl], ch[3], 0.0) for ch in chains]
            xc = [_solve_dot(x, c, solve_precision) for x, c in zip(xs, cs)]
            xs = [x - _solve_dot(t, x, solve_precision) for x, t in zip(xs, xc)]
        u16 = [_solve_dot(ainv * ch[4], v_ref[pl.ds(ch[2], CHUNK), :], solve_precision).astype(BF16)
               for ch, ainv in zip(chains, xs)]
        w16 = [_solve_dot(ainv * ch[5], k_ref[pl.ds(ch[2], CHUNK), :], solve_precision).astype(BF16)
               for ch, ainv in zip(chains, xs)]
        ket = [ch[8].T.astype(BF16) for ch in chains]
        s_lin = [_dot(kt, w) for kt, w in zip(ket, w16)]
        s_add = [_dot(kt, u) for kt, u in zip(ket, u16)]
        o_lin = [ch[7] - _dot(ch[6], w) for ch, w in zip(chains, w16)]
        o_add = [_dot(ch[6], u) for ch, u in zip(chains, u16)]
        for ch, sl, sa, ol, oa in zip(chains, s_lin, s_add, o_lin, o_add):
            d = ch[0]
            base = pl.multiple_of(ch[1] * 2 * CHUNK, 2 * CHUNK)
            lin_scr[d, pl.ds(base, CHUNK), :] = sl.astype(BF16)
            lin_scr[d, pl.ds(base + CHUNK, CHUNK), :] = ol.astype(BF16)
            add_scr[d, pl.ds(base, CHUNK), :] = sa
            add_scr[d, pl.ds(base + CHUNK, CHUNK), :] = oa
        return carry

    lax.fori_loop(0, n_chunks // group, prep, 0)

    def scan(s, st):
        new = []
        for d in range(2):
            c = s if d == 0 else n_chunks - 1 - s
            base = pl.multiple_of(c * 2 * CHUNK, 2 * CHUNK)
            sm = st[d]
            prod = _dot(lin_scr[d, pl.ds(base, 2 * CHUNK), :], sm.astype(BF16))
            add = add_scr[d, pl.ds(base, 2 * CHUNK), :]
            acc_scr[pl.ds(pl.multiple_of(c * CHUNK, CHUNK), CHUNK), :] += prod[CHUNK:] + add[CHUNK:]
            egl = egl_scr[d, pl.ds(pl.multiple_of(c * 8, 8), 8), :][0:1, :]
            new.append(sm * egl - prod[:CHUNK] + add[:CHUNK])
        return tuple(new)

    state = lax.fori_loop(0, n_chunks, scan, state, unroll=SCAN_UNROLL)

    y = _rms(acc_scr[0:length, :], norm_g) * _silu(og_ref[...])
    y_ref[...] = y.astype(BF16)
    return state


def _dn_kernel(q_ref, k_ref, v_ref, og_ref, gates_ref, qc_ref, kc_ref, vc_ref, ogc_ref, gatesc_ref,
               gadd_ref, gmul_ref, norm_ref, y_ref, yc_ref,
               lin_scr, add_scr, egl_scr, acc_scr, *, seq, lc, solve_precision):
    h = pl.program_id(1)
    gadd = gadd_ref[...]
    gmul = gmul_ref[...]
    norm_g = norm_ref[...]
    scr = (lin_scr, add_scr, egl_scr, acc_scr)
    zero = jnp.zeros((HEAD_DIM, HEAD_DIM), F32)
    state = _dn_pass(qc_ref, kc_ref, vc_ref, ogc_ref, gatesc_ref, gadd, gmul, norm_g, yc_ref, lc, h, (zero, zero),
                     *scr, solve_precision)
    _dn_pass(q_ref, k_ref, v_ref, og_ref, gates_ref, gadd, gmul, norm_g, y_ref, seq, h, state,
             *scr, solve_precision)


def _dn_call(dq_lat, dq_ctx, p_lat, p_ctx, pg_lat, pg_ctx, gadd, gmul, norm_g, batch, seq, lc, solve_precision):
    hd = HEAD_DIM
    heads = 4

    def blk(n, col):
        return pl.BlockSpec((n, hd), lambda b, h: (b, col + h))

    def gate_blk(n):
        return pl.BlockSpec((n, hd), lambda b, h: (b, 0))

    vec = pl.BlockSpec((1, hd), lambda b, h: (0, 0))
    return pl.pallas_call(
        functools.partial(_dn_kernel, seq=seq, lc=lc, solve_precision=solve_precision),
        grid=(batch, heads),
        in_specs=[blk(seq, 0), blk(seq, heads), blk(seq, 2 * heads), blk(seq, DN_OG), gate_blk(seq),
                  blk(lc, 0), blk(lc, heads), blk(lc, 2 * heads), blk(lc, DN_OG), gate_blk(lc),
                  vec, vec, vec],
        out_specs=[pl.BlockSpec((seq, hd), lambda b, h: (b, h)),
                   pl.BlockSpec((lc, hd), lambda b, h: (b, h))],
        out_shape=[jax.ShapeDtypeStruct((batch * seq, heads * hd), BF16),
                   jax.ShapeDtypeStruct((batch * lc, heads * hd), BF16)],
        scratch_shapes=[pltpu.VMEM((2, 2 * seq, hd), BF16),
                        pltpu.VMEM((2, 2 * seq, hd), F32),
                        pltpu.VMEM((2, seq // CHUNK * 8, hd), F32),
                        pltpu.VMEM((seq, hd), F32)],
        compiler_params=_cparams(("arbitrary", "arbitrary"), DN_VMEM_LIMIT),
        name="deltanet",
    )(dq_lat, dq_lat, dq_lat, p_lat, pg_lat, dq_ctx, dq_ctx, dq_ctx, p_ctx, pg_ctx, gadd, gmul, norm_g)


def _ml_pass(q_ref, k_ref, v_ref, og_ref, gates_ref, gadd, norm_g, y_ref, length, h, state, acc_scr):
    n_chunks = length // CHUNK
    acc_scr[0:length, :] = jnp.zeros((length, HEAD_DIM), F32)
    shp = (CHUNK, CHUNK)
    ri = _iota(shp, 0)
    ci = _iota(shp, 1)
    masks = (ci <= ri, ci >= ri)
    head_lanes = (_iota((1, HEAD_DIM), 1) // ML_QK) == (h % 2)

    group = min(n_chunks, ML_GROUP)

    def each(f, *lists):
        return [f(*a) for a in zip(*lists)]

    def step(gi, st):
        dirs, r0s = [], []
        for j in range(group):
            s = gi * group + j
            for d in range(2):
                c = s if d == 0 else n_chunks - 1 - s
                dirs.append(d)
                r0s.append(pl.multiple_of(c * CHUNK, CHUNK))
        incl = [masks[d] for d in dirs]
        q = [jnp.where(head_lanes, q_ref[pl.ds(r0, CHUNK), :], 0.0) for r0 in r0s]
        k = [jnp.where(head_lanes, k_ref[pl.ds(r0, CHUNK), :], 0.0) * ML_QK ** -0.5 for r0 in r0s]
        v16 = [v_ref[pl.ds(r0, CHUNK), :].astype(BF16) for r0 in r0s]
        gb = [gates_ref[pl.ds(r0, CHUNK), :] + gadd for r0 in r0s]
        ig_col = [_lane_pick(g, LANE_IG + 4 * d + h) for g, d in zip(gb, dirs)]
        lf_col = [_log_sigmoid(_lane_pick(g, LANE_FG + 4 * d + h)) for g, d in zip(gb, dirs)]
        ig_row = each(_to_row, ig_col)
        lf_row = each(_to_row, lf_col)
        b_col = each(lambda m, r: jnp.sum(jnp.where(m, r, 0.0), axis=-1, keepdims=True), incl, lf_row)
        b_row = each(_to_row, b_col)
        b_last = each(lambda x: jnp.sum(x, axis=0, keepdims=True), lf_col)
        g_end = each(lambda bl, b, i: bl - b + i, b_last, b_col, ig_col)
        m_loc = each(lambda x: jnp.max(x, axis=0, keepdims=True), g_end)
        kw = each(lambda kk, g, m: kk * jnp.exp(g - m), k, g_end, m_loc)
        q16 = each(lambda x: x.astype(BF16), q)
        k16 = each(lambda x: x.astype(BF16), k)
        c_loc = each(lambda a, b: _dot_tn(a.astype(BF16), b), kw, v16)
        n_loc = each(lambda x: jnp.sum(x, axis=0, keepdims=True), kw)
        log_d = each(lambda m, bc, br, ir: jnp.where(m, bc - br + ir, NEG_INF), incl, b_col, b_row, ig_row)
        mx_col = each(lambda x: jnp.max(x, axis=-1, keepdims=True), log_d)
        s0 = each(lambda a, b, ld, mx: _dot_nt(a, b) * jnp.exp(ld - mx), q16, k16, log_d, mx_col)
        sv0 = each(lambda s, vv: _dot(s.astype(BF16), vv), s0, v16)
        rs0 = each(lambda s: jnp.sum(s, axis=-1, keepdims=True), s0)
        st = list(st)
        for i, (d, r0) in enumerate(zip(dirs, r0s)):
            c_st, n_st, m_st = st[d]
            m_inter = b_col[i] + m_st
            m_t = jnp.maximum(mx_col[i], m_inter)
            intra = jnp.exp(mx_col[i] - m_t)
            inter = jnp.exp(m_inter - m_t)
            num = intra * sv0[i] + inter * _dot(q16[i], c_st.astype(BF16))
            den = intra * rs0[i] + inter * jnp.sum(q[i] * n_st, axis=-1, keepdims=True)
            acc_scr[pl.ds(r0, CHUNK), :] += num / jnp.maximum(jnp.abs(den), jnp.exp(-m_t))
            m_new = jnp.maximum(b_last[i] + m_st, m_loc[i])
            a = jnp.exp(b_last[i] + m_st - m_new)
            e = jnp.exp(m_loc[i] - m_new)
            st[d] = (a * c_st + e * c_loc[i], a * n_st + e * n_loc[i], m_new)
        return tuple(st)

    state = lax.fori_loop(0, n_chunks // group, step, state)
    y = jax.nn.sigmoid(og_ref[...]) * _rms(acc_scr[0:length, :], norm_g)
    y_ref[...] = y.astype(BF16)
    return state


def _ml_kernel(q_ref, k_ref, v_ref, og_ref, gates_ref, qc_ref, kc_ref, vc_ref, ogc_ref, gatesc_ref,
               gadd_ref, norm_ref, y_ref, yc_ref, acc_scr, *, seq, lc):
    h = pl.program_id(1)
    gadd = gadd_ref[...]
    norm_g = norm_ref[0]
    zero = (jnp.zeros((HEAD_DIM, HEAD_DIM), F32), jnp.zeros((1, HEAD_DIM), F32), jnp.zeros((1, 1), F32))
    state = _ml_pass(qc_ref, kc_ref, vc_ref, ogc_ref, gatesc_ref, gadd, norm_g, yc_ref, lc, h, (zero, zero), acc_scr)
    _ml_pass(q_ref, k_ref, v_ref, og_ref, gates_ref, gadd, norm_g, y_ref, seq, h, state, acc_scr)


def _ml_call(p_lat, p_ctx, pg_lat, pg_ctx, gadd, norm_g, batch, seq, lc):
    hd = HEAD_DIM
    heads = 4

    def specs(n):
        return [pl.BlockSpec((n, hd), lambda b, h: (b, ML_Q + h // 2)),
                pl.BlockSpec((n, hd), lambda b, h: (b, ML_K + h // 2)),
                pl.BlockSpec((n, hd), lambda b, h: (b, ML_V + h)),
                pl.BlockSpec((n, hd), lambda b, h: (b, ML_OG + h)),
                pl.BlockSpec((n, hd), lambda b, h: (b, 0))]

    return pl.pallas_call(
        functools.partial(_ml_kernel, seq=seq, lc=lc),
        grid=(batch, heads),
        in_specs=specs(seq) + specs(lc) + [pl.BlockSpec((1, hd), lambda b, h: (0, 0)),
                                            pl.BlockSpec((1, 1, hd), lambda b, h: (h, 0, 0))],
        out_specs=[pl.BlockSpec((seq, hd), lambda b, h: (b, h)),
                   pl.BlockSpec((lc, hd), lambda b, h: (b, h))],
        out_shape=[jax.ShapeDtypeStruct((batch * seq, heads * hd), BF16),
                   jax.ShapeDtypeStruct((batch * lc, heads * hd), BF16)],
        scratch_shapes=[pltpu.VMEM((seq, hd), F32)],
        compiler_params=_cparams(("arbitrary", "arbitrary")),
        name="mlstm",
    )(p_lat, p_lat, p_lat, p_lat, pg_lat, p_ctx, p_ctx, p_ctx, p_ctx, pg_ctx, gadd, norm_g.reshape(heads, 1, hd))


def _regroup_kernel(wt_ref, main_ref, gate_ref, *, cuts):
    (a0, a1), (b0, b1), (c0, c1), (g0, g1), (h0, h1) = cuts
    main_ref[0, 0:a1 - a0, :] = wt_ref[0, a0:a1, :].astype(BF16)
    main_ref[0, a1 - a0:a1 - a0 + b1 - b0, :] = wt_ref[0, b0:b1, :].astype(BF16)
    main_ref[0, a1 - a0 + b1 - b0:, :] = wt_ref[0, c0:c1, :].astype(BF16)
    ng = (g1 - g0) + (h1 - h0)
    gate_ref[0, 0:g1 - g0, :] = wt_ref[0, g0:g1, :].astype(BF16)
    gate_ref[0, g1 - g0:ng, :] = wt_ref[0, h0:h1, :].astype(BF16)
    gate_ref[0, ng:, :] = jnp.zeros((gate_ref.shape[1] - ng, gate_ref.shape[2]), BF16)


def _regroup_call(w_in):
    depth, d, n = w_in.shape
    gw = 512
    dn_gate0 = 3 * gw + 4 * gw
    ml0 = dn_gate0 + 16
    ml_gate0 = ml0 + 3 * gw
    wa0 = ml_gate0 + 16
    cuts = ((0, dn_gate0), (ml0, ml_gate0), (wa0, n), (dn_gate0, ml0), (ml_gate0, wa0))
    tc = 256
    return pl.pallas_call(
        functools.partial(_regroup_kernel, cuts=cuts),
        grid=(depth, d // tc),
        in_specs=[pl.BlockSpec((1, n, tc), lambda l, i: (l, 0, i))],
        out_specs=[pl.BlockSpec((1, MAIN_COLS, tc), lambda l, i: (l, 0, i)),
                   pl.BlockSpec((1, 128, tc), lambda l, i: (l, 0, i))],
        out_shape=[jax.ShapeDtypeStruct((depth, MAIN_COLS, d), BF16), jax.ShapeDtypeStruct((depth, 128, d), BF16)],
        compiler_params=_cparams(("arbitrary", "arbitrary")),
        name="regroup_w_in",
    )(jnp.swapaxes(w_in, 1, 2))


def _rope_tables(seq):
    t = jnp.arange(seq)
    n_freq = HEAD_DIM // 4
    inv_freq = ROPE_THETA ** (-jnp.arange(n_freq, dtype=F32) / n_freq)
    pos = jnp.stack([t // GRID_W, t % GRID_W], axis=-1).astype(F32)
    ang = pos[:, :, None] * inv_freq
    cos, sin = jnp.cos(ang), jnp.sin(ang)
    cos_t = jnp.concatenate([cos, cos], axis=-1).reshape(seq, HEAD_DIM)
    sin_t = jnp.concatenate([-sin, sin], axis=-1).reshape(seq, HEAD_DIM)
    return cos_t, sin_t


def _gate_vectors(dt_bias, a_log, i_bias, f_bias):
    zeros8 = jnp.zeros((8,), F32)
    gadd = jnp.concatenate([zeros8, dt_bias.reshape(-1), i_bias.reshape(-1), f_bias.reshape(-1),
                            jnp.zeros((96,), F32)]).astype(F32)
    gmul = jnp.concatenate([zeros8, jnp.exp(a_log.astype(F32)).reshape(-1), jnp.zeros((112,), F32)])
    return gadd.reshape(1, 128), gmul.reshape(1, 128)


def _forward(x, c, ctx, c_ctx, w_ada, b_ada, norm_mix, norm_ffn, w_in, w_out, na_qk_gain, na_rpb, dn_conv,
             dn_a_log, dn_dt_bias, dn_norm, ml_i_bias, ml_f_bias, ml_norm, wa_qk_gain, wa_sink, w_ffn_in,
             w_ffn_out, solve_precision):
    batch, seq, d = x.shape
    lc = ctx.shape[1]
    depth = w_ada.shape[0]
    tm = 512
    tn_ffn = 512

    cos_t, sin_t = _rope_tables(seq)
    na_bias = _na_col_tables(na_rpb)
    cs = jnp.concatenate([c, c_ctx[None, :], jnp.zeros((8 - batch - 1, d), F32)], axis=0)
    mods = _ada_call(cs, w_ada, b_ada).reshape(depth, 8, 6, d)
    mods = jnp.pad(mods, ((0, 0), (0, 0), (0, 2), (0, 0)))

    w_main, w_gate = _regroup_call(w_in)
    w_o = w_out.astype(BF16)
    w_f1 = w_ffn_in.astype(BF16)
    w_f2 = w_ffn_out.astype(BF16)

    xl = x.reshape(batch * seq, d)
    xc = ctx.reshape(batch * lc, d)
    hl = _norm_call(xl, mods[0, :batch], norm_mix[0:1], seq, tm)
    hc = _norm_call(xc, mods[0, batch:batch + 1], norm_mix[0:1], batch * lc, tm)

    for l in range(depth):
        need_ctx = l < depth - 1
        mod_l, mod_c = mods[l, :batch], mods[l, batch:batch + 1]
        p_lat = _matmul_call(hl, w_main, l, tm, 1024, "inproj")
        pg_lat = _matmul_call(hl, w_gate, l, tm, 128, "inproj_gates")
        p_ctx = _matmul_call(hc, w_main, l, tm, 1024, "inproj_ctx")
        pg_ctx = _matmul_call(hc, w_gate, l, tm, 128, "inproj_gates_ctx")

        ya = _na_call(p_lat, p_ctx, na_qk_gain[l], na_bias, l, batch, seq, lc)
        gadd, gmul = _gate_vectors(dn_dt_bias[l], dn_a_log[l], ml_i_bias[l], ml_f_bias[l])
        dq_lat = _dn_prep_call(p_lat, dn_conv[l], batch, seq, 256)
        dq_ctx = _dn_prep_call(p_ctx, dn_conv[l], batch, lc, 256)
        yb, ycb = _dn_call(dq_lat, dq_ctx, p_lat, p_ctx, pg_lat, pg_ctx, gadd, gmul, dn_norm[l].reshape(1, -1),
                           batch, seq, lc, solve_precision)
        ym, ycm = _ml_call(p_lat, p_ctx, pg_lat, pg_ctx, gadd, ml_norm[l], batch, seq, lc)
        yw = _wa_call(p_lat, p_ctx, wa_qk_gain[l], wa_sink[l], cos_t, sin_t, batch, seq, lc)

        nl = min(l + 1, depth - 1)
        xl, h2 = _outproj_call((ya, yb, ym, yw), w_o, l, xl, mod_l, norm_ffn[l:l + 1], seq, tm)
        xl, hl = _ffn_call(h2, w_f1, w_f2, l, xl, mod_l, mods[nl, :batch], norm_mix[nl:nl + 1], seq, tm, tn_ffn,
                           need_ctx)
        if need_ctx:
            yca = _ctx_attn_call(p_ctx, na_qk_gain[l], wa_sink[l], batch, lc, 4, 1, NA_Q, NA_K, NA_V, False,
                                 "na_ctx_attn")
            ycw = _ctx_attn_call(p_ctx, wa_qk_gain[l], wa_sink[l], batch, lc, 4, 2, WA_Q, WA_K, WA_V, True,
                                 "wa_ctx_attn")
            xc, h2c = _outproj_call((yca, ycb, ycm, ycw), w_o, l, xc, mod_c, norm_ffn[l:l + 1], batch * lc, tm)
            xc, hc = _ffn_call(h2c, w_f1, w_f2, l, xc, mod_c, mods[nl, batch:batch + 1], norm_mix[nl:nl + 1],
                               batch * lc, tm, tn_ffn, True)
    return xl.reshape(batch, seq, d)


def kernel(x, c, ctx, c_ctx, w_ada, b_ada, norm_mix, norm_ffn, w_in, w_out, na_qk_gain, na_rpb, dn_conv, dn_a_log,
           dn_dt_bias, dn_norm, ml_i_bias, ml_f_bias, ml_norm, wa_qk_gain, wa_sink, w_ffn_in, w_ffn_out):
    return _forward(x, c, ctx, c_ctx, w_ada, b_ada, norm_mix, norm_ffn, w_in, w_out, na_qk_gain, na_rpb, dn_conv,
                    dn_a_log, dn_dt_bias, dn_norm, ml_i_bias, ml_f_bias, ml_norm, wa_qk_gain, wa_sink, w_ffn_in,
                    w_ffn_out, "bf16")
```

```python
import functools

import jax
import jax.numpy as jnp
from jax import lax
from jax.experimental import pallas as pl
from jax.experimental.pallas import tpu as pltpu

F32 = jnp.float32
BF16 = jnp.bfloat16

HEAD_DIM = 128
GRID_W = 64
CHUNK = 128
PREP_GROUP = 4
ML_GROUP = 4
SCAN_UNROLL = 2
NA_WIN_ROWS = 8
NA_WIN_COLS = 16
NA_QROWS = 4
NA_KROWS = NA_QROWS + NA_WIN_ROWS
NA_HEADS_PER_STEP = 2
WA_BLOCK = 128
ML_QK = 64
ROPE_THETA = 10000.0
EPS = 1e-6
NEG_INF = float("-inf")
VMEM_LIMIT = 48 * 1024 * 1024
DN_VMEM_LIMIT = 56 * 1024 * 1024

NA_Q, NA_K, NA_V = 0, 4, 8
DN_OG = 24
ML_Q, ML_K, ML_V, ML_OG = 28, 30, 32, 36
WA_Q, WA_K, WA_V = 40, 44, 46
MAIN_COLS = 48 * 128
LANE_BETA, LANE_ALPHA, LANE_IG, LANE_FG = 0, 8, 16, 24


def _cparams(sem, vmem_limit=VMEM_LIMIT):
    return pltpu.CompilerParams(dimension_semantics=sem, vmem_limit_bytes=vmem_limit)


def _rms(x, gain):
    return x * lax.rsqrt(jnp.mean(x * x, axis=-1, keepdims=True) + EPS) * gain


def _silu(x):
    return x * jax.nn.sigmoid(x)


def _softplus(x):
    return jnp.maximum(x, 0.0) + jnp.log1p(jnp.exp(-jnp.abs(x)))


def _log_sigmoid(x):
    return jnp.minimum(x, 0.0) - jnp.log1p(jnp.exp(-jnp.abs(x)))


def _dot(a, b, precision=None):
    return jnp.dot(a, b, preferred_element_type=F32, precision=precision)


def _dot_nt(a, b):
    return lax.dot_general(a, b, (((1,), (1,)), ((), ())), preferred_element_type=F32)


def _dot_tn(a, b):
    return lax.dot_general(a, b, (((0,), (0,)), ((), ())), preferred_element_type=F32)


def _iota(shape, dim):
    return lax.broadcasted_iota(jnp.int32, shape, dim)


def _lane_pick(x, lane):
    return jnp.sum(jnp.where(_iota(x.shape, 1) == lane, x, 0.0), axis=-1, keepdims=True)


def _to_row(col):
    n = col.shape[0]
    eye = _iota((n, n), 0) == _iota((n, n), 1)
    return jnp.sum(jnp.where(eye, col, 0.0), axis=0, keepdims=True)


def _ada_kernel(c_ref, w_ref, b_ref, o_ref):
    a = _silu(c_ref[...]).astype(BF16)
    o_ref[0] = _dot(a, w_ref[0].astype(BF16)) + b_ref[0]


def _ada_call(cs, w_ada, b_ada):
    depth, d, n = w_ada.shape
    tn = 2048
    return pl.pallas_call(
        _ada_kernel,
        grid=(depth, n // tn),
        in_specs=[pl.BlockSpec((8, d), lambda l, j: (0, 0)),
                  pl.BlockSpec((1, d, tn), lambda l, j: (l, 0, j)),
                  pl.BlockSpec((1, 1, tn), lambda l, j: (l, 0, j))],
        out_specs=pl.BlockSpec((1, 8, tn), lambda l, j: (l, 0, j)),
        out_shape=jax.ShapeDtypeStruct((depth, 8, n), F32),
        compiler_params=_cparams(("arbitrary", "arbitrary")),
        name="ada",
    )(cs, w_ada, b_ada.reshape(depth, 1, n))


def _norm_kernel(x_ref, mod_ref, g_ref, h_ref):
    mod = mod_ref[0]
    h = _rms(x_ref[...], g_ref[...])
    h_ref[...] = (h * (1.0 + mod[1:2]) + mod[0:1]).astype(BF16)


def _norm_call(x, mod, gain, rows_per_mod, tm):
    m, d = x.shape
    per = rows_per_mod // tm
    return pl.pallas_call(
        _norm_kernel,
        grid=(m // tm,),
        in_specs=[pl.BlockSpec((tm, d), lambda i: (i, 0)),
                  pl.BlockSpec((1, 8, d), lambda i: (i // per, 0, 0)),
                  pl.BlockSpec((1, d), lambda i: (0, 0))],
        out_specs=pl.BlockSpec((tm, d), lambda i: (i, 0)),
        out_shape=jax.ShapeDtypeStruct((m, d), BF16),
        compiler_params=_cparams(("arbitrary",)),
        name="norm_mod",
    )(x, mod, gain)


def _matmul_kernel(a_ref, wt_ref, o_ref):
    o_ref[...] = _dot_nt(a_ref[...], wt_ref[0])


def _matmul_call(a, wt, layer, tm, tn, name):
    m, k = a.shape
    n = wt.shape[1]
    return pl.pallas_call(
        _matmul_kernel,
        grid=(n // tn, m // tm),
        in_specs=[pl.BlockSpec((tm, k), lambda j, i: (i, 0)),
                  pl.BlockSpec((1, tn, k), lambda j, i: (layer, j, 0))],
        out_specs=pl.BlockSpec((tm, tn), lambda j, i: (i, j)),
        out_shape=jax.ShapeDtypeStruct((m, n), F32),
        compiler_params=_cparams(("arbitrary", "arbitrary")),
        name=name,
    )(a, wt)


def _outproj_kernel(ya_ref, yb_ref, ym_ref, yw_ref, w_ref, x_ref, mod_ref, g_ref, xo_ref, h_ref):
    gw = ya_ref.shape[1]
    acc = _dot(ya_ref[...], w_ref[0, 0:gw, :])
    acc += _dot(yb_ref[...], w_ref[0, gw:2 * gw, :])
    acc += _dot(ym_ref[...], w_ref[0, 2 * gw:3 * gw, :])
    acc += _dot(yw_ref[...], w_ref[0, 3 * gw:4 * gw, :])
    mod = mod_ref[0]
    xn = x_ref[...] + mod[2:3] * acc
    xo_ref[...] = xn
    h = _rms(xn, g_ref[...])
    h_ref[...] = (h * (1.0 + mod[4:5]) + mod[3:4]).astype(BF16)


def _outproj_call(ys, w, layer, x, mod, gain, rows_per_mod, tm):
    m, d = x.shape
    gw = ys[0].shape[1]
    per = rows_per_mod // tm
    yspec = pl.BlockSpec((tm, gw), lambda i: (i, 0))
    return pl.pallas_call(
        _outproj_kernel,
        grid=(m // tm,),
        in_specs=[yspec, yspec, yspec, yspec,
                  pl.BlockSpec((1, 4 * gw, d), lambda i: (layer, 0, 0)),
                  pl.BlockSpec((tm, d), lambda i: (i, 0)),
                  pl.BlockSpec((1, 8, d), lambda i: (i // per, 0, 0)),
                  pl.BlockSpec((1, d), lambda i: (0, 0))],
        out_specs=[pl.BlockSpec((tm, d), lambda i: (i, 0)),
                   pl.BlockSpec((tm, d), lambda i: (i, 0))],
        out_shape=[jax.ShapeDtypeStruct((m, d), F32), jax.ShapeDtypeStruct((m, d), BF16)],
        compiler_params=_cparams(("arbitrary",)),
        name="outproj",
    )(*ys, w, x, mod, gain)


def _ffn_kernel(h_ref, wg_ref, wu_ref, w2_ref, x_ref, mod_ref, modn_ref, gn_ref, xo_ref, hn_ref, acc_ref,
                *, with_next):
    j = pl.program_id(1)

    @pl.when(j == 0)
    def _():
        acc_ref[...] = jnp.zeros_like(acc_ref)

    h = h_ref[...]
    act = _silu(_dot(h, wg_ref[0])) * _dot(h, wu_ref[0])
    acc_ref[...] += _dot(act.astype(BF16), w2_ref[0])

    @pl.when(j == pl.num_programs(1) - 1)
    def _():
        xn = x_ref[...] + mod_ref[0][5:6] * acc_ref[...]
        xo_ref[...] = xn
        if with_next:
            modn = modn_ref[0]
            hn = _rms(xn, gn_ref[...])
            hn_ref[...] = (hn * (1.0 + modn[1:2]) + modn[0:1]).astype(BF16)
        else:
            hn_ref[...] = jnp.zeros_like(hn_ref)


def _ffn_call(h, w_in, w_out, layer, x, mod, mod_next, gain_next, rows_per_mod, tm, tn, with_next):
    m, d = x.shape
    hidden = w_out.shape[1]
    nj = hidden // tn
    per = rows_per_mod // tm
    return pl.pallas_call(
        functools.partial(_ffn_kernel, with_next=with_next),
        grid=(m // tm, nj),
        in_specs=[pl.BlockSpec((tm, d), lambda i, j: (i, 0)),
                  pl.BlockSpec((1, d, tn), lambda i, j: (layer, 0, j)),
                  pl.BlockSpec((1, d, tn), lambda i, j: (layer, 0, nj + j)),
                  pl.BlockSpec((1, tn, d), lambda i, j: (layer, j, 0)),
                  pl.BlockSpec((tm, d), lambda i, j: (i, 0)),
                  pl.BlockSpec((1, 8, d), lambda i, j: (i // per, 0, 0)),
                  pl.BlockSpec((1, 8, d), lambda i, j: (i // per, 0, 0)),
                  pl.BlockSpec((1, d), lambda i, j: (0, 0))],
        out_specs=[pl.BlockSpec((tm, d), lambda i, j: (i, 0)),
                   pl.BlockSpec((tm, d), lambda i, j: (i, 0))],
        out_shape=[jax.ShapeDtypeStruct((m, d), F32), jax.ShapeDtypeStruct((m, d), BF16)],
        scratch_shapes=[pltpu.VMEM((tm, d), F32)],
        compiler_params=_cparams(("arbitrary", "arbitrary")),
        name="ffn",
    )(h, w_in, w_in, w_out, x, mod, mod_next, gain_next)


def _na_kernel(q_ref, k_ref, v_ref, kc_ref, vc_ref, gain_ref, colbias_ref, o_ref,
               kn_scr, v_scr, kcn_scr, vc_scr, bias_scr, sl0_scr, sc0_scr, sl1_scr, sc1_scr, *, rows):
    step = pl.program_id(2)
    n_rb = pl.num_programs(2) - 1
    hd = HEAD_DIM
    heads = q_ref.shape[1] // hd

    @pl.when(step == 0)
    def _():
        sl1_scr[...] = jnp.zeros_like(sl1_scr)
        sc1_scr[...] = jnp.zeros_like(sc1_scr)
        g1 = gain_ref[1:2, :]
        closed = jnp.full((GRID_W, GRID_W), NEG_INF, F32)
        for hh in range(heads):
            cols = slice(hh * hd, (hh + 1) * hd)
            kn_scr[hh] = _rms(k_ref[:, cols], g1).astype(BF16)
            v_scr[hh] = v_ref[:, cols].astype(BF16)
            kcn_scr[hh] = _rms(kc_ref[:, cols], g1).astype(BF16)
            vc_scr[hh] = vc_ref[:, cols].astype(BF16)
            for typ, (offs, inside) in enumerate(_na_row_plan(rows)):
                for i in range(NA_QROWS):
                    for jp in range(NA_KROWS // 2):
                        halves = [colbias_ref[0, hh, offs[i][j]] if inside[i][j] else closed
                                  for j in (2 * jp, 2 * jp + 1)]
                        bias_scr[hh, typ, i * GRID_W:(i + 1) * GRID_W, 2 * jp * GRID_W:(2 * jp + 2) * GRID_W] = (
                            jnp.concatenate(halves, axis=1))

    scale = hd ** -0.5
    nk = NA_KROWS * GRID_W

    def window_start(blk):
        ws = jnp.clip(NA_QROWS * blk - NA_WIN_ROWS // 2, 0, rows - NA_KROWS)
        return pl.multiple_of(ws * GRID_W, GRID_W)

    def stages(write_loc, write_ctx, read_loc, read_ctx):
        prev = jnp.maximum(step - 1, 0)
        cur = jnp.minimum(step, n_rb - 1)
        typ = jnp.where(cur == 0, 0, jnp.where(cur == n_rb - 1, 2, 1))
        for hh in range(heads):
            cols = slice(hh * hd, (hh + 1) * hd)
            s_loc = read_loc[hh]
            s_ctx = read_ctx[hh]
            m = jnp.maximum(jnp.max(s_loc, axis=-1, keepdims=True), jnp.max(s_ctx, axis=-1, keepdims=True))
            p_loc = jnp.exp(s_loc - m)
            p_ctx = jnp.exp(s_ctx - m)
            den = jnp.sum(p_loc, axis=-1, keepdims=True) + jnp.sum(p_ctx, axis=-1, keepdims=True)
            vw = v_scr[hh, pl.ds(window_start(prev), nk), :]
            o = _dot(p_loc.astype(BF16), vw) + _dot(p_ctx.astype(BF16), vc_scr[hh])
            o_ref[:, cols] = (o / den).astype(BF16)
            qn = _rms(q_ref[:, cols], gain_ref[0:1, :]).astype(BF16)
            kw = kn_scr[hh, pl.ds(window_start(cur), nk), :]
            write_loc[hh] = _dot_nt(qn, kw) * scale + bias_scr[hh, pl.ds(typ, 1)][0]
            write_ctx[hh] = _dot_nt(qn, kcn_scr[hh]) * scale

    @pl.when(step % 2 == 0)
    def _():
        stages(sl0_scr, sc0_scr, sl1_scr, sc1_scr)

    @pl.when(step % 2 == 1)
    def _():
        stages(sl1_scr, sc1_scr, sl0_scr, sc0_scr)


def _na_row_plan(rows):
    n_rb = rows // NA_QROWS
    kh = NA_WIN_ROWS
    plan = []
    for rb in (0, 1, n_rb - 1):
        ws = min(max(NA_QROWS * rb - kh // 2, 0), rows - NA_KROWS)
        offs, inside = [], []
        for i in range(NA_QROWS):
            r = NA_QROWS * rb + i
            rs = min(max(r - kh // 2, 0), rows - kh)
            offs.append([min(max(ws + j - r, 1 - kh), kh - 1) + kh - 1 for j in range(NA_KROWS)])
            inside.append([rs <= ws + j < rs + kh for j in range(NA_KROWS)])
        plan.append((offs, inside))
    return plan


def _na_col_tables(rpb):
    kw = NA_WIN_COLS
    c = jnp.arange(GRID_W)
    cs = jnp.clip(c - kw // 2, 0, GRID_W - kw)
    in_col = (c[None, :] >= cs[:, None]) & (c[None, :] < cs[:, None] + kw)
    dc = jnp.clip(c[None, :] - c[:, None], 1 - kw, kw - 1) + kw - 1
    dc_hot = jax.nn.one_hot(dc, 2 * kw - 1, dtype=F32)
    val = jnp.einsum("lhab,ckb->lhack", rpb.astype(F32), dc_hot, precision=lax.Precision.HIGHEST)
    return jnp.where(in_col, val, NEG_INF)


def _na_call(p_lat, p_ctx, gain, colbias, layer, batch, seq, lc):
    rows = seq // GRID_W
    n_rb = rows // NA_QROWS
    tq = NA_QROWS * GRID_W
    nk = NA_KROWS * GRID_W
    heads = colbias.shape[1]
    n_off = colbias.shape[2]
    hd = HEAD_DIM
    hp = NA_HEADS_PER_STEP
    wide = hp * hd
    assert heads % hp == 0 and NA_Q % hp == 0 and NA_K % hp == 0 and NA_V % hp == 0
    return pl.pallas_call(
        functools.partial(_na_kernel, rows=rows),
        grid=(batch, heads // hp, n_rb + 1),
        in_specs=[pl.BlockSpec((tq, wide), lambda b, h, r: (b * n_rb + jnp.minimum(r, n_rb - 1), NA_Q // hp + h)),
                  pl.BlockSpec((seq, wide), lambda b, h, r: (b, NA_K // hp + h)),
                  pl.BlockSpec((seq, wide), lambda b, h, r: (b, NA_V // hp + h)),
                  pl.BlockSpec((lc, wide), lambda b, h, r: (b, NA_K // hp + h)),
                  pl.BlockSpec((lc, wide), lambda b, h, r: (b, NA_V // hp + h)),
                  pl.BlockSpec((2, hd), lambda b, h, r: (0, 0)),
                  pl.BlockSpec((1, hp, n_off, GRID_W, GRID_W), lambda b, h, r: (layer, h, 0, 0, 0))],
        out_specs=pl.BlockSpec((tq, wide), lambda b, h, r: (b * n_rb + jnp.maximum(r - 1, 0), h)),
        out_shape=jax.ShapeDtypeStruct((batch * seq, heads * hd), BF16),
        scratch_shapes=[pltpu.VMEM((hp, seq, hd), BF16), pltpu.VMEM((hp, seq, hd), BF16),
                        pltpu.VMEM((hp, lc, hd), BF16), pltpu.VMEM((hp, lc, hd), BF16),
                        pltpu.VMEM((hp, 3, tq, nk), F32),
                        pltpu.VMEM((hp, tq, nk), F32), pltpu.VMEM((hp, tq, lc), F32),
                        pltpu.VMEM((hp, tq, nk), F32), pltpu.VMEM((hp, tq, lc), F32)],
        compiler_params=_cparams(("arbitrary", "arbitrary", "arbitrary")),
        name="na_attn",
    )(p_lat, p_lat, p_lat, p_ctx, p_ctx, gain, colbias)


def _ctx_attn_kernel(sink_ref, q_ref, k_ref, v_ref, gain_ref, o_ref, *, use_sink):
    scale = HEAD_DIM ** -0.5
    qn = _rms(q_ref[...], gain_ref[0:1, :]).astype(BF16)
    kn = _rms(k_ref[...], gain_ref[1:2, :]).astype(BF16)
    s = _dot_nt(qn, kn) * scale
    m = jnp.max(s, axis=-1, keepdims=True)
    if use_sink:
        sk = sink_ref[pl.program_id(1)]
        m = jnp.maximum(m, sk)
    p = jnp.exp(s - m)
    den = jnp.sum(p, axis=-1, keepdims=True)
    if use_sink:
        den = den + jnp.exp(sk - m)
    o = _dot(p.astype(BF16), v_ref[...].astype(BF16))
    o_ref[...] = (o / den).astype(BF16)


def _ctx_attn_call(p_ctx, gain, sink, batch, lc, heads, rep, qoff, koff, voff, use_sink, name):
    hd = HEAD_DIM
    return pl.pallas_call(
        functools.partial(_ctx_attn_kernel, use_sink=use_sink),
        grid=(batch, heads),
        in_specs=[pl.BlockSpec(memory_space=pltpu.SMEM),
                  pl.BlockSpec((lc, hd), lambda b, h: (b, qoff + h)),
                  pl.BlockSpec((lc, hd), lambda b, h: (b, koff + h // rep)),
                  pl.BlockSpec((lc, hd), lambda b, h: (b, voff + h // rep)),
                  pl.BlockSpec((2, hd), lambda b, h: (0, 0))],
        out_specs=pl.BlockSpec((lc, hd), lambda b, h: (b, h)),
        out_shape=jax.ShapeDtypeStruct((batch * lc, heads * hd), BF16),
        compiler_params=_cparams(("arbitrary", "arbitrary")),
        name=name,
    )(sink, p_ctx, p_ctx, p_ctx, gain)


def _rope(x, cos, sin_signed):
    lane = _iota(x.shape, 1)
    swapped = jnp.where(lane % 64 < 32, pltpu.roll(x, 96, axis=1), pltpu.roll(x, 32, axis=1))
    return x * cos + swapped * sin_signed


def _wa_kernel(sink_ref, q_ref, k_ref, v_ref, kc_ref, vc_ref, gain_ref, cq_ref, sq_ref, ck_ref, sk_ref, o_ref,
               kpad_scr, vpad_scr, kcn_scr, vc_scr, sl0_scr, sc0_scr, sl1_scr, sc1_scr, *, seq):
    step = pl.program_id(1)
    nb = pl.num_programs(1) - 1
    wb = WA_BLOCK
    hd = HEAD_DIM
    groups = k_ref.shape[1] // hd

    @pl.when(step == 0)
    def _():
        sl1_scr[...] = jnp.zeros_like(sl1_scr)
        sc1_scr[...] = jnp.zeros_like(sc1_scr)
        g1 = gain_ref[1:2, :]
        zeros = jnp.zeros((wb, hd), BF16)
        for g in range(groups):
            cols = slice(g * hd, (g + 1) * hd)
            kpad_scr[g, 0:wb, :] = zeros
            kpad_scr[g, seq + wb:seq + 2 * wb, :] = zeros
            vpad_scr[g, 0:wb, :] = zeros
            vpad_scr[g, seq + wb:seq + 2 * wb, :] = zeros
            kpad_scr[g, wb:seq + wb, :] = _rope(_rms(k_ref[:, cols], g1), ck_ref[...], sk_ref[...]).astype(BF16)
            vpad_scr[g, wb:seq + wb, :] = v_ref[:, cols].astype(BF16)
            kcn_scr[g] = _rms(kc_ref[:, cols], g1).astype(BF16)
            vc_scr[g] = vc_ref[:, cols].astype(BF16)

    scale = hd ** -0.5

    def stages(write_loc, write_ctx, read_loc, read_ctx):
        prev = jnp.maximum(step - 1, 0)
        cur = jnp.minimum(step, nb - 1)
        g0 = gain_ref[0:1, :]
        cq = cq_ref[...]
        sq = sq_ref[...]
        shp = (2 * wb, 3 * wb)
        i = _iota(shp, 0) % wb
        j = _iota(shp, 1)
        kpos = (cur - 1) * wb + j
        ok = (j - i >= 0) & (j - i <= 2 * wb) & (kpos >= 0) & (kpos < seq)
        for g in range(groups):
            s_loc = read_loc[g]
            s_ctx = read_ctx[g]
            sink = jnp.where(_iota((2 * wb, 1), 0) < wb, sink_ref[2 * g], sink_ref[2 * g + 1])
            m = jnp.maximum(jnp.maximum(jnp.max(s_loc, axis=-1, keepdims=True),
                                        jnp.max(s_ctx, axis=-1, keepdims=True)), sink)
            p_loc = jnp.exp(s_loc - m)
            p_ctx = jnp.exp(s_ctx - m)
            den = jnp.sum(p_loc, axis=-1, keepdims=True) + jnp.sum(p_ctx, axis=-1, keepdims=True) + jnp.exp(sink - m)
            vw = vpad_scr[g, pl.ds(pl.multiple_of(prev * wb, wb), 3 * wb), :]
            o = (_dot(p_loc.astype(BF16), vw) + _dot(p_ctx.astype(BF16), vc_scr[g])) / den
            o_ref[:, 2 * g * hd:(2 * g + 1) * hd] = o[0:wb].astype(BF16)
            o_ref[:, (2 * g + 1) * hd:(2 * g + 2) * hd] = o[wb:2 * wb].astype(BF16)
            q2 = jnp.concatenate([_rope(_rms(q_ref[:, 2 * g * hd:(2 * g + 1) * hd], g0), cq, sq),
                                  _rope(_rms(q_ref[:, (2 * g + 1) * hd:(2 * g + 2) * hd], g0), cq, sq)],
                                 axis=0).astype(BF16)
            kw = kpad_scr[g, pl.ds(pl.multiple_of(cur * wb, wb), 3 * wb), :]
            write_loc[g] = jnp.where(ok, _dot_nt(q2, kw) * scale, NEG_INF)
            write_ctx[g] = _dot_nt(q2, kcn_scr[g]) * scale

    @pl.when(step % 2 == 0)
    def _():
        stages(sl0_scr, sc0_scr, sl1_scr, sc1_scr)

    @pl.when(step % 2 == 1)
    def _():
        stages(sl1_scr, sc1_scr, sl0_scr, sc0_scr)


def _wa_call(p_lat, p_ctx, gain, sink, cos_t, sin_t, batch, seq, lc):
    wb = WA_BLOCK
    hd = HEAD_DIM
    nb = seq // wb
    kvh = 2
    qw = 2 * kvh * hd
    kw_ = kvh * hd
    assert (WA_Q * hd) % qw == 0 and (WA_K * hd) % kw_ == 0 and (WA_V * hd) % kw_ == 0
    return pl.pallas_call(
        functools.partial(_wa_kernel, seq=seq),
        grid=(batch, nb + 1),
        in_specs=[pl.BlockSpec(memory_space=pltpu.SMEM),
                  pl.BlockSpec((wb, qw), lambda b, n: (b * nb + jnp.minimum(n, nb - 1), WA_Q * hd // qw)),
                  pl.BlockSpec((seq, kw_), lambda b, n: (b, WA_K * hd // kw_)),
                  pl.BlockSpec((seq, kw_), lambda b, n: (b, WA_V * hd // kw_)),
                  pl.BlockSpec((lc, kw_), lambda b, n: (b, WA_K * hd // kw_)),
                  pl.BlockSpec((lc, kw_), lambda b, n: (b, WA_V * hd // kw_)),
                  pl.BlockSpec((2, hd), lambda b, n: (0, 0)),
                  pl.BlockSpec((wb, hd), lambda b, n: (jnp.minimum(n, nb - 1), 0)),
                  pl.BlockSpec((wb, hd), lambda b, n: (jnp.minimum(n, nb - 1), 0)),
                  pl.BlockSpec((seq, hd), lambda b, n: (0, 0)),
                  pl.BlockSpec((seq, hd), lambda b, n: (0, 0))],
        out_specs=pl.BlockSpec((wb, qw), lambda b, n: (b * nb + jnp.maximum(n - 1, 0), 0)),
        out_shape=jax.ShapeDtypeStruct((batch * seq, qw), BF16),
        scratch_shapes=[pltpu.VMEM((kvh, seq + 2 * wb, hd), BF16), pltpu.VMEM((kvh, seq + 2 * wb, hd), BF16),
                        pltpu.VMEM((kvh, lc, hd), BF16), pltpu.VMEM((kvh, lc, hd), BF16),
                        pltpu.VMEM((kvh, 2 * wb, 3 * wb), F32), pltpu.VMEM((kvh, 2 * wb, lc), F32),
                        pltpu.VMEM((kvh, 2 * wb, 3 * wb), F32), pltpu.VMEM((kvh, 2 * wb, lc), F32)],
        compiler_params=_cparams(("arbitrary", "arbitrary")),
        name="wa_attn",
    )(sink, p_lat, p_lat, p_lat, p_ctx, p_ctx, gain, cos_t, sin_t, cos_t, sin_t)


def _dn_prep_kernel(x_ref, prev_ref, next_ref, w_ref, o_ref, *, n_tiles):
    t = pl.program_id(1)
    tl = x_ref.shape[0]
    x = x_ref[...]
    prev = jnp.where(t > 0, prev_ref[...], 0.0)
    nxt = jnp.where(t < n_tiles - 1, next_ref[...], 0.0)
    xe = jnp.concatenate([prev, x, nxt], axis=0)
    n = tl + 16
    taps = w_ref.shape[0]
    acc = jnp.zeros_like(x)
    for j in range(taps):
        off = 8 - taps // 2 + j
        shifted = xe if off % n == 0 else pltpu.roll(xe, n - off, axis=0)
        acc += shifted[0:tl] * w_ref[j:j + 1, :]
    y = _silu(acc)
    hd = HEAD_DIM
    nh = y.shape[1] // (3 * hd)
    for blk in range(3 * nh):
        yb = y[:, blk * hd:(blk + 1) * hd]
        if blk < 2 * nh:
            yb = yb * lax.rsqrt(jnp.sum(yb * yb, axis=-1, keepdims=True) + EPS)
            if blk < nh:
                yb = yb * hd ** -0.5
        o_ref[:, blk * hd:(blk + 1) * hd] = yb


def _dn_prep_call(p, conv_w, batch, length, tl):
    width = conv_w.shape[1]
    n_tiles = length // tl
    per8 = tl // 8
    last8 = batch * length // 8 - 1
    return pl.pallas_call(
        functools.partial(_dn_prep_kernel, n_tiles=n_tiles),
        grid=(batch, n_tiles),
        in_specs=[pl.BlockSpec((tl, width), lambda b, t: (b * n_tiles + t, 1)),
                  pl.BlockSpec((8, width), lambda b, t: (jnp.maximum((b * n_tiles + t) * per8 - 1, 0), 1)),
                  pl.BlockSpec((8, width), lambda b, t: (jnp.minimum((b * n_tiles + t + 1) * per8, last8), 1)),
                  pl.BlockSpec((conv_w.shape[0], width), lambda b, t: (0, 0))],
        out_specs=pl.BlockSpec((tl, width), lambda b, t: (b * n_tiles + t, 0)),
        out_shape=jax.ShapeDtypeStruct((batch * length, width), F32),
        compiler_params=_cparams(("arbitrary", "arbitrary")),
        name="dn_prep",
    )(p, p, p, conv_w)


def _split_bf16(a):
    hi = a.astype(BF16)
    return hi, (a - hi.astype(F32)).astype(BF16)


def _solve_dot(a, b, mode):
    if mode == "highest":
        return _dot(a, b, lax.Precision.HIGHEST)
    if mode == "bf16":
        return _dot(a.astype(BF16), b.astype(BF16))
    a_hi, a_lo = _split_bf16(a)
    b_hi, b_lo = _split_bf16(b)
    return _dot(a_hi, b_hi) + (_dot(a_hi, b_lo) + _dot(a_lo, b_hi))


def _dn_pass(q_ref, k_ref, v_ref, og_ref, gates_ref, gadd, gmul, norm_g, y_ref, length, h, state,
             lin_scr, add_scr, egl_scr, acc_scr, solve_precision):
    assert CHUNK == HEAD_DIM
    n_chunks = length // CHUNK

    acc_scr[0:length, :] = jnp.zeros((length, HEAD_DIM), F32)

    shp = (CHUNK, CHUNK)
    ri = _iota(shp, 0)
    ci = _iota(shp, 1)
    masks = ((ci <= ri, ci < ri), (ci >= ri, ci > ri))
    couple = ([], [])
    s = 1
    while s < CHUNK:
        same = (ri // (2 * s)) == (ci // (2 * s))
        couple[0].append(same & ((ri // s) % 2 == 1) & ((ci // s) % 2 == 0))
        couple[1].append(same & ((ri // s) % 2 == 0) & ((ci // s) % 2 == 1))
        s *= 2
    group = min(n_chunks, PREP_GROUP)

    def prep(gi, carry):
        chains = []
        for g in range(group):
            c = gi * group + g
            r0 = pl.multiple_of(c * CHUNK, CHUNK)
            q = q_ref[pl.ds(r0, CHUNK), :]
            k = k_ref[pl.ds(r0, CHUNK), :]
            raw = gates_ref[pl.ds(r0, CHUNK), :]
            gb = jnp.where(_iota(raw.shape, 1) < LANE_ALPHA, jax.nn.sigmoid(raw), -gmul * _softplus(raw + gadd))
            k16 = k.astype(BF16)
            kk = _dot_nt(k16, k16)
            qk = _dot_nt(q.astype(BF16), k16)
            for d in range(2):
                incl, strict = masks[d]
                beta_col = _lane_pick(gb, LANE_BETA + 4 * d + h)
                g_col = _lane_pick(gb, LANE_ALPHA + 4 * d + h)
                beta_row = _to_row(beta_col)
                g_row = _to_row(g_col)
                cum_col = jnp.sum(jnp.where(incl, g_row, 0.0), axis=-1, keepdims=True)
                cum_row = _to_row(cum_col)
                tot = jnp.sum(g_col, axis=0, keepdims=True)
                decay = jnp.exp(jnp.where(incl, cum_col - cum_row, NEG_INF))
                lm = jnp.where(strict, kk * beta_col * decay, 0.0)
                egl_scr[d, pl.ds(pl.multiple_of(c * 8, 8), 8), :] = jnp.broadcast_to(jnp.exp(tot), (8, HEAD_DIM))
                chains.append((d, c, r0, lm, beta_row, beta_row * jnp.exp(cum_row), (qk * decay).astype(BF16),
                               q * jnp.exp(cum_col), k * jnp.exp(tot - cum_col)))
        eye = (ri == ci).astype(F32)
        xs = [eye - jnp.where(couple[ch[0]][0], ch[3], 0.0) for ch in chains]
        for lvl in range(1, len(couple[0])):
            cs = [jnp.where(couple[ch[0]][lvl], ch[3], 0.0) for ch in chains]
            xc = [_solve_dot(x, c, solve_precision) for x, c in zip(xs, cs)]
            xs = [x - _solve_dot(t, x, solve_precision) for x, t in zip(xs, xc)]
        u16 = [_solve_dot(ainv * ch[4], v_ref[pl.ds(ch[2], CHUNK), :], solve_precision).astype(BF16)
               for ch, ainv in zip(chains, xs)]
        w16 = [_solve_dot(ainv * ch[5], k_ref[pl.ds(ch[2], CHUNK), :], solve_precision).astype(BF16)
               for ch, ainv in zip(chains, xs)]
        ket = [ch[8].T.astype(BF16) for ch in chains]
        s_lin = [_dot(kt, w) for kt, w in zip(ket, w16)]
        s_add = [_dot(kt, u) for kt, u in zip(ket, u16)]
        o_lin = [ch[7] - _dot(ch[6], w) for ch, w in zip(chains, w16)]
        o_add = [_dot(ch[6], u) for ch, u in zip(chains, u16)]
        for ch, sl, sa, ol, oa in zip(chains, s_lin, s_add, o_lin, o_add):
            d = ch[0]
            base = pl.multiple_of(ch[1] * 2 * CHUNK, 2 * CHUNK)
            lin_scr[d, pl.ds(base, CHUNK), :] = sl.astype(BF16)
            lin_scr[d, pl.ds(base + CHUNK, CHUNK), :] = ol.astype(BF16)
            add_scr[d, pl.ds(base, CHUNK), :] = sa
            add_scr[d, pl.ds(base + CHUNK, CHUNK), :] = oa
        return carry

    lax.fori_loop(0, n_chunks // group, prep, 0)

    def scan(s, st):
        new = []
        for d in range(2):
            c = s if d == 0 else n_chunks - 1 - s
            base = pl.multiple_of(c * 2 * CHUNK, 2 * CHUNK)
            sm = st[d]
            prod = _dot(lin_scr[d, pl.ds(base, 2 * CHUNK), :], sm.astype(BF16))
            add = add_scr[d, pl.ds(base, 2 * CHUNK), :]
            acc_scr[pl.ds(pl.multiple_of(c * CHUNK, CHUNK), CHUNK), :] += prod[CHUNK:] + add[CHUNK:]
            egl = egl_scr[d, pl.ds(pl.multiple_of(c * 8, 8), 8), :][0:1, :]
            new.append(sm * egl - prod[:CHUNK] + add[:CHUNK])
        return tuple(new)

    state = lax.fori_loop(0, n_chunks, scan, state, unroll=SCAN_UNROLL)

    y = _rms(acc_scr[0:length, :], norm_g) * _silu(og_ref[...])
    y_ref[...] = y.astype(BF16)
    return state


def _dn_kernel(q_ref, k_ref, v_ref, og_ref, gates_ref, qc_ref, kc_ref, vc_ref, ogc_ref, gatesc_ref,
               gadd_ref, gmul_ref, norm_ref, y_ref, yc_ref,
               lin_scr, add_scr, egl_scr, acc_scr, *, seq, lc, solve_precision):
    h = pl.program_id(1)
    gadd = gadd_ref[...]
    gmul = gmul_ref[...]
    norm_g = norm_ref[...]
    scr = (lin_scr, add_scr, egl_scr, acc_scr)
    zero = jnp.zeros((HEAD_DIM, HEAD_DIM), F32)
    state = _dn_pass(qc_ref, kc_ref, vc_ref, ogc_ref, gatesc_ref, gadd, gmul, norm_g, yc_ref, lc, h, (zero, zero),
                     *scr, solve_precision)
    _dn_pass(q_ref, k_ref, v_ref, og_ref, gates_ref, gadd, gmul, norm_g, y_ref, seq, h, state,
             *scr, solve_precision)


def _dn_call(dq_lat, dq_ctx, p_lat, p_ctx, pg_lat, pg_ctx, gadd, gmul, norm_g, batch, seq, lc, solve_precision):
    hd = HEAD_DIM
    heads = 4

    def blk(n, col):
        return pl.BlockSpec((n, hd), lambda b, h: (b, col + h))

    def gate_blk(n):
        return pl.BlockSpec((n, hd), lambda b, h: (b, 0))

    vec = pl.BlockSpec((1, hd), lambda b, h: (0, 0))
    return pl.pallas_call(
        functools.partial(_dn_kernel, seq=seq, lc=lc, solve_precision=solve_precision),
        grid=(batch, heads),
        in_specs=[blk(seq, 0), blk(seq, heads), blk(seq, 2 * heads), blk(seq, DN_OG), gate_blk(seq),
                  blk(lc, 0), blk(lc, heads), blk(lc, 2 * heads), blk(lc, DN_OG), gate_blk(lc),
                  vec, vec, vec],
        out_specs=[pl.BlockSpec((seq, hd), lambda b, h: (b, h)),
                   pl.BlockSpec((lc, hd), lambda b, h: (b, h))],
        out_shape=[jax.ShapeDtypeStruct((batch * seq, heads * hd), BF16),
                   jax.ShapeDtypeStruct((batch * lc, heads * hd), BF16)],
        scratch_shapes=[pltpu.VMEM((2, 2 * seq, hd), BF16),
                        pltpu.VMEM((2, 2 * seq, hd), F32),
                        pltpu.VMEM((2, seq // CHUNK * 8, hd), F32),
                        pltpu.VMEM((seq, hd), F32)],
        compiler_params=_cparams(("arbitrary", "arbitrary"), DN_VMEM_LIMIT),
        name="deltanet",
    )(dq_lat, dq_lat, dq_lat, p_lat, pg_lat, dq_ctx, dq_ctx, dq_ctx, p_ctx, pg_ctx, gadd, gmul, norm_g)


def _ml_pass(q_ref, k_ref, v_ref, og_ref, gates_ref, gadd, norm_g, y_ref, length, h, state, acc_scr):
    n_chunks = length // CHUNK
    acc_scr[0:length, :] = jnp.zeros((length, HEAD_DIM), F32)
    shp = (CHUNK, CHUNK)
    ri = _iota(shp, 0)
    ci = _iota(shp, 1)
    masks = (ci <= ri, ci >= ri)
    head_lanes = (_iota((1, HEAD_DIM), 1) // ML_QK) == (h % 2)

    group = min(n_chunks, ML_GROUP)

    def each(f, *lists):
        return [f(*a) for a in zip(*lists)]

    def step(gi, st):
        dirs, r0s = [], []
        for j in range(group):
            s = gi * group + j
            for d in range(2):
                c = s if d == 0 else n_chunks - 1 - s
                dirs.append(d)
                r0s.append(pl.multiple_of(c * CHUNK, CHUNK))
        incl = [masks[d] for d in dirs]
        q = [jnp.where(head_lanes, q_ref[pl.ds(r0, CHUNK), :], 0.0) for r0 in r0s]
        k = [jnp.where(head_lanes, k_ref[pl.ds(r0, CHUNK), :], 0.0) * ML_QK ** -0.5 for r0 in r0s]
        v16 = [v_ref[pl.ds(r0, CHUNK), :].astype(BF16) for r0 in r0s]
        gb = [gates_ref[pl.ds(r0, CHUNK), :] + gadd for r0 in r0s]
        ig_col = [_lane_pick(g, LANE_IG + 4 * d + h) for g, d in zip(gb, dirs)]
        lf_col = [_log_sigmoid(_lane_pick(g, LANE_FG + 4 * d + h)) for g, d in zip(gb, dirs)]
        ig_row = each(_to_row, ig_col)
        lf_row = each(_to_row, lf_col)
        b_col = each(lambda m, r: jnp.sum(jnp.where(m, r, 0.0), axis=-1, keepdims=True), incl, lf_row)
        b_row = each(_to_row, b_col)
        b_last = each(lambda x: jnp.sum(x, axis=0, keepdims=True), lf_col)
        g_end = each(lambda bl, b, i: bl - b + i, b_last, b_col, ig_col)
        m_loc = each(lambda x: jnp.max(x, axis=0, keepdims=True), g_end)
        kw = each(lambda kk, g, m: kk * jnp.exp(g - m), k, g_end, m_loc)
        q16 = each(lambda x: x.astype(BF16), q)
        k16 = each(lambda x: x.astype(BF16), k)
        c_loc = each(lambda a, b: _dot_tn(a.astype(BF16), b), kw, v16)
        n_loc = each(lambda x: jnp.sum(x, axis=0, keepdims=True), kw)
        log_d = each(lambda m, bc, br, ir: jnp.where(m, bc - br + ir, NEG_INF), incl, b_col, b_row, ig_row)
        mx_col = each(lambda x: jnp.max(x, axis=-1, keepdims=True), log_d)
        s0 = each(lambda a, b, ld, mx: _dot_nt(a, b) * jnp.exp(ld - mx), q16, k16, log_d, mx_col)
        sv0 = each(lambda s, vv: _dot(s.astype(BF16), vv), s0, v16)
        rs0 = each(lambda s: jnp.sum(s, axis=-1, keepdims=True), s0)
        st = list(st)
        for i, (d, r0) in enumerate(zip(dirs, r0s)):
            c_st, n_st, m_st = st[d]
            m_inter = b_col[i] + m_st
            m_t = jnp.maximum(mx_col[i], m_inter)
            intra = jnp.exp(mx_col[i] - m_t)
            inter = jnp.exp(m_inter - m_t)
            num = intra * sv0[i] + inter * _dot(q16[i], c_st.astype(BF16))
            den = intra * rs0[i] + inter * jnp.sum(q[i] * n_st, axis=-1, keepdims=True)
            acc_scr[pl.ds(r0, CHUNK), :] += num / jnp.maximum(jnp.abs(den), jnp.exp(-m_t))
            m_new = jnp.maximum(b_last[i] + m_st, m_loc[i])
            a = jnp.exp(b_last[i] + m_st - m_new)
            e = jnp.exp(m_loc[i] - m_new)
            st[d] = (a * c_st + e * c_loc[i], a * n_st + e * n_loc[i], m_new)
        return tuple(st)

    state = lax.fori_loop(0, n_chunks // group, step, state)
    y = jax.nn.sigmoid(og_ref[...]) * _rms(acc_scr[0:length, :], norm_g)
    y_ref[...] = y.astype(BF16)
    return state


def _ml_kernel(q_ref, k_ref, v_ref, og_ref, gates_ref, qc_ref, kc_ref, vc_ref, ogc_ref, gatesc_ref,
               gadd_ref, norm_ref, y_ref, yc_ref, acc_scr, *, seq, lc):
    h = pl.program_id(1)
    gadd = gadd_ref[...]
    norm_g = norm_ref[0]
    zero = (jnp.zeros((HEAD_DIM, HEAD_DIM), F32), jnp.zeros((1, HEAD_DIM), F32), jnp.zeros((1, 1), F32))
    state = _ml_pass(qc_ref, kc_ref, vc_ref, ogc_ref, gatesc_ref, gadd, norm_g, yc_ref, lc, h, (zero, zero), acc_scr)
    _ml_pass(q_ref, k_ref, v_ref, og_ref, gates_ref, gadd, norm_g, y_ref, seq, h, state, acc_scr)


def _ml_call(p_lat, p_ctx, pg_lat, pg_ctx, gadd, norm_g, batch, seq, lc):
    hd = HEAD_DIM
    heads = 4

    def specs(n):
        return [pl.BlockSpec((n, hd), lambda b, h: (b, ML_Q + h // 2)),
                pl.BlockSpec((n, hd), lambda b, h: (b, ML_K + h // 2)),
                pl.BlockSpec((n, hd), lambda b, h: (b, ML_V + h)),
                pl.BlockSpec((n, hd), lambda b, h: (b, ML_OG + h)),
                pl.BlockSpec((n, hd), lambda b, h: (b, 0))]

    return pl.pallas_call(
        functools.partial(_ml_kernel, seq=seq, lc=lc),
        grid=(batch, heads),
        in_specs=specs(seq) + specs(lc) + [pl.BlockSpec((1, hd), lambda b, h: (0, 0)),
                                            pl.BlockSpec((1, 1, hd), lambda b, h: (h, 0, 0))],
        out_specs=[pl.BlockSpec((seq, hd), lambda b, h: (b, h)),
                   pl.BlockSpec((lc, hd), lambda b, h: (b, h))],
        out_shape=[jax.ShapeDtypeStruct((batch * seq, heads * hd), BF16),
                   jax.ShapeDtypeStruct((batch * lc, heads * hd), BF16)],
        scratch_shapes=[pltpu.VMEM((seq, hd), F32)],
        compiler_params=_cparams(("arbitrary", "arbitrary")),
        name="mlstm",
    )(p_lat, p_lat, p_lat, p_lat, pg_lat, p_ctx, p_ctx, p_ctx, p_ctx, pg_ctx, gadd, norm_g.reshape(heads, 1, hd))


def _regroup_kernel(wt_ref, main_ref, gate_ref, *, cuts):
    (a0, a1), (b0, b1), (c0, c1), (g0, g1), (h0, h1) = cuts
    main_ref[0, 0:a1 - a0, :] = wt_ref[0, a0:a1, :].astype(BF16)
    main_ref[0, a1 - a0:a1 - a0 + b1 - b0, :] = wt_ref[0, b0:b1, :].astype(BF16)
    main_ref[0, a1 - a0 + b1 - b0:, :] = wt_ref[0, c0:c1, :].astype(BF16)
    ng = (g1 - g0) + (h1 - h0)
    gate_ref[0, 0:g1 - g0, :] = wt_ref[0, g0:g1, :].astype(BF16)
    gate_ref[0, g1 - g0:ng, :] = wt_ref[0, h0:h1, :].astype(BF16)
    gate_ref[0, ng:, :] = jnp.zeros((gate_ref.shape[1] - ng, gate_ref.shape[2]), BF16)


def _regroup_call(w_in):
    depth, d, n = w_in.shape
    gw = 512
    dn_gate0 = 3 * gw + 4 * gw
    ml0 = dn_gate0 + 16
    ml_gate0 = ml0 + 3 * gw
    wa0 = ml_gate0 + 16
    cuts = ((0, dn_gate0), (ml0, ml_gate0), (wa0, n), (dn_gate0, ml0), (ml_gate0, wa0))
    tc = 256
    return pl.pallas_call(
        functools.partial(_regroup_kernel, cuts=cuts),
        grid=(depth, d // tc),
        in_specs=[pl.BlockSpec((1, n, tc), lambda l, i: (l, 0, i))],
        out_specs=[pl.BlockSpec((1, MAIN_COLS, tc), lambda l, i: (l, 0, i)),
                   pl.BlockSpec((1, 128, tc), lambda l, i: (l, 0, i))],
        out_shape=[jax.ShapeDtypeStruct((depth, MAIN_COLS, d), BF16), jax.ShapeDtypeStruct((depth, 128, d), BF16)],
        compiler_params=_cparams(("arbitrary", "arbitrary")),
        name="regroup_w_in",
    )(jnp.swapaxes(w_in, 1, 2))


def _rope_tables(seq):
    t = jnp.arange(seq)
    n_freq = HEAD_DIM // 4
    inv_freq = ROPE_THETA ** (-jnp.arange(n_freq, dtype=F32) / n_freq)
    pos = jnp.stack([t // GRID_W, t % GRID_W], axis=-1).astype(F32)
    ang = pos[:, :, None] * inv_freq
    cos, sin = jnp.cos(ang), jnp.sin(ang)
    cos_t = jnp.concatenate([cos, cos], axis=-1).reshape(seq, HEAD_DIM)
    sin_t = jnp.concatenate([-sin, sin], axis=-1).reshape(seq, HEAD_DIM)
    return cos_t, sin_t


def _gate_vectors(dt_bias, a_log, i_bias, f_bias):
    zeros8 = jnp.zeros((8,), F32)
    gadd = jnp.concatenate([zeros8, dt_bias.reshape(-1), i_bias.reshape(-1), f_bias.reshape(-1),
                            jnp.zeros((96,), F32)]).astype(F32)
    gmul = jnp.concatenate([zeros8, jnp.exp(a_log.astype(F32)).reshape(-1), jnp.zeros((112,), F32)])
    return gadd.reshape(1, 128), gmul.reshape(1, 128)


def _forward(x, c, ctx, c_ctx, w_ada, b_ada, norm_mix, norm_ffn, w_in, w_out, na_qk_gain, na_rpb, dn_conv,
             dn_a_log, dn_dt_bias, dn_norm, ml_i_bias, ml_f_bias, ml_norm, wa_qk_gain, wa_sink, w_ffn_in,
             w_ffn_out, solve_precision):
    batch, seq, d = x.shape
    lc = ctx.shape[1]
    depth = w_ada.shape[0]
    tm = 512
    tn_ffn = 512

    cos_t, sin_t = _rope_tables(seq)
    na_bias = _na_col_tables(na_rpb)
    cs = jnp.concatenate([c, c_ctx[None, :], jnp.zeros((8 - batch - 1, d), F32)], axis=0)
    mods = _ada_call(cs, w_ada, b_ada).reshape(depth, 8, 6, d)
    mods = jnp.pad(mods, ((0, 0), (0, 0), (0, 2), (0, 0)))

    w_main, w_gate = _regroup_call(w_in)
    w_o = w_out.astype(BF16)
    w_f1 = w_ffn_in.astype(BF16)
    w_f2 = w_ffn_out.astype(BF16)

    xl = x.reshape(batch * seq, d)
    xc = ctx.reshape(batch * lc, d)
    hl = _norm_call(xl, mods[0, :batch], norm_mix[0:1], seq, tm)
    hc = _norm_call(xc, mods[0, batch:batch + 1], norm_mix[0:1], batch * lc, tm)

    for l in range(depth):
        need_ctx = l < depth - 1
        mod_l, mod_c = mods[l, :batch], mods[l, batch:batch + 1]
        p_lat = _matmul_call(hl, w_main, l, 2 * tm, 1024, "inproj")
        pg_lat = _matmul_call(hl, w_gate, l, 2 * tm, 128, "inproj_gates")
        p_ctx = _matmul_call(hc, w_main, l, tm, 1024, "inproj_ctx")
        pg_ctx = _matmul_call(hc, w_gate, l, tm, 128, "inproj_gates_ctx")

        ya = _na_call(p_lat, p_ctx, na_qk_gain[l], na_bias, l, batch, seq, lc)
        gadd, gmul = _gate_vectors(dn_dt_bias[l], dn_a_log[l], ml_i_bias[l], ml_f_bias[l])
        dq_lat = _dn_prep_call(p_lat, dn_conv[l], batch, seq, 512)
        dq_ctx = _dn_prep_call(p_ctx, dn_conv[l], batch, lc, 256)
        yb, ycb = _dn_call(dq_lat, dq_ctx, p_lat, p_ctx, pg_lat, pg_ctx, gadd, gmul, dn_norm[l].reshape(1, -1),
                           batch, seq, lc, solve_precision)
        ym, ycm = _ml_call(p_lat, p_ctx, pg_lat, pg_ctx, gadd, ml_norm[l], batch, seq, lc)
        yw = _wa_call(p_lat, p_ctx, wa_qk_gain[l], wa_sink[l], cos_t, sin_t, batch, seq, lc)

        nl = min(l + 1, depth - 1)
        xl, h2 = _outproj_call((ya, yb, ym, yw), w_o, l, xl, mod_l, norm_ffn[l:l + 1], seq, tm)
        xl, hl = _ffn_call(h2, w_f1, w_f2, l, xl, mod_l, mods[nl, :batch], norm_mix[nl:nl + 1], seq, tm, tn_ffn,
                           need_ctx)
        if need_ctx:
            yca = _ctx_attn_call(p_ctx, na_qk_gain[l], wa_sink[l], batch, lc, 4, 1, NA_Q, NA_K, NA_V, False,
                                 "na_ctx_attn")
            ycw = _ctx_attn_call(p_ctx, wa_qk_gain[l], wa_sink[l], batch, lc, 4, 2, WA_Q, WA_K, WA_V, True,
                                 "wa_ctx_attn")
            xc, h2c = _outproj_call((yca, ycb, ycm, ycw), w_o, l, xc, mod_c, norm_ffn[l:l + 1], batch * lc, tm)
            xc, hc = _ffn_call(h2c, w_f1, w_f2, l, xc, mod_c, mods[nl, batch:batch + 1], norm_mix[nl:nl + 1],
                               batch * lc, tm, tn_ffn, True)
    return xl.reshape(batch, seq, d)


def kernel(x, c, ctx, c_ctx, w_ada, b_ada, norm_mix, norm_ffn, w_in, w_out, na_qk_gain, na_rpb, dn_conv, dn_a_log,
           dn_dt_bias, dn_norm, ml_i_bias, ml_f_bias, ml_norm, wa_qk_gain, wa_sink, w_ffn_in, w_ffn_out):
    return _forward(x, c, ctx, c_ctx, w_ada, b_ada, norm_mix, norm_ffn, w_in, w_out, na_qk_gain, na_rpb, dn_conv,
                    dn_a_log, dn_dt_bias, dn_norm, ml_i_bias, ml_f_bias, ml_norm, wa_qk_gain, wa_sink, w_ffn_in,
                    w_ffn_out, "bf16")
```

```python
import functools

import jax
import jax.numpy as jnp
from jax import lax
from jax.experimental import pallas as pl
from jax.experimental.pallas import tpu as pltpu

F32 = jnp.float32
BF16 = jnp.bfloat16

HEAD_DIM = 128
GRID_W = 64
CHUNK = 128
PREP_GROUP = 8
ML_GROUP = 4
SCAN_UNROLL = 2
NA_WIN_ROWS = 8
NA_WIN_COLS = 16
NA_QROWS = 4
NA_KROWS = NA_QROWS + NA_WIN_ROWS
NA_HEADS_PER_STEP = 2
WA_BLOCK = 128
ML_QK = 64
ROPE_THETA = 10000.0
EPS = 1e-6
NEG_INF = float("-inf")
VMEM_LIMIT = 48 * 1024 * 1024
DN_VMEM_LIMIT = 56 * 1024 * 1024

NA_Q, NA_K, NA_V = 0, 4, 8
DN_OG = 24
ML_Q, ML_K, ML_V, ML_OG = 28, 30, 32, 36
WA_Q, WA_K, WA_V = 40, 44, 46
MAIN_COLS = 48 * 128
LANE_BETA, LANE_ALPHA, LANE_IG, LANE_FG = 0, 8, 16, 24


def _cparams(sem, vmem_limit=VMEM_LIMIT):
    return pltpu.CompilerParams(dimension_semantics=sem, vmem_limit_bytes=vmem_limit)


def _rms(x, gain):
    return x * lax.rsqrt(jnp.mean(x * x, axis=-1, keepdims=True) + EPS) * gain


def _silu(x):
    return x * jax.nn.sigmoid(x)


def _softplus(x):
    return jnp.maximum(x, 0.0) + jnp.log1p(jnp.exp(-jnp.abs(x)))


def _log_sigmoid(x):
    return jnp.minimum(x, 0.0) - jnp.log1p(jnp.exp(-jnp.abs(x)))


def _dot(a, b, precision=None):
    return jnp.dot(a, b, preferred_element_type=F32, precision=precision)


def _dot_nt(a, b):
    return lax.dot_general(a, b, (((1,), (1,)), ((), ())), preferred_element_type=F32)


def _dot_tn(a, b):
    return lax.dot_general(a, b, (((0,), (0,)), ((), ())), preferred_element_type=F32)


def _iota(shape, dim):
    return lax.broadcasted_iota(jnp.int32, shape, dim)


def _lane_pick(x, lane):
    return jnp.sum(jnp.where(_iota(x.shape, 1) == lane, x, 0.0), axis=-1, keepdims=True)


def _to_row(col):
    n = col.shape[0]
    eye = _iota((n, n), 0) == _iota((n, n), 1)
    return jnp.sum(jnp.where(eye, col, 0.0), axis=0, keepdims=True)


def _ada_kernel(c_ref, w_ref, b_ref, o_ref):
    a = _silu(c_ref[...]).astype(BF16)
    o_ref[0] = _dot(a, w_ref[0].astype(BF16)) + b_ref[0]


def _ada_call(cs, w_ada, b_ada):
    depth, d, n = w_ada.shape
    tn = 1024
    return pl.pallas_call(
        _ada_kernel,
        grid=(depth, n // tn),
        in_specs=[pl.BlockSpec((8, d), lambda l, j: (0, 0)),
                  pl.BlockSpec((1, d, tn), lambda l, j: (l, 0, j)),
                  pl.BlockSpec((1, 1, tn), lambda l, j: (l, 0, j))],
        out_specs=pl.BlockSpec((1, 8, tn), lambda l, j: (l, 0, j)),
        out_shape=jax.ShapeDtypeStruct((depth, 8, n), F32),
        compiler_params=_cparams(("arbitrary", "arbitrary")),
        name="ada",
    )(cs, w_ada, b_ada.reshape(depth, 1, n))


def _norm_kernel(x_ref, mod_ref, g_ref, h_ref):
    mod = mod_ref[0]
    h = _rms(x_ref[...], g_ref[...])
    h_ref[...] = (h * (1.0 + mod[1:2]) + mod[0:1]).astype(BF16)


def _norm_call(x, mod, gain, rows_per_mod, tm):
    m, d = x.shape
    per = rows_per_mod // tm
    return pl.pallas_call(
        _norm_kernel,
        grid=(m // tm,),
        in_specs=[pl.BlockSpec((tm, d), lambda i: (i, 0)),
                  pl.BlockSpec((1, 8, d), lambda i: (i // per, 0, 0)),
                  pl.BlockSpec((1, d), lambda i: (0, 0))],
        out_specs=pl.BlockSpec((tm, d), lambda i: (i, 0)),
        out_shape=jax.ShapeDtypeStruct((m, d), BF16),
        compiler_params=_cparams(("arbitrary",)),
        name="norm_mod",
    )(x, mod, gain)


def _matmul_kernel(a_ref, wt_ref, o_ref):
    o_ref[...] = _dot_nt(a_ref[...], wt_ref[0])


def _matmul_call(a, wt, layer, tm, tn, name):
    m, k = a.shape
    n = wt.shape[1]
    return pl.pallas_call(
        _matmul_kernel,
        grid=(n // tn, m // tm),
        in_specs=[pl.BlockSpec((tm, k), lambda j, i: (i, 0)),
                  pl.BlockSpec((1, tn, k), lambda j, i: (layer, j, 0))],
        out_specs=pl.BlockSpec((tm, tn), lambda j, i: (i, j)),
        out_shape=jax.ShapeDtypeStruct((m, n), F32),
        compiler_params=_cparams(("arbitrary", "arbitrary")),
        name=name,
    )(a, wt)


def _outproj_kernel(ya_ref, yb_ref, ym_ref, yw_ref, w_ref, x_ref, mod_ref, g_ref, xo_ref, h_ref):
    gw = ya_ref.shape[1]
    acc = _dot(ya_ref[...], w_ref[0, 0:gw, :])
    acc += _dot(yb_ref[...], w_ref[0, gw:2 * gw, :])
    acc += _dot(ym_ref[...], w_ref[0, 2 * gw:3 * gw, :])
    acc += _dot(yw_ref[...], w_ref[0, 3 * gw:4 * gw, :])
    mod = mod_ref[0]
    xn = x_ref[...] + mod[2:3] * acc
    xo_ref[...] = xn
    h = _rms(xn, g_ref[...])
    h_ref[...] = (h * (1.0 + mod[4:5]) + mod[3:4]).astype(BF16)


def _outproj_call(ys, w, layer, x, mod, gain, rows_per_mod, tm):
    m, d = x.shape
    gw = ys[0].shape[1]
    per = rows_per_mod // tm
    yspec = pl.BlockSpec((tm, gw), lambda i: (i, 0))
    return pl.pallas_call(
        _outproj_kernel,
        grid=(m // tm,),
        in_specs=[yspec, yspec, yspec, yspec,
                  pl.BlockSpec((1, 4 * gw, d), lambda i: (layer, 0, 0)),
                  pl.BlockSpec((tm, d), lambda i: (i, 0)),
                  pl.BlockSpec((1, 8, d), lambda i: (i // per, 0, 0)),
                  pl.BlockSpec((1, d), lambda i: (0, 0))],
        out_specs=[pl.BlockSpec((tm, d), lambda i: (i, 0)),
                   pl.BlockSpec((tm, d), lambda i: (i, 0))],
        out_shape=[jax.ShapeDtypeStruct((m, d), F32), jax.ShapeDtypeStruct((m, d), BF16)],
        compiler_params=_cparams(("arbitrary",)),
        name="outproj",
    )(*ys, w, x, mod, gain)


def _ffn_kernel(h_ref, wg_ref, wu_ref, w2_ref, x_ref, mod_ref, modn_ref, gn_ref, xo_ref, hn_ref, acc_ref,
                *, with_next):
    j = pl.program_id(1)

    @pl.when(j == 0)
    def _():
        acc_ref[...] = jnp.zeros_like(acc_ref)

    h = h_ref[...]
    act = _silu(_dot(h, wg_ref[0])) * _dot(h, wu_ref[0])
    acc_ref[...] += _dot(act.astype(BF16), w2_ref[0])

    @pl.when(j == pl.num_programs(1) - 1)
    def _():
        xn = x_ref[...] + mod_ref[0][5:6] * acc_ref[...]
        xo_ref[...] = xn
        if with_next:
            modn = modn_ref[0]
            hn = _rms(xn, gn_ref[...])
            hn_ref[...] = (hn * (1.0 + modn[1:2]) + modn[0:1]).astype(BF16)
        else:
            hn_ref[...] = jnp.zeros_like(hn_ref)


def _ffn_call(h, w_in, w_out, layer, x, mod, mod_next, gain_next, rows_per_mod, tm, tn, with_next):
    m, d = x.shape
    hidden = w_out.shape[1]
    nj = hidden // tn
    per = rows_per_mod // tm
    return pl.pallas_call(
        functools.partial(_ffn_kernel, with_next=with_next),
        grid=(m // tm, nj),
        in_specs=[pl.BlockSpec((tm, d), lambda i, j: (i, 0)),
                  pl.BlockSpec((1, d, tn), lambda i, j: (layer, 0, j)),
                  pl.BlockSpec((1, d, tn), lambda i, j: (layer, 0, nj + j)),
                  pl.BlockSpec((1, tn, d), lambda i, j: (layer, j, 0)),
                  pl.BlockSpec((tm, d), lambda i, j: (i, 0)),
                  pl.BlockSpec((1, 8, d), lambda i, j: (i // per, 0, 0)),
                  pl.BlockSpec((1, 8, d), lambda i, j: (i // per, 0, 0)),
                  pl.BlockSpec((1, d), lambda i, j: (0, 0))],
        out_specs=[pl.BlockSpec((tm, d), lambda i, j: (i, 0)),
                   pl.BlockSpec((tm, d), lambda i, j: (i, 0))],
        out_shape=[jax.ShapeDtypeStruct((m, d), F32), jax.ShapeDtypeStruct((m, d), BF16)],
        scratch_shapes=[pltpu.VMEM((tm, d), F32)],
        compiler_params=_cparams(("arbitrary", "arbitrary")),
        name="ffn",
    )(h, w_in, w_in, w_out, x, mod, mod_next, gain_next)


def _na_kernel(q_ref, k_ref, v_ref, kc_ref, vc_ref, gain_ref, colbias_ref, o_ref,
               kn_scr, v_scr, kcn_scr, vc_scr, bias_scr, sl0_scr, sc0_scr, sl1_scr, sc1_scr, *, rows):
    step = pl.program_id(2)
    n_rb = pl.num_programs(2) - 1
    hd = HEAD_DIM
    heads = q_ref.shape[1] // hd

    @pl.when(step == 0)
    def _():
        sl1_scr[...] = jnp.zeros_like(sl1_scr)
        sc1_scr[...] = jnp.zeros_like(sc1_scr)
        g1 = gain_ref[1:2, :]
        closed = jnp.full((GRID_W, GRID_W), NEG_INF, F32)
        for hh in range(heads):
            cols = slice(hh * hd, (hh + 1) * hd)
            kn_scr[hh] = _rms(k_ref[:, cols], g1).astype(BF16)
            v_scr[hh] = v_ref[:, cols].astype(BF16)
            kcn_scr[hh] = _rms(kc_ref[:, cols], g1).astype(BF16)
            vc_scr[hh] = vc_ref[:, cols].astype(BF16)
            for typ, (offs, inside) in enumerate(_na_row_plan(rows)):
                for i in range(NA_QROWS):
                    for jp in range(NA_KROWS // 2):
                        halves = [colbias_ref[0, hh, offs[i][j]] if inside[i][j] else closed
                                  for j in (2 * jp, 2 * jp + 1)]
                        bias_scr[hh, typ, i * GRID_W:(i + 1) * GRID_W, 2 * jp * GRID_W:(2 * jp + 2) * GRID_W] = (
                            jnp.concatenate(halves, axis=1))

    scale = hd ** -0.5
    nk = NA_KROWS * GRID_W

    def window_start(blk):
        ws = jnp.clip(NA_QROWS * blk - NA_WIN_ROWS // 2, 0, rows - NA_KROWS)
        return pl.multiple_of(ws * GRID_W, GRID_W)

    def stages(write_loc, write_ctx, read_loc, read_ctx):
        prev = jnp.maximum(step - 1, 0)
        cur = jnp.minimum(step, n_rb - 1)
        typ = jnp.where(cur == 0, 0, jnp.where(cur == n_rb - 1, 2, 1))
        for hh in range(heads):
            cols = slice(hh * hd, (hh + 1) * hd)
            s_loc = read_loc[hh]
            s_ctx = read_ctx[hh]
            m = jnp.maximum(jnp.max(s_loc, axis=-1, keepdims=True), jnp.max(s_ctx, axis=-1, keepdims=True))
            p_loc = jnp.exp(s_loc - m)
            p_ctx = jnp.exp(s_ctx - m)
            den = jnp.sum(p_loc, axis=-1, keepdims=True) + jnp.sum(p_ctx, axis=-1, keepdims=True)
            vw = v_scr[hh, pl.ds(window_start(prev), nk), :]
            o = _dot(p_loc.astype(BF16), vw) + _dot(p_ctx.astype(BF16), vc_scr[hh])
            o_ref[:, cols] = (o / den).astype(BF16)
            qn = _rms(q_ref[:, cols], gain_ref[0:1, :]).astype(BF16)
            kw = kn_scr[hh, pl.ds(window_start(cur), nk), :]
            write_loc[hh] = _dot_nt(qn, kw) * scale + bias_scr[hh, pl.ds(typ, 1)][0]
            write_ctx[hh] = _dot_nt(qn, kcn_scr[hh]) * scale

    @pl.when(step % 2 == 0)
    def _():
        stages(sl0_scr, sc0_scr, sl1_scr, sc1_scr)

    @pl.when(step % 2 == 1)
    def _():
        stages(sl1_scr, sc1_scr, sl0_scr, sc0_scr)


def _na_row_plan(rows):
    n_rb = rows // NA_QROWS
    kh = NA_WIN_ROWS
    plan = []
    for rb in (0, 1, n_rb - 1):
        ws = min(max(NA_QROWS * rb - kh // 2, 0), rows - NA_KROWS)
        offs, inside = [], []
        for i in range(NA_QROWS):
            r = NA_QROWS * rb + i
            rs = min(max(r - kh // 2, 0), rows - kh)
            offs.append([min(max(ws + j - r, 1 - kh), kh - 1) + kh - 1 for j in range(NA_KROWS)])
            inside.append([rs <= ws + j < rs + kh for j in range(NA_KROWS)])
        plan.append((offs, inside))
    return plan


def _na_col_tables(rpb):
    kw = NA_WIN_COLS
    c = jnp.arange(GRID_W)
    cs = jnp.clip(c - kw // 2, 0, GRID_W - kw)
    in_col = (c[None, :] >= cs[:, None]) & (c[None, :] < cs[:, None] + kw)
    dc = jnp.clip(c[None, :] - c[:, None], 1 - kw, kw - 1) + kw - 1
    dc_hot = jax.nn.one_hot(dc, 2 * kw - 1, dtype=F32)
    val = jnp.einsum("lhab,ckb->lhack", rpb.astype(F32), dc_hot, precision=lax.Precision.HIGHEST)
    return jnp.where(in_col, val, NEG_INF)


def _na_call(p_lat, p_ctx, gain, colbias, layer, batch, seq, lc):
    rows = seq // GRID_W
    n_rb = rows // NA_QROWS
    tq = NA_QROWS * GRID_W
    nk = NA_KROWS * GRID_W
    heads = colbias.shape[1]
    n_off = colbias.shape[2]
    hd = HEAD_DIM
    hp = NA_HEADS_PER_STEP
    wide = hp * hd
    assert heads % hp == 0 and NA_Q % hp == 0 and NA_K % hp == 0 and NA_V % hp == 0
    return pl.pallas_call(
        functools.partial(_na_kernel, rows=rows),
        grid=(batch, heads // hp, n_rb + 1),
        in_specs=[pl.BlockSpec((tq, wide), lambda b, h, r: (b * n_rb + jnp.minimum(r, n_rb - 1), NA_Q // hp + h)),
                  pl.BlockSpec((seq, wide), lambda b, h, r: (b, NA_K // hp + h)),
                  pl.BlockSpec((seq, wide), lambda b, h, r: (b, NA_V // hp + h)),
                  pl.BlockSpec((lc, wide), lambda b, h, r: (b, NA_K // hp + h)),
                  pl.BlockSpec((lc, wide), lambda b, h, r: (b, NA_V // hp + h)),
                  pl.BlockSpec((2, hd), lambda b, h, r: (0, 0)),
                  pl.BlockSpec((1, hp, n_off, GRID_W, GRID_W), lambda b, h, r: (layer, h, 0, 0, 0))],
        out_specs=pl.BlockSpec((tq, wide), lambda b, h, r: (b * n_rb + jnp.maximum(r - 1, 0), h)),
        out_shape=jax.ShapeDtypeStruct((batch * seq, heads * hd), BF16),
        scratch_shapes=[pltpu.VMEM((hp, seq, hd), BF16), pltpu.VMEM((hp, seq, hd), BF16),
                        pltpu.VMEM((hp, lc, hd), BF16), pltpu.VMEM((hp, lc, hd), BF16),
                        pltpu.VMEM((hp, 3, tq, nk), F32),
                        pltpu.VMEM((hp, tq, nk), F32), pltpu.VMEM((hp, tq, lc), F32),
                        pltpu.VMEM((hp, tq, nk), F32), pltpu.VMEM((hp, tq, lc), F32)],
        compiler_params=_cparams(("arbitrary", "arbitrary", "arbitrary")),
        name="na_attn",
    )(p_lat, p_lat, p_lat, p_ctx, p_ctx, gain, colbias)


def _ctx_attn_kernel(sink_ref, q_ref, k_ref, v_ref, gain_ref, o_ref, *, use_sink):
    scale = HEAD_DIM ** -0.5
    qn = _rms(q_ref[...], gain_ref[0:1, :]).astype(BF16)
    kn = _rms(k_ref[...], gain_ref[1:2, :]).astype(BF16)
    s = _dot_nt(qn, kn) * scale
    m = jnp.max(s, axis=-1, keepdims=True)
    if use_sink:
        sk = sink_ref[pl.program_id(1)]
        m = jnp.maximum(m, sk)
    p = jnp.exp(s - m)
    den = jnp.sum(p, axis=-1, keepdims=True)
    if use_sink:
        den = den + jnp.exp(sk - m)
    o = _dot(p.astype(BF16), v_ref[...].astype(BF16))
    o_ref[...] = (o / den).astype(BF16)


def _ctx_attn_call(p_ctx, gain, sink, batch, lc, heads, rep, qoff, koff, voff, use_sink, name):
    hd = HEAD_DIM
    return pl.pallas_call(
        functools.partial(_ctx_attn_kernel, use_sink=use_sink),
        grid=(batch, heads),
        in_specs=[pl.BlockSpec(memory_space=pltpu.SMEM),
                  pl.BlockSpec((lc, hd), lambda b, h: (b, qoff + h)),
                  pl.BlockSpec((lc, hd), lambda b, h: (b, koff + h // rep)),
                  pl.BlockSpec((lc, hd), lambda b, h: (b, voff + h // rep)),
                  pl.BlockSpec((2, hd), lambda b, h: (0, 0))],
        out_specs=pl.BlockSpec((lc, hd), lambda b, h: (b, h)),
        out_shape=jax.ShapeDtypeStruct((batch * lc, heads * hd), BF16),
        compiler_params=_cparams(("arbitrary", "arbitrary")),
        name=name,
    )(sink, p_ctx, p_ctx, p_ctx, gain)


def _rope(x, cos, sin_signed):
    lane = _iota(x.shape, 1)
    swapped = jnp.where(lane % 64 < 32, pltpu.roll(x, 96, axis=1), pltpu.roll(x, 32, axis=1))
    return x * cos + swapped * sin_signed


def _wa_kernel(sink_ref, q_ref, k_ref, v_ref, kc_ref, vc_ref, gain_ref, cq_ref, sq_ref, ck_ref, sk_ref, o_ref,
               kpad_scr, vpad_scr, kcn_scr, vc_scr, sl0_scr, sc0_scr, sl1_scr, sc1_scr, *, seq):
    step = pl.program_id(1)
    nb = pl.num_programs(1) - 1
    wb = WA_BLOCK
    hd = HEAD_DIM
    groups = k_ref.shape[1] // hd

    @pl.when(step == 0)
    def _():
        sl1_scr[...] = jnp.zeros_like(sl1_scr)
        sc1_scr[...] = jnp.zeros_like(sc1_scr)
        g1 = gain_ref[1:2, :]
        zeros = jnp.zeros((wb, hd), BF16)
        for g in range(groups):
            cols = slice(g * hd, (g + 1) * hd)
            kpad_scr[g, 0:wb, :] = zeros
            kpad_scr[g, seq + wb:seq + 2 * wb, :] = zeros
            vpad_scr[g, 0:wb, :] = zeros
            vpad_scr[g, seq + wb:seq + 2 * wb, :] = zeros
            kpad_scr[g, wb:seq + wb, :] = _rope(_rms(k_ref[:, cols], g1), ck_ref[...], sk_ref[...]).astype(BF16)
            vpad_scr[g, wb:seq + wb, :] = v_ref[:, cols].astype(BF16)
            kcn_scr[g] = _rms(kc_ref[:, cols], g1).astype(BF16)
            vc_scr[g] = vc_ref[:, cols].astype(BF16)

    scale = hd ** -0.5

    def stages(write_loc, write_ctx, read_loc, read_ctx):
        prev = jnp.maximum(step - 1, 0)
        cur = jnp.minimum(step, nb - 1)
        g0 = gain_ref[0:1, :]
        cq = cq_ref[...]
        sq = sq_ref[...]
        shp = (2 * wb, 3 * wb)
        i = _iota(shp, 0) % wb
        j = _iota(shp, 1)
        kpos = (cur - 1) * wb + j
        ok = (j - i >= 0) & (j - i <= 2 * wb) & (kpos >= 0) & (kpos < seq)
        for g in range(groups):
            s_loc = read_loc[g]
            s_ctx = read_ctx[g]
            sink = jnp.where(_iota((2 * wb, 1), 0) < wb, sink_ref[2 * g], sink_ref[2 * g + 1])
            m = jnp.maximum(jnp.maximum(jnp.max(s_loc, axis=-1, keepdims=True),
                                        jnp.max(s_ctx, axis=-1, keepdims=True)), sink)
            p_loc = jnp.exp(s_loc - m)
            p_ctx = jnp.exp(s_ctx - m)
            den = jnp.sum(p_loc, axis=-1, keepdims=True) + jnp.sum(p_ctx, axis=-1, keepdims=True) + jnp.exp(sink - m)
            vw = vpad_scr[g, pl.ds(pl.multiple_of(prev * wb, wb), 3 * wb), :]
            o = (_dot(p_loc.astype(BF16), vw) + _dot(p_ctx.astype(BF16), vc_scr[g])) / den
            o_ref[:, 2 * g * hd:(2 * g + 1) * hd] = o[0:wb].astype(BF16)
            o_ref[:, (2 * g + 1) * hd:(2 * g + 2) * hd] = o[wb:2 * wb].astype(BF16)
            q2 = jnp.concatenate([_rope(_rms(q_ref[:, 2 * g * hd:(2 * g + 1) * hd], g0), cq, sq),
                                  _rope(_rms(q_ref[:, (2 * g + 1) * hd:(2 * g + 2) * hd], g0), cq, sq)],
                                 axis=0).astype(BF16)
            kw = kpad_scr[g, pl.ds(pl.multiple_of(cur * wb, wb), 3 * wb), :]
            write_loc[g] = jnp.where(ok, _dot_nt(q2, kw) * scale, NEG_INF)
            write_ctx[g] = _dot_nt(q2, kcn_scr[g]) * scale

    @pl.when(step % 2 == 0)
    def _():
        stages(sl0_scr, sc0_scr, sl1_scr, sc1_scr)

    @pl.when(step % 2 == 1)
    def _():
        stages(sl1_scr, sc1_scr, sl0_scr, sc0_scr)


def _wa_call(p_lat, p_ctx, gain, sink, cos_t, sin_t, batch, seq, lc):
    wb = WA_BLOCK
    hd = HEAD_DIM
    nb = seq // wb
    kvh = 2
    qw = 2 * kvh * hd
    kw_ = kvh * hd
    assert (WA_Q * hd) % qw == 0 and (WA_K * hd) % kw_ == 0 and (WA_V * hd) % kw_ == 0
    return pl.pallas_call(
        functools.partial(_wa_kernel, seq=seq),
        grid=(batch, nb + 1),
        in_specs=[pl.BlockSpec(memory_space=pltpu.SMEM),
                  pl.BlockSpec((wb, qw), lambda b, n: (b * nb + jnp.minimum(n, nb - 1), WA_Q * hd // qw)),
                  pl.BlockSpec((seq, kw_), lambda b, n: (b, WA_K * hd // kw_)),
                  pl.BlockSpec((seq, kw_), lambda b, n: (b, WA_V * hd // kw_)),
                  pl.BlockSpec((lc, kw_), lambda b, n: (b, WA_K * hd // kw_)),
                  pl.BlockSpec((lc, kw_), lambda b, n: (b, WA_V * hd // kw_)),
                  pl.BlockSpec((2, hd), lambda b, n: (0, 0)),
                  pl.BlockSpec((wb, hd), lambda b, n: (jnp.minimum(n, nb - 1), 0)),
                  pl.BlockSpec((wb, hd), lambda b, n: (jnp.minimum(n, nb - 1), 0)),
                  pl.BlockSpec((seq, hd), lambda b, n: (0, 0)),
                  pl.BlockSpec((seq, hd), lambda b, n: (0, 0))],
        out_specs=pl.BlockSpec((wb, qw), lambda b, n: (b * nb + jnp.maximum(n - 1, 0), 0)),
        out_shape=jax.ShapeDtypeStruct((batch * seq, qw), BF16),
        scratch_shapes=[pltpu.VMEM((kvh, seq + 2 * wb, hd), BF16), pltpu.VMEM((kvh, seq + 2 * wb, hd), BF16),
                        pltpu.VMEM((kvh, lc, hd), BF16), pltpu.VMEM((kvh, lc, hd), BF16),
                        pltpu.VMEM((kvh, 2 * wb, 3 * wb), F32), pltpu.VMEM((kvh, 2 * wb, lc), F32),
                        pltpu.VMEM((kvh, 2 * wb, 3 * wb), F32), pltpu.VMEM((kvh, 2 * wb, lc), F32)],
        compiler_params=_cparams(("arbitrary", "arbitrary")),
        name="wa_attn",
    )(sink, p_lat, p_lat, p_lat, p_ctx, p_ctx, gain, cos_t, sin_t, cos_t, sin_t)


def _dn_prep_kernel(x_ref, prev_ref, next_ref, w_ref, o_ref, *, n_tiles):
    t = pl.program_id(1)
    tl = x_ref.shape[0]
    x = x_ref[...]
    prev = jnp.where(t > 0, prev_ref[...], 0.0)
    nxt = jnp.where(t < n_tiles - 1, next_ref[...], 0.0)
    xe = jnp.concatenate([prev, x, nxt], axis=0)
    n = tl + 16
    taps = w_ref.shape[0]
    acc = jnp.zeros_like(x)
    for j in range(taps):
        off = 8 - taps // 2 + j
        shifted = xe if off % n == 0 else pltpu.roll(xe, n - off, axis=0)
        acc += shifted[0:tl] * w_ref[j:j + 1, :]
    y = _silu(acc)
    hd = HEAD_DIM
    nh = y.shape[1] // (3 * hd)
    for blk in range(3 * nh):
        yb = y[:, blk * hd:(blk + 1) * hd]
        if blk < 2 * nh:
            yb = yb * lax.rsqrt(jnp.sum(yb * yb, axis=-1, keepdims=True) + EPS)
            if blk < nh:
                yb = yb * hd ** -0.5
        o_ref[:, blk * hd:(blk + 1) * hd] = yb


def _dn_prep_call(p, conv_w, batch, length, tl):
    width = conv_w.shape[1]
    n_tiles = length // tl
    per8 = tl // 8
    last8 = batch * length // 8 - 1
    return pl.pallas_call(
        functools.partial(_dn_prep_kernel, n_tiles=n_tiles),
        grid=(batch, n_tiles),
        in_specs=[pl.BlockSpec((tl, width), lambda b, t: (b * n_tiles + t, 1)),
                  pl.BlockSpec((8, width), lambda b, t: (jnp.maximum((b * n_tiles + t) * per8 - 1, 0), 1)),
                  pl.BlockSpec((8, width), lambda b, t: (jnp.minimum((b * n_tiles + t + 1) * per8, last8), 1)),
                  pl.BlockSpec((conv_w.shape[0], width), lambda b, t: (0, 0))],
        out_specs=pl.BlockSpec((tl, width), lambda b, t: (b * n_tiles + t, 0)),
        out_shape=jax.ShapeDtypeStruct((batch * length, width), F32),
        compiler_params=_cparams(("arbitrary", "arbitrary")),
        name="dn_prep",
    )(p, p, p, conv_w)


def _split_bf16(a):
    hi = a.astype(BF16)
    return hi, (a - hi.astype(F32)).astype(BF16)


def _solve_dot(a, b, mode):
    if mode == "highest":
        return _dot(a, b, lax.Precision.HIGHEST)
    if mode == "bf16":
        return _dot(a.astype(BF16), b.astype(BF16))
    a_hi, a_lo = _split_bf16(a)
    b_hi, b_lo = _split_bf16(b)
    return _dot(a_hi, b_hi) + (_dot(a_hi, b_lo) + _dot(a_lo, b_hi))


def _dn_pass(q_ref, k_ref, v_ref, og_ref, gates_ref, gadd, gmul, norm_g, y_ref, length, h, state,
             lin_scr, add_scr, egl_scr, acc_scr, solve_precision):
    assert CHUNK == HEAD_DIM
    n_chunks = length // CHUNK

    acc_scr[0:length, :] = jnp.zeros((length, HEAD_DIM), F32)

    shp = (CHUNK, CHUNK)
    ri = _iota(shp, 0)
    ci = _iota(shp, 1)
    masks = ((ci <= ri, ci < ri), (ci >= ri, ci > ri))
    couple = ([], [])
    s = 1
    while s < CHUNK:
        same = (ri // (2 * s)) == (ci // (2 * s))
        couple[0].append(same & ((ri // s) % 2 == 1) & ((ci // s) % 2 == 0))
        couple[1].append(same & ((ri // s) % 2 == 0) & ((ci // s) % 2 == 1))
        s *= 2
    group = min(n_chunks, PREP_GROUP)

    def prep(gi, carry):
        chains = []
        for g in range(group):
            c = gi * group + g
            r0 = pl.multiple_of(c * CHUNK, CHUNK)
            q = q_ref[pl.ds(r0, CHUNK), :]
            k = k_ref[pl.ds(r0, CHUNK), :]
            raw = gates_ref[pl.ds(r0, CHUNK), :]
            gb = jnp.where(_iota(raw.shape, 1) < LANE_ALPHA, jax.nn.sigmoid(raw), -gmul * _softplus(raw + gadd))
            k16 = k.astype(BF16)
            kk = _dot_nt(k16, k16)
            qk = _dot_nt(q.astype(BF16), k16)
            for d in range(2):
                incl, strict = masks[d]
                beta_col = _lane_pick(gb, LANE_BETA + 4 * d + h)
                g_col = _lane_pick(gb, LANE_ALPHA + 4 * d + h)
                beta_row = _to_row(beta_col)
                g_row = _to_row(g_col)
                cum_col = jnp.sum(jnp.where(incl, g_row, 0.0), axis=-1, keepdims=True)
                cum_row = _to_row(cum_col)
                tot = jnp.sum(g_col, axis=0, keepdims=True)
                decay = jnp.exp(jnp.where(incl, cum_col - cum_row, NEG_INF))
                lm = jnp.where(strict, kk * beta_col * decay, 0.0)
                egl_scr[d, pl.ds(pl.multiple_of(c * 8, 8), 8), :] = jnp.broadcast_to(jnp.exp(tot), (8, HEAD_DIM))
                chains.append((d, c, r0, lm, beta_row, beta_row * jnp.exp(cum_row), (qk * decay).astype(BF16),
                               q * jnp.exp(cum_col), k * jnp.exp(tot - cum_col)))
        eye = (ri == ci).astype(F32)
        xs = [eye - jnp.where(couple[ch[0]][0], ch[3], 0.0) for ch in chains]
        for lvl in range(1, len(couple[0])):
            cs = [jnp.where(couple[ch[0]][lvl], ch[3], 0.0) for ch in chains]
            xc = [_solve_dot(x, c, solve_precision) for x, c in zip(xs, cs)]
            xs = [x - _solve_dot(t, x, solve_precision) for x, t in zip(xs, xc)]
        u16 = [_solve_dot(ainv * ch[4], v_ref[pl.ds(ch[2], CHUNK), :], solve_precision).astype(BF16)
               for ch, ainv in zip(chains, xs)]
        w16 = [_solve_dot(ainv * ch[5], k_ref[pl.ds(ch[2], CHUNK), :], solve_precision).astype(BF16)
               for ch, ainv in zip(chains, xs)]
        ket = [ch[8].T.astype(BF16) for ch in chains]
        s_lin = [_dot(kt, w) for kt, w in zip(ket, w16)]
        s_add = [_dot(kt, u) for kt, u in zip(ket, u16)]
        o_lin = [ch[7] - _dot(ch[6], w) for ch, w in zip(chains, w16)]
        o_add = [_dot(ch[6], u) for ch, u in zip(chains, u16)]
        for ch, sl, sa, ol, oa in zip(chains, s_lin, s_add, o_lin, o_add):
            d = ch[0]
            base = pl.multiple_of(ch[1] * 2 * CHUNK, 2 * CHUNK)
            lin_scr[d, pl.ds(base, CHUNK), :] = sl.astype(BF16)
            lin_scr[d, pl.ds(base + CHUNK, CHUNK), :] = ol.astype(BF16)
            add_scr[d, pl.ds(base, CHUNK), :] = sa
            add_scr[d, pl.ds(base + CHUNK, CHUNK), :] = oa
        return carry

    lax.fori_loop(0, n_chunks // group, prep, 0)

    def scan(s, st):
        new = []
        for d in range(2):
            c = s if d == 0 else n_chunks - 1 - s
            base = pl.multiple_of(c * 2 * CHUNK, 2 * CHUNK)
            sm = st[d]
            prod = _dot(lin_scr[d, pl.ds(base, 2 * CHUNK), :], sm.astype(BF16))
            add = add_scr[d, pl.ds(base, 2 * CHUNK), :]
            acc_scr[pl.ds(pl.multiple_of(c * CHUNK, CHUNK), CHUNK), :] += prod[CHUNK:] + add[CHUNK:]
            egl = egl_scr[d, pl.ds(pl.multiple_of(c * 8, 8), 8), :][0:1, :]
            new.append(sm * egl - prod[:CHUNK] + add[:CHUNK])
        return tuple(new)

    state = lax.fori_loop(0, n_chunks, scan, state, unroll=SCAN_UNROLL)

    y = _rms(acc_scr[0:length, :], norm_g) * _silu(og_ref[...])
    y_ref[...] = y.astype(BF16)
    return state


def _dn_kernel(q_ref, k_ref, v_ref, og_ref, gates_ref, qc_ref, kc_ref, vc_ref, ogc_ref, gatesc_ref,
               gadd_ref, gmul_ref, norm_ref, y_ref, yc_ref,
               lin_scr, add_scr, egl_scr, acc_scr, *, seq, lc, solve_precision):
    h = pl.program_id(1)
    gadd = gadd_ref[...]
    gmul = gmul_ref[...]
    norm_g = norm_ref[...]
    scr = (lin_scr, add_scr, egl_scr, acc_scr)
    zero = jnp.zeros((HEAD_DIM, HEAD_DIM), F32)
    state = _dn_pass(qc_ref, kc_ref, vc_ref, ogc_ref, gatesc_ref, gadd, gmul, norm_g, yc_ref, lc, h, (zero, zero),
                     *scr, solve_precision)
    _dn_pass(q_ref, k_ref, v_ref, og_ref, gates_ref, gadd, gmul, norm_g, y_ref, seq, h, state,
             *scr, solve_precision)


def _dn_call(dq_lat, dq_ctx, p_lat, p_ctx, pg_lat, pg_ctx, gadd, gmul, norm_g, batch, seq, lc, solve_precision):
    hd = HEAD_DIM
    heads = 4

    def blk(n, col):
        return pl.BlockSpec((n, hd), lambda b, h: (b, col + h))

    def gate_blk(n):
        return pl.BlockSpec((n, hd), lambda b, h: (b, 0))

    vec = pl.BlockSpec((1, hd), lambda b, h: (0, 0))
    return pl.pallas_call(
        functools.partial(_dn_kernel, seq=seq, lc=lc, solve_precision=solve_precision),
        grid=(batch, heads),
        in_specs=[blk(seq, 0), blk(seq, heads), blk(seq, 2 * heads), blk(seq, DN_OG), gate_blk(seq),
                  blk(lc, 0), blk(lc, heads), blk(lc, 2 * heads), blk(lc, DN_OG), gate_blk(lc),
                  vec, vec, vec],
        out_specs=[pl.BlockSpec((seq, hd), lambda b, h: (b, h)),
                   pl.BlockSpec((lc, hd), lambda b, h: (b, h))],
        out_shape=[jax.ShapeDtypeStruct((batch * seq, heads * hd), BF16),
                   jax.ShapeDtypeStruct((batch * lc, heads * hd), BF16)],
        scratch_shapes=[pltpu.VMEM((2, 2 * seq, hd), BF16),
                        pltpu.VMEM((2, 2 * seq, hd), F32),
                        pltpu.VMEM((2, seq // CHUNK * 8, hd), F32),
                        pltpu.VMEM((seq, hd), F32)],
        compiler_params=_cparams(("arbitrary", "arbitrary"), DN_VMEM_LIMIT),
        name="deltanet",
    )(dq_lat, dq_lat, dq_lat, p_lat, pg_lat, dq_ctx, dq_ctx, dq_ctx, p_ctx, pg_ctx, gadd, gmul, norm_g)


def _ml_pass(q_ref, k_ref, v_ref, og_ref, gates_ref, gadd, norm_g, y_ref, length, h, state, acc_scr):
    n_chunks = length // CHUNK
    acc_scr[0:length, :] = jnp.zeros((length, HEAD_DIM), F32)
    shp = (CHUNK, CHUNK)
    ri = _iota(shp, 0)
    ci = _iota(shp, 1)
    masks = (ci <= ri, ci >= ri)
    head_lanes = (_iota((1, HEAD_DIM), 1) // ML_QK) == (h % 2)

    group = min(n_chunks, ML_GROUP)

    def each(f, *lists):
        return [f(*a) for a in zip(*lists)]

    def step(gi, st):
        dirs, r0s = [], []
        for j in range(group):
            s = gi * group + j
            for d in range(2):
                c = s if d == 0 else n_chunks - 1 - s
                dirs.append(d)
                r0s.append(pl.multiple_of(c * CHUNK, CHUNK))
        incl = [masks[d] for d in dirs]
        q = [jnp.where(head_lanes, q_ref[pl.ds(r0, CHUNK), :], 0.0) for r0 in r0s]
        k = [jnp.where(head_lanes, k_ref[pl.ds(r0, CHUNK), :], 0.0) * ML_QK ** -0.5 for r0 in r0s]
        v16 = [v_ref[pl.ds(r0, CHUNK), :].astype(BF16) for r0 in r0s]
        gb = [gates_ref[pl.ds(r0, CHUNK), :] + gadd for r0 in r0s]
        ig_col = [_lane_pick(g, LANE_IG + 4 * d + h) for g, d in zip(gb, dirs)]
        lf_col = [_log_sigmoid(_lane_pick(g, LANE_FG + 4 * d + h)) for g, d in zip(gb, dirs)]
        ig_row = each(_to_row, ig_col)
        lf_row = each(_to_row, lf_col)
        b_col = each(lambda m, r: jnp.sum(jnp.where(m, r, 0.0), axis=-1, keepdims=True), incl, lf_row)
        b_row = each(_to_row, b_col)
        b_last = each(lambda x: jnp.sum(x, axis=0, keepdims=True), lf_col)
        g_end = each(lambda bl, b, i: bl - b + i, b_last, b_col, ig_col)
        m_loc = each(lambda x: jnp.max(x, axis=0, keepdims=True), g_end)
        kw = each(lambda kk, g, m: kk * jnp.exp(g - m), k, g_end, m_loc)
        q16 = each(lambda x: x.astype(BF16), q)
        k16 = each(lambda x: x.astype(BF16), k)
        c_loc = each(lambda a, b: _dot_tn(a.astype(BF16), b), kw, v16)
        n_loc = each(lambda x: jnp.sum(x, axis=0, keepdims=True), kw)
        log_d = each(lambda m, bc, br, ir: jnp.where(m, bc - br + ir, NEG_INF), incl, b_col, b_row, ig_row)
        mx_col = each(lambda x: jnp.max(x, axis=-1, keepdims=True), log_d)
        s0 = each(lambda a, b, ld, mx: _dot_nt(a, b) * jnp.exp(ld - mx), q16, k16, log_d, mx_col)
        sv0 = each(lambda s, vv: _dot(s.astype(BF16), vv), s0, v16)
        rs0 = each(lambda s: jnp.sum(s, axis=-1, keepdims=True), s0)
        st = list(st)
        for i, (d, r0) in enumerate(zip(dirs, r0s)):
            c_st, n_st, m_st = st[d]
            m_inter = b_col[i] + m_st
            m_t = jnp.maximum(mx_col[i], m_inter)
            intra = jnp.exp(mx_col[i] - m_t)
            inter = jnp.exp(m_inter - m_t)
            num = intra * sv0[i] + inter * _dot(q16[i], c_st.astype(BF16))
            den = intra * rs0[i] + inter * jnp.sum(q[i] * n_st, axis=-1, keepdims=True)
            acc_scr[pl.ds(r0, CHUNK), :] += num / jnp.maximum(jnp.abs(den), jnp.exp(-m_t))
            m_new = jnp.maximum(b_last[i] + m_st, m_loc[i])
            a = jnp.exp(b_last[i] + m_st - m_new)
            e = jnp.exp(m_loc[i] - m_new)
            st[d] = (a * c_st + e * c_loc[i], a * n_st + e * n_loc[i], m_new)
        return tuple(st)

    state = lax.fori_loop(0, n_chunks // group, step, state)
    y = jax.nn.sigmoid(og_ref[...]) * _rms(acc_scr[0:length, :], norm_g)
    y_ref[...] = y.astype(BF16)
    return state


def _ml_kernel(q_ref, k_ref, v_ref, og_ref, gates_ref, qc_ref, kc_ref, vc_ref, ogc_ref, gatesc_ref,
               gadd_ref, norm_ref, y_ref, yc_ref, acc_scr, *, seq, lc):
    h = pl.program_id(1)
    gadd = gadd_ref[...]
    norm_g = norm_ref[0]
    zero = (jnp.zeros((HEAD_DIM, HEAD_DIM), F32), jnp.zeros((1, HEAD_DIM), F32), jnp.zeros((1, 1), F32))
    state = _ml_pass(qc_ref, kc_ref, vc_ref, ogc_ref, gatesc_ref, gadd, norm_g, yc_ref, lc, h, (zero, zero), acc_scr)
    _ml_pass(q_ref, k_ref, v_ref, og_ref, gates_ref, gadd, norm_g, y_ref, seq, h, state, acc_scr)


def _ml_call(p_lat, p_ctx, pg_lat, pg_ctx, gadd, norm_g, batch, seq, lc):
    hd = HEAD_DIM
    heads = 4

    def specs(n):
        return [pl.BlockSpec((n, hd), lambda b, h: (b, ML_Q + h // 2)),
                pl.BlockSpec((n, hd), lambda b, h: (b, ML_K + h // 2)),
                pl.BlockSpec((n, hd), lambda b, h: (b, ML_V + h)),
                pl.BlockSpec((n, hd), lambda b, h: (b, ML_OG + h)),
                pl.BlockSpec((n, hd), lambda b, h: (b, 0))]

    return pl.pallas_call(
        functools.partial(_ml_kernel, seq=seq, lc=lc),
        grid=(batch, heads),
        in_specs=specs(seq) + specs(lc) + [pl.BlockSpec((1, hd), lambda b, h: (0, 0)),
                                            pl.BlockSpec((1, 1, hd), lambda b, h: (h, 0, 0))],
        out_specs=[pl.BlockSpec((seq, hd), lambda b, h: (b, h)),
                   pl.BlockSpec((lc, hd), lambda b, h: (b, h))],
        out_shape=[jax.ShapeDtypeStruct((batch * seq, heads * hd), BF16),
                   jax.ShapeDtypeStruct((batch * lc, heads * hd), BF16)],
        scratch_shapes=[pltpu.VMEM((seq, hd), F32)],
        compiler_params=_cparams(("arbitrary", "arbitrary")),
        name="mlstm",
    )(p_lat, p_lat, p_lat, p_lat, pg_lat, p_ctx, p_ctx, p_ctx, p_ctx, pg_ctx, gadd, norm_g.reshape(heads, 1, hd))


def _regroup_kernel(wt_ref, main_ref, gate_ref, *, cuts):
    (a0, a1), (b0, b1), (c0, c1), (g0, g1), (h0, h1) = cuts
    main_ref[0, 0:a1 - a0, :] = wt_ref[0, a0:a1, :].astype(BF16)
    main_ref[0, a1 - a0:a1 - a0 + b1 - b0, :] = wt_ref[0, b0:b1, :].astype(BF16)
    main_ref[0, a1 - a0 + b1 - b0:, :] = wt_ref[0, c0:c1, :].astype(BF16)
    ng = (g1 - g0) + (h1 - h0)
    gate_ref[0, 0:g1 - g0, :] = wt_ref[0, g0:g1, :].astype(BF16)
    gate_ref[0, g1 - g0:ng, :] = wt_ref[0, h0:h1, :].astype(BF16)
    gate_ref[0, ng:, :] = jnp.zeros((gate_ref.shape[1] - ng, gate_ref.shape[2]), BF16)


def _regroup_call(w_in):
    depth, d, n = w_in.shape
    gw = 512
    dn_gate0 = 3 * gw + 4 * gw
    ml0 = dn_gate0 + 16
    ml_gate0 = ml0 + 3 * gw
    wa0 = ml_gate0 + 16
    cuts = ((0, dn_gate0), (ml0, ml_gate0), (wa0, n), (dn_gate0, ml0), (ml_gate0, wa0))
    tc = 256
    return pl.pallas_call(
        functools.partial(_regroup_kernel, cuts=cuts),
        grid=(depth, d // tc),
        in_specs=[pl.BlockSpec((1, n, tc), lambda l, i: (l, 0, i))],
        out_specs=[pl.BlockSpec((1, MAIN_COLS, tc), lambda l, i: (l, 0, i)),
                   pl.BlockSpec((1, 128, tc), lambda l, i: (l, 0, i))],
        out_shape=[jax.ShapeDtypeStruct((depth, MAIN_COLS, d), BF16), jax.ShapeDtypeStruct((depth, 128, d), BF16)],
        compiler_params=_cparams(("arbitrary", "arbitrary")),
        name="regroup_w_in",
    )(jnp.swapaxes(w_in, 1, 2))


def _rope_tables(seq):
    t = jnp.arange(seq)
    n_freq = HEAD_DIM // 4
    inv_freq = ROPE_THETA ** (-jnp.arange(n_freq, dtype=F32) / n_freq)
    pos = jnp.stack([t // GRID_W, t % GRID_W], axis=-1).astype(F32)
    ang = pos[:, :, None] * inv_freq
    cos, sin = jnp.cos(ang), jnp.sin(ang)
    cos_t = jnp.concatenate([cos, cos], axis=-1).reshape(seq, HEAD_DIM)
    sin_t = jnp.concatenate([-sin, sin], axis=-1).reshape(seq, HEAD_DIM)
    return cos_t, sin_t


def _gate_vectors(dt_bias, a_log, i_bias, f_bias):
    zeros8 = jnp.zeros((8,), F32)
    gadd = jnp.concatenate([zeros8, dt_bias.reshape(-1), i_bias.reshape(-1), f_bias.reshape(-1),
                            jnp.zeros((96,), F32)]).astype(F32)
    gmul = jnp.concatenate([zeros8, jnp.exp(a_log.astype(F32)).reshape(-1), jnp.zeros((112,), F32)])
    return gadd.reshape(1, 128), gmul.reshape(1, 128)


def _forward(x, c, ctx, c_ctx, w_ada, b_ada, norm_mix, norm_ffn, w_in, w_out, na_qk_gain, na_rpb, dn_conv,
             dn_a_log, dn_dt_bias, dn_norm, ml_i_bias, ml_f_bias, ml_norm, wa_qk_gain, wa_sink, w_ffn_in,
             w_ffn_out, solve_precision):
    batch, seq, d = x.shape
    lc = ctx.shape[1]
    depth = w_ada.shape[0]
    tm = 512
    tn_ffn = 512

    cos_t, sin_t = _rope_tables(seq)
    na_bias = _na_col_tables(na_rpb)
    cs = jnp.concatenate([c, c_ctx[None, :], jnp.zeros((8 - batch - 1, d), F32)], axis=0)
    mods = _ada_call(cs, w_ada, b_ada).reshape(depth, 8, 6, d)
    mods = jnp.pad(mods, ((0, 0), (0, 0), (0, 2), (0, 0)))

    w_main, w_gate = _regroup_call(w_in)
    w_o = w_out.astype(BF16)
    w_f1 = w_ffn_in.astype(BF16)
    w_f2 = w_ffn_out.astype(BF16)

    xl = x.reshape(batch * seq, d)
    xc = ctx.reshape(batch * lc, d)
    hl = _norm_call(xl, mods[0, :batch], norm_mix[0:1], seq, tm)
    hc = _norm_call(xc, mods[0, batch:batch + 1], norm_mix[0:1], batch * lc, tm)

    for l in range(depth):
        need_ctx = l < depth - 1
        mod_l, mod_c = mods[l, :batch], mods[l, batch:batch + 1]
        p_lat = _matmul_call(hl, w_main, l, 2 * tm, 2048, "inproj")
        pg_lat = _matmul_call(hl, w_gate, l, 2 * tm, 128, "inproj_gates")
        p_ctx = _matmul_call(hc, w_main, l, tm, 2048, "inproj_ctx")
        pg_ctx = _matmul_call(hc, w_gate, l, tm, 128, "inproj_gates_ctx")

        ya = _na_call(p_lat, p_ctx, na_qk_gain[l], na_bias, l, batch, seq, lc)
        gadd, gmul = _gate_vectors(dn_dt_bias[l], dn_a_log[l], ml_i_bias[l], ml_f_bias[l])
        dq_lat = _dn_prep_call(p_lat, dn_conv[l], batch, seq, 512)
        dq_ctx = _dn_prep_call(p_ctx, dn_conv[l], batch, lc, 256)
        yb, ycb = _dn_call(dq_lat, dq_ctx, p_lat, p_ctx, pg_lat, pg_ctx, gadd, gmul, dn_norm[l].reshape(1, -1),
                           batch, seq, lc, solve_precision)
        ym, ycm = _ml_call(p_lat, p_ctx, pg_lat, pg_ctx, gadd, ml_norm[l], batch, seq, lc)
        yw = _wa_call(p_lat, p_ctx, wa_qk_gain[l], wa_sink[l], cos_t, sin_t, batch, seq, lc)

        nl = min(l + 1, depth - 1)
        xl, h2 = _outproj_call((ya, yb, ym, yw), w_o, l, xl, mod_l, norm_ffn[l:l + 1], seq, tm)
        xl, hl = _ffn_call(h2, w_f1, w_f2, l, xl, mod_l, mods[nl, :batch], norm_mix[nl:nl + 1], seq, tm, tn_ffn,
                           need_ctx)
        if need_ctx:
            yca = _ctx_attn_call(p_ctx, na_qk_gain[l], wa_sink[l], batch, lc, 4, 1, NA_Q, NA_K, NA_V, False,
                                 "na_ctx_attn")
            ycw = _ctx_attn_call(p_ctx, wa_qk_gain[l], wa_sink[l], batch, lc, 4, 2, WA_Q, WA_K, WA_V, True,
                                 "wa_ctx_attn")
            xc, h2c = _outproj_call((yca, ycb, ycm, ycw), w_o, l, xc, mod_c, norm_ffn[l:l + 1], batch * lc, tm)
            xc, hc = _ffn_call(h2c, w_f1, w_f2, l, xc, mod_c, mods[nl, batch:batch + 1], norm_mix[nl:nl + 1],
                               batch * lc, tm, tn_ffn, True)
    return xl.reshape(batch, seq, d)


def kernel(x, c, ctx, c_ctx, w_ada, b_ada, norm_mix, norm_ffn, w_in, w_out, na_qk_gain, na_rpb, dn_conv, dn_a_log,
           dn_dt_bias, dn_norm, ml_i_bias, ml_f_bias, ml_norm, wa_qk_gain, wa_sink, w_ffn_in, w_ffn_out):
    return _forward(x, c, ctx, c_ctx, w_ada, b_ada, norm_mix, norm_ffn, w_in, w_out, na_qk_gain, na_rpb, dn_conv,
                    dn_a_log, dn_dt_bias, dn_norm, ml_i_bias, ml_f_bias, ml_norm, wa_qk_gain, wa_sink, w_ffn_in,
                    w_ffn_out, "bf16")
```

```python
import functools

import jax
import jax.numpy as jnp
from jax import lax
from jax.experimental import pallas as pl
from jax.experimental.pallas import tpu as pltpu

F32 = jnp.float32
BF16 = jnp.bfloat16

HEAD_DIM = 128
GRID_W = 64
CHUNK = 128
PREP_GROUP = 8
ML_GROUP = 4
SCAN_UNROLL = 2
NA_WIN_ROWS = 8
NA_WIN_COLS = 16
NA_QROWS = 4
NA_KROWS = NA_QROWS + NA_WIN_ROWS
NA_HEADS_PER_STEP = 2
WA_BLOCK = 128
ML_QK = 64
ROPE_THETA = 10000.0
EPS = 1e-6
NEG_INF = float("-inf")
VMEM_LIMIT = 48 * 1024 * 1024
DN_VMEM_LIMIT = 56 * 1024 * 1024
FFN_VMEM_LIMIT = 62 * 1024 * 1024

NA_Q, NA_K, NA_V = 0, 4, 8
DN_OG = 24
ML_Q, ML_K, ML_V, ML_OG = 28, 30, 32, 36
WA_Q, WA_K, WA_V = 40, 44, 46
MAIN_COLS = 48 * 128
LANE_BETA, LANE_ALPHA, LANE_IG, LANE_FG = 0, 8, 16, 24


def _cparams(sem, vmem_limit=VMEM_LIMIT):
    return pltpu.CompilerParams(dimension_semantics=sem, vmem_limit_bytes=vmem_limit)


def _rms(x, gain):
    return x * lax.rsqrt(jnp.mean(x * x, axis=-1, keepdims=True) + EPS) * gain


def _silu(x):
    return x * jax.nn.sigmoid(x)


def _softplus(x):
    return jnp.maximum(x, 0.0) + jnp.log1p(jnp.exp(-jnp.abs(x)))


def _log_sigmoid(x):
    return jnp.minimum(x, 0.0) - jnp.log1p(jnp.exp(-jnp.abs(x)))


def _dot(a, b, precision=None):
    return jnp.dot(a, b, preferred_element_type=F32, precision=precision)


def _dot_nt(a, b):
    return lax.dot_general(a, b, (((1,), (1,)), ((), ())), preferred_element_type=F32)


def _dot_tn(a, b):
    return lax.dot_general(a, b, (((0,), (0,)), ((), ())), preferred_element_type=F32)


def _iota(shape, dim):
    return lax.broadcasted_iota(jnp.int32, shape, dim)


def _lane_pick(x, lane):
    return jnp.sum(jnp.where(_iota(x.shape, 1) == lane, x, 0.0), axis=-1, keepdims=True)


def _to_row(col):
    n = col.shape[0]
    eye = _iota((n, n), 0) == _iota((n, n), 1)
    return jnp.sum(jnp.where(eye, col, 0.0), axis=0, keepdims=True)


def _ada_kernel(c_ref, w_ref, b_ref, o_ref):
    a = _silu(c_ref[...]).astype(BF16)
    o_ref[0] = _dot(a, w_ref[0].astype(BF16)) + b_ref[0]


def _ada_call(cs, w_ada, b_ada):
    depth, d, n = w_ada.shape
    tn = 1024
    return pl.pallas_call(
        _ada_kernel,
        grid=(depth, n // tn),
        in_specs=[pl.BlockSpec((8, d), lambda l, j: (0, 0)),
                  pl.BlockSpec((1, d, tn), lambda l, j: (l, 0, j)),
                  pl.BlockSpec((1, 1, tn), lambda l, j: (l, 0, j))],
        out_specs=pl.BlockSpec((1, 8, tn), lambda l, j: (l, 0, j)),
        out_shape=jax.ShapeDtypeStruct((depth, 8, n), F32),
        compiler_params=_cparams(("arbitrary", "arbitrary")),
        name="ada",
    )(cs, w_ada, b_ada.reshape(depth, 1, n))


def _norm_kernel(x_ref, mod_ref, g_ref, h_ref):
    mod = mod_ref[0]
    h = _rms(x_ref[...], g_ref[...])
    h_ref[...] = (h * (1.0 + mod[1:2]) + mod[0:1]).astype(BF16)


def _norm_call(x, mod, gain, rows_per_mod, tm):
    m, d = x.shape
    per = rows_per_mod // tm
    return pl.pallas_call(
        _norm_kernel,
        grid=(m // tm,),
        in_specs=[pl.BlockSpec((tm, d), lambda i: (i, 0)),
                  pl.BlockSpec((1, 8, d), lambda i: (i // per, 0, 0)),
                  pl.BlockSpec((1, d), lambda i: (0, 0))],
        out_specs=pl.BlockSpec((tm, d), lambda i: (i, 0)),
        out_shape=jax.ShapeDtypeStruct((m, d), BF16),
        compiler_params=_cparams(("arbitrary",)),
        name="norm_mod",
    )(x, mod, gain)


def _matmul_kernel(a_ref, wt_ref, o_ref):
    o_ref[...] = _dot_nt(a_ref[...], wt_ref[0])


def _matmul_call(a, wt, layer, tm, tn, name):
    m, k = a.shape
    n = wt.shape[1]
    return pl.pallas_call(
        _matmul_kernel,
        grid=(n // tn, m // tm),
        in_specs=[pl.BlockSpec((tm, k), lambda j, i: (i, 0)),
                  pl.BlockSpec((1, tn, k), lambda j, i: (layer, j, 0))],
        out_specs=pl.BlockSpec((tm, tn), lambda j, i: (i, j)),
        out_shape=jax.ShapeDtypeStruct((m, n), F32),
        compiler_params=_cparams(("arbitrary", "arbitrary")),
        name=name,
    )(a, wt)


def _outproj_kernel(ya_ref, yb_ref, ym_ref, yw_ref, w_ref, x_ref, mod_ref, g_ref, xo_ref, h_ref):
    gw = ya_ref.shape[1]
    acc = _dot(ya_ref[...], w_ref[0, 0:gw, :])
    acc += _dot(yb_ref[...], w_ref[0, gw:2 * gw, :])
    acc += _dot(ym_ref[...], w_ref[0, 2 * gw:3 * gw, :])
    acc += _dot(yw_ref[...], w_ref[0, 3 * gw:4 * gw, :])
    mod = mod_ref[0]
    xn = x_ref[...] + mod[2:3] * acc
    xo_ref[...] = xn
    h = _rms(xn, g_ref[...])
    h_ref[...] = (h * (1.0 + mod[4:5]) + mod[3:4]).astype(BF16)


def _outproj_call(ys, w, layer, x, mod, gain, rows_per_mod, tm):
    m, d = x.shape
    gw = ys[0].shape[1]
    per = rows_per_mod // tm
    yspec = pl.BlockSpec((tm, gw), lambda i: (i, 0))
    return pl.pallas_call(
        _outproj_kernel,
        grid=(m // tm,),
        in_specs=[yspec, yspec, yspec, yspec,
                  pl.BlockSpec((1, 4 * gw, d), lambda i: (layer, 0, 0)),
                  pl.BlockSpec((tm, d), lambda i: (i, 0)),
                  pl.BlockSpec((1, 8, d), lambda i: (i // per, 0, 0)),
                  pl.BlockSpec((1, d), lambda i: (0, 0))],
        out_specs=[pl.BlockSpec((tm, d), lambda i: (i, 0)),
                   pl.BlockSpec((tm, d), lambda i: (i, 0))],
        out_shape=[jax.ShapeDtypeStruct((m, d), F32), jax.ShapeDtypeStruct((m, d), BF16)],
        compiler_params=_cparams(("arbitrary",)),
        name="outproj",
    )(*ys, w, x, mod, gain)


def _ffn_kernel(h_ref, wg_ref, wu_ref, w2_ref, x_hbm, mod_ref, modn_ref, gn_ref, xo_ref, hn_ref, x_sem,
                *, with_next):
    i = pl.program_id(0)
    j = pl.program_id(1)
    tm = xo_ref.shape[0]
    x_copy = pltpu.make_async_copy(x_hbm.at[pl.ds(pl.multiple_of(i * tm, tm), tm), :], xo_ref, x_sem)

    @pl.when(j == 0)
    def _():
        x_copy.start()

    h = h_ref[...]
    act = _silu(_dot(h, wg_ref[0])) * _dot(h, wu_ref[0])
    contrib = mod_ref[0][5:6] * _dot(act.astype(BF16), w2_ref[0])

    @pl.when(j == 0)
    def _():
        x_copy.wait()

    xo_ref[...] += contrib

    @pl.when(j == pl.num_programs(1) - 1)
    def _():
        if with_next:
            modn = modn_ref[0]
            hn = _rms(xo_ref[...], gn_ref[...])
            hn_ref[...] = (hn * (1.0 + modn[1:2]) + modn[0:1]).astype(BF16)
        else:
            hn_ref[...] = jnp.zeros_like(hn_ref)


def _ffn_call(h, w_in, w_out, layer, x, mod, mod_next, gain_next, rows_per_mod, tm, tn, with_next):
    m, d = x.shape
    hidden = w_out.shape[1]
    nj = hidden // tn
    per = rows_per_mod // tm
    return pl.pallas_call(
        functools.partial(_ffn_kernel, with_next=with_next),
        grid=(m // tm, nj),
        in_specs=[pl.BlockSpec((tm, d), lambda i, j: (i, 0)),
                  pl.BlockSpec((1, d, tn), lambda i, j: (layer, 0, j)),
                  pl.BlockSpec((1, d, tn), lambda i, j: (layer, 0, nj + j)),
                  pl.BlockSpec((1, tn, d), lambda i, j: (layer, j, 0)),
                  pl.BlockSpec(memory_space=pl.ANY),
                  pl.BlockSpec((1, 8, d), lambda i, j: (i // per, 0, 0)),
                  pl.BlockSpec((1, 8, d), lambda i, j: (i // per, 0, 0)),
                  pl.BlockSpec((1, d), lambda i, j: (0, 0))],
        out_specs=[pl.BlockSpec((tm, d), lambda i, j: (i, 0)),
                   pl.BlockSpec((tm, d), lambda i, j: (i, 0))],
        out_shape=[jax.ShapeDtypeStruct((m, d), F32), jax.ShapeDtypeStruct((m, d), BF16)],
        scratch_shapes=[pltpu.SemaphoreType.DMA(())],
        compiler_params=_cparams(("arbitrary", "arbitrary"), FFN_VMEM_LIMIT),
        name="ffn",
    )(h, w_in, w_in, w_out, x, mod, mod_next, gain_next)


def _na_kernel(q_ref, k_ref, v_ref, kc_ref, vc_ref, gain_ref, colbias_ref, o_ref,
               kn_scr, v_scr, kcn_scr, vc_scr, bias_scr, sl0_scr, sc0_scr, sl1_scr, sc1_scr, *, rows):
    step = pl.program_id(2)
    n_rb = pl.num_programs(2) - 1
    hd = HEAD_DIM
    heads = q_ref.shape[1] // hd

    @pl.when(step == 0)
    def _():
        sl1_scr[...] = jnp.zeros_like(sl1_scr)
        sc1_scr[...] = jnp.zeros_like(sc1_scr)
        g1 = gain_ref[1:2, :]
        closed = jnp.full((GRID_W, GRID_W), NEG_INF, F32)
        for hh in range(heads):
            cols = slice(hh * hd, (hh + 1) * hd)
            kn_scr[hh] = _rms(k_ref[:, cols], g1).astype(BF16)
            v_scr[hh] = v_ref[:, cols].astype(BF16)
            kcn_scr[hh] = _rms(kc_ref[:, cols], g1).astype(BF16)
            vc_scr[hh] = vc_ref[:, cols].astype(BF16)
            for typ, (offs, inside) in enumerate(_na_row_plan(rows)):
                for i in range(NA_QROWS):
                    for jp in range(NA_KROWS // 2):
                        halves = [colbias_ref[0, hh, offs[i][j]] if inside[i][j] else closed
                                  for j in (2 * jp, 2 * jp + 1)]
                        bias_scr[hh, typ, i * GRID_W:(i + 1) * GRID_W, 2 * jp * GRID_W:(2 * jp + 2) * GRID_W] = (
                            jnp.concatenate(halves, axis=1))

    scale = hd ** -0.5
    nk = NA_KROWS * GRID_W

    def window_start(blk):
        ws = jnp.clip(NA_QROWS * blk - NA_WIN_ROWS // 2, 0, rows - NA_KROWS)
        return pl.multiple_of(ws * GRID_W, GRID_W)

    def stages(write_loc, write_ctx, read_loc, read_ctx):
        prev = jnp.maximum(step - 1, 0)
        cur = jnp.minimum(step, n_rb - 1)
        typ = jnp.where(cur == 0, 0, jnp.where(cur == n_rb - 1, 2, 1))
        for hh in range(heads):
            cols = slice(hh * hd, (hh + 1) * hd)
            s_loc = read_loc[hh]
            s_ctx = read_ctx[hh]
            m = jnp.maximum(jnp.max(s_loc, axis=-1, keepdims=True), jnp.max(s_ctx, axis=-1, keepdims=True))
            p_loc = jnp.exp(s_loc - m)
            p_ctx = jnp.exp(s_ctx - m)
            den = jnp.sum(p_loc, axis=-1, keepdims=True) + jnp.sum(p_ctx, axis=-1, keepdims=True)
            vw = v_scr[hh, pl.ds(window_start(prev), nk), :]
            o = _dot(p_loc.astype(BF16), vw) + _dot(p_ctx.astype(BF16), vc_scr[hh])
            o_ref[:, cols] = (o / den).astype(BF16)
            qn = _rms(q_ref[:, cols], gain_ref[0:1, :]).astype(BF16)
            kw = kn_scr[hh, pl.ds(window_start(cur), nk), :]
            write_loc[hh] = _dot_nt(qn, kw) * scale + bias_scr[hh, pl.ds(typ, 1)][0]
            write_ctx[hh] = _dot_nt(qn, kcn_scr[hh]) * scale

    @pl.when(step % 2 == 0)
    def _():
        stages(sl0_scr, sc0_scr, sl1_scr, sc1_scr)

    @pl.when(step % 2 == 1)
    def _():
        stages(sl1_scr, sc1_scr, sl0_scr, sc0_scr)


def _na_row_plan(rows):
    n_rb = rows // NA_QROWS
    kh = NA_WIN_ROWS
    plan = []
    for rb in (0, 1, n_rb - 1):
        ws = min(max(NA_QROWS * rb - kh // 2, 0), rows - NA_KROWS)
        offs, inside = [], []
        for i in range(NA_QROWS):
            r = NA_QROWS * rb + i
            rs = min(max(r - kh // 2, 0), rows - kh)
            offs.append([min(max(ws + j - r, 1 - kh), kh - 1) + kh - 1 for j in range(NA_KROWS)])
            inside.append([rs <= ws + j < rs + kh for j in range(NA_KROWS)])
        plan.append((offs, inside))
    return plan


def _na_col_tables(rpb):
    kw = NA_WIN_COLS
    c = jnp.arange(GRID_W)
    cs = jnp.clip(c - kw // 2, 0, GRID_W - kw)
    in_col = (c[None, :] >= cs[:, None]) & (c[None, :] < cs[:, None] + kw)
    dc = jnp.clip(c[None, :] - c[:, None], 1 - kw, kw - 1) + kw - 1
    dc_hot = jax.nn.one_hot(dc, 2 * kw - 1, dtype=F32)
    val = jnp.einsum("lhab,ckb->lhack", rpb.astype(F32), dc_hot, precision=lax.Precision.HIGHEST)
    return jnp.where(in_col, val, NEG_INF)


def _na_call(p_lat, p_ctx, gain, colbias, layer, batch, seq, lc):
    rows = seq // GRID_W
    n_rb = rows // NA_QROWS
    tq = NA_QROWS * GRID_W
    nk = NA_KROWS * GRID_W
    heads = colbias.shape[1]
    n_off = colbias.shape[2]
    hd = HEAD_DIM
    hp = NA_HEADS_PER_STEP
    wide = hp * hd
    assert heads % hp == 0 and NA_Q % hp == 0 and NA_K % hp == 0 and NA_V % hp == 0
    return pl.pallas_call(
        functools.partial(_na_kernel, rows=rows),
        grid=(batch, heads // hp, n_rb + 1),
        in_specs=[pl.BlockSpec((tq, wide), lambda b, h, r: (b * n_rb + jnp.minimum(r, n_rb - 1), NA_Q // hp + h)),
                  pl.BlockSpec((seq, wide), lambda b, h, r: (b, NA_K // hp + h)),
                  pl.BlockSpec((seq, wide), lambda b, h, r: (b, NA_V // hp + h)),
                  pl.BlockSpec((lc, wide), lambda b, h, r: (b, NA_K // hp + h)),
                  pl.BlockSpec((lc, wide), lambda b, h, r: (b, NA_V // hp + h)),
                  pl.BlockSpec((2, hd), lambda b, h, r: (0, 0)),
                  pl.BlockSpec((1, hp, n_off, GRID_W, GRID_W), lambda b, h, r: (layer, h, 0, 0, 0))],
        out_specs=pl.BlockSpec((tq, wide), lambda b, h, r: (b * n_rb + jnp.maximum(r - 1, 0), h)),
        out_shape=jax.ShapeDtypeStruct((batch * seq, heads * hd), BF16),
        scratch_shapes=[pltpu.VMEM((hp, seq, hd), BF16), pltpu.VMEM((hp, seq, hd), BF16),
                        pltpu.VMEM((hp, lc, hd), BF16), pltpu.VMEM((hp, lc, hd), BF16),
                        pltpu.VMEM((hp, 3, tq, nk), F32),
                        pltpu.VMEM((hp, tq, nk), F32), pltpu.VMEM((hp, tq, lc), F32),
                        pltpu.VMEM((hp, tq, nk), F32), pltpu.VMEM((hp, tq, lc), F32)],
        compiler_params=_cparams(("arbitrary", "arbitrary", "arbitrary")),
        name="na_attn",
    )(p_lat, p_lat, p_lat, p_ctx, p_ctx, gain, colbias)


def _ctx_attn_kernel(sink_ref, q_ref, k_ref, v_ref, gain_ref, o_ref, *, use_sink):
    scale = HEAD_DIM ** -0.5
    qn = _rms(q_ref[...], gain_ref[0:1, :]).astype(BF16)
    kn = _rms(k_ref[...], gain_ref[1:2, :]).astype(BF16)
    s = _dot_nt(qn, kn) * scale
    m = jnp.max(s, axis=-1, keepdims=True)
    if use_sink:
        sk = sink_ref[pl.program_id(1)]
        m = jnp.maximum(m, sk)
    p = jnp.exp(s - m)
    den = jnp.sum(p, axis=-1, keepdims=True)
    if use_sink:
        den = den + jnp.exp(sk - m)
    o = _dot(p.astype(BF16), v_ref[...].astype(BF16))
    o_ref[...] = (o / den).astype(BF16)


def _ctx_attn_call(p_ctx, gain, sink, batch, lc, heads, rep, qoff, koff, voff, use_sink, name):
    hd = HEAD_DIM
    return pl.pallas_call(
        functools.partial(_ctx_attn_kernel, use_sink=use_sink),
        grid=(batch, heads),
        in_specs=[pl.BlockSpec(memory_space=pltpu.SMEM),
                  pl.BlockSpec((lc, hd), lambda b, h: (b, qoff + h)),
                  pl.BlockSpec((lc, hd), lambda b, h: (b, koff + h // rep)),
                  pl.BlockSpec((lc, hd), lambda b, h: (b, voff + h // rep)),
                  pl.BlockSpec((2, hd), lambda b, h: (0, 0))],
        out_specs=pl.BlockSpec((lc, hd), lambda b, h: (b, h)),
        out_shape=jax.ShapeDtypeStruct((batch * lc, heads * hd), BF16),
        compiler_params=_cparams(("arbitrary", "arbitrary")),
        name=name,
    )(sink, p_ctx, p_ctx, p_ctx, gain)


def _rope(x, cos, sin_signed):
    lane = _iota(x.shape, 1)
    swapped = jnp.where(lane % 64 < 32, pltpu.roll(x, 96, axis=1), pltpu.roll(x, 32, axis=1))
    return x * cos + swapped * sin_signed


def _wa_kernel(sink_ref, q_ref, k_ref, v_ref, kc_ref, vc_ref, gain_ref, cq_ref, sq_ref, ck_ref, sk_ref, o_ref,
               kpad_scr, vpad_scr, kcn_scr, vc_scr, sl0_scr, sc0_scr, sl1_scr, sc1_scr, *, seq):
    step = pl.program_id(1)
    nb = pl.num_programs(1) - 1
    wb = WA_BLOCK
    hd = HEAD_DIM
    groups = k_ref.shape[1] // hd

    @pl.when(step == 0)
    def _():
        sl1_scr[...] = jnp.zeros_like(sl1_scr)
        sc1_scr[...] = jnp.zeros_like(sc1_scr)
        g1 = gain_ref[1:2, :]
        zeros = jnp.zeros((wb, hd), BF16)
        for g in range(groups):
            cols = slice(g * hd, (g + 1) * hd)
            kpad_scr[g, 0:wb, :] = zeros
            kpad_scr[g, seq + wb:seq + 2 * wb, :] = zeros
            vpad_scr[g, 0:wb, :] = zeros
            vpad_scr[g, seq + wb:seq + 2 * wb, :] = zeros
            kpad_scr[g, wb:seq + wb, :] = _rope(_rms(k_ref[:, cols], g1), ck_ref[...], sk_ref[...]).astype(BF16)
            vpad_scr[g, wb:seq + wb, :] = v_ref[:, cols].astype(BF16)
            kcn_scr[g] = _rms(kc_ref[:, cols], g1).astype(BF16)
            vc_scr[g] = vc_ref[:, cols].astype(BF16)

    scale = hd ** -0.5

    def stages(write_loc, write_ctx, read_loc, read_ctx):
        prev = jnp.maximum(step - 1, 0)
        cur = jnp.minimum(step, nb - 1)
        g0 = gain_ref[0:1, :]
        cq = cq_ref[...]
        sq = sq_ref[...]
        shp = (2 * wb, 3 * wb)
        i = _iota(shp, 0) % wb
        j = _iota(shp, 1)
        kpos = (cur - 1) * wb + j
        ok = (j - i >= 0) & (j - i <= 2 * wb) & (kpos >= 0) & (kpos < seq)
        for g in range(groups):
            s_loc = read_loc[g]
            s_ctx = read_ctx[g]
            sink = jnp.where(_iota((2 * wb, 1), 0) < wb, sink_ref[2 * g], sink_ref[2 * g + 1])
            m = jnp.maximum(jnp.maximum(jnp.max(s_loc, axis=-1, keepdims=True),
                                        jnp.max(s_ctx, axis=-1, keepdims=True)), sink)
            p_loc = jnp.exp(s_loc - m)
            p_ctx = jnp.exp(s_ctx - m)
            den = jnp.sum(p_loc, axis=-1, keepdims=True) + jnp.sum(p_ctx, axis=-1, keepdims=True) + jnp.exp(sink - m)
            vw = vpad_scr[g, pl.ds(pl.multiple_of(prev * wb, wb), 3 * wb), :]
            o = (_dot(p_loc.astype(BF16), vw) + _dot(p_ctx.astype(BF16), vc_scr[g])) / den
            o_ref[:, 2 * g * hd:(2 * g + 1) * hd] = o[0:wb].astype(BF16)
            o_ref[:, (2 * g + 1) * hd:(2 * g + 2) * hd] = o[wb:2 * wb].astype(BF16)
            q2 = jnp.concatenate([_rope(_rms(q_ref[:, 2 * g * hd:(2 * g + 1) * hd], g0), cq, sq),
                                  _rope(_rms(q_ref[:, (2 * g + 1) * hd:(2 * g + 2) * hd], g0), cq, sq)],
                                 axis=0).astype(BF16)
            kw = kpad_scr[g, pl.ds(pl.multiple_of(cur * wb, wb), 3 * wb), :]
            write_loc[g] = jnp.where(ok, _dot_nt(q2, kw) * scale, NEG_INF)
            write_ctx[g] = _dot_nt(q2, kcn_scr[g]) * scale

    @pl.when(step % 2 == 0)
    def _():
        stages(sl0_scr, sc0_scr, sl1_scr, sc1_scr)

    @pl.when(step % 2 == 1)
    def _():
        stages(sl1_scr, sc1_scr, sl0_scr, sc0_scr)


def _wa_call(p_lat, p_ctx, gain, sink, cos_t, sin_t, batch, seq, lc):
    wb = WA_BLOCK
    hd = HEAD_DIM
    nb = seq // wb
    kvh = 2
    qw = 2 * kvh * hd
    kw_ = kvh * hd
    assert (WA_Q * hd) % qw == 0 and (WA_K * hd) % kw_ == 0 and (WA_V * hd) % kw_ == 0
    return pl.pallas_call(
        functools.partial(_wa_kernel, seq=seq),
        grid=(batch, nb + 1),
        in_specs=[pl.BlockSpec(memory_space=pltpu.SMEM),
                  pl.BlockSpec((wb, qw), lambda b, n: (b * nb + jnp.minimum(n, nb - 1), WA_Q * hd // qw)),
                  pl.BlockSpec((seq, kw_), lambda b, n: (b, WA_K * hd // kw_)),
                  pl.BlockSpec((seq, kw_), lambda b, n: (b, WA_V * hd // kw_)),
                  pl.BlockSpec((lc, kw_), lambda b, n: (b, WA_K * hd // kw_)),
                  pl.BlockSpec((lc, kw_), lambda b, n: (b, WA_V * hd // kw_)),
                  pl.BlockSpec((2, hd), lambda b, n: (0, 0)),
                  pl.BlockSpec((wb, hd), lambda b, n: (jnp.minimum(n, nb - 1), 0)),
                  pl.BlockSpec((wb, hd), lambda b, n: (jnp.minimum(n, nb - 1), 0)),
                  pl.BlockSpec((seq, hd), lambda b, n: (0, 0)),
                  pl.BlockSpec((seq, hd), lambda b, n: (0, 0))],
        out_specs=pl.BlockSpec((wb, qw), lambda b, n: (b * nb + jnp.maximum(n - 1, 0), 0)),
        out_shape=jax.ShapeDtypeStruct((batch * seq, qw), BF16),
        scratch_shapes=[pltpu.VMEM((kvh, seq + 2 * wb, hd), BF16), pltpu.VMEM((kvh, seq + 2 * wb, hd), BF16),
                        pltpu.VMEM((kvh, lc, hd), BF16), pltpu.VMEM((kvh, lc, hd), BF16),
                        pltpu.VMEM((kvh, 2 * wb, 3 * wb), F32), pltpu.VMEM((kvh, 2 * wb, lc), F32),
                        pltpu.VMEM((kvh, 2 * wb, 3 * wb), F32), pltpu.VMEM((kvh, 2 * wb, lc), F32)],
        compiler_params=_cparams(("arbitrary", "arbitrary")),
        name="wa_attn",
    )(sink, p_lat, p_lat, p_lat, p_ctx, p_ctx, gain, cos_t, sin_t, cos_t, sin_t)


def _dn_prep_kernel(x_ref, prev_ref, next_ref, w_ref, o_ref, *, n_tiles):
    t = pl.program_id(1)
    tl = x_ref.shape[0]
    x = x_ref[...]
    prev = jnp.where(t > 0, prev_ref[...], 0.0)
    nxt = jnp.where(t < n_tiles - 1, next_ref[...], 0.0)
    xe = jnp.concatenate([prev, x, nxt], axis=0)
    n = tl + 16
    taps = w_ref.shape[0]
    acc = jnp.zeros_like(x)
    for j in range(taps):
        off = 8 - taps // 2 + j
        shifted = xe if off % n == 0 else pltpu.roll(xe, n - off, axis=0)
        acc += shifted[0:tl] * w_ref[j:j + 1, :]
    y = _silu(acc)
    hd = HEAD_DIM
    nh = y.shape[1] // (3 * hd)
    for blk in range(3 * nh):
        yb = y[:, blk * hd:(blk + 1) * hd]
        if blk < 2 * nh:
            yb = yb * lax.rsqrt(jnp.sum(yb * yb, axis=-1, keepdims=True) + EPS)
            if blk < nh:
                yb = yb * hd ** -0.5
        o_ref[:, blk * hd:(blk + 1) * hd] = yb


def _dn_prep_call(p, conv_w, batch, length, tl):
    width = conv_w.shape[1]
    n_tiles = length // tl
    per8 = tl // 8
    last8 = batch * length // 8 - 1
    return pl.pallas_call(
        functools.partial(_dn_prep_kernel, n_tiles=n_tiles),
        grid=(batch, n_tiles),
        in_specs=[pl.BlockSpec((tl, width), lambda b, t: (b * n_tiles + t, 1)),
                  pl.BlockSpec((8, width), lambda b, t: (jnp.maximum((b * n_tiles + t) * per8 - 1, 0), 1)),
                  pl.BlockSpec((8, width), lambda b, t: (jnp.minimum((b * n_tiles + t + 1) * per8, last8), 1)),
                  pl.BlockSpec((conv_w.shape[0], width), lambda b, t: (0, 0))],
        out_specs=pl.BlockSpec((tl, width), lambda b, t: (b * n_tiles + t, 0)),
        out_shape=jax.ShapeDtypeStruct((batch * length, width), F32),
        compiler_params=_cparams(("arbitrary", "arbitrary")),
        name="dn_prep",
    )(p, p, p, conv_w)


def _split_bf16(a):
    hi = a.astype(BF16)
    return hi, (a - hi.astype(F32)).astype(BF16)


def _solve_dot(a, b, mode):
    if mode == "highest":
        return _dot(a, b, lax.Precision.HIGHEST)
    if mode == "bf16":
        return _dot(a.astype(BF16), b.astype(BF16))
    a_hi, a_lo = _split_bf16(a)
    b_hi, b_lo = _split_bf16(b)
    return _dot(a_hi, b_hi) + (_dot(a_hi, b_lo) + _dot(a_lo, b_hi))


def _dn_pass(q_ref, k_ref, v_ref, og_ref, gates_ref, gadd, gmul, norm_g, y_ref, length, h, state,
             lin_scr, add_scr, egl_scr, acc_scr, solve_precision):
    assert CHUNK == HEAD_DIM
    n_chunks = length // CHUNK

    acc_scr[0:length, :] = jnp.zeros((length, HEAD_DIM), F32)

    shp = (CHUNK, CHUNK)
    ri = _iota(shp, 0)
    ci = _iota(shp, 1)
    masks = ((ci <= ri, ci < ri), (ci >= ri, ci > ri))
    couple = ([], [])
    s = 1
    while s < CHUNK:
        same = (ri // (2 * s)) == (ci // (2 * s))
        couple[0].append(same & ((ri // s) % 2 == 1) & ((ci // s) % 2 == 0))
        couple[1].append(same & ((ri // s) % 2 == 0) & ((ci // s) % 2 == 1))
        s *= 2
    group = min(n_chunks, PREP_GROUP)

    def prep(gi, carry):
        chains = []
        for g in range(group):
            c = gi * group + g
            r0 = pl.multiple_of(c * CHUNK, CHUNK)
            q = q_ref[pl.ds(r0, CHUNK), :]
            k = k_ref[pl.ds(r0, CHUNK), :]
            raw = gates_ref[pl.ds(r0, CHUNK), :]
            gb = jnp.where(_iota(raw.shape, 1) < LANE_ALPHA, jax.nn.sigmoid(raw), -gmul * _softplus(raw + gadd))
            k16 = k.astype(BF16)
            kk = _dot_nt(k16, k16)
            qk = _dot_nt(q.astype(BF16), k16)
            for d in range(2):
                incl, strict = masks[d]
                beta_col = _lane_pick(gb, LANE_BETA + 4 * d + h)
                g_col = _lane_pick(gb, LANE_ALPHA + 4 * d + h)
                beta_row = _to_row(beta_col)
                g_row = _to_row(g_col)
                cum_col = jnp.sum(jnp.where(incl, g_row, 0.0), axis=-1, keepdims=True)
                cum_row = _to_row(cum_col)
                tot = jnp.sum(g_col, axis=0, keepdims=True)
                decay = jnp.exp(jnp.where(incl, cum_col - cum_row, NEG_INF))
                lm = jnp.where(strict, kk * beta_col * decay, 0.0)
                egl_scr[d, pl.ds(pl.multiple_of(c * 8, 8), 8), :] = jnp.broadcast_to(jnp.exp(tot), (8, HEAD_DIM))
                chains.append((d, c, r0, lm, beta_row, beta_row * jnp.exp(cum_row), (qk * decay).astype(BF16),
                               q * jnp.exp(cum_col), k * jnp.exp(tot - cum_col)))
        eye = (ri == ci).astype(F32)
        xs = [eye - jnp.where(couple[ch[0]][0], ch[3], 0.0) for ch in chains]
        for lvl in range(1, len(couple[0])):
            cs = [jnp.where(couple[ch[0]][lvl], ch[3], 0.0) for ch in chains]
            xc = [_solve_dot(x, c, solve_precision) for x, c in zip(xs, cs)]
            xs = [x - _solve_dot(t, x, solve_precision) for x, t in zip(xs, xc)]
        u16 = [_solve_dot(ainv * ch[4], v_ref[pl.ds(ch[2], CHUNK), :], solve_precision).astype(BF16)
               for ch, ainv in zip(chains, xs)]
        w16 = [_solve_dot(ainv * ch[5], k_ref[pl.ds(ch[2], CHUNK), :], solve_precision).astype(BF16)
               for ch, ainv in zip(chains, xs)]
        ket = [ch[8].T.astype(BF16) for ch in chains]
        s_lin = [_dot(kt, w) for kt, w in zip(ket, w16)]
        s_add = [_dot(kt, u) for kt, u in zip(ket, u16)]
        o_lin = [ch[7] - _dot(ch[6], w) for ch, w in zip(chains, w16)]
        o_add = [_dot(ch[6], u) for ch, u in zip(chains, u16)]
        for ch, sl, sa, ol, oa in zip(chains, s_lin, s_add, o_lin, o_add):
            d = ch[0]
            base = pl.multiple_of(ch[1] * 2 * CHUNK, 2 * CHUNK)
            lin_scr[d, pl.ds(base, CHUNK), :] = sl.astype(BF16)
            lin_scr[d, pl.ds(base + CHUNK, CHUNK), :] = ol.astype(BF16)
            add_scr[d, pl.ds(base, CHUNK), :] = sa
            add_scr[d, pl.ds(base + CHUNK, CHUNK), :] = oa
        return carry

    lax.fori_loop(0, n_chunks // group, prep, 0)

    def scan(s, st):
        new = []
        for d in range(2):
            c = s if d == 0 else n_chunks - 1 - s
            base = pl.multiple_of(c * 2 * CHUNK, 2 * CHUNK)
            sm = st[d]
            prod = _dot(lin_scr[d, pl.ds(base, 2 * CHUNK), :], sm.astype(BF16))
            add = add_scr[d, pl.ds(base, 2 * CHUNK), :]
            acc_scr[pl.ds(pl.multiple_of(c * CHUNK, CHUNK), CHUNK), :] += prod[CHUNK:] + add[CHUNK:]
            egl = egl_scr[d, pl.ds(pl.multiple_of(c * 8, 8), 8), :][0:1, :]
            new.append(sm * egl - prod[:CHUNK] + add[:CHUNK])
        return tuple(new)

    state = lax.fori_loop(0, n_chunks, scan, state, unroll=SCAN_UNROLL)

    y = _rms(acc_scr[0:length, :], norm_g) * _silu(og_ref[...])
    y_ref[...] = y.astype(BF16)
    return state


def _dn_kernel(q_ref, k_ref, v_ref, og_ref, gates_ref, qc_ref, kc_ref, vc_ref, ogc_ref, gatesc_ref,
               gadd_ref, gmul_ref, norm_ref, y_ref, yc_ref,
               lin_scr, add_scr, egl_scr, acc_scr, *, seq, lc, solve_precision):
    h = pl.program_id(1)
    gadd = gadd_ref[...]
    gmul = gmul_ref[...]
    norm_g = norm_ref[...]
    scr = (lin_scr, add_scr, egl_scr, acc_scr)
    zero = jnp.zeros((HEAD_DIM, HEAD_DIM), F32)
    state = _dn_pass(qc_ref, kc_ref, vc_ref, ogc_ref, gatesc_ref, gadd, gmul, norm_g, yc_ref, lc, h, (zero, zero),
                     *scr, solve_precision)
    _dn_pass(q_ref, k_ref, v_ref, og_ref, gates_ref, gadd, gmul, norm_g, y_ref, seq, h, state,
             *scr, solve_precision)


def _dn_call(dq_lat, dq_ctx, p_lat, p_ctx, pg_lat, pg_ctx, gadd, gmul, norm_g, batch, seq, lc, solve_precision):
    hd = HEAD_DIM
    heads = 4

    def blk(n, col):
        return pl.BlockSpec((n, hd), lambda b, h: (b, col + h))

    def gate_blk(n):
        return pl.BlockSpec((n, hd), lambda b, h: (b, 0))

    vec = pl.BlockSpec((1, hd), lambda b, h: (0, 0))
    return pl.pallas_call(
        functools.partial(_dn_kernel, seq=seq, lc=lc, solve_precision=solve_precision),
        grid=(batch, heads),
        in_specs=[blk(seq, 0), blk(seq, heads), blk(seq, 2 * heads), blk(seq, DN_OG), gate_blk(seq),
                  blk(lc, 0), blk(lc, heads), blk(lc, 2 * heads), blk(lc, DN_OG), gate_blk(lc),
                  vec, vec, vec],
        out_specs=[pl.BlockSpec((seq, hd), lambda b, h: (b, h)),
                   pl.BlockSpec((lc, hd), lambda b, h: (b, h))],
        out_shape=[jax.ShapeDtypeStruct((batch * seq, heads * hd), BF16),
                   jax.ShapeDtypeStruct((batch * lc, heads * hd), BF16)],
        scratch_shapes=[pltpu.VMEM((2, 2 * seq, hd), BF16),
                        pltpu.VMEM((2, 2 * seq, hd), F32),
                        pltpu.VMEM((2, seq // CHUNK * 8, hd), F32),
                        pltpu.VMEM((seq, hd), F32)],
        compiler_params=_cparams(("arbitrary", "arbitrary"), DN_VMEM_LIMIT),
        name="deltanet",
    )(dq_lat, dq_lat, dq_lat, p_lat, pg_lat, dq_ctx, dq_ctx, dq_ctx, p_ctx, pg_ctx, gadd, gmul, norm_g)


def _ml_pass(q_ref, k_ref, v_ref, og_ref, gates_ref, gadd, norm_g, y_ref, length, h, state, acc_scr):
    n_chunks = length // CHUNK
    acc_scr[0:length, :] = jnp.zeros((length, HEAD_DIM), F32)
    shp = (CHUNK, CHUNK)
    ri = _iota(shp, 0)
    ci = _iota(shp, 1)
    masks = (ci <= ri, ci >= ri)
    head_lanes = (_iota((1, HEAD_DIM), 1) // ML_QK) == (h % 2)

    group = min(n_chunks, ML_GROUP)

    def each(f, *lists):
        return [f(*a) for a in zip(*lists)]

    def step(gi, st):
        dirs, r0s = [], []
        for j in range(group):
            s = gi * group + j
            for d in range(2):
                c = s if d == 0 else n_chunks - 1 - s
                dirs.append(d)
                r0s.append(pl.multiple_of(c * CHUNK, CHUNK))
        incl = [masks[d] for d in dirs]
        q = [jnp.where(head_lanes, q_ref[pl.ds(r0, CHUNK), :], 0.0) for r0 in r0s]
        k = [jnp.where(head_lanes, k_ref[pl.ds(r0, CHUNK), :], 0.0) * ML_QK ** -0.5 for r0 in r0s]
        v16 = [v_ref[pl.ds(r0, CHUNK), :].astype(BF16) for r0 in r0s]
        gb = [gates_ref[pl.ds(r0, CHUNK), :] + gadd for r0 in r0s]
        ig_col = [_lane_pick(g, LANE_IG + 4 * d + h) for g, d in zip(gb, dirs)]
        lf_col = [_log_sigmoid(_lane_pick(g, LANE_FG + 4 * d + h)) for g, d in zip(gb, dirs)]
        ig_row = each(_to_row, ig_col)
        lf_row = each(_to_row, lf_col)
        b_col = each(lambda m, r: jnp.sum(jnp.where(m, r, 0.0), axis=-1, keepdims=True), incl, lf_row)
        b_row = each(_to_row, b_col)
        b_last = each(lambda x: jnp.sum(x, axis=0, keepdims=True), lf_col)
        g_end = each(lambda bl, b, i: bl - b + i, b_last, b_col, ig_col)
        m_loc = each(lambda x: jnp.max(x, axis=0, keepdims=True), g_end)
        kw = each(lambda kk, g, m: kk * jnp.exp(g - m), k, g_end, m_loc)
        q16 = each(lambda x: x.astype(BF16), q)
        k16 = each(lambda x: x.astype(BF16), k)
        c_loc = each(lambda a, b: _dot_tn(a.astype(BF16), b), kw, v16)
        n_loc = each(lambda x: jnp.sum(x, axis=0, keepdims=True), kw)
        log_d = each(lambda m, bc, br, ir: jnp.where(m, bc - br + ir, NEG_INF), incl, b_col, b_row, ig_row)
        mx_col = each(lambda x: jnp.max(x, axis=-1, keepdims=True), log_d)
        s0 = each(lambda a, b, ld, mx: _dot_nt(a, b) * jnp.exp(ld - mx), q16, k16, log_d, mx_col)
        sv0 = each(lambda s, vv: _dot(s.astype(BF16), vv), s0, v16)
        rs0 = each(lambda s: jnp.sum(s, axis=-1, keepdims=True), s0)
        st = list(st)
        for i, (d, r0) in enumerate(zip(dirs, r0s)):
            c_st, n_st, m_st = st[d]
            m_inter = b_col[i] + m_st
            m_t = jnp.maximum(mx_col[i], m_inter)
            intra = jnp.exp(mx_col[i] - m_t)
            inter = jnp.exp(m_inter - m_t)
            num = intra * sv0[i] + inter * _dot(q16[i], c_st.astype(BF16))
            den = intra * rs0[i] + inter * jnp.sum(q[i] * n_st, axis=-1, keepdims=True)
            acc_scr[pl.ds(r0, CHUNK), :] += num / jnp.maximum(jnp.abs(den), jnp.exp(-m_t))
            m_new = jnp.maximum(b_last[i] + m_st, m_loc[i])
            a = jnp.exp(b_last[i] + m_st - m_new)
            e = jnp.exp(m_loc[i] - m_new)
            st[d] = (a * c_st + e * c_loc[i], a * n_st + e * n_loc[i], m_new)
        return tuple(st)

    state = lax.fori_loop(0, n_chunks // group, step, state)
    y = jax.nn.sigmoid(og_ref[...]) * _rms(acc_scr[0:length, :], norm_g)
    y_ref[...] = y.astype(BF16)
    return state


def _ml_kernel(q_ref, k_ref, v_ref, og_ref, gates_ref, qc_ref, kc_ref, vc_ref, ogc_ref, gatesc_ref,
               gadd_ref, norm_ref, y_ref, yc_ref, acc_scr, *, seq, lc):
    h = pl.program_id(1)
    gadd = gadd_ref[...]
    norm_g = norm_ref[0]
    zero = (jnp.zeros((HEAD_DIM, HEAD_DIM), F32), jnp.zeros((1, HEAD_DIM), F32), jnp.zeros((1, 1), F32))
    state = _ml_pass(qc_ref, kc_ref, vc_ref, ogc_ref, gatesc_ref, gadd, norm_g, yc_ref, lc, h, (zero, zero), acc_scr)
    _ml_pass(q_ref, k_ref, v_ref, og_ref, gates_ref, gadd, norm_g, y_ref, seq, h, state, acc_scr)


def _ml_call(p_lat, p_ctx, pg_lat, pg_ctx, gadd, norm_g, batch, seq, lc):
    hd = HEAD_DIM
    heads = 4

    def specs(n):
        return [pl.BlockSpec((n, hd), lambda b, h: (b, ML_Q + h // 2)),
                pl.BlockSpec((n, hd), lambda b, h: (b, ML_K + h // 2)),
                pl.BlockSpec((n, hd), lambda b, h: (b, ML_V + h)),
                pl.BlockSpec((n, hd), lambda b, h: (b, ML_OG + h)),
                pl.BlockSpec((n, hd), lambda b, h: (b, 0))]

    return pl.pallas_call(
        functools.partial(_ml_kernel, seq=seq, lc=lc),
        grid=(batch, heads),
        in_specs=specs(seq) + specs(lc) + [pl.BlockSpec((1, hd), lambda b, h: (0, 0)),
                                            pl.BlockSpec((1, 1, hd), lambda b, h: (h, 0, 0))],
        out_specs=[pl.BlockSpec((seq, hd), lambda b, h: (b, h)),
                   pl.BlockSpec((lc, hd), lambda b, h: (b, h))],
        out_shape=[jax.ShapeDtypeStruct((batch * seq, heads * hd), BF16),
                   jax.ShapeDtypeStruct((batch * lc, heads * hd), BF16)],
        scratch_shapes=[pltpu.VMEM((seq, hd), F32)],
        compiler_params=_cparams(("arbitrary", "arbitrary")),
        name="mlstm",
    )(p_lat, p_lat, p_lat, p_lat, pg_lat, p_ctx, p_ctx, p_ctx, p_ctx, pg_ctx, gadd, norm_g.reshape(heads, 1, hd))


def _regroup_kernel(wt_ref, main_ref, gate_ref, *, cuts):
    (a0, a1), (b0, b1), (c0, c1), (g0, g1), (h0, h1) = cuts
    main_ref[0, 0:a1 - a0, :] = wt_ref[0, a0:a1, :].astype(BF16)
    main_ref[0, a1 - a0:a1 - a0 + b1 - b0, :] = wt_ref[0, b0:b1, :].astype(BF16)
    main_ref[0, a1 - a0 + b1 - b0:, :] = wt_ref[0, c0:c1, :].astype(BF16)
    ng = (g1 - g0) + (h1 - h0)
    gate_ref[0, 0:g1 - g0, :] = wt_ref[0, g0:g1, :].astype(BF16)
    gate_ref[0, g1 - g0:ng, :] = wt_ref[0, h0:h1, :].astype(BF16)
    gate_ref[0, ng:, :] = jnp.zeros((gate_ref.shape[1] - ng, gate_ref.shape[2]), BF16)


def _regroup_call(w_in):
    depth, d, n = w_in.shape
    gw = 512
    dn_gate0 = 3 * gw + 4 * gw
    ml0 = dn_gate0 + 16
    ml_gate0 = ml0 + 3 * gw
    wa0 = ml_gate0 + 16
    cuts = ((0, dn_gate0), (ml0, ml_gate0), (wa0, n), (dn_gate0, ml0), (ml_gate0, wa0))
    tc = 256
    return pl.pallas_call(
        functools.partial(_regroup_kernel, cuts=cuts),
        grid=(depth, d // tc),
        in_specs=[pl.BlockSpec((1, n, tc), lambda l, i: (l, 0, i))],
        out_specs=[pl.BlockSpec((1, MAIN_COLS, tc), lambda l, i: (l, 0, i)),
                   pl.BlockSpec((1, 128, tc), lambda l, i: (l, 0, i))],
        out_shape=[jax.ShapeDtypeStruct((depth, MAIN_COLS, d), BF16), jax.ShapeDtypeStruct((depth, 128, d), BF16)],
        compiler_params=_cparams(("arbitrary", "arbitrary")),
        name="regroup_w_in",
    )(jnp.swapaxes(w_in, 1, 2))


def _rope_tables(seq):
    t = jnp.arange(seq)
    n_freq = HEAD_DIM // 4
    inv_freq = ROPE_THETA ** (-jnp.arange(n_freq, dtype=F32) / n_freq)
    pos = jnp.stack([t // GRID_W, t % GRID_W], axis=-1).astype(F32)
    ang = pos[:, :, None] * inv_freq
    cos, sin = jnp.cos(ang), jnp.sin(ang)
    cos_t = jnp.concatenate([cos, cos], axis=-1).reshape(seq, HEAD_DIM)
    sin_t = jnp.concatenate([-sin, sin], axis=-1).reshape(seq, HEAD_DIM)
    return cos_t, sin_t


def _gate_vectors(dt_bias, a_log, i_bias, f_bias):
    zeros8 = jnp.zeros((8,), F32)
    gadd = jnp.concatenate([zeros8, dt_bias.reshape(-1), i_bias.reshape(-1), f_bias.reshape(-1),
                            jnp.zeros((96,), F32)]).astype(F32)
    gmul = jnp.concatenate([zeros8, jnp.exp(a_log.astype(F32)).reshape(-1), jnp.zeros((112,), F32)])
    return gadd.reshape(1, 128), gmul.reshape(1, 128)


def _forward(x, c, ctx, c_ctx, w_ada, b_ada, norm_mix, norm_ffn, w_in, w_out, na_qk_gain, na_rpb, dn_conv,
             dn_a_log, dn_dt_bias, dn_norm, ml_i_bias, ml_f_bias, ml_norm, wa_qk_gain, wa_sink, w_ffn_in,
             w_ffn_out, solve_precision):
    batch, seq, d = x.shape
    lc = ctx.shape[1]
    depth = w_ada.shape[0]
    tm = 512
    tn_ffn = 512

    cos_t, sin_t = _rope_tables(seq)
    na_bias = _na_col_tables(na_rpb)
    cs = jnp.concatenate([c, c_ctx[None, :], jnp.zeros((8 - batch - 1, d), F32)], axis=0)
    mods = _ada_call(cs, w_ada, b_ada).reshape(depth, 8, 6, d)
    mods = jnp.pad(mods, ((0, 0), (0, 0), (0, 2), (0, 0)))

    w_main, w_gate = _regroup_call(w_in)
    w_o = w_out.astype(BF16)
    w_f1 = w_ffn_in.astype(BF16)
    w_f2 = w_ffn_out.astype(BF16)

    xl = x.reshape(batch * seq, d)
    xc = ctx.reshape(batch * lc, d)
    hl = _norm_call(xl, mods[0, :batch], norm_mix[0:1], seq, tm)
    hc = _norm_call(xc, mods[0, batch:batch + 1], norm_mix[0:1], batch * lc, tm)

    for l in range(depth):
        need_ctx = l < depth - 1
        mod_l, mod_c = mods[l, :batch], mods[l, batch:batch + 1]
        p_lat = _matmul_call(hl, w_main, l, 2 * tm, 2048, "inproj")
        pg_lat = _matmul_call(hl, w_gate, l, 2 * tm, 128, "inproj_gates")
        p_ctx = _matmul_call(hc, w_main, l, tm, 2048, "inproj_ctx")
        pg_ctx = _matmul_call(hc, w_gate, l, tm, 128, "inproj_gates_ctx")

        ya = _na_call(p_lat, p_ctx, na_qk_gain[l], na_bias, l, batch, seq, lc)
        gadd, gmul = _gate_vectors(dn_dt_bias[l], dn_a_log[l], ml_i_bias[l], ml_f_bias[l])
        dq_lat = _dn_prep_call(p_lat, dn_conv[l], batch, seq, 512)
        dq_ctx = _dn_prep_call(p_ctx, dn_conv[l], batch, lc, 256)
        yb, ycb = _dn_call(dq_lat, dq_ctx, p_lat, p_ctx, pg_lat, pg_ctx, gadd, gmul, dn_norm[l].reshape(1, -1),
                           batch, seq, lc, solve_precision)
        ym, ycm = _ml_call(p_lat, p_ctx, pg_lat, pg_ctx, gadd, ml_norm[l], batch, seq, lc)
        yw = _wa_call(p_lat, p_ctx, wa_qk_gain[l], wa_sink[l], cos_t, sin_t, batch, seq, lc)

        nl = min(l + 1, depth - 1)
        xl, h2 = _outproj_call((ya, yb, ym, yw), w_o, l, xl, mod_l, norm_ffn[l:l + 1], seq, tm)
        xl, hl = _ffn_call(h2, w_f1, w_f2, l, xl, mod_l, mods[nl, :batch], norm_mix[nl:nl + 1], seq, 2 * tm, tn_ffn,
                           need_ctx)
        if need_ctx:
            yca = _ctx_attn_call(p_ctx, na_qk_gain[l], wa_sink[l], batch, lc, 4, 1, NA_Q, NA_K, NA_V, False,
                                 "na_ctx_attn")
            ycw = _ctx_attn_call(p_ctx, wa_qk_gain[l], wa_sink[l], batch, lc, 4, 2, WA_Q, WA_K, WA_V, True,
                                 "wa_ctx_attn")
            xc, h2c = _outproj_call((yca, ycb, ycm, ycw), w_o, l, xc, mod_c, norm_ffn[l:l + 1], batch * lc, tm)
            xc, hc = _ffn_call(h2c, w_f1, w_f2, l, xc, mod_c, mods[nl, batch:batch + 1], norm_mix[nl:nl + 1],
                               batch * lc, tm, tn_ffn, True)
    return xl.reshape(batch, seq, d)


def kernel(x, c, ctx, c_ctx, w_ada, b_ada, norm_mix, norm_ffn, w_in, w_out, na_qk_gain, na_rpb, dn_conv, dn_a_log,
           dn_dt_bias, dn_norm, ml_i_bias, ml_f_bias, ml_norm, wa_qk_gain, wa_sink, w_ffn_in, w_ffn_out):
    return _forward(x, c, ctx, c_ctx, w_ada, b_ada, norm_mix, norm_ffn, w_in, w_out, na_qk_gain, na_rpb, dn_conv,
                    dn_a_log, dn_dt_bias, dn_norm, ml_i_bias, ml_f_bias, ml_norm, wa_qk_gain, wa_sink, w_ffn_in,
                    w_ffn_out, "bf16")
```

```python
import functools

import jax
import jax.numpy as jnp
from jax import lax
from jax.experimental import pallas as pl
from jax.experimental.pallas import tpu as pltpu

F32 = jnp.float32
BF16 = jnp.bfloat16

HEAD_DIM = 128
GRID_W = 64
CHUNK = 128
FFN_CAST_SLABS = 44
PREP_GROUP = 8
ML_GROUP = 4
SCAN_UNROLL = 2
NA_WIN_ROWS = 8
NA_WIN_COLS = 16
NA_QROWS = 4
NA_KROWS = NA_QROWS + NA_WIN_ROWS
NA_HEADS_PER_STEP = 2
WA_BLOCK = 128
ML_QK = 64
ROPE_THETA = 10000.0
EPS = 1e-6
NEG_INF = float("-inf")
VMEM_LIMIT = 48 * 1024 * 1024
DN_VMEM_LIMIT = 56 * 1024 * 1024

NA_Q, NA_K, NA_V = 0, 4, 8
DN_OG = 24
ML_Q, ML_K, ML_V, ML_OG = 28, 30, 32, 36
WA_Q, WA_K, WA_V = 40, 44, 46
MAIN_COLS = 48 * 128
LANE_BETA, LANE_ALPHA, LANE_IG, LANE_FG = 0, 8, 16, 24


def _cparams(sem, vmem_limit=VMEM_LIMIT):
    return pltpu.CompilerParams(dimension_semantics=sem, vmem_limit_bytes=vmem_limit)


def _rms(x, gain):
    return x * lax.rsqrt(jnp.mean(x * x, axis=-1, keepdims=True) + EPS) * gain


def _silu(x):
    return x * jax.nn.sigmoid(x)


def _softplus(x):
    return jnp.maximum(x, 0.0) + jnp.log1p(jnp.exp(-jnp.abs(x)))


def _log_sigmoid(x):
    return jnp.minimum(x, 0.0) - jnp.log1p(jnp.exp(-jnp.abs(x)))


def _dot(a, b, precision=None):
    return jnp.dot(a, b, preferred_element_type=F32, precision=precision)


def _dot_nt(a, b):
    return lax.dot_general(a, b, (((1,), (1,)), ((), ())), preferred_element_type=F32)


def _dot_tn(a, b):
    return lax.dot_general(a, b, (((0,), (0,)), ((), ())), preferred_element_type=F32)


def _iota(shape, dim):
    return lax.broadcasted_iota(jnp.int32, shape, dim)


def _lane_pick(x, lane):
    return jnp.sum(jnp.where(_iota(x.shape, 1) == lane, x, 0.0), axis=-1, keepdims=True)


def _to_row(col):
    n = col.shape[0]
    eye = _iota((n, n), 0) == _iota((n, n), 1)
    return jnp.sum(jnp.where(eye, col, 0.0), axis=0, keepdims=True)


def _ada_kernel(c_ref, w_ref, b_ref, o_ref):
    a = _silu(c_ref[...]).astype(BF16)
    o_ref[0] = _dot(a, w_ref[0].astype(BF16)) + b_ref[0]


def _ada_call(cs, w_ada, b_ada):
    depth, d, n = w_ada.shape
    tn = 1024
    return pl.pallas_call(
        _ada_kernel,
        grid=(depth, n // tn),
        in_specs=[pl.BlockSpec((8, d), lambda l, j: (0, 0)),
                  pl.BlockSpec((1, d, tn), lambda l, j: (l, 0, j)),
                  pl.BlockSpec((1, 1, tn), lambda l, j: (l, 0, j))],
        out_specs=pl.BlockSpec((1, 8, tn), lambda l, j: (l, 0, j)),
        out_shape=jax.ShapeDtypeStruct((depth, 8, n), F32),
        compiler_params=_cparams(("arbitrary", "arbitrary")),
        name="ada",
    )(cs, w_ada, b_ada.reshape(depth, 1, n))


def _norm_kernel(x_ref, mod_ref, g_ref, h_ref):
    mod = mod_ref[0]
    h = _rms(x_ref[...], g_ref[...])
    h_ref[...] = (h * (1.0 + mod[1:2]) + mod[0:1]).astype(BF16)


def _norm_call(x, mod, gain, rows_per_mod, tm):
    m, d = x.shape
    per = rows_per_mod // tm
    return pl.pallas_call(
        _norm_kernel,
        grid=(m // tm,),
        in_specs=[pl.BlockSpec((tm, d), lambda i: (i, 0)),
                  pl.BlockSpec((1, 8, d), lambda i: (i // per, 0, 0)),
                  pl.BlockSpec((1, d), lambda i: (0, 0))],
        out_specs=pl.BlockSpec((tm, d), lambda i: (i, 0)),
        out_shape=jax.ShapeDtypeStruct((m, d), BF16),
        compiler_params=_cparams(("arbitrary",)),
        name="norm_mod",
    )(x, mod, gain)


def _matmul_kernel(a_ref, wt_ref, o_ref):
    o_ref[...] = _dot_nt(a_ref[...], wt_ref[0])


def _matmul_call(a, wt, layer, tm, tn, name):
    m, k = a.shape
    n = wt.shape[1]
    return pl.pallas_call(
        _matmul_kernel,
        grid=(n // tn, m // tm),
        in_specs=[pl.BlockSpec((tm, k), lambda j, i: (i, 0)),
                  pl.BlockSpec((1, tn, k), lambda j, i: (layer, j, 0))],
        out_specs=pl.BlockSpec((tm, tn), lambda j, i: (i, j)),
        out_shape=jax.ShapeDtypeStruct((m, n), F32),
        compiler_params=_cparams(("arbitrary", "arbitrary")),
        name=name,
    )(a, wt)


def _inproj_cast_kernel(a_ref, wt_ref, f1_ref, f2_ref, o_ref, f1o_ref, f2o_ref):
    o_ref[...] = _dot_nt(a_ref[...], wt_ref[0])
    f1o_ref[...] = f1_ref[0].astype(BF16)
    f2o_ref[...] = f2_ref[0].astype(BF16)


def _inproj_cast_call(a, wt, layer, w_ffn_in, w_ffn_out, tm, tn):
    m, k = a.shape
    n = wt.shape[1]
    _, d, two_hidden = w_ffn_in.shape
    hidden = w_ffn_out.shape[1]
    slabs = FFN_CAST_SLABS
    n_i = m // tm
    assert (n // tn) * n_i >= slabs and two_hidden % (slabs * 128) == 0 and hidden % (slabs * 8) == 0
    c1 = two_hidden // slabs
    r2 = hidden // slabs

    def slab(j, i):
        return jnp.minimum(j * n_i + i, slabs - 1)

    return pl.pallas_call(
        _inproj_cast_kernel,
        grid=(n // tn, n_i),
        in_specs=[pl.BlockSpec((tm, k), lambda j, i: (i, 0)),
                  pl.BlockSpec((1, tn, k), lambda j, i: (layer, j, 0)),
                  pl.BlockSpec((1, d, c1), lambda j, i: (layer, 0, slab(j, i))),
                  pl.BlockSpec((1, r2, d), lambda j, i: (layer, slab(j, i), 0))],
        out_specs=[pl.BlockSpec((tm, tn), lambda j, i: (i, j)),
                   pl.BlockSpec((d, c1), lambda j, i: (0, slab(j, i))),
                   pl.BlockSpec((r2, d), lambda j, i: (slab(j, i), 0))],
        out_shape=[jax.ShapeDtypeStruct((m, n), F32),
                   jax.ShapeDtypeStruct((d, two_hidden), BF16),
                   jax.ShapeDtypeStruct((hidden, d), BF16)],
        compiler_params=_cparams(("arbitrary", "arbitrary")),
        name="inproj",
    )(a, wt, w_ffn_in, w_ffn_out)


def _outproj_kernel(ya_ref, yb_ref, ym_ref, yw_ref, w_ref, x_ref, mod_ref, g_ref, xo_ref, h_ref):
    gw = ya_ref.shape[1]
    acc = _dot(ya_ref[...], w_ref[0, 0:gw, :])
    acc += _dot(yb_ref[...], w_ref[0, gw:2 * gw, :])
    acc += _dot(ym_ref[...], w_ref[0, 2 * gw:3 * gw, :])
    acc += _dot(yw_ref[...], w_ref[0, 3 * gw:4 * gw, :])
    mod = mod_ref[0]
    xn = x_ref[...] + mod[2:3] * acc
    xo_ref[...] = xn
    h = _rms(xn, g_ref[...])
    h_ref[...] = (h * (1.0 + mod[4:5]) + mod[3:4]).astype(BF16)


def _outproj_call(ys, w, layer, x, mod, gain, rows_per_mod, tm):
    m, d = x.shape
    gw = ys[0].shape[1]
    per = rows_per_mod // tm
    yspec = pl.BlockSpec((tm, gw), lambda i: (i, 0))
    return pl.pallas_call(
        _outproj_kernel,
        grid=(m // tm,),
        in_specs=[yspec, yspec, yspec, yspec,
                  pl.BlockSpec((1, 4 * gw, d), lambda i: (layer, 0, 0)),
                  pl.BlockSpec((tm, d), lambda i: (i, 0)),
                  pl.BlockSpec((1, 8, d), lambda i: (i // per, 0, 0)),
                  pl.BlockSpec((1, d), lambda i: (0, 0))],
        out_specs=[pl.BlockSpec((tm, d), lambda i: (i, 0)),
                   pl.BlockSpec((tm, d), lambda i: (i, 0))],
        out_shape=[jax.ShapeDtypeStruct((m, d), F32), jax.ShapeDtypeStruct((m, d), BF16)],
        compiler_params=_cparams(("arbitrary",)),
        name="outproj",
    )(*ys, w, x, mod, gain)


def _ffn_kernel(h_ref, wg_ref, wu_ref, w2_ref, x_ref, mod_ref, modn_ref, gn_ref, xo_ref, hn_ref, acc_ref,
                *, with_next):
    j = pl.program_id(1)

    @pl.when(j == 0)
    def _():
        acc_ref[...] = jnp.zeros_like(acc_ref)

    h = h_ref[...]
    act = _silu(_dot(h, wg_ref[0])) * _dot(h, wu_ref[0])
    acc_ref[...] += _dot(act.astype(BF16), w2_ref[0])

    @pl.when(j == pl.num_programs(1) - 1)
    def _():
        xn = x_ref[...] + mod_ref[0][5:6] * acc_ref[...]
        xo_ref[...] = xn
        if with_next:
            modn = modn_ref[0]
            hn = _rms(xn, gn_ref[...])
            hn_ref[...] = (hn * (1.0 + modn[1:2]) + modn[0:1]).astype(BF16)
        else:
            hn_ref[...] = jnp.zeros_like(hn_ref)


def _ffn_call(h, w_in, w_out, layer, x, mod, mod_next, gain_next, rows_per_mod, tm, tn, with_next):
    m, d = x.shape
    hidden = w_out.shape[1]
    nj = hidden // tn
    per = rows_per_mod // tm
    return pl.pallas_call(
        functools.partial(_ffn_kernel, with_next=with_next),
        grid=(m // tm, nj),
        in_specs=[pl.BlockSpec((tm, d), lambda i, j: (i, 0)),
                  pl.BlockSpec((1, d, tn), lambda i, j: (layer, 0, j)),
                  pl.BlockSpec((1, d, tn), lambda i, j: (layer, 0, nj + j)),
                  pl.BlockSpec((1, tn, d), lambda i, j: (layer, j, 0)),
                  pl.BlockSpec((tm, d), lambda i, j: (i, 0)),
                  pl.BlockSpec((1, 8, d), lambda i, j: (i // per, 0, 0)),
                  pl.BlockSpec((1, 8, d), lambda i, j: (i // per, 0, 0)),
                  pl.BlockSpec((1, d), lambda i, j: (0, 0))],
        out_specs=[pl.BlockSpec((tm, d), lambda i, j: (i, 0)),
                   pl.BlockSpec((tm, d), lambda i, j: (i, 0))],
        out_shape=[jax.ShapeDtypeStruct((m, d), F32), jax.ShapeDtypeStruct((m, d), BF16)],
        scratch_shapes=[pltpu.VMEM((tm, d), F32)],
        compiler_params=_cparams(("arbitrary", "arbitrary")),
        name="ffn",
    )(h, w_in, w_in, w_out, x, mod, mod_next, gain_next)


def _na_kernel(q_ref, k_ref, v_ref, kc_ref, vc_ref, gain_ref, colbias_ref, o_ref,
               kn_scr, v_scr, kcn_scr, vc_scr, bias_scr, sl0_scr, sc0_scr, sl1_scr, sc1_scr, *, rows):
    step = pl.program_id(2)
    n_rb = pl.num_programs(2) - 1
    hd = HEAD_DIM
    heads = q_ref.shape[1] // hd

    @pl.when(step == 0)
    def _():
        sl1_scr[...] = jnp.zeros_like(sl1_scr)
        sc1_scr[...] = jnp.zeros_like(sc1_scr)
        g1 = gain_ref[1:2, :]
        closed = jnp.full((GRID_W, GRID_W), NEG_INF, F32)
        for hh in range(heads):
            cols = slice(hh * hd, (hh + 1) * hd)
            kn_scr[hh] = _rms(k_ref[:, cols], g1).astype(BF16)
            v_scr[hh] = v_ref[:, cols].astype(BF16)
            kcn_scr[hh] = _rms(kc_ref[:, cols], g1).astype(BF16)
            vc_scr[hh] = vc_ref[:, cols].astype(BF16)
            for typ, (offs, inside) in enumerate(_na_row_plan(rows)):
                for i in range(NA_QROWS):
                    for jp in range(NA_KROWS // 2):
                        halves = [colbias_ref[0, hh, offs[i][j]] if inside[i][j] else closed
                                  for j in (2 * jp, 2 * jp + 1)]
                        bias_scr[hh, typ, i * GRID_W:(i + 1) * GRID_W, 2 * jp * GRID_W:(2 * jp + 2) * GRID_W] = (
                            jnp.concatenate(halves, axis=1))

    scale = hd ** -0.5
    nk = NA_KROWS * GRID_W

    def window_start(blk):
        ws = jnp.clip(NA_QROWS * blk - NA_WIN_ROWS // 2, 0, rows - NA_KROWS)
        return pl.multiple_of(ws * GRID_W, GRID_W)

    def stages(write_loc, write_ctx, read_loc, read_ctx):
        prev = jnp.maximum(step - 1, 0)
        cur = jnp.minimum(step, n_rb - 1)
        typ = jnp.where(cur == 0, 0, jnp.where(cur == n_rb - 1, 2, 1))
        for hh in range(heads):
            cols = slice(hh * hd, (hh + 1) * hd)
            s_loc = read_loc[hh]
            s_ctx = read_ctx[hh]
            m = jnp.maximum(jnp.max(s_loc, axis=-1, keepdims=True), jnp.max(s_ctx, axis=-1, keepdims=True))
            p_loc = jnp.exp(s_loc - m)
            p_ctx = jnp.exp(s_ctx - m)
            den = jnp.sum(p_loc, axis=-1, keepdims=True) + jnp.sum(p_ctx, axis=-1, keepdims=True)
            vw = v_scr[hh, pl.ds(window_start(prev), nk), :]
            o = _dot(p_loc.astype(BF16), vw) + _dot(p_ctx.astype(BF16), vc_scr[hh])
            o_ref[:, cols] = (o / den).astype(BF16)
            qn = _rms(q_ref[:, cols], gain_ref[0:1, :]).astype(BF16)
            kw = kn_scr[hh, pl.ds(window_start(cur), nk), :]
            write_loc[hh] = _dot_nt(qn, kw) * scale + bias_scr[hh, pl.ds(typ, 1)][0]
            write_ctx[hh] = _dot_nt(qn, kcn_scr[hh]) * scale

    @pl.when(step % 2 == 0)
    def _():
        stages(sl0_scr, sc0_scr, sl1_scr, sc1_scr)

    @pl.when(step % 2 == 1)
    def _():
        stages(sl1_scr, sc1_scr, sl0_scr, sc0_scr)


def _na_row_plan(rows):
    n_rb = rows // NA_QROWS
    kh = NA_WIN_ROWS
    plan = []
    for rb in (0, 1, n_rb - 1):
        ws = min(max(NA_QROWS * rb - kh // 2, 0), rows - NA_KROWS)
        offs, inside = [], []
        for i in range(NA_QROWS):
            r = NA_QROWS * rb + i
            rs = min(max(r - kh // 2, 0), rows - kh)
            offs.append([min(max(ws + j - r, 1 - kh), kh - 1) + kh - 1 for j in range(NA_KROWS)])
            inside.append([rs <= ws + j < rs + kh for j in range(NA_KROWS)])
        plan.append((offs, inside))
    return plan


def _na_col_tables(rpb):
    kw = NA_WIN_COLS
    c = jnp.arange(GRID_W)
    cs = jnp.clip(c - kw // 2, 0, GRID_W - kw)
    in_col = (c[None, :] >= cs[:, None]) & (c[None, :] < cs[:, None] + kw)
    dc = jnp.clip(c[None, :] - c[:, None], 1 - kw, kw - 1) + kw - 1
    dc_hot = jax.nn.one_hot(dc, 2 * kw - 1, dtype=F32)
    val = jnp.einsum("lhab,ckb->lhack", rpb.astype(F32), dc_hot, precision=lax.Precision.HIGHEST)
    return jnp.where(in_col, val, NEG_INF)


def _na_call(p_lat, p_ctx, gain, colbias, layer, batch, seq, lc):
    rows = seq // GRID_W
    n_rb = rows // NA_QROWS
    tq = NA_QROWS * GRID_W
    nk = NA_KROWS * GRID_W
    heads = colbias.shape[1]
    n_off = colbias.shape[2]
    hd = HEAD_DIM
    hp = NA_HEADS_PER_STEP
    wide = hp * hd
    assert heads % hp == 0 and NA_Q % hp == 0 and NA_K % hp == 0 and NA_V % hp == 0
    return pl.pallas_call(
        functools.partial(_na_kernel, rows=rows),
        grid=(batch, heads // hp, n_rb + 1),
        in_specs=[pl.BlockSpec((tq, wide), lambda b, h, r: (b * n_rb + jnp.minimum(r, n_rb - 1), NA_Q // hp + h)),
                  pl.BlockSpec((seq, wide), lambda b, h, r: (b, NA_K // hp + h)),
                  pl.BlockSpec((seq, wide), lambda b, h, r: (b, NA_V // hp + h)),
                  pl.BlockSpec((lc, wide), lambda b, h, r: (b, NA_K // hp + h)),
                  pl.BlockSpec((lc, wide), lambda b, h, r: (b, NA_V // hp + h)),
                  pl.BlockSpec((2, hd), lambda b, h, r: (0, 0)),
                  pl.BlockSpec((1, hp, n_off, GRID_W, GRID_W), lambda b, h, r: (layer, h, 0, 0, 0))],
        out_specs=pl.BlockSpec((tq, wide), lambda b, h, r: (b * n_rb + jnp.maximum(r - 1, 0), h)),
        out_shape=jax.ShapeDtypeStruct((batch * seq, heads * hd), BF16),
        scratch_shapes=[pltpu.VMEM((hp, seq, hd), BF16), pltpu.VMEM((hp, seq, hd), BF16),
                        pltpu.VMEM((hp, lc, hd), BF16), pltpu.VMEM((hp, lc, hd), BF16),
                        pltpu.VMEM((hp, 3, tq, nk), F32),
                        pltpu.VMEM((hp, tq, nk), F32), pltpu.VMEM((hp, tq, lc), F32),
                        pltpu.VMEM((hp, tq, nk), F32), pltpu.VMEM((hp, tq, lc), F32)],
        compiler_params=_cparams(("arbitrary", "arbitrary", "arbitrary")),
        name="na_attn",
    )(p_lat, p_lat, p_lat, p_ctx, p_ctx, gain, colbias)


def _ctx_attn_kernel(sink_ref, q_ref, k_ref, v_ref, gain_ref, o_ref, *, use_sink):
    scale = HEAD_DIM ** -0.5
    qn = _rms(q_ref[...], gain_ref[0:1, :]).astype(BF16)
    kn = _rms(k_ref[...], gain_ref[1:2, :]).astype(BF16)
    s = _dot_nt(qn, kn) * scale
    m = jnp.max(s, axis=-1, keepdims=True)
    if use_sink:
        sk = sink_ref[pl.program_id(1)]
        m = jnp.maximum(m, sk)
    p = jnp.exp(s - m)
    den = jnp.sum(p, axis=-1, keepdims=True)
    if use_sink:
        den = den + jnp.exp(sk - m)
    o = _dot(p.astype(BF16), v_ref[...].astype(BF16))
    o_ref[...] = (o / den).astype(BF16)


def _ctx_attn_call(p_ctx, gain, sink, batch, lc, heads, rep, qoff, koff, voff, use_sink, name):
    hd = HEAD_DIM
    return pl.pallas_call(
        functools.partial(_ctx_attn_kernel, use_sink=use_sink),
        grid=(batch, heads),
        in_specs=[pl.BlockSpec(memory_space=pltpu.SMEM),
                  pl.BlockSpec((lc, hd), lambda b, h: (b, qoff + h)),
                  pl.BlockSpec((lc, hd), lambda b, h: (b, koff + h // rep)),
                  pl.BlockSpec((lc, hd), lambda b, h: (b, voff + h // rep)),
                  pl.BlockSpec((2, hd), lambda b, h: (0, 0))],
        out_specs=pl.BlockSpec((lc, hd), lambda b, h: (b, h)),
        out_shape=jax.ShapeDtypeStruct((batch * lc, heads * hd), BF16),
        compiler_params=_cparams(("arbitrary", "arbitrary")),
        name=name,
    )(sink, p_ctx, p_ctx, p_ctx, gain)


def _rope(x, cos, sin_signed):
    lane = _iota(x.shape, 1)
    swapped = jnp.where(lane % 64 < 32, pltpu.roll(x, 96, axis=1), pltpu.roll(x, 32, axis=1))
    return x * cos + swapped * sin_signed


def _wa_kernel(sink_ref, q_ref, k_ref, v_ref, kc_ref, vc_ref, gain_ref, cq_ref, sq_ref, ck_ref, sk_ref, o_ref,
               kpad_scr, vpad_scr, kcn_scr, vc_scr, sl0_scr, sc0_scr, sl1_scr, sc1_scr, *, seq):
    step = pl.program_id(1)
    nb = pl.num_programs(1) - 1
    wb = WA_BLOCK
    hd = HEAD_DIM
    groups = k_ref.shape[1] // hd

    @pl.when(step == 0)
    def _():
        sl1_scr[...] = jnp.zeros_like(sl1_scr)
        sc1_scr[...] = jnp.zeros_like(sc1_scr)
        g1 = gain_ref[1:2, :]
        zeros = jnp.zeros((wb, hd), BF16)
        for g in range(groups):
            cols = slice(g * hd, (g + 1) * hd)
            kpad_scr[g, 0:wb, :] = zeros
            kpad_scr[g, seq + wb:seq + 2 * wb, :] = zeros
            vpad_scr[g, 0:wb, :] = zeros
            vpad_scr[g, seq + wb:seq + 2 * wb, :] = zeros
            kpad_scr[g, wb:seq + wb, :] = _rope(_rms(k_ref[:, cols], g1), ck_ref[...], sk_ref[...]).astype(BF16)
            vpad_scr[g, wb:seq + wb, :] = v_ref[:, cols].astype(BF16)
            kcn_scr[g] = _rms(kc_ref[:, cols], g1).astype(BF16)
            vc_scr[g] = vc_ref[:, cols].astype(BF16)

    scale = hd ** -0.5

    def stages(write_loc, write_ctx, read_loc, read_ctx):
        prev = jnp.maximum(step - 1, 0)
        cur = jnp.minimum(step, nb - 1)
        g0 = gain_ref[0:1, :]
        cq = cq_ref[...]
        sq = sq_ref[...]
        shp = (2 * wb, 3 * wb)
        i = _iota(shp, 0) % wb
        j = _iota(shp, 1)
        kpos = (cur - 1) * wb + j
        ok = (j - i >= 0) & (j - i <= 2 * wb) & (kpos >= 0) & (kpos < seq)
        for g in range(groups):
            s_loc = read_loc[g]
            s_ctx = read_ctx[g]
            sink = jnp.where(_iota((2 * wb, 1), 0) < wb, sink_ref[2 * g], sink_ref[2 * g + 1])
            m = jnp.maximum(jnp.maximum(jnp.max(s_loc, axis=-1, keepdims=True),
                                        jnp.max(s_ctx, axis=-1, keepdims=True)), sink)
            p_loc = jnp.exp(s_loc - m)
            p_ctx = jnp.exp(s_ctx - m)
            den = jnp.sum(p_loc, axis=-1, keepdims=True) + jnp.sum(p_ctx, axis=-1, keepdims=True) + jnp.exp(sink - m)
            vw = vpad_scr[g, pl.ds(pl.multiple_of(prev * wb, wb), 3 * wb), :]
            o = (_dot(p_loc.astype(BF16), vw) + _dot(p_ctx.astype(BF16), vc_scr[g])) / den
            o_ref[:, 2 * g * hd:(2 * g + 1) * hd] = o[0:wb].astype(BF16)
            o_ref[:, (2 * g + 1) * hd:(2 * g + 2) * hd] = o[wb:2 * wb].astype(BF16)
            q2 = jnp.concatenate([_rope(_rms(q_ref[:, 2 * g * hd:(2 * g + 1) * hd], g0), cq, sq),
                                  _rope(_rms(q_ref[:, (2 * g + 1) * hd:(2 * g + 2) * hd], g0), cq, sq)],
                                 axis=0).astype(BF16)
            kw = kpad_scr[g, pl.ds(pl.multiple_of(cur * wb, wb), 3 * wb), :]
            write_loc[g] = jnp.where(ok, _dot_nt(q2, kw) * scale, NEG_INF)
            write_ctx[g] = _dot_nt(q2, kcn_scr[g]) * scale

    @pl.when(step % 2 == 0)
    def _():
        stages(sl0_scr, sc0_scr, sl1_scr, sc1_scr)

    @pl.when(step % 2 == 1)
    def _():
        stages(sl1_scr, sc1_scr, sl0_scr, sc0_scr)


def _wa_call(p_lat, p_ctx, gain, sink, cos_t, sin_t, batch, seq, lc):
    wb = WA_BLOCK
    hd = HEAD_DIM
    nb = seq // wb
    kvh = 2
    qw = 2 * kvh * hd
    kw_ = kvh * hd
    assert (WA_Q * hd) % qw == 0 and (WA_K * hd) % kw_ == 0 and (WA_V * hd) % kw_ == 0
    return pl.pallas_call(
        functools.partial(_wa_kernel, seq=seq),
        grid=(batch, nb + 1),
        in_specs=[pl.BlockSpec(memory_space=pltpu.SMEM),
                  pl.BlockSpec((wb, qw), lambda b, n: (b * nb + jnp.minimum(n, nb - 1), WA_Q * hd // qw)),
                  pl.BlockSpec((seq, kw_), lambda b, n: (b, WA_K * hd // kw_)),
                  pl.BlockSpec((seq, kw_), lambda b, n: (b, WA_V * hd // kw_)),
                  pl.BlockSpec((lc, kw_), lambda b, n: (b, WA_K * hd // kw_)),
                  pl.BlockSpec((lc, kw_), lambda b, n: (b, WA_V * hd // kw_)),
                  pl.BlockSpec((2, hd), lambda b, n: (0, 0)),
                  pl.BlockSpec((wb, hd), lambda b, n: (jnp.minimum(n, nb - 1), 0)),
                  pl.BlockSpec((wb, hd), lambda b, n: (jnp.minimum(n, nb - 1), 0)),
                  pl.BlockSpec((seq, hd), lambda b, n: (0, 0)),
                  pl.BlockSpec((seq, hd), lambda b, n: (0, 0))],
        out_specs=pl.BlockSpec((wb, qw), lambda b, n: (b * nb + jnp.maximum(n - 1, 0), 0)),
        out_shape=jax.ShapeDtypeStruct((batch * seq, qw), BF16),
        scratch_shapes=[pltpu.VMEM((kvh, seq + 2 * wb, hd), BF16), pltpu.VMEM((kvh, seq + 2 * wb, hd), BF16),
                        pltpu.VMEM((kvh, lc, hd), BF16), pltpu.VMEM((kvh, lc, hd), BF16),
                        pltpu.VMEM((kvh, 2 * wb, 3 * wb), F32), pltpu.VMEM((kvh, 2 * wb, lc), F32),
                        pltpu.VMEM((kvh, 2 * wb, 3 * wb), F32), pltpu.VMEM((kvh, 2 * wb, lc), F32)],
        compiler_params=_cparams(("arbitrary", "arbitrary")),
        name="wa_attn",
    )(sink, p_lat, p_lat, p_lat, p_ctx, p_ctx, gain, cos_t, sin_t, cos_t, sin_t)


def _dn_prep_kernel(x_ref, prev_ref, next_ref, w_ref, o_ref, *, n_tiles):
    t = pl.program_id(1)
    tl = x_ref.shape[0]
    x = x_ref[...]
    prev = jnp.where(t > 0, prev_ref[...], 0.0)
    nxt = jnp.where(t < n_tiles - 1, next_ref[...], 0.0)
    xe = jnp.concatenate([prev, x, nxt], axis=0)
    n = tl + 16
    taps = w_ref.shape[0]
    acc = jnp.zeros_like(x)
    for j in range(taps):
        off = 8 - taps // 2 + j
        shifted = xe if off % n == 0 else pltpu.roll(xe, n - off, axis=0)
        acc += shifted[0:tl] * w_ref[j:j + 1, :]
    y = _silu(acc)
    hd = HEAD_DIM
    nh = y.shape[1] // (3 * hd)
    for blk in range(3 * nh):
        yb = y[:, blk * hd:(blk + 1) * hd]
        if blk < 2 * nh:
            yb = yb * lax.rsqrt(jnp.sum(yb * yb, axis=-1, keepdims=True) + EPS)
            if blk < nh:
                yb = yb * hd ** -0.5
        o_ref[:, blk * hd:(blk + 1) * hd] = yb


def _dn_prep_call(p, conv_w, batch, length, tl):
    width = conv_w.shape[1]
    n_tiles = length // tl
    per8 = tl // 8
    last8 = batch * length // 8 - 1
    return pl.pallas_call(
        functools.partial(_dn_prep_kernel, n_tiles=n_tiles),
        grid=(batch, n_tiles),
        in_specs=[pl.BlockSpec((tl, width), lambda b, t: (b * n_tiles + t, 1)),
                  pl.BlockSpec((8, width), lambda b, t: (jnp.maximum((b * n_tiles + t) * per8 - 1, 0), 1)),
                  pl.BlockSpec((8, width), lambda b, t: (jnp.minimum((b * n_tiles + t + 1) * per8, last8), 1)),
                  pl.BlockSpec((conv_w.shape[0], width), lambda b, t: (0, 0))],
        out_specs=pl.BlockSpec((tl, width), lambda b, t: (b * n_tiles + t, 0)),
        out_shape=jax.ShapeDtypeStruct((batch * length, width), F32),
        compiler_params=_cparams(("arbitrary", "arbitrary")),
        name="dn_prep",
    )(p, p, p, conv_w)


def _split_bf16(a):
    hi = a.astype(BF16)
    return hi, (a - hi.astype(F32)).astype(BF16)


def _solve_dot(a, b, mode):
    if mode == "highest":
        return _dot(a, b, lax.Precision.HIGHEST)
    if mode == "bf16":
        return _dot(a.astype(BF16), b.astype(BF16))
    a_hi, a_lo = _split_bf16(a)
    b_hi, b_lo = _split_bf16(b)
    return _dot(a_hi, b_hi) + (_dot(a_hi, b_lo) + _dot(a_lo, b_hi))


def _dn_pass(q_ref, k_ref, v_ref, og_ref, gates_ref, gadd, gmul, norm_g, y_ref, length, h, state,
             lin_scr, add_scr, egl_scr, acc_scr, solve_precision):
    assert CHUNK == HEAD_DIM
    n_chunks = length // CHUNK

    acc_scr[0:length, :] = jnp.zeros((length, HEAD_DIM), F32)

    shp = (CHUNK, CHUNK)
    ri = _iota(shp, 0)
    ci = _iota(shp, 1)
    masks = ((ci <= ri, ci < ri), (ci >= ri, ci > ri))
    couple = ([], [])
    s = 1
    while s < CHUNK:
        same = (ri // (2 * s)) == (ci // (2 * s))
        couple[0].append(same & ((ri // s) % 2 == 1) & ((ci // s) % 2 == 0))
        couple[1].append(same & ((ri // s) % 2 == 0) & ((ci // s) % 2 == 1))
        s *= 2
    group = min(n_chunks, PREP_GROUP)

    def prep(gi, carry):
        chains = []
        for g in range(group):
            c = gi * group + g
            r0 = pl.multiple_of(c * CHUNK, CHUNK)
            q = q_ref[pl.ds(r0, CHUNK), :]
            k = k_ref[pl.ds(r0, CHUNK), :]
            raw = gates_ref[pl.ds(r0, CHUNK), :]
            gb = jnp.where(_iota(raw.shape, 1) < LANE_ALPHA, jax.nn.sigmoid(raw), -gmul * _softplus(raw + gadd))
            k16 = k.astype(BF16)
            kk = _dot_nt(k16, k16)
            qk = _dot_nt(q.astype(BF16), k16)
            for d in range(2):
                incl, strict = masks[d]
                beta_col = _lane_pick(gb, LANE_BETA + 4 * d + h)
                g_col = _lane_pick(gb, LANE_ALPHA + 4 * d + h)
                beta_row = _to_row(beta_col)
                g_row = _to_row(g_col)
                cum_col = jnp.sum(jnp.where(incl, g_row, 0.0), axis=-1, keepdims=True)
                cum_row = _to_row(cum_col)
                tot = jnp.sum(g_col, axis=0, keepdims=True)
                decay = jnp.exp(jnp.where(incl, cum_col - cum_row, NEG_INF))
                lm = jnp.where(strict, kk * beta_col * decay, 0.0)
                egl_scr[d, pl.ds(pl.multiple_of(c * 8, 8), 8), :] = jnp.broadcast_to(jnp.exp(tot), (8, HEAD_DIM))
                chains.append((d, c, r0, lm, beta_row, beta_row * jnp.exp(cum_row), (qk * decay).astype(BF16),
                               q * jnp.exp(cum_col), k * jnp.exp(tot - cum_col)))
        eye = (ri == ci).astype(F32)
        xs = [eye - jnp.where(couple[ch[0]][0], ch[3], 0.0) for ch in chains]
        for lvl in range(1, len(couple[0])):
            cs = [jnp.where(couple[ch[0]][lvl], ch[3], 0.0) for ch in chains]
            xc = [_solve_dot(x, c, solve_precision) for x, c in zip(xs, cs)]
            xs = [x - _solve_dot(t, x, solve_precision) for x, t in zip(xs, xc)]
        u16 = [_solve_dot(ainv * ch[4], v_ref[pl.ds(ch[2], CHUNK), :], solve_precision).astype(BF16)
               for ch, ainv in zip(chains, xs)]
        w16 = [_solve_dot(ainv * ch[5], k_ref[pl.ds(ch[2], CHUNK), :], solve_precision).astype(BF16)
               for ch, ainv in zip(chains, xs)]
        ket = [ch[8].T.astype(BF16) for ch in chains]
        s_lin = [_dot(kt, w) for kt, w in zip(ket, w16)]
        s_add = [_dot(kt, u) for kt, u in zip(ket, u16)]
        o_lin = [ch[7] - _dot(ch[6], w) for ch, w in zip(chains, w16)]
        o_add = [_dot(ch[6], u) for ch, u in zip(chains, u16)]
        for ch, sl, sa, ol, oa in zip(chains, s_lin, s_add, o_lin, o_add):
            d = ch[0]
            base = pl.multiple_of(ch[1] * 2 * CHUNK, 2 * CHUNK)
            lin_scr[d, pl.ds(base, CHUNK), :] = sl.astype(BF16)
            lin_scr[d, pl.ds(base + CHUNK, CHUNK), :] = ol.astype(BF16)
            add_scr[d, pl.ds(base, CHUNK), :] = sa
            add_scr[d, pl.ds(base + CHUNK, CHUNK), :] = oa
        return carry

    lax.fori_loop(0, n_chunks // group, prep, 0)

    def scan(s, st):
        new = []
        for d in range(2):
            c = s if d == 0 else n_chunks - 1 - s
            base = pl.multiple_of(c * 2 * CHUNK, 2 * CHUNK)
            sm = st[d]
            prod = _dot(lin_scr[d, pl.ds(base, 2 * CHUNK), :], sm.astype(BF16))
            add = add_scr[d, pl.ds(base, 2 * CHUNK), :]
            acc_scr[pl.ds(pl.multiple_of(c * CHUNK, CHUNK), CHUNK), :] += prod[CHUNK:] + add[CHUNK:]
            egl = egl_scr[d, pl.ds(pl.multiple_of(c * 8, 8), 8), :][0:1, :]
            new.append(sm * egl - prod[:CHUNK] + add[:CHUNK])
        return tuple(new)

    state = lax.fori_loop(0, n_chunks, scan, state, unroll=SCAN_UNROLL)

    y = _rms(acc_scr[0:length, :], norm_g) * _silu(og_ref[...])
    y_ref[...] = y.astype(BF16)
    return state


def _dn_kernel(q_ref, k_ref, v_ref, og_ref, gates_ref, qc_ref, kc_ref, vc_ref, ogc_ref, gatesc_ref,
               gadd_ref, gmul_ref, norm_ref, y_ref, yc_ref,
               lin_scr, add_scr, egl_scr, acc_scr, *, seq, lc, solve_precision):
    h = pl.program_id(1)
    gadd = gadd_ref[...]
    gmul = gmul_ref[...]
    norm_g = norm_ref[...]
    scr = (lin_scr, add_scr, egl_scr, acc_scr)
    zero = jnp.zeros((HEAD_DIM, HEAD_DIM), F32)
    state = _dn_pass(qc_ref, kc_ref, vc_ref, ogc_ref, gatesc_ref, gadd, gmul, norm_g, yc_ref, lc, h, (zero, zero),
                     *scr, solve_precision)
    _dn_pass(q_ref, k_ref, v_ref, og_ref, gates_ref, gadd, gmul, norm_g, y_ref, seq, h, state,
             *scr, solve_precision)


def _dn_call(dq_lat, dq_ctx, p_lat, p_ctx, pg_lat, pg_ctx, gadd, gmul, norm_g, batch, seq, lc, solve_precision):
    hd = HEAD_DIM
    heads = 4

    def blk(n, col):
        return pl.BlockSpec((n, hd), lambda b, h: (b, col + h))

    def gate_blk(n):
        return pl.BlockSpec((n, hd), lambda b, h: (b, 0))

    vec = pl.BlockSpec((1, hd), lambda b, h: (0, 0))
    return pl.pallas_call(
        functools.partial(_dn_kernel, seq=seq, lc=lc, solve_precision=solve_precision),
        grid=(batch, heads),
        in_specs=[blk(seq, 0), blk(seq, heads), blk(seq, 2 * heads), blk(seq, DN_OG), gate_blk(seq),
                  blk(lc, 0), blk(lc, heads), blk(lc, 2 * heads), blk(lc, DN_OG), gate_blk(lc),
                  vec, vec, vec],
        out_specs=[pl.BlockSpec((seq, hd), lambda b, h: (b, h)),
                   pl.BlockSpec((lc, hd), lambda b, h: (b, h))],
        out_shape=[jax.ShapeDtypeStruct((batch * seq, heads * hd), BF16),
                   jax.ShapeDtypeStruct((batch * lc, heads * hd), BF16)],
        scratch_shapes=[pltpu.VMEM((2, 2 * seq, hd), BF16),
                        pltpu.VMEM((2, 2 * seq, hd), F32),
                        pltpu.VMEM((2, seq // CHUNK * 8, hd), F32),
                        pltpu.VMEM((seq, hd), F32)],
        compiler_params=_cparams(("arbitrary", "arbitrary"), DN_VMEM_LIMIT),
        name="deltanet",
    )(dq_lat, dq_lat, dq_lat, p_lat, pg_lat, dq_ctx, dq_ctx, dq_ctx, p_ctx, pg_ctx, gadd, gmul, norm_g)


def _ml_pass(q_ref, k_ref, v_ref, og_ref, gates_ref, gadd, norm_g, y_ref, length, h, state, acc_scr):
    n_chunks = length // CHUNK
    acc_scr[0:length, :] = jnp.zeros((length, HEAD_DIM), F32)
    shp = (CHUNK, CHUNK)
    ri = _iota(shp, 0)
    ci = _iota(shp, 1)
    masks = (ci <= ri, ci >= ri)
    head_lanes = (_iota((1, HEAD_DIM), 1) // ML_QK) == (h % 2)

    group = min(n_chunks, ML_GROUP)

    def each(f, *lists):
        return [f(*a) for a in zip(*lists)]

    def step(gi, st):
        dirs, r0s = [], []
        for j in range(group):
            s = gi * group + j
            for d in range(2):
                c = s if d == 0 else n_chunks - 1 - s
                dirs.append(d)
                r0s.append(pl.multiple_of(c * CHUNK, CHUNK))
        incl = [masks[d] for d in dirs]
        q = [jnp.where(head_lanes, q_ref[pl.ds(r0, CHUNK), :], 0.0) for r0 in r0s]
        k = [jnp.where(head_lanes, k_ref[pl.ds(r0, CHUNK), :], 0.0) * ML_QK ** -0.5 for r0 in r0s]
        v16 = [v_ref[pl.ds(r0, CHUNK), :].astype(BF16) for r0 in r0s]
        gb = [gates_ref[pl.ds(r0, CHUNK), :] + gadd for r0 in r0s]
        ig_col = [_lane_pick(g, LANE_IG + 4 * d + h) for g, d in zip(gb, dirs)]
        lf_col = [_log_sigmoid(_lane_pick(g, LANE_FG + 4 * d + h)) for g, d in zip(gb, dirs)]
        ig_row = each(_to_row, ig_col)
        lf_row = each(_to_row, lf_col)
        b_col = each(lambda m, r: jnp.sum(jnp.where(m, r, 0.0), axis=-1, keepdims=True), incl, lf_row)
        b_row = each(_to_row, b_col)
        b_last = each(lambda x: jnp.sum(x, axis=0, keepdims=True), lf_col)
        g_end = each(lambda bl, b, i: bl - b + i, b_last, b_col, ig_col)
        m_loc = each(lambda x: jnp.max(x, axis=0, keepdims=True), g_end)
        kw = each(lambda kk, g, m: kk * jnp.exp(g - m), k, g_end, m_loc)
        q16 = each(lambda x: x.astype(BF16), q)
        k16 = each(lambda x: x.astype(BF16), k)
        c_loc = each(lambda a, b: _dot_tn(a.astype(BF16), b), kw, v16)
        n_loc = each(lambda x: jnp.sum(x, axis=0, keepdims=True), kw)
        log_d = each(lambda m, bc, br, ir: jnp.where(m, bc - br + ir, NEG_INF), incl, b_col, b_row, ig_row)
        mx_col = each(lambda x: jnp.max(x, axis=-1, keepdims=True), log_d)
        s0 = each(lambda a, b, ld, mx: _dot_nt(a, b) * jnp.exp(ld - mx), q16, k16, log_d, mx_col)
        sv0 = each(lambda s, vv: _dot(s.astype(BF16), vv), s0, v16)
        rs0 = each(lambda s: jnp.sum(s, axis=-1, keepdims=True), s0)
        st = list(st)
        for i, (d, r0) in enumerate(zip(dirs, r0s)):
            c_st, n_st, m_st = st[d]
            m_inter = b_col[i] + m_st
            m_t = jnp.maximum(mx_col[i], m_inter)
            intra = jnp.exp(mx_col[i] - m_t)
            inter = jnp.exp(m_inter - m_t)
            num = intra * sv0[i] + inter * _dot(q16[i], c_st.astype(BF16))
            den = intra * rs0[i] + inter * jnp.sum(q[i] * n_st, axis=-1, keepdims=True)
            acc_scr[pl.ds(r0, CHUNK), :] += num / jnp.maximum(jnp.abs(den), jnp.exp(-m_t))
            m_new = jnp.maximum(b_last[i] + m_st, m_loc[i])
            a = jnp.exp(b_last[i] + m_st - m_new)
            e = jnp.exp(m_loc[i] - m_new)
            st[d] = (a * c_st + e * c_loc[i], a * n_st + e * n_loc[i], m_new)
        return tuple(st)

    state = lax.fori_loop(0, n_chunks // group, step, state)
    y = jax.nn.sigmoid(og_ref[...]) * _rms(acc_scr[0:length, :], norm_g)
    y_ref[...] = y.astype(BF16)
    return state


def _ml_kernel(q_ref, k_ref, v_ref, og_ref, gates_ref, qc_ref, kc_ref, vc_ref, ogc_ref, gatesc_ref,
               gadd_ref, norm_ref, y_ref, yc_ref, acc_scr, *, seq, lc):
    h = pl.program_id(1)
    gadd = gadd_ref[...]
    norm_g = norm_ref[0]
    zero = (jnp.zeros((HEAD_DIM, HEAD_DIM), F32), jnp.zeros((1, HEAD_DIM), F32), jnp.zeros((1, 1), F32))
    state = _ml_pass(qc_ref, kc_ref, vc_ref, ogc_ref, gatesc_ref, gadd, norm_g, yc_ref, lc, h, (zero, zero), acc_scr)
    _ml_pass(q_ref, k_ref, v_ref, og_ref, gates_ref, gadd, norm_g, y_ref, seq, h, state, acc_scr)


def _ml_call(p_lat, p_ctx, pg_lat, pg_ctx, gadd, norm_g, batch, seq, lc):
    hd = HEAD_DIM
    heads = 4

    def specs(n):
        return [pl.BlockSpec((n, hd), lambda b, h: (b, ML_Q + h // 2)),
                pl.BlockSpec((n, hd), lambda b, h: (b, ML_K + h // 2)),
                pl.BlockSpec((n, hd), lambda b, h: (b, ML_V + h)),
                pl.BlockSpec((n, hd), lambda b, h: (b, ML_OG + h)),
                pl.BlockSpec((n, hd), lambda b, h: (b, 0))]

    return pl.pallas_call(
        functools.partial(_ml_kernel, seq=seq, lc=lc),
        grid=(batch, heads),
        in_specs=specs(seq) + specs(lc) + [pl.BlockSpec((1, hd), lambda b, h: (0, 0)),
                                            pl.BlockSpec((1, 1, hd), lambda b, h: (h, 0, 0))],
        out_specs=[pl.BlockSpec((seq, hd), lambda b, h: (b, h)),
                   pl.BlockSpec((lc, hd), lambda b, h: (b, h))],
        out_shape=[jax.ShapeDtypeStruct((batch * seq, heads * hd), BF16),
                   jax.ShapeDtypeStruct((batch * lc, heads * hd), BF16)],
        scratch_shapes=[pltpu.VMEM((seq, hd), F32)],
        compiler_params=_cparams(("arbitrary", "arbitrary")),
        name="mlstm",
    )(p_lat, p_lat, p_lat, p_lat, pg_lat, p_ctx, p_ctx, p_ctx, p_ctx, pg_ctx, gadd, norm_g.reshape(heads, 1, hd))


def _regroup_kernel(wt_ref, main_ref, gate_ref, *, cuts):
    (a0, a1), (b0, b1), (c0, c1), (g0, g1), (h0, h1) = cuts
    main_ref[0, 0:a1 - a0, :] = wt_ref[0, a0:a1, :].astype(BF16)
    main_ref[0, a1 - a0:a1 - a0 + b1 - b0, :] = wt_ref[0, b0:b1, :].astype(BF16)
    main_ref[0, a1 - a0 + b1 - b0:, :] = wt_ref[0, c0:c1, :].astype(BF16)
    ng = (g1 - g0) + (h1 - h0)
    gate_ref[0, 0:g1 - g0, :] = wt_ref[0, g0:g1, :].astype(BF16)
    gate_ref[0, g1 - g0:ng, :] = wt_ref[0, h0:h1, :].astype(BF16)
    gate_ref[0, ng:, :] = jnp.zeros((gate_ref.shape[1] - ng, gate_ref.shape[2]), BF16)


def _regroup_call(w_in):
    depth, d, n = w_in.shape
    gw = 512
    dn_gate0 = 3 * gw + 4 * gw
    ml0 = dn_gate0 + 16
    ml_gate0 = ml0 + 3 * gw
    wa0 = ml_gate0 + 16
    cuts = ((0, dn_gate0), (ml0, ml_gate0), (wa0, n), (dn_gate0, ml0), (ml_gate0, wa0))
    tc = 256
    return pl.pallas_call(
        functools.partial(_regroup_kernel, cuts=cuts),
        grid=(depth, d // tc),
        in_specs=[pl.BlockSpec((1, n, tc), lambda l, i: (l, 0, i))],
        out_specs=[pl.BlockSpec((1, MAIN_COLS, tc), lambda l, i: (l, 0, i)),
                   pl.BlockSpec((1, 128, tc), lambda l, i: (l, 0, i))],
        out_shape=[jax.ShapeDtypeStruct((depth, MAIN_COLS, d), BF16), jax.ShapeDtypeStruct((depth, 128, d), BF16)],
        compiler_params=_cparams(("arbitrary", "arbitrary")),
        name="regroup_w_in",
    )(jnp.swapaxes(w_in, 1, 2))


def _rope_tables(seq):
    t = jnp.arange(seq)
    n_freq = HEAD_DIM // 4
    inv_freq = ROPE_THETA ** (-jnp.arange(n_freq, dtype=F32) / n_freq)
    pos = jnp.stack([t // GRID_W, t % GRID_W], axis=-1).astype(F32)
    ang = pos[:, :, None] * inv_freq
    cos, sin = jnp.cos(ang), jnp.sin(ang)
    cos_t = jnp.concatenate([cos, cos], axis=-1).reshape(seq, HEAD_DIM)
    sin_t = jnp.concatenate([-sin, sin], axis=-1).reshape(seq, HEAD_DIM)
    return cos_t, sin_t


def _gate_vectors(dt_bias, a_log, i_bias, f_bias):
    zeros8 = jnp.zeros((8,), F32)
    gadd = jnp.concatenate([zeros8, dt_bias.reshape(-1), i_bias.reshape(-1), f_bias.reshape(-1),
                            jnp.zeros((96,), F32)]).astype(F32)
    gmul = jnp.concatenate([zeros8, jnp.exp(a_log.astype(F32)).reshape(-1), jnp.zeros((112,), F32)])
    return gadd.reshape(1, 128), gmul.reshape(1, 128)


def _forward(x, c, ctx, c_ctx, w_ada, b_ada, norm_mix, norm_ffn, w_in, w_out, na_qk_gain, na_rpb, dn_conv,
             dn_a_log, dn_dt_bias, dn_norm, ml_i_bias, ml_f_bias, ml_norm, wa_qk_gain, wa_sink, w_ffn_in,
             w_ffn_out, solve_precision):
    batch, seq, d = x.shape
    lc = ctx.shape[1]
    depth = w_ada.shape[0]
    tm = 512
    tn_ffn = 512

    cos_t, sin_t = _rope_tables(seq)
    na_bias = _na_col_tables(na_rpb)
    cs = jnp.concatenate([c, c_ctx[None, :], jnp.zeros((8 - batch - 1, d), F32)], axis=0)
    mods = _ada_call(cs, w_ada, b_ada).reshape(depth, 8, 6, d)
    mods = jnp.pad(mods, ((0, 0), (0, 0), (0, 2), (0, 0)))

    w_main, w_gate = _regroup_call(w_in)
    w_o = w_out.astype(BF16)

    xl = x.reshape(batch * seq, d)
    xc = ctx.reshape(batch * lc, d)
    hl = _norm_call(xl, mods[0, :batch], norm_mix[0:1], seq, tm)
    hc = _norm_call(xc, mods[0, batch:batch + 1], norm_mix[0:1], batch * lc, tm)

    for l in range(depth):
        need_ctx = l < depth - 1
        mod_l, mod_c = mods[l, :batch], mods[l, batch:batch + 1]
        p_lat, w_f1, w_f2 = _inproj_cast_call(hl, w_main, l, w_ffn_in, w_ffn_out, 2 * tm, 1024)
        w_f1 = w_f1[None]
        w_f2 = w_f2[None]
        pg_lat = _matmul_call(hl, w_gate, l, 2 * tm, 128, "inproj_gates")
        p_ctx = _matmul_call(hc, w_main, l, tm, 2048, "inproj_ctx")
        pg_ctx = _matmul_call(hc, w_gate, l, tm, 128, "inproj_gates_ctx")

        ya = _na_call(p_lat, p_ctx, na_qk_gain[l], na_bias, l, batch, seq, lc)
        gadd, gmul = _gate_vectors(dn_dt_bias[l], dn_a_log[l], ml_i_bias[l], ml_f_bias[l])
        dq_lat = _dn_prep_call(p_lat, dn_conv[l], batch, seq, 512)
        dq_ctx = _dn_prep_call(p_ctx, dn_conv[l], batch, lc, 256)
        yb, ycb = _dn_call(dq_lat, dq_ctx, p_lat, p_ctx, pg_lat, pg_ctx, gadd, gmul, dn_norm[l].reshape(1, -1),
                           batch, seq, lc, solve_precision)
        ym, ycm = _ml_call(p_lat, p_ctx, pg_lat, pg_ctx, gadd, ml_norm[l], batch, seq, lc)
        yw = _wa_call(p_lat, p_ctx, wa_qk_gain[l], wa_sink[l], cos_t, sin_t, batch, seq, lc)

        nl = min(l + 1, depth - 1)
        xl, h2 = _outproj_call((ya, yb, ym, yw), w_o, l, xl, mod_l, norm_ffn[l:l + 1], seq, tm)
        xl, hl = _ffn_call(h2, w_f1, w_f2, 0, xl, mod_l, mods[nl, :batch], norm_mix[nl:nl + 1], seq, tm, tn_ffn,
                           need_ctx)
        if need_ctx:
            yca = _ctx_attn_call(p_ctx, na_qk_gain[l], wa_sink[l], batch, lc, 4, 1, NA_Q, NA_K, NA_V, False,
                                 "na_ctx_attn")
            ycw = _ctx_attn_call(p_ctx, wa_qk_gain[l], wa_sink[l], batch, lc, 4, 2, WA_Q, WA_K, WA_V, True,
                                 "wa_ctx_attn")
            xc, h2c = _outproj_call((yca, ycb, ycm, ycw), w_o, l, xc, mod_c, norm_ffn[l:l + 1], batch * lc, tm)
            xc, hc = _ffn_call(h2c, w_f1, w_f2, 0, xc, mod_c, mods[nl, batch:batch + 1], norm_mix[nl:nl + 1],
                               batch * lc, tm, tn_ffn, True)
    return xl.reshape(batch, seq, d)


def kernel(x, c, ctx, c_ctx, w_ada, b_ada, norm_mix, norm_ffn, w_in, w_out, na_qk_gain, na_rpb, dn_conv, dn_a_log,
           dn_dt_bias, dn_norm, ml_i_bias, ml_f_bias, ml_norm, wa_qk_gain, wa_sink, w_ffn_in, w_ffn_out):
    return _forward(x, c, ctx, c_ctx, w_ada, b_ada, norm_mix, norm_ffn, w_in, w_out, na_qk_gain, na_rpb, dn_conv,
                    dn_a_log, dn_dt_bias, dn_norm, ml_i_bias, ml_f_bias, ml_norm, wa_qk_gain, wa_sink, w_ffn_in,
                    w_ffn_out, "bf16")
```

```python
import functools

import jax
import jax.numpy as jnp
from jax import lax
from jax.experimental import pallas as pl
from jax.experimental.pallas import tpu as pltpu

F32 = jnp.float32
BF16 = jnp.bfloat16

HEAD_DIM = 128
GRID_W = 64
CHUNK = 128
FFN_CAST_SLABS = 44
PREP_GROUP = 8
ML_GROUP = 4
SCAN_UNROLL = 2
NA_WIN_ROWS = 8
NA_WIN_COLS = 16
NA_QROWS = 4
NA_KROWS = NA_QROWS + NA_WIN_ROWS
NA_HEADS_PER_STEP = 2
WA_BLOCK = 128
ML_QK = 64
ROPE_THETA = 10000.0
EPS = 1e-6
NEG_INF = float("-inf")
VMEM_LIMIT = 48 * 1024 * 1024
DN_VMEM_LIMIT = 56 * 1024 * 1024

NA_Q, NA_K, NA_V = 0, 4, 8
DN_OG = 24
ML_Q, ML_K, ML_V, ML_OG = 28, 30, 32, 36
WA_Q, WA_K, WA_V = 40, 44, 46
MAIN_COLS = 48 * 128
LANE_BETA, LANE_ALPHA, LANE_IG, LANE_FG = 0, 8, 16, 24


def _cparams(sem, vmem_limit=VMEM_LIMIT):
    return pltpu.CompilerParams(dimension_semantics=sem, vmem_limit_bytes=vmem_limit)


def _rms(x, gain):
    return x * lax.rsqrt(jnp.mean(x * x, axis=-1, keepdims=True) + EPS) * gain


def _silu(x):
    return x * jax.nn.sigmoid(x)


def _softplus(x):
    return jnp.maximum(x, 0.0) + jnp.log1p(jnp.exp(-jnp.abs(x)))


def _log_sigmoid(x):
    return jnp.minimum(x, 0.0) - jnp.log1p(jnp.exp(-jnp.abs(x)))


def _dot(a, b):
    return jnp.dot(a, b, preferred_element_type=F32)


def _dot_nt(a, b):
    return lax.dot_general(a, b, (((1,), (1,)), ((), ())), preferred_element_type=F32)


def _dot_tn(a, b):
    return lax.dot_general(a, b, (((0,), (0,)), ((), ())), preferred_element_type=F32)


def _iota(shape, dim):
    return lax.broadcasted_iota(jnp.int32, shape, dim)


def _lane_pick(x, lane):
    return jnp.sum(jnp.where(_iota(x.shape, 1) == lane, x, 0.0), axis=-1, keepdims=True)


def _to_row(col):
    n = col.shape[0]
    eye = _iota((n, n), 0) == _iota((n, n), 1)
    return jnp.sum(jnp.where(eye, col, 0.0), axis=0, keepdims=True)


def _ada_kernel(c_ref, w_ref, b_ref, o_ref):
    a = _silu(c_ref[...]).astype(BF16)
    o_ref[0] = _dot(a, w_ref[0].astype(BF16)) + b_ref[0]


def _ada_call(cs, w_ada, b_ada):
    depth, d, n = w_ada.shape
    tn = 1024
    return pl.pallas_call(
        _ada_kernel,
        grid=(depth, n // tn),
        in_specs=[pl.BlockSpec((8, d), lambda l, j: (0, 0)),
                  pl.BlockSpec((1, d, tn), lambda l, j: (l, 0, j)),
                  pl.BlockSpec((1, 1, tn), lambda l, j: (l, 0, j))],
        out_specs=pl.BlockSpec((1, 8, tn), lambda l, j: (l, 0, j)),
        out_shape=jax.ShapeDtypeStruct((depth, 8, n), F32),
        compiler_params=_cparams(("arbitrary", "arbitrary")),
        name="ada",
    )(cs, w_ada, b_ada.reshape(depth, 1, n))


def _norm_kernel(x_ref, mod_ref, g_ref, h_ref):
    mod = mod_ref[0]
    h = _rms(x_ref[...], g_ref[...])
    h_ref[...] = (h * (1.0 + mod[1:2]) + mod[0:1]).astype(BF16)


def _norm_call(x, mod, gain, rows_per_mod, tm):
    m, d = x.shape
    per = rows_per_mod // tm
    return pl.pallas_call(
        _norm_kernel,
        grid=(m // tm,),
        in_specs=[pl.BlockSpec((tm, d), lambda i: (i, 0)),
                  pl.BlockSpec((1, 8, d), lambda i: (i // per, 0, 0)),
                  pl.BlockSpec((1, d), lambda i: (0, 0))],
        out_specs=pl.BlockSpec((tm, d), lambda i: (i, 0)),
        out_shape=jax.ShapeDtypeStruct((m, d), BF16),
        compiler_params=_cparams(("arbitrary",)),
        name="norm_mod",
    )(x, mod, gain)


def _matmul_kernel(a_ref, wt_ref, o_ref):
    o_ref[...] = _dot_nt(a_ref[...], wt_ref[0])


def _matmul_call(a, wt, layer, tm, tn, name):
    m, k = a.shape
    n = wt.shape[1]
    return pl.pallas_call(
        _matmul_kernel,
        grid=(n // tn, m // tm),
        in_specs=[pl.BlockSpec((tm, k), lambda j, i: (i, 0)),
                  pl.BlockSpec((1, tn, k), lambda j, i: (layer, j, 0))],
        out_specs=pl.BlockSpec((tm, tn), lambda j, i: (i, j)),
        out_shape=jax.ShapeDtypeStruct((m, n), F32),
        compiler_params=_cparams(("arbitrary", "arbitrary")),
        name=name,
    )(a, wt)


def _inproj_cast_kernel(a_ref, wt_ref, f1_ref, f2_ref, o_ref, f1o_ref, f2o_ref):
    o_ref[...] = _dot_nt(a_ref[...], wt_ref[0])
    f1o_ref[...] = f1_ref[0].astype(BF16)
    f2o_ref[...] = f2_ref[0].astype(BF16)


def _inproj_cast_call(a, wt, layer, w_ffn_in, w_ffn_out, tm, tn):
    m, k = a.shape
    n = wt.shape[1]
    _, d, two_hidden = w_ffn_in.shape
    hidden = w_ffn_out.shape[1]
    slabs = FFN_CAST_SLABS
    n_i = m // tm
    assert (n // tn) * n_i >= slabs and two_hidden % (slabs * 128) == 0 and hidden % (slabs * 8) == 0
    c1 = two_hidden // slabs
    r2 = hidden // slabs

    def slab(j, i):
        return jnp.minimum(j * n_i + i, slabs - 1)

    return pl.pallas_call(
        _inproj_cast_kernel,
        grid=(n // tn, n_i),
        in_specs=[pl.BlockSpec((tm, k), lambda j, i: (i, 0)),
                  pl.BlockSpec((1, tn, k), lambda j, i: (layer, j, 0)),
                  pl.BlockSpec((1, d, c1), lambda j, i: (layer, 0, slab(j, i))),
                  pl.BlockSpec((1, r2, d), lambda j, i: (layer, slab(j, i), 0))],
        out_specs=[pl.BlockSpec((tm, tn), lambda j, i: (i, j)),
                   pl.BlockSpec((d, c1), lambda j, i: (0, slab(j, i))),
                   pl.BlockSpec((r2, d), lambda j, i: (slab(j, i), 0))],
        out_shape=[jax.ShapeDtypeStruct((m, n), F32),
                   jax.ShapeDtypeStruct((d, two_hidden), BF16),
                   jax.ShapeDtypeStruct((hidden, d), BF16)],
        compiler_params=_cparams(("arbitrary", "arbitrary")),
        name="inproj",
    )(a, wt, w_ffn_in, w_ffn_out)


def _outproj_kernel(ya_ref, yb_ref, ym_ref, yw_ref, w_ref, x_ref, mod_ref, g_ref, xo_ref, h_ref):
    gw = ya_ref.shape[1]
    acc = _dot(ya_ref[...], w_ref[0, 0:gw, :])
    acc += _dot(yb_ref[...], w_ref[0, gw:2 * gw, :])
    acc += _dot(ym_ref[...], w_ref[0, 2 * gw:3 * gw, :])
    acc += _dot(yw_ref[...], w_ref[0, 3 * gw:4 * gw, :])
    mod = mod_ref[0]
    xn = x_ref[...] + mod[2:3] * acc
    xo_ref[...] = xn
    h = _rms(xn, g_ref[...])
    h_ref[...] = (h * (1.0 + mod[4:5]) + mod[3:4]).astype(BF16)


def _outproj_call(ys, w, layer, x, mod, gain, rows_per_mod, tm):
    m, d = x.shape
    gw = ys[0].shape[1]
    per = rows_per_mod // tm
    yspec = pl.BlockSpec((tm, gw), lambda i: (i, 0))
    return pl.pallas_call(
        _outproj_kernel,
        grid=(m // tm,),
        in_specs=[yspec, yspec, yspec, yspec,
                  pl.BlockSpec((1, 4 * gw, d), lambda i: (layer, 0, 0)),
                  pl.BlockSpec((tm, d), lambda i: (i, 0)),
                  pl.BlockSpec((1, 8, d), lambda i: (i // per, 0, 0)),
                  pl.BlockSpec((1, d), lambda i: (0, 0))],
        out_specs=[pl.BlockSpec((tm, d), lambda i: (i, 0)),
                   pl.BlockSpec((tm, d), lambda i: (i, 0))],
        out_shape=[jax.ShapeDtypeStruct((m, d), F32), jax.ShapeDtypeStruct((m, d), BF16)],
        compiler_params=_cparams(("arbitrary",)),
        name="outproj",
    )(*ys, w, x, mod, gain)


def _ffn_kernel(h_ref, wg_ref, wu_ref, w2_ref, x_ref, mod_ref, modn_ref, gn_ref, xo_ref, hn_ref, acc_ref,
                *, with_next):
    j = pl.program_id(1)

    @pl.when(j == 0)
    def _():
        acc_ref[...] = jnp.zeros_like(acc_ref)

    h = h_ref[...]
    act = _silu(_dot(h, wg_ref[0])) * _dot(h, wu_ref[0])
    acc_ref[...] += _dot(act.astype(BF16), w2_ref[0])

    @pl.when(j == pl.num_programs(1) - 1)
    def _():
        xn = x_ref[...] + mod_ref[0][5:6] * acc_ref[...]
        xo_ref[...] = xn
        if with_next:
            modn = modn_ref[0]
            hn = _rms(xn, gn_ref[...])
            hn_ref[...] = (hn * (1.0 + modn[1:2]) + modn[0:1]).astype(BF16)
        else:
            hn_ref[...] = jnp.zeros_like(hn_ref)


def _ffn_call(h, w_in, w_out, layer, x, mod, mod_next, gain_next, rows_per_mod, tm, tn, with_next):
    m, d = x.shape
    hidden = w_out.shape[1]
    nj = hidden // tn
    per = rows_per_mod // tm
    return pl.pallas_call(
        functools.partial(_ffn_kernel, with_next=with_next),
        grid=(m // tm, nj),
        in_specs=[pl.BlockSpec((tm, d), lambda i, j: (i, 0)),
                  pl.BlockSpec((1, d, tn), lambda i, j: (layer, 0, j)),
                  pl.BlockSpec((1, d, tn), lambda i, j: (layer, 0, nj + j)),
                  pl.BlockSpec((1, tn, d), lambda i, j: (layer, j, 0)),
                  pl.BlockSpec((tm, d), lambda i, j: (i, 0)),
                  pl.BlockSpec((1, 8, d), lambda i, j: (i // per, 0, 0)),
                  pl.BlockSpec((1, 8, d), lambda i, j: (i // per, 0, 0)),
                  pl.BlockSpec((1, d), lambda i, j: (0, 0))],
        out_specs=[pl.BlockSpec((tm, d), lambda i, j: (i, 0)),
                   pl.BlockSpec((tm, d), lambda i, j: (i, 0))],
        out_shape=[jax.ShapeDtypeStruct((m, d), F32), jax.ShapeDtypeStruct((m, d), BF16)],
        scratch_shapes=[pltpu.VMEM((tm, d), F32)],
        compiler_params=_cparams(("arbitrary", "arbitrary")),
        name="ffn",
    )(h, w_in, w_in, w_out, x, mod, mod_next, gain_next)


def _na_kernel(q_ref, k_ref, v_ref, kc_ref, vc_ref, gain_ref, colbias_ref, o_ref,
               kn_scr, v_scr, kcn_scr, vc_scr, bias_scr, sl0_scr, sc0_scr, sl1_scr, sc1_scr, *, rows):
    step = pl.program_id(2)
    n_rb = pl.num_programs(2) - 1
    hd = HEAD_DIM
    heads = q_ref.shape[1] // hd

    @pl.when(step == 0)
    def _():
        sl1_scr[...] = jnp.zeros_like(sl1_scr)
        sc1_scr[...] = jnp.zeros_like(sc1_scr)
        g1 = gain_ref[1:2, :]
        closed = jnp.full((GRID_W, GRID_W), NEG_INF, F32)
        for hh in range(heads):
            cols = slice(hh * hd, (hh + 1) * hd)
            kn_scr[hh] = _rms(k_ref[:, cols], g1).astype(BF16)
            v_scr[hh] = v_ref[:, cols].astype(BF16)
            kcn_scr[hh] = _rms(kc_ref[:, cols], g1).astype(BF16)
            vc_scr[hh] = vc_ref[:, cols].astype(BF16)
            for typ, (offs, inside) in enumerate(_na_row_plan(rows)):
                for i in range(NA_QROWS):
                    for jp in range(NA_KROWS // 2):
                        halves = [colbias_ref[0, hh, offs[i][j]] if inside[i][j] else closed
                                  for j in (2 * jp, 2 * jp + 1)]
                        bias_scr[hh, typ, i * GRID_W:(i + 1) * GRID_W, 2 * jp * GRID_W:(2 * jp + 2) * GRID_W] = (
                            jnp.concatenate(halves, axis=1))

    scale = hd ** -0.5
    nk = NA_KROWS * GRID_W

    def window_start(blk):
        ws = jnp.clip(NA_QROWS * blk - NA_WIN_ROWS // 2, 0, rows - NA_KROWS)
        return pl.multiple_of(ws * GRID_W, GRID_W)

    def stages(write_loc, write_ctx, read_loc, read_ctx):
        prev = jnp.maximum(step - 1, 0)
        cur = jnp.minimum(step, n_rb - 1)
        typ = jnp.where(cur == 0, 0, jnp.where(cur == n_rb - 1, 2, 1))
        for hh in range(heads):
            cols = slice(hh * hd, (hh + 1) * hd)
            s_loc = read_loc[hh]
            s_ctx = read_ctx[hh]
            m = jnp.maximum(jnp.max(s_loc, axis=-1, keepdims=True), jnp.max(s_ctx, axis=-1, keepdims=True))
            p_loc = jnp.exp(s_loc - m)
            p_ctx = jnp.exp(s_ctx - m)
            den = jnp.sum(p_loc, axis=-1, keepdims=True) + jnp.sum(p_ctx, axis=-1, keepdims=True)
            vw = v_scr[hh, pl.ds(window_start(prev), nk), :]
            o = _dot(p_loc.astype(BF16), vw) + _dot(p_ctx.astype(BF16), vc_scr[hh])
            o_ref[:, cols] = (o / den).astype(BF16)
            qn = _rms(q_ref[:, cols], gain_ref[0:1, :]).astype(BF16)
            kw = kn_scr[hh, pl.ds(window_start(cur), nk), :]
            write_loc[hh] = _dot_nt(qn, kw) * scale + bias_scr[hh, pl.ds(typ, 1)][0]
            write_ctx[hh] = _dot_nt(qn, kcn_scr[hh]) * scale

    @pl.when(step % 2 == 0)
    def _():
        stages(sl0_scr, sc0_scr, sl1_scr, sc1_scr)

    @pl.when(step % 2 == 1)
    def _():
        stages(sl1_scr, sc1_scr, sl0_scr, sc0_scr)


def _na_row_plan(rows):
    n_rb = rows // NA_QROWS
    kh = NA_WIN_ROWS
    plan = []
    for rb in (0, 1, n_rb - 1):
        ws = min(max(NA_QROWS * rb - kh // 2, 0), rows - NA_KROWS)
        offs, inside = [], []
        for i in range(NA_QROWS):
            r = NA_QROWS * rb + i
            rs = min(max(r - kh // 2, 0), rows - kh)
            offs.append([min(max(ws + j - r, 1 - kh), kh - 1) + kh - 1 for j in range(NA_KROWS)])
            inside.append([rs <= ws + j < rs + kh for j in range(NA_KROWS)])
        plan.append((offs, inside))
    return plan


def _na_col_tables(rpb):
    kw = NA_WIN_COLS
    c = jnp.arange(GRID_W)
    cs = jnp.clip(c - kw // 2, 0, GRID_W - kw)
    in_col = (c[None, :] >= cs[:, None]) & (c[None, :] < cs[:, None] + kw)
    dc = jnp.clip(c[None, :] - c[:, None], 1 - kw, kw - 1) + kw - 1
    dc_hot = jax.nn.one_hot(dc, 2 * kw - 1, dtype=F32)
    val = jnp.einsum("lhab,ckb->lhack", rpb.astype(F32), dc_hot, precision=lax.Precision.HIGHEST)
    return jnp.where(in_col, val, NEG_INF)


def _na_call(p_lat, p_ctx, gain, colbias, layer, batch, seq, lc):
    rows = seq // GRID_W
    n_rb = rows // NA_QROWS
    tq = NA_QROWS * GRID_W
    nk = NA_KROWS * GRID_W
    heads = colbias.shape[1]
    n_off = colbias.shape[2]
    hd = HEAD_DIM
    hp = NA_HEADS_PER_STEP
    wide = hp * hd
    assert heads % hp == 0 and NA_Q % hp == 0 and NA_K % hp == 0 and NA_V % hp == 0
    return pl.pallas_call(
        functools.partial(_na_kernel, rows=rows),
        grid=(batch, heads // hp, n_rb + 1),
        in_specs=[pl.BlockSpec((tq, wide), lambda b, h, r: (b * n_rb + jnp.minimum(r, n_rb - 1), NA_Q // hp + h)),
                  pl.BlockSpec((seq, wide), lambda b, h, r: (b, NA_K // hp + h)),
                  pl.BlockSpec((seq, wide), lambda b, h, r: (b, NA_V // hp + h)),
                  pl.BlockSpec((lc, wide), lambda b, h, r: (b, NA_K // hp + h)),
                  pl.BlockSpec((lc, wide), lambda b, h, r: (b, NA_V // hp + h)),
                  pl.BlockSpec((2, hd), lambda b, h, r: (0, 0)),
                  pl.BlockSpec((1, hp, n_off, GRID_W, GRID_W), lambda b, h, r: (layer, h, 0, 0, 0))],
        out_specs=pl.BlockSpec((tq, wide), lambda b, h, r: (b * n_rb + jnp.maximum(r - 1, 0), h)),
        out_shape=jax.ShapeDtypeStruct((batch * seq, heads * hd), BF16),
        scratch_shapes=[pltpu.VMEM((hp, seq, hd), BF16), pltpu.VMEM((hp, seq, hd), BF16),
                        pltpu.VMEM((hp, lc, hd), BF16), pltpu.VMEM((hp, lc, hd), BF16),
                        pltpu.VMEM((hp, 3, tq, nk), F32),
                        pltpu.VMEM((hp, tq, nk), F32), pltpu.VMEM((hp, tq, lc), F32),
                        pltpu.VMEM((hp, tq, nk), F32), pltpu.VMEM((hp, tq, lc), F32)],
        compiler_params=_cparams(("arbitrary", "arbitrary", "arbitrary")),
        name="na_attn",
    )(p_lat, p_lat, p_lat, p_ctx, p_ctx, gain, colbias)


def _ctx_attn_kernel(sink_ref, q_ref, k_ref, v_ref, gain_ref, o_ref, *, use_sink):
    scale = HEAD_DIM ** -0.5
    qn = _rms(q_ref[...], gain_ref[0:1, :]).astype(BF16)
    kn = _rms(k_ref[...], gain_ref[1:2, :]).astype(BF16)
    s = _dot_nt(qn, kn) * scale
    m = jnp.max(s, axis=-1, keepdims=True)
    if use_sink:
        sk = sink_ref[pl.program_id(1)]
        m = jnp.maximum(m, sk)
    p = jnp.exp(s - m)
    den = jnp.sum(p, axis=-1, keepdims=True)
    if use_sink:
        den = den + jnp.exp(sk - m)
    o = _dot(p.astype(BF16), v_ref[...].astype(BF16))
    o_ref[...] = (o / den).astype(BF16)


def _ctx_attn_call(p_ctx, gain, sink, batch, lc, heads, rep, qoff, koff, voff, use_sink, name):
    hd = HEAD_DIM
    return pl.pallas_call(
        functools.partial(_ctx_attn_kernel, use_sink=use_sink),
        grid=(batch, heads),
        in_specs=[pl.BlockSpec(memory_space=pltpu.SMEM),
                  pl.BlockSpec((lc, hd), lambda b, h: (b, qoff + h)),
                  pl.BlockSpec((lc, hd), lambda b, h: (b, koff + h // rep)),
                  pl.BlockSpec((lc, hd), lambda b, h: (b, voff + h // rep)),
                  pl.BlockSpec((2, hd), lambda b, h: (0, 0))],
        out_specs=pl.BlockSpec((lc, hd), lambda b, h: (b, h)),
        out_shape=jax.ShapeDtypeStruct((batch * lc, heads * hd), BF16),
        compiler_params=_cparams(("arbitrary", "arbitrary")),
        name=name,
    )(sink, p_ctx, p_ctx, p_ctx, gain)


def _rope(x, cos, sin_signed):
    lane = _iota(x.shape, 1)
    swapped = jnp.where(lane % 64 < 32, pltpu.roll(x, 96, axis=1), pltpu.roll(x, 32, axis=1))
    return x * cos + swapped * sin_signed


def _wa_kernel(sink_ref, q_ref, k_ref, v_ref, kc_ref, vc_ref, gain_ref, cq_ref, sq_ref, ck_ref, sk_ref, o_ref,
               kpad_scr, vpad_scr, kcn_scr, vc_scr, sl0_scr, sc0_scr, sl1_scr, sc1_scr, *, seq):
    step = pl.program_id(1)
    nb = pl.num_programs(1) - 1
    wb = WA_BLOCK
    hd = HEAD_DIM
    groups = k_ref.shape[1] // hd

    @pl.when(step == 0)
    def _():
        sl1_scr[...] = jnp.zeros_like(sl1_scr)
        sc1_scr[...] = jnp.zeros_like(sc1_scr)
        g1 = gain_ref[1:2, :]
        zeros = jnp.zeros((wb, hd), BF16)
        for g in range(groups):
            cols = slice(g * hd, (g + 1) * hd)
            kpad_scr[g, 0:wb, :] = zeros
            kpad_scr[g, seq + wb:seq + 2 * wb, :] = zeros
            vpad_scr[g, 0:wb, :] = zeros
            vpad_scr[g, seq + wb:seq + 2 * wb, :] = zeros
            kpad_scr[g, wb:seq + wb, :] = _rope(_rms(k_ref[:, cols], g1), ck_ref[...], sk_ref[...]).astype(BF16)
            vpad_scr[g, wb:seq + wb, :] = v_ref[:, cols].astype(BF16)
            kcn_scr[g] = _rms(kc_ref[:, cols], g1).astype(BF16)
            vc_scr[g] = vc_ref[:, cols].astype(BF16)

    scale = hd ** -0.5

    def stages(write_loc, write_ctx, read_loc, read_ctx):
        prev = jnp.maximum(step - 1, 0)
        cur = jnp.minimum(step, nb - 1)
        g0 = gain_ref[0:1, :]
        cq = cq_ref[...]
        sq = sq_ref[...]
        shp = (2 * wb, 3 * wb)
        i = _iota(shp, 0) % wb
        j = _iota(shp, 1)
        kpos = (cur - 1) * wb + j
        ok = (j - i >= 0) & (j - i <= 2 * wb) & (kpos >= 0) & (kpos < seq)
        for g in range(groups):
            s_loc = read_loc[g]
            s_ctx = read_ctx[g]
            sink = jnp.where(_iota((2 * wb, 1), 0) < wb, sink_ref[2 * g], sink_ref[2 * g + 1])
            m = jnp.maximum(jnp.maximum(jnp.max(s_loc, axis=-1, keepdims=True),
                                        jnp.max(s_ctx, axis=-1, keepdims=True)), sink)
            p_loc = jnp.exp(s_loc - m)
            p_ctx = jnp.exp(s_ctx - m)
            den = jnp.sum(p_loc, axis=-1, keepdims=True) + jnp.sum(p_ctx, axis=-1, keepdims=True) + jnp.exp(sink - m)
            vw = vpad_scr[g, pl.ds(pl.multiple_of(prev * wb, wb), 3 * wb), :]
            o = (_dot(p_loc.astype(BF16), vw) + _dot(p_ctx.astype(BF16), vc_scr[g])) / den
            o_ref[:, 2 * g * hd:(2 * g + 1) * hd] = o[0:wb].astype(BF16)
            o_ref[:, (2 * g + 1) * hd:(2 * g + 2) * hd] = o[wb:2 * wb].astype(BF16)
            q2 = jnp.concatenate([_rope(_rms(q_ref[:, 2 * g * hd:(2 * g + 1) * hd], g0), cq, sq),
                                  _rope(_rms(q_ref[:, (2 * g + 1) * hd:(2 * g + 2) * hd], g0), cq, sq)],
                                 axis=0).astype(BF16)
            kw = kpad_scr[g, pl.ds(pl.multiple_of(cur * wb, wb), 3 * wb), :]
            write_loc[g] = jnp.where(ok, _dot_nt(q2, kw) * scale, NEG_INF)
            write_ctx[g] = _dot_nt(q2, kcn_scr[g]) * scale

    @pl.when(step % 2 == 0)
    def _():
        stages(sl0_scr, sc0_scr, sl1_scr, sc1_scr)

    @pl.when(step % 2 == 1)
    def _():
        stages(sl1_scr, sc1_scr, sl0_scr, sc0_scr)


def _wa_call(p_lat, p_ctx, gain, sink, cos_t, sin_t, batch, seq, lc):
    wb = WA_BLOCK
    hd = HEAD_DIM
    nb = seq // wb
    kvh = 2
    qw = 2 * kvh * hd
    kw_ = kvh * hd
    assert (WA_Q * hd) % qw == 0 and (WA_K * hd) % kw_ == 0 and (WA_V * hd) % kw_ == 0
    return pl.pallas_call(
        functools.partial(_wa_kernel, seq=seq),
        grid=(batch, nb + 1),
        in_specs=[pl.BlockSpec(memory_space=pltpu.SMEM),
                  pl.BlockSpec((wb, qw), lambda b, n: (b * nb + jnp.minimum(n, nb - 1), WA_Q * hd // qw)),
                  pl.BlockSpec((seq, kw_), lambda b, n: (b, WA_K * hd // kw_)),
                  pl.BlockSpec((seq, kw_), lambda b, n: (b, WA_V * hd // kw_)),
                  pl.BlockSpec((lc, kw_), lambda b, n: (b, WA_K * hd // kw_)),
                  pl.BlockSpec((lc, kw_), lambda b, n: (b, WA_V * hd // kw_)),
                  pl.BlockSpec((2, hd), lambda b, n: (0, 0)),
                  pl.BlockSpec((wb, hd), lambda b, n: (jnp.minimum(n, nb - 1), 0)),
                  pl.BlockSpec((wb, hd), lambda b, n: (jnp.minimum(n, nb - 1), 0)),
                  pl.BlockSpec((seq, hd), lambda b, n: (0, 0)),
                  pl.BlockSpec((seq, hd), lambda b, n: (0, 0))],
        out_specs=pl.BlockSpec((wb, qw), lambda b, n: (b * nb + jnp.maximum(n - 1, 0), 0)),
        out_shape=jax.ShapeDtypeStruct((batch * seq, qw), BF16),
        scratch_shapes=[pltpu.VMEM((kvh, seq + 2 * wb, hd), BF16), pltpu.VMEM((kvh, seq + 2 * wb, hd), BF16),
                        pltpu.VMEM((kvh, lc, hd), BF16), pltpu.VMEM((kvh, lc, hd), BF16),
                        pltpu.VMEM((kvh, 2 * wb, 3 * wb), F32), pltpu.VMEM((kvh, 2 * wb, lc), F32),
                        pltpu.VMEM((kvh, 2 * wb, 3 * wb), F32), pltpu.VMEM((kvh, 2 * wb, lc), F32)],
        compiler_params=_cparams(("arbitrary", "arbitrary")),
        name="wa_attn",
    )(sink, p_lat, p_lat, p_lat, p_ctx, p_ctx, gain, cos_t, sin_t, cos_t, sin_t)


def _dn_prep_kernel(x_ref, prev_ref, next_ref, w_ref, o_ref, *, n_tiles):
    t = pl.program_id(1)
    tl = x_ref.shape[0]
    x = x_ref[...]
    prev = jnp.where(t > 0, prev_ref[...], 0.0)
    nxt = jnp.where(t < n_tiles - 1, next_ref[...], 0.0)
    xe = jnp.concatenate([prev, x, nxt], axis=0)
    n = tl + 16
    taps = w_ref.shape[0]
    acc = jnp.zeros_like(x)
    for j in range(taps):
        off = 8 - taps // 2 + j
        shifted = xe if off % n == 0 else pltpu.roll(xe, n - off, axis=0)
        acc += shifted[0:tl] * w_ref[j:j + 1, :]
    y = _silu(acc)
    hd = HEAD_DIM
    nh = y.shape[1] // (3 * hd)
    for blk in range(3 * nh):
        yb = y[:, blk * hd:(blk + 1) * hd]
        if blk < 2 * nh:
            yb = yb * lax.rsqrt(jnp.sum(yb * yb, axis=-1, keepdims=True) + EPS)
            if blk < nh:
                yb = yb * hd ** -0.5
        o_ref[:, blk * hd:(blk + 1) * hd] = yb


def _dn_prep_call(p, conv_w, batch, length, tl):
    width = conv_w.shape[1]
    n_tiles = length // tl
    per8 = tl // 8
    last8 = batch * length // 8 - 1
    return pl.pallas_call(
        functools.partial(_dn_prep_kernel, n_tiles=n_tiles),
        grid=(batch, n_tiles),
        in_specs=[pl.BlockSpec((tl, width), lambda b, t: (b * n_tiles + t, 1)),
                  pl.BlockSpec((8, width), lambda b, t: (jnp.maximum((b * n_tiles + t) * per8 - 1, 0), 1)),
                  pl.BlockSpec((8, width), lambda b, t: (jnp.minimum((b * n_tiles + t + 1) * per8, last8), 1)),
                  pl.BlockSpec((conv_w.shape[0], width), lambda b, t: (0, 0))],
        out_specs=pl.BlockSpec((tl, width), lambda b, t: (b * n_tiles + t, 0)),
        out_shape=jax.ShapeDtypeStruct((batch * length, width), F32),
        compiler_params=_cparams(("arbitrary", "arbitrary")),
        name="dn_prep",
    )(p, p, p, conv_w)


def _solve_dot(a, b):
    return _dot(a.astype(BF16), b.astype(BF16))


def _dn_pass(q_ref, k_ref, v_ref, og_ref, gates_ref, gadd, gmul, norm_g, y_ref, length, h, state,
             lin_scr, add_scr, egl_scr, acc_scr):
    assert CHUNK == HEAD_DIM
    n_chunks = length // CHUNK

    acc_scr[0:length, :] = jnp.zeros((length, HEAD_DIM), F32)

    shp = (CHUNK, CHUNK)
    ri = _iota(shp, 0)
    ci = _iota(shp, 1)
    masks = ((ci <= ri, ci < ri), (ci >= ri, ci > ri))
    couple = ([], [])
    s = 1
    while s < CHUNK:
        same = (ri // (2 * s)) == (ci // (2 * s))
        couple[0].append(same & ((ri // s) % 2 == 1) & ((ci // s) % 2 == 0))
        couple[1].append(same & ((ri // s) % 2 == 0) & ((ci // s) % 2 == 1))
        s *= 2
    group = min(n_chunks, PREP_GROUP)

    def prep(gi, carry):
        chains = []
        for g in range(group):
            c = gi * group + g
            r0 = pl.multiple_of(c * CHUNK, CHUNK)
            q = q_ref[pl.ds(r0, CHUNK), :]
            k = k_ref[pl.ds(r0, CHUNK), :]
            raw = gates_ref[pl.ds(r0, CHUNK), :]
            gb = jnp.where(_iota(raw.shape, 1) < LANE_ALPHA, jax.nn.sigmoid(raw), -gmul * _softplus(raw + gadd))
            k16 = k.astype(BF16)
            kk = _dot_nt(k16, k16)
            qk = _dot_nt(q.astype(BF16), k16)
            for d in range(2):
                incl, strict = masks[d]
                beta_col = _lane_pick(gb, LANE_BETA + 4 * d + h)
                g_col = _lane_pick(gb, LANE_ALPHA + 4 * d + h)
                beta_row = _to_row(beta_col)
                g_row = _to_row(g_col)
                cum_col = jnp.sum(jnp.where(incl, g_row, 0.0), axis=-1, keepdims=True)
                cum_row = _to_row(cum_col)
                tot = jnp.sum(g_col, axis=0, keepdims=True)
                decay = jnp.exp(jnp.where(incl, cum_col - cum_row, NEG_INF))
                lm = jnp.where(strict, kk * beta_col * decay, 0.0)
                egl_scr[d, pl.ds(pl.multiple_of(c * 8, 8), 8), :] = jnp.broadcast_to(jnp.exp(tot), (8, HEAD_DIM))
                chains.append((d, c, r0, lm, beta_row, beta_row * jnp.exp(cum_row), (qk * decay).astype(BF16),
                               q * jnp.exp(cum_col), k * jnp.exp(tot - cum_col)))
        eye = (ri == ci).astype(F32)
        xs = [eye - jnp.where(couple[ch[0]][0], ch[3], 0.0) for ch in chains]
        for lvl in range(1, len(couple[0])):
            cs = [jnp.where(couple[ch[0]][lvl], ch[3], 0.0) for ch in chains]
            xc = [_solve_dot(x, c) for x, c in zip(xs, cs)]
            xs = [x - _solve_dot(t, x) for x, t in zip(xs, xc)]
        u16 = [_solve_dot(ainv * ch[4], v_ref[pl.ds(ch[2], CHUNK), :]).astype(BF16) for ch, ainv in zip(chains, xs)]
        w16 = [_solve_dot(ainv * ch[5], k_ref[pl.ds(ch[2], CHUNK), :]).astype(BF16) for ch, ainv in zip(chains, xs)]
        ket = [ch[8].T.astype(BF16) for ch in chains]
        s_lin = [_dot(kt, w) for kt, w in zip(ket, w16)]
        s_add = [_dot(kt, u) for kt, u in zip(ket, u16)]
        o_lin = [ch[7] - _dot(ch[6], w) for ch, w in zip(chains, w16)]
        o_add = [_dot(ch[6], u) for ch, u in zip(chains, u16)]
        for ch, sl, sa, ol, oa in zip(chains, s_lin, s_add, o_lin, o_add):
            d = ch[0]
            base = pl.multiple_of(ch[1] * 2 * CHUNK, 2 * CHUNK)
            lin_scr[d, pl.ds(base, CHUNK), :] = sl.astype(BF16)
            lin_scr[d, pl.ds(base + CHUNK, CHUNK), :] = ol.astype(BF16)
            add_scr[d, pl.ds(base, CHUNK), :] = sa
            add_scr[d, pl.ds(base + CHUNK, CHUNK), :] = oa
        return carry

    lax.fori_loop(0, n_chunks // group, prep, 0)

    def scan(s, st):
        new = []
        for d in range(2):
            c = s if d == 0 else n_chunks - 1 - s
            base = pl.multiple_of(c * 2 * CHUNK, 2 * CHUNK)
            sm = st[d]
            prod = _dot(lin_scr[d, pl.ds(base, 2 * CHUNK), :], sm.astype(BF16))
            add = add_scr[d, pl.ds(base, 2 * CHUNK), :]
            acc_scr[pl.ds(pl.multiple_of(c * CHUNK, CHUNK), CHUNK), :] += prod[CHUNK:] + add[CHUNK:]
            egl = egl_scr[d, pl.ds(pl.multiple_of(c * 8, 8), 8), :][0:1, :]
            new.append(sm * egl - prod[:CHUNK] + add[:CHUNK])
        return tuple(new)

    state = lax.fori_loop(0, n_chunks, scan, state, unroll=SCAN_UNROLL)

    y = _rms(acc_scr[0:length, :], norm_g) * _silu(og_ref[...])
    y_ref[...] = y.astype(BF16)
    return state


def _dn_kernel(q_ref, k_ref, v_ref, og_ref, gates_ref, qc_ref, kc_ref, vc_ref, ogc_ref, gatesc_ref,
               gadd_ref, gmul_ref, norm_ref, y_ref, yc_ref,
               lin_scr, add_scr, egl_scr, acc_scr, *, seq, lc):
    h = pl.program_id(1)
    gadd = gadd_ref[...]
    gmul = gmul_ref[...]
    norm_g = norm_ref[...]
    scr = (lin_scr, add_scr, egl_scr, acc_scr)
    zero = jnp.zeros((HEAD_DIM, HEAD_DIM), F32)
    state = _dn_pass(qc_ref, kc_ref, vc_ref, ogc_ref, gatesc_ref, gadd, gmul, norm_g, yc_ref, lc, h, (zero, zero), *scr)
    _dn_pass(q_ref, k_ref, v_ref, og_ref, gates_ref, gadd, gmul, norm_g, y_ref, seq, h, state, *scr)


def _dn_call(dq_lat, dq_ctx, p_lat, p_ctx, pg_lat, pg_ctx, gadd, gmul, norm_g, batch, seq, lc):
    hd = HEAD_DIM
    heads = 4

    def blk(n, col):
        return pl.BlockSpec((n, hd), lambda b, h: (b, col + h))

    def gate_blk(n):
        return pl.BlockSpec((n, hd), lambda b, h: (b, 0))

    vec = pl.BlockSpec((1, hd), lambda b, h: (0, 0))
    return pl.pallas_call(
        functools.partial(_dn_kernel, seq=seq, lc=lc),
        grid=(batch, heads),
        in_specs=[blk(seq, 0), blk(seq, heads), blk(seq, 2 * heads), blk(seq, DN_OG), gate_blk(seq),
                  blk(lc, 0), blk(lc, heads), blk(lc, 2 * heads), blk(lc, DN_OG), gate_blk(lc),
                  vec, vec, vec],
        out_specs=[pl.BlockSpec((seq, hd), lambda b, h: (b, h)),
                   pl.BlockSpec((lc, hd), lambda b, h: (b, h))],
        out_shape=[jax.ShapeDtypeStruct((batch * seq, heads * hd), BF16),
                   jax.ShapeDtypeStruct((batch * lc, heads * hd), BF16)],
        scratch_shapes=[pltpu.VMEM((2, 2 * seq, hd), BF16),
                        pltpu.VMEM((2, 2 * seq, hd), F32),
                        pltpu.VMEM((2, seq // CHUNK * 8, hd), F32),
                        pltpu.VMEM((seq, hd), F32)],
        compiler_params=_cparams(("arbitrary", "arbitrary"), DN_VMEM_LIMIT),
        name="deltanet",
    )(dq_lat, dq_lat, dq_lat, p_lat, pg_lat, dq_ctx, dq_ctx, dq_ctx, p_ctx, pg_ctx, gadd, gmul, norm_g)


def _ml_pass(q_ref, k_ref, v_ref, og_ref, gates_ref, gadd, norm_g, y_ref, length, h, state, acc_scr):
    n_chunks = length // CHUNK
    acc_scr[0:length, :] = jnp.zeros((length, HEAD_DIM), F32)
    shp = (CHUNK, CHUNK)
    ri = _iota(shp, 0)
    ci = _iota(shp, 1)
    masks = (ci <= ri, ci >= ri)
    head_lanes = (_iota((1, HEAD_DIM), 1) // ML_QK) == (h % 2)

    group = min(n_chunks, ML_GROUP)

    def each(f, *lists):
        return [f(*a) for a in zip(*lists)]

    def step(gi, st):
        dirs, r0s = [], []
        for j in range(group):
            s = gi * group + j
            for d in range(2):
                c = s if d == 0 else n_chunks - 1 - s
                dirs.append(d)
                r0s.append(pl.multiple_of(c * CHUNK, CHUNK))
        incl = [masks[d] for d in dirs]
        q = [jnp.where(head_lanes, q_ref[pl.ds(r0, CHUNK), :], 0.0) for r0 in r0s]
        k = [jnp.where(head_lanes, k_ref[pl.ds(r0, CHUNK), :], 0.0) * ML_QK ** -0.5 for r0 in r0s]
        v16 = [v_ref[pl.ds(r0, CHUNK), :].astype(BF16) for r0 in r0s]
        gb = [gates_ref[pl.ds(r0, CHUNK), :] + gadd for r0 in r0s]
        ig_col = [_lane_pick(g, LANE_IG + 4 * d + h) for g, d in zip(gb, dirs)]
        lf_col = [_log_sigmoid(_lane_pick(g, LANE_FG + 4 * d + h)) for g, d in zip(gb, dirs)]
        ig_row = each(_to_row, ig_col)
        lf_row = each(_to_row, lf_col)
        b_col = each(lambda m, r: jnp.sum(jnp.where(m, r, 0.0), axis=-1, keepdims=True), incl, lf_row)
        b_row = each(_to_row, b_col)
        b_last = each(lambda x: jnp.sum(x, axis=0, keepdims=True), lf_col)
        g_end = each(lambda bl, b, i: bl - b + i, b_last, b_col, ig_col)
        m_loc = each(lambda x: jnp.max(x, axis=0, keepdims=True), g_end)
        kw = each(lambda kk, g, m: kk * jnp.exp(g - m), k, g_end, m_loc)
        q16 = each(lambda x: x.astype(BF16), q)
        k16 = each(lambda x: x.astype(BF16), k)
        c_loc = each(lambda a, b: _dot_tn(a.astype(BF16), b), kw, v16)
        n_loc = each(lambda x: jnp.sum(x, axis=0, keepdims=True), kw)
        log_d = each(lambda m, bc, br, ir: jnp.where(m, bc - br + ir, NEG_INF), incl, b_col, b_row, ig_row)
        mx_col = each(lambda x: jnp.max(x, axis=-1, keepdims=True), log_d)
        s0 = each(lambda a, b, ld, mx: _dot_nt(a, b) * jnp.exp(ld - mx), q16, k16, log_d, mx_col)
        sv0 = each(lambda s, vv: _dot(s.astype(BF16), vv), s0, v16)
        rs0 = each(lambda s: jnp.sum(s, axis=-1, keepdims=True), s0)
        st = list(st)
        for i, (d, r0) in enumerate(zip(dirs, r0s)):
            c_st, n_st, m_st = st[d]
            m_inter = b_col[i] + m_st
            m_t = jnp.maximum(mx_col[i], m_inter)
            intra = jnp.exp(mx_col[i] - m_t)
            inter = jnp.exp(m_inter - m_t)
            num = intra * sv0[i] + inter * _dot(q16[i], c_st.astype(BF16))
            den = intra * rs0[i] + inter * jnp.sum(q[i] * n_st, axis=-1, keepdims=True)
            acc_scr[pl.ds(r0, CHUNK), :] += num / jnp.maximum(jnp.abs(den), jnp.exp(-m_t))
            m_new = jnp.maximum(b_last[i] + m_st, m_loc[i])
            a = jnp.exp(b_last[i] + m_st - m_new)
            e = jnp.exp(m_loc[i] - m_new)
            st[d] = (a * c_st + e * c_loc[i], a * n_st + e * n_loc[i], m_new)
        return tuple(st)

    state = lax.fori_loop(0, n_chunks // group, step, state)
    y = jax.nn.sigmoid(og_ref[...]) * _rms(acc_scr[0:length, :], norm_g)
    y_ref[...] = y.astype(BF16)
    return state


def _ml_kernel(q_ref, k_ref, v_ref, og_ref, gates_ref, qc_ref, kc_ref, vc_ref, ogc_ref, gatesc_ref,
               gadd_ref, norm_ref, y_ref, yc_ref, acc_scr, *, seq, lc):
    h = pl.program_id(1)
    gadd = gadd_ref[...]
    norm_g = norm_ref[0]
    zero = (jnp.zeros((HEAD_DIM, HEAD_DIM), F32), jnp.zeros((1, HEAD_DIM), F32), jnp.zeros((1, 1), F32))
    state = _ml_pass(qc_ref, kc_ref, vc_ref, ogc_ref, gatesc_ref, gadd, norm_g, yc_ref, lc, h, (zero, zero), acc_scr)
    _ml_pass(q_ref, k_ref, v_ref, og_ref, gates_ref, gadd, norm_g, y_ref, seq, h, state, acc_scr)


def _ml_call(p_lat, p_ctx, pg_lat, pg_ctx, gadd, norm_g, batch, seq, lc):
    hd = HEAD_DIM
    heads = 4

    def specs(n):
        return [pl.BlockSpec((n, hd), lambda b, h: (b, ML_Q + h // 2)),
                pl.BlockSpec((n, hd), lambda b, h: (b, ML_K + h // 2)),
                pl.BlockSpec((n, hd), lambda b, h: (b, ML_V + h)),
                pl.BlockSpec((n, hd), lambda b, h: (b, ML_OG + h)),
                pl.BlockSpec((n, hd), lambda b, h: (b, 0))]

    return pl.pallas_call(
        functools.partial(_ml_kernel, seq=seq, lc=lc),
        grid=(batch, heads),
        in_specs=specs(seq) + specs(lc) + [pl.BlockSpec((1, hd), lambda b, h: (0, 0)),
                                            pl.BlockSpec((1, 1, hd), lambda b, h: (h, 0, 0))],
        out_specs=[pl.BlockSpec((seq, hd), lambda b, h: (b, h)),
                   pl.BlockSpec((lc, hd), lambda b, h: (b, h))],
        out_shape=[jax.ShapeDtypeStruct((batch * seq, heads * hd), BF16),
                   jax.ShapeDtypeStruct((batch * lc, heads * hd), BF16)],
        scratch_shapes=[pltpu.VMEM((seq, hd), F32)],
        compiler_params=_cparams(("arbitrary", "arbitrary")),
        name="mlstm",
    )(p_lat, p_lat, p_lat, p_lat, pg_lat, p_ctx, p_ctx, p_ctx, p_ctx, pg_ctx, gadd, norm_g.reshape(heads, 1, hd))


def _regroup_kernel(wt_ref, main_ref, gate_ref, *, cuts):
    (a0, a1), (b0, b1), (c0, c1), (g0, g1), (h0, h1) = cuts
    main_ref[0, 0:a1 - a0, :] = wt_ref[0, a0:a1, :].astype(BF16)
    main_ref[0, a1 - a0:a1 - a0 + b1 - b0, :] = wt_ref[0, b0:b1, :].astype(BF16)
    main_ref[0, a1 - a0 + b1 - b0:, :] = wt_ref[0, c0:c1, :].astype(BF16)
    ng = (g1 - g0) + (h1 - h0)
    gate_ref[0, 0:g1 - g0, :] = wt_ref[0, g0:g1, :].astype(BF16)
    gate_ref[0, g1 - g0:ng, :] = wt_ref[0, h0:h1, :].astype(BF16)
    gate_ref[0, ng:, :] = jnp.zeros((gate_ref.shape[1] - ng, gate_ref.shape[2]), BF16)


def _regroup_call(w_in):
    depth, d, n = w_in.shape
    gw = 512
    dn_gate0 = 3 * gw + 4 * gw
    ml0 = dn_gate0 + 16
    ml_gate0 = ml0 + 3 * gw
    wa0 = ml_gate0 + 16
    cuts = ((0, dn_gate0), (ml0, ml_gate0), (wa0, n), (dn_gate0, ml0), (ml_gate0, wa0))
    tc = 256
    return pl.pallas_call(
        functools.partial(_regroup_kernel, cuts=cuts),
        grid=(depth, d // tc),
        in_specs=[pl.BlockSpec((1, n, tc), lambda l, i: (l, 0, i))],
        out_specs=[pl.BlockSpec((1, MAIN_COLS, tc), lambda l, i: (l, 0, i)),
                   pl.BlockSpec((1, 128, tc), lambda l, i: (l, 0, i))],
        out_shape=[jax.ShapeDtypeStruct((depth, MAIN_COLS, d), BF16), jax.ShapeDtypeStruct((depth, 128, d), BF16)],
        compiler_params=_cparams(("arbitrary", "arbitrary")),
        name="regroup_w_in",
    )(jnp.swapaxes(w_in, 1, 2))


def _rope_tables(seq):
    t = jnp.arange(seq)
    n_freq = HEAD_DIM // 4
    inv_freq = ROPE_THETA ** (-jnp.arange(n_freq, dtype=F32) / n_freq)
    pos = jnp.stack([t // GRID_W, t % GRID_W], axis=-1).astype(F32)
    ang = pos[:, :, None] * inv_freq
    cos, sin = jnp.cos(ang), jnp.sin(ang)
    cos_t = jnp.concatenate([cos, cos], axis=-1).reshape(seq, HEAD_DIM)
    sin_t = jnp.concatenate([-sin, sin], axis=-1).reshape(seq, HEAD_DIM)
    return cos_t, sin_t


def _gate_vectors(dt_bias, a_log, i_bias, f_bias):
    zeros8 = jnp.zeros((8,), F32)
    gadd = jnp.concatenate([zeros8, dt_bias.reshape(-1), i_bias.reshape(-1), f_bias.reshape(-1),
                            jnp.zeros((96,), F32)]).astype(F32)
    gmul = jnp.concatenate([zeros8, jnp.exp(a_log.astype(F32)).reshape(-1), jnp.zeros((112,), F32)])
    return gadd.reshape(1, 128), gmul.reshape(1, 128)


def _forward(x, c, ctx, c_ctx, w_ada, b_ada, norm_mix, norm_ffn, w_in, w_out, na_qk_gain, na_rpb, dn_conv,
             dn_a_log, dn_dt_bias, dn_norm, ml_i_bias, ml_f_bias, ml_norm, wa_qk_gain, wa_sink, w_ffn_in,
             w_ffn_out):
    batch, seq, d = x.shape
    lc = ctx.shape[1]
    depth = w_ada.shape[0]
    tm = 512
    tn_ffn = 512

    cos_t, sin_t = _rope_tables(seq)
    na_bias = _na_col_tables(na_rpb)
    cs = jnp.concatenate([c, c_ctx[None, :], jnp.zeros((8 - batch - 1, d), F32)], axis=0)
    mods = _ada_call(cs, w_ada, b_ada).reshape(depth, 8, 6, d)
    mods = jnp.pad(mods, ((0, 0), (0, 0), (0, 2), (0, 0)))

    w_main, w_gate = _regroup_call(w_in)
    w_o = w_out.astype(BF16)

    xl = x.reshape(batch * seq, d)
    xc = ctx.reshape(batch * lc, d)
    hl = _norm_call(xl, mods[0, :batch], norm_mix[0:1], seq, tm)
    hc = _norm_call(xc, mods[0, batch:batch + 1], norm_mix[0:1], batch * lc, tm)

    for l in range(depth):
        need_ctx = l < depth - 1
        mod_l, mod_c = mods[l, :batch], mods[l, batch:batch + 1]
        p_lat, w_f1, w_f2 = _inproj_cast_call(hl, w_main, l, w_ffn_in, w_ffn_out, 2 * tm, 1024)
        w_f1 = w_f1[None]
        w_f2 = w_f2[None]
        pg_lat = _matmul_call(hl, w_gate, l, 2 * tm, 128, "inproj_gates")
        p_ctx = _matmul_call(hc, w_main, l, tm, 2048, "inproj_ctx")
        pg_ctx = _matmul_call(hc, w_gate, l, tm, 128, "inproj_gates_ctx")

        ya = _na_call(p_lat, p_ctx, na_qk_gain[l], na_bias, l, batch, seq, lc)
        gadd, gmul = _gate_vectors(dn_dt_bias[l], dn_a_log[l], ml_i_bias[l], ml_f_bias[l])
        dq_lat = _dn_prep_call(p_lat, dn_conv[l], batch, seq, 512)
        dq_ctx = _dn_prep_call(p_ctx, dn_conv[l], batch, lc, 256)
        yb, ycb = _dn_call(dq_lat, dq_ctx, p_lat, p_ctx, pg_lat, pg_ctx, gadd, gmul, dn_norm[l].reshape(1, -1),
                           batch, seq, lc)
        ym, ycm = _ml_call(p_lat, p_ctx, pg_lat, pg_ctx, gadd, ml_norm[l], batch, seq, lc)
        yw = _wa_call(p_lat, p_ctx, wa_qk_gain[l], wa_sink[l], cos_t, sin_t, batch, seq, lc)

        nl = min(l + 1, depth - 1)
        xl, h2 = _outproj_call((ya, yb, ym, yw), w_o, l, xl, mod_l, norm_ffn[l:l + 1], seq, tm)
        xl, hl = _ffn_call(h2, w_f1, w_f2, 0, xl, mod_l, mods[nl, :batch], norm_mix[nl:nl + 1], seq, tm, tn_ffn,
                           need_ctx)
        if need_ctx:
            yca = _ctx_attn_call(p_ctx, na_qk_gain[l], wa_sink[l], batch, lc, 4, 1, NA_Q, NA_K, NA_V, False,
                                 "na_ctx_attn")
            ycw = _ctx_attn_call(p_ctx, wa_qk_gain[l], wa_sink[l], batch, lc, 4, 2, WA_Q, WA_K, WA_V, True,
                                 "wa_ctx_attn")
            xc, h2c = _outproj_call((yca, ycb, ycm, ycw), w_o, l, xc, mod_c, norm_ffn[l:l + 1], batch * lc, tm)
            xc, hc = _ffn_call(h2c, w_f1, w_f2, 0, xc, mod_c, mods[nl, batch:batch + 1], norm_mix[nl:nl + 1],
                               batch * lc, tm, tn_ffn, True)
    return xl.reshape(batch, seq, d)


def kernel(x, c, ctx, c_ctx, w_ada, b_ada, norm_mix, norm_ffn, w_in, w_out, na_qk_gain, na_rpb, dn_conv, dn_a_log,
           dn_dt_bias, dn_norm, ml_i_bias, ml_f_bias, ml_norm, wa_qk_gain, wa_sink, w_ffn_in, w_ffn_out):
    return _forward(x, c, ctx, c_ctx, w_ada, b_ada, norm_mix, norm_ffn, w_in, w_out, na_qk_gain, na_rpb, dn_conv,
                    dn_a_log, dn_dt_bias, dn_norm, ml_i_bias, ml_f_bias, ml_norm, wa_qk_gain, wa_sink, w_ffn_in,
                    w_ffn_out)
```

```python
import functools

import jax
import jax.numpy as jnp
from jax import lax
from jax.experimental import pallas as pl
from jax.experimental.pallas import tpu as pltpu

F32 = jnp.float32
BF16 = jnp.bfloat16

HEAD_DIM = 128
GRID_W = 64
CHUNK = 128
FFN_CAST_SLABS = 44
PREP_GROUP = 8
ML_GROUP = 4
SCAN_UNROLL = 2
NA_WIN_ROWS = 8
NA_WIN_COLS = 16
NA_QROWS = 4
NA_KROWS = NA_QROWS + NA_WIN_ROWS
NA_HEADS_PER_STEP = 2
WA_BLOCK = 128
ML_QK = 64
ROPE_THETA = 10000.0
EPS = 1e-6
NEG_INF = float("-inf")
VMEM_LIMIT = 48 * 1024 * 1024
DN_VMEM_LIMIT = 56 * 1024 * 1024

NA_Q, NA_K, NA_V = 0, 4, 8
DN_OG = 24
ML_Q, ML_K, ML_V, ML_OG = 28, 30, 32, 36
WA_Q, WA_K, WA_V = 40, 44, 46
MAIN_COLS = 48 * 128
LANE_BETA, LANE_ALPHA, LANE_IG, LANE_FG = 0, 8, 16, 24


def _cparams(sem, vmem_limit=VMEM_LIMIT):
    return pltpu.CompilerParams(dimension_semantics=sem, vmem_limit_bytes=vmem_limit)


def _rms(x, gain):
    return x * lax.rsqrt(jnp.mean(x * x, axis=-1, keepdims=True) + EPS) * gain


def _silu(x):
    return x * jax.nn.sigmoid(x)


def _softplus(x):
    return jnp.maximum(x, 0.0) + jnp.log1p(jnp.exp(-jnp.abs(x)))


def _log_sigmoid(x):
    return jnp.minimum(x, 0.0) - jnp.log1p(jnp.exp(-jnp.abs(x)))


def _dot(a, b):
    return jnp.dot(a, b, preferred_element_type=F32)


def _dot_nt(a, b):
    return lax.dot_general(a, b, (((1,), (1,)), ((), ())), preferred_element_type=F32)


def _dot_tn(a, b):
    return lax.dot_general(a, b, (((0,), (0,)), ((), ())), preferred_element_type=F32)


def _iota(shape, dim):
    return lax.broadcasted_iota(jnp.int32, shape, dim)


def _lane_pick(x, lane):
    return jnp.sum(jnp.where(_iota(x.shape, 1) == lane, x, 0.0), axis=-1, keepdims=True)


def _to_row(col):
    n = col.shape[0]
    eye = _iota((n, n), 0) == _iota((n, n), 1)
    return jnp.sum(jnp.where(eye, col, 0.0), axis=0, keepdims=True)


def _ada_kernel(c_ref, w_ref, b_ref, o_ref):
    a = _silu(c_ref[...]).astype(BF16)
    o_ref[0] = _dot(a, w_ref[0].astype(BF16)) + b_ref[0]


def _ada_call(cs, w_ada, b_ada):
    depth, d, n = w_ada.shape
    tn = 1024
    return pl.pallas_call(
        _ada_kernel,
        grid=(depth, n // tn),
        in_specs=[pl.BlockSpec((8, d), lambda l, j: (0, 0)),
                  pl.BlockSpec((1, d, tn), lambda l, j: (l, 0, j)),
                  pl.BlockSpec((1, 1, tn), lambda l, j: (l, 0, j))],
        out_specs=pl.BlockSpec((1, 8, tn), lambda l, j: (l, 0, j)),
        out_shape=jax.ShapeDtypeStruct((depth, 8, n), F32),
        compiler_params=_cparams(("arbitrary", "arbitrary")),
        name="ada",
    )(cs, w_ada, b_ada.reshape(depth, 1, n))


def _norm_kernel(x_ref, mod_ref, g_ref, h_ref):
    mod = mod_ref[0]
    h = _rms(x_ref[...], g_ref[...])
    h_ref[...] = (h * (1.0 + mod[1:2]) + mod[0:1]).astype(BF16)


def _norm_call(x, mod, gain, rows_per_mod, tm):
    m, d = x.shape
    per = rows_per_mod // tm
    return pl.pallas_call(
        _norm_kernel,
        grid=(m // tm,),
        in_specs=[pl.BlockSpec((tm, d), lambda i: (i, 0)),
                  pl.BlockSpec((1, 8, d), lambda i: (i // per, 0, 0)),
                  pl.BlockSpec((1, d), lambda i: (0, 0))],
        out_specs=pl.BlockSpec((tm, d), lambda i: (i, 0)),
        out_shape=jax.ShapeDtypeStruct((m, d), BF16),
        compiler_params=_cparams(("arbitrary",)),
        name="norm_mod",
    )(x, mod, gain)


def _matmul_kernel(a_ref, wt_ref, o_ref):
    o_ref[...] = _dot_nt(a_ref[...], wt_ref[0])


def _matmul_call(a, wt, layer, tm, tn, name):
    m, k = a.shape
    n = wt.shape[1]
    return pl.pallas_call(
        _matmul_kernel,
        grid=(n // tn, m // tm),
        in_specs=[pl.BlockSpec((tm, k), lambda j, i: (i, 0)),
                  pl.BlockSpec((1, tn, k), lambda j, i: (layer, j, 0))],
        out_specs=pl.BlockSpec((tm, tn), lambda j, i: (i, j)),
        out_shape=jax.ShapeDtypeStruct((m, n), F32),
        compiler_params=_cparams(("arbitrary", "arbitrary")),
        name=name,
    )(a, wt)


def _inproj_cast_kernel(a_ref, wt_ref, f1_ref, f2_ref, o_ref, f1o_ref, f2o_ref):
    o_ref[...] = _dot_nt(a_ref[...], wt_ref[0])
    f1o_ref[...] = f1_ref[0].astype(BF16)
    f2o_ref[...] = f2_ref[0].astype(BF16)


def _inproj_cast_call(a, wt, layer, w_ffn_in, w_ffn_out, tm, tn):
    m, k = a.shape
    n = wt.shape[1]
    _, d, two_hidden = w_ffn_in.shape
    hidden = w_ffn_out.shape[1]
    slabs = FFN_CAST_SLABS
    n_i = m // tm
    assert (n // tn) * n_i >= slabs and two_hidden % (slabs * 128) == 0 and hidden % (slabs * 8) == 0
    c1 = two_hidden // slabs
    r2 = hidden // slabs

    def slab(j, i):
        return jnp.minimum(j * n_i + i, slabs - 1)

    return pl.pallas_call(
        _inproj_cast_kernel,
        grid=(n // tn, n_i),
        in_specs=[pl.BlockSpec((tm, k), lambda j, i: (i, 0)),
                  pl.BlockSpec((1, tn, k), lambda j, i: (layer, j, 0)),
                  pl.BlockSpec((1, d, c1), lambda j, i: (layer, 0, slab(j, i))),
                  pl.BlockSpec((1, r2, d), lambda j, i: (layer, slab(j, i), 0))],
        out_specs=[pl.BlockSpec((tm, tn), lambda j, i: (i, j)),
                   pl.BlockSpec((d, c1), lambda j, i: (0, slab(j, i))),
                   pl.BlockSpec((r2, d), lambda j, i: (slab(j, i), 0))],
        out_shape=[jax.ShapeDtypeStruct((m, n), F32),
                   jax.ShapeDtypeStruct((d, two_hidden), BF16),
                   jax.ShapeDtypeStruct((hidden, d), BF16)],
        compiler_params=_cparams(("arbitrary", "arbitrary")),
        name="inproj",
    )(a, wt, w_ffn_in, w_ffn_out)


def _outproj_kernel(ya_ref, yb_ref, ym_ref, yw_ref, w_ref, x_ref, mod_ref, g_ref, xo_ref, h_ref):
    gw = ya_ref.shape[1]
    acc = _dot(ya_ref[...], w_ref[0, 0:gw, :])
    acc += _dot(yb_ref[...], w_ref[0, gw:2 * gw, :])
    acc += _dot(ym_ref[...], w_ref[0, 2 * gw:3 * gw, :])
    acc += _dot(yw_ref[...], w_ref[0, 3 * gw:4 * gw, :])
    mod = mod_ref[0]
    xn = x_ref[...] + mod[2:3] * acc
    xo_ref[...] = xn
    h = _rms(xn, g_ref[...])
    h_ref[...] = (h * (1.0 + mod[4:5]) + mod[3:4]).astype(BF16)


def _outproj_call(ys, w, layer, x, mod, gain, rows_per_mod, tm):
    m, d = x.shape
    gw = ys[0].shape[1]
    per = rows_per_mod // tm
    yspec = pl.BlockSpec((tm, gw), lambda i: (i, 0))
    return pl.pallas_call(
        _outproj_kernel,
        grid=(m // tm,),
        in_specs=[yspec, yspec, yspec, yspec,
                  pl.BlockSpec((1, 4 * gw, d), lambda i: (layer, 0, 0)),
                  pl.BlockSpec((tm, d), lambda i: (i, 0)),
                  pl.BlockSpec((1, 8, d), lambda i: (i // per, 0, 0)),
                  pl.BlockSpec((1, d), lambda i: (0, 0))],
        out_specs=[pl.BlockSpec((tm, d), lambda i: (i, 0)),
                   pl.BlockSpec((tm, d), lambda i: (i, 0))],
        out_shape=[jax.ShapeDtypeStruct((m, d), F32), jax.ShapeDtypeStruct((m, d), BF16)],
        compiler_params=_cparams(("arbitrary",)),
        name="outproj",
    )(*ys, w, x, mod, gain)


def _ffn_kernel(h_ref, wg_ref, wu_ref, w2_ref, x_ref, mod_ref, modn_ref, gn_ref, *rest, with_next, cast_next):
    if cast_next:
        c1_ref, c2_ref, xo_ref, hn_ref, c1o_ref, c2o_ref, acc_ref = rest
        c1o_ref[...] = c1_ref[0].astype(BF16)
        c2o_ref[...] = c2_ref[0].astype(BF16)
    else:
        xo_ref, hn_ref, acc_ref = rest
    j = pl.program_id(1)

    @pl.when(j == 0)
    def _():
        acc_ref[...] = jnp.zeros_like(acc_ref)

    h = h_ref[...]
    act = _silu(_dot(h, wg_ref[0])) * _dot(h, wu_ref[0])
    acc_ref[...] += _dot(act.astype(BF16), w2_ref[0])

    @pl.when(j == pl.num_programs(1) - 1)
    def _():
        xn = x_ref[...] + mod_ref[0][5:6] * acc_ref[...]
        xo_ref[...] = xn
        if with_next:
            modn = modn_ref[0]
            hn = _rms(xn, gn_ref[...])
            hn_ref[...] = (hn * (1.0 + modn[1:2]) + modn[0:1]).astype(BF16)
        else:
            hn_ref[...] = jnp.zeros_like(hn_ref)


def _ffn_call(h, w_in, w_out, layer, x, mod, mod_next, gain_next, rows_per_mod, tm, tn, with_next, cast_next=None):
    m, d = x.shape
    hidden = w_out.shape[1]
    nj = hidden // tn
    n_i = m // tm
    per = rows_per_mod // tm
    in_specs = [pl.BlockSpec((tm, d), lambda i, j: (i, 0)),
                pl.BlockSpec((1, d, tn), lambda i, j: (layer, 0, j)),
                pl.BlockSpec((1, d, tn), lambda i, j: (layer, 0, nj + j)),
                pl.BlockSpec((1, tn, d), lambda i, j: (layer, j, 0)),
                pl.BlockSpec((tm, d), lambda i, j: (i, 0)),
                pl.BlockSpec((1, 8, d), lambda i, j: (i // per, 0, 0)),
                pl.BlockSpec((1, 8, d), lambda i, j: (i // per, 0, 0)),
                pl.BlockSpec((1, d), lambda i, j: (0, 0))]
    out_specs = [pl.BlockSpec((tm, d), lambda i, j: (i, 0)),
                 pl.BlockSpec((tm, d), lambda i, j: (i, 0))]
    out_shape = [jax.ShapeDtypeStruct((m, d), F32), jax.ShapeDtypeStruct((m, d), BF16)]
    args = [h, w_in, w_in, w_out, x, mod, mod_next, gain_next]
    if cast_next is not None:
        w1, w2, nxt = cast_next
        slabs = 2 * FFN_CAST_SLABS
        assert n_i * nj >= slabs and w1.shape[2] % (slabs * 128) == 0 and w2.shape[1] % (slabs * 16) == 0
        c1 = w1.shape[2] // slabs
        r2 = w2.shape[1] // slabs

        def slab(i, j):
            return jnp.minimum(i * nj + j, slabs - 1)

        in_specs += [pl.BlockSpec((1, d, c1), lambda i, j: (nxt, 0, slab(i, j))),
                     pl.BlockSpec((1, r2, d), lambda i, j: (nxt, slab(i, j), 0))]
        out_specs += [pl.BlockSpec((d, c1), lambda i, j: (0, slab(i, j))),
                      pl.BlockSpec((r2, d), lambda i, j: (slab(i, j), 0))]
        out_shape += [jax.ShapeDtypeStruct((d, w1.shape[2]), BF16), jax.ShapeDtypeStruct((w2.shape[1], d), BF16)]
        args += [w1, w2]
    return pl.pallas_call(
        functools.partial(_ffn_kernel, with_next=with_next, cast_next=cast_next is not None),
        grid=(n_i, nj),
        in_specs=in_specs,
        out_specs=out_specs,
        out_shape=out_shape,
        scratch_shapes=[pltpu.VMEM((tm, d), F32)],
        compiler_params=_cparams(("arbitrary", "arbitrary"), DN_VMEM_LIMIT if cast_next is not None else VMEM_LIMIT),
        name="ffn",
    )(*args)


def _na_kernel(q_ref, k_ref, v_ref, kc_ref, vc_ref, gain_ref, colbias_ref, o_ref,
               kn_scr, v_scr, kcn_scr, vc_scr, bias_scr, sl0_scr, sc0_scr, sl1_scr, sc1_scr, *, rows):
    step = pl.program_id(2)
    n_rb = pl.num_programs(2) - 1
    hd = HEAD_DIM
    heads = q_ref.shape[1] // hd

    @pl.when(step == 0)
    def _():
        sl1_scr[...] = jnp.zeros_like(sl1_scr)
        sc1_scr[...] = jnp.zeros_like(sc1_scr)
        g1 = gain_ref[1:2, :]
        closed = jnp.full((GRID_W, GRID_W), NEG_INF, F32)
        for hh in range(heads):
            cols = slice(hh * hd, (hh + 1) * hd)
            kn_scr[hh] = _rms(k_ref[:, cols], g1).astype(BF16)
            v_scr[hh] = v_ref[:, cols].astype(BF16)
            kcn_scr[hh] = _rms(kc_ref[:, cols], g1).astype(BF16)
            vc_scr[hh] = vc_ref[:, cols].astype(BF16)
            for typ, (offs, inside) in enumerate(_na_row_plan(rows)):
                for i in range(NA_QROWS):
                    for jp in range(NA_KROWS // 2):
                        halves = [colbias_ref[0, hh, offs[i][j]] if inside[i][j] else closed
                                  for j in (2 * jp, 2 * jp + 1)]
                        bias_scr[hh, typ, i * GRID_W:(i + 1) * GRID_W, 2 * jp * GRID_W:(2 * jp + 2) * GRID_W] = (
                            jnp.concatenate(halves, axis=1))

    scale = hd ** -0.5
    nk = NA_KROWS * GRID_W

    def window_start(blk):
        ws = jnp.clip(NA_QROWS * blk - NA_WIN_ROWS // 2, 0, rows - NA_KROWS)
        return pl.multiple_of(ws * GRID_W, GRID_W)

    def stages(write_loc, write_ctx, read_loc, read_ctx):
        prev = jnp.maximum(step - 1, 0)
        cur = jnp.minimum(step, n_rb - 1)
        typ = jnp.where(cur == 0, 0, jnp.where(cur == n_rb - 1, 2, 1))
        for hh in range(heads):
            cols = slice(hh * hd, (hh + 1) * hd)
            s_loc = read_loc[hh]
            s_ctx = read_ctx[hh]
            m = jnp.maximum(jnp.max(s_loc, axis=-1, keepdims=True), jnp.max(s_ctx, axis=-1, keepdims=True))
            p_loc = jnp.exp(s_loc - m)
            p_ctx = jnp.exp(s_ctx - m)
            den = jnp.sum(p_loc, axis=-1, keepdims=True) + jnp.sum(p_ctx, axis=-1, keepdims=True)
            vw = v_scr[hh, pl.ds(window_start(prev), nk), :]
            o = _dot(p_loc.astype(BF16), vw) + _dot(p_ctx.astype(BF16), vc_scr[hh])
            o_ref[:, cols] = (o / den).astype(BF16)
            qn = _rms(q_ref[:, cols], gain_ref[0:1, :]).astype(BF16)
            kw = kn_scr[hh, pl.ds(window_start(cur), nk), :]
            write_loc[hh] = _dot_nt(qn, kw) * scale + bias_scr[hh, pl.ds(typ, 1)][0]
            write_ctx[hh] = _dot_nt(qn, kcn_scr[hh]) * scale

    @pl.when(step % 2 == 0)
    def _():
        stages(sl0_scr, sc0_scr, sl1_scr, sc1_scr)

    @pl.when(step % 2 == 1)
    def _():
        stages(sl1_scr, sc1_scr, sl0_scr, sc0_scr)


def _na_row_plan(rows):
    n_rb = rows // NA_QROWS
    kh = NA_WIN_ROWS
    plan = []
    for rb in (0, 1, n_rb - 1):
        ws = min(max(NA_QROWS * rb - kh // 2, 0), rows - NA_KROWS)
        offs, inside = [], []
        for i in range(NA_QROWS):
            r = NA_QROWS * rb + i
            rs = min(max(r - kh // 2, 0), rows - kh)
            offs.append([min(max(ws + j - r, 1 - kh), kh - 1) + kh - 1 for j in range(NA_KROWS)])
            inside.append([rs <= ws + j < rs + kh for j in range(NA_KROWS)])
        plan.append((offs, inside))
    return plan


def _na_col_tables(rpb):
    kw = NA_WIN_COLS
    c = jnp.arange(GRID_W)
    cs = jnp.clip(c - kw // 2, 0, GRID_W - kw)
    in_col = (c[None, :] >= cs[:, None]) & (c[None, :] < cs[:, None] + kw)
    dc = jnp.clip(c[None, :] - c[:, None], 1 - kw, kw - 1) + kw - 1
    dc_hot = jax.nn.one_hot(dc, 2 * kw - 1, dtype=F32)
    val = jnp.einsum("lhab,ckb->lhack", rpb.astype(F32), dc_hot, precision=lax.Precision.HIGHEST)
    return jnp.where(in_col, val, NEG_INF)


def _na_call(p_lat, p_ctx, gain, colbias, layer, batch, seq, lc):
    rows = seq // GRID_W
    n_rb = rows // NA_QROWS
    tq = NA_QROWS * GRID_W
    nk = NA_KROWS * GRID_W
    heads = colbias.shape[1]
    n_off = colbias.shape[2]
    hd = HEAD_DIM
    hp = NA_HEADS_PER_STEP
    wide = hp * hd
    assert heads % hp == 0 and NA_Q % hp == 0 and NA_K % hp == 0 and NA_V % hp == 0
    return pl.pallas_call(
        functools.partial(_na_kernel, rows=rows),
        grid=(batch, heads // hp, n_rb + 1),
        in_specs=[pl.BlockSpec((tq, wide), lambda b, h, r: (b * n_rb + jnp.minimum(r, n_rb - 1), NA_Q // hp + h)),
                  pl.BlockSpec((seq, wide), lambda b, h, r: (b, NA_K // hp + h)),
                  pl.BlockSpec((seq, wide), lambda b, h, r: (b, NA_V // hp + h)),
                  pl.BlockSpec((lc, wide), lambda b, h, r: (b, NA_K // hp + h)),
                  pl.BlockSpec((lc, wide), lambda b, h, r: (b, NA_V // hp + h)),
                  pl.BlockSpec((2, hd), lambda b, h, r: (0, 0)),
                  pl.BlockSpec((1, hp, n_off, GRID_W, GRID_W), lambda b, h, r: (layer, h, 0, 0, 0))],
        out_specs=pl.BlockSpec((tq, wide), lambda b, h, r: (b * n_rb + jnp.maximum(r - 1, 0), h)),
        out_shape=jax.ShapeDtypeStruct((batch * seq, heads * hd), BF16),
        scratch_shapes=[pltpu.VMEM((hp, seq, hd), BF16), pltpu.VMEM((hp, seq, hd), BF16),
                        pltpu.VMEM((hp, lc, hd), BF16), pltpu.VMEM((hp, lc, hd), BF16),
                        pltpu.VMEM((hp, 3, tq, nk), F32),
                        pltpu.VMEM((hp, tq, nk), F32), pltpu.VMEM((hp, tq, lc), F32),
                        pltpu.VMEM((hp, tq, nk), F32), pltpu.VMEM((hp, tq, lc), F32)],
        compiler_params=_cparams(("arbitrary", "arbitrary", "arbitrary")),
        name="na_attn",
    )(p_lat, p_lat, p_lat, p_ctx, p_ctx, gain, colbias)


def _ctx_attn_kernel(sink_ref, q_ref, k_ref, v_ref, gain_ref, o_ref, *, use_sink):
    scale = HEAD_DIM ** -0.5
    qn = _rms(q_ref[...], gain_ref[0:1, :]).astype(BF16)
    kn = _rms(k_ref[...], gain_ref[1:2, :]).astype(BF16)
    s = _dot_nt(qn, kn) * scale
    m = jnp.max(s, axis=-1, keepdims=True)
    if use_sink:
        sk = sink_ref[pl.program_id(1)]
        m = jnp.maximum(m, sk)
    p = jnp.exp(s - m)
    den = jnp.sum(p, axis=-1, keepdims=True)
    if use_sink:
        den = den + jnp.exp(sk - m)
    o = _dot(p.astype(BF16), v_ref[...].astype(BF16))
    o_ref[...] = (o / den).astype(BF16)


def _ctx_attn_call(p_ctx, gain, sink, batch, lc, heads, rep, qoff, koff, voff, use_sink, name):
    hd = HEAD_DIM
    return pl.pallas_call(
        functools.partial(_ctx_attn_kernel, use_sink=use_sink),
        grid=(batch, heads),
        in_specs=[pl.BlockSpec(memory_space=pltpu.SMEM),
                  pl.BlockSpec((lc, hd), lambda b, h: (b, qoff + h)),
                  pl.BlockSpec((lc, hd), lambda b, h: (b, koff + h // rep)),
                  pl.BlockSpec((lc, hd), lambda b, h: (b, voff + h // rep)),
                  pl.BlockSpec((2, hd), lambda b, h: (0, 0))],
        out_specs=pl.BlockSpec((lc, hd), lambda b, h: (b, h)),
        out_shape=jax.ShapeDtypeStruct((batch * lc, heads * hd), BF16),
        compiler_params=_cparams(("arbitrary", "arbitrary")),
        name=name,
    )(sink, p_ctx, p_ctx, p_ctx, gain)


def _rope(x, cos, sin_signed):
    lane = _iota(x.shape, 1)
    swapped = jnp.where(lane % 64 < 32, pltpu.roll(x, 96, axis=1), pltpu.roll(x, 32, axis=1))
    return x * cos + swapped * sin_signed


def _wa_kernel(sink_ref, q_ref, k_ref, v_ref, kc_ref, vc_ref, gain_ref, cq_ref, sq_ref, ck_ref, sk_ref, o_ref,
               kpad_scr, vpad_scr, kcn_scr, vc_scr, sl0_scr, sc0_scr, sl1_scr, sc1_scr, *, seq):
    step = pl.program_id(1)
    nb = pl.num_programs(1) - 1
    wb = WA_BLOCK
    hd = HEAD_DIM
    groups = k_ref.shape[1] // hd

    @pl.when(step == 0)
    def _():
        sl1_scr[...] = jnp.zeros_like(sl1_scr)
        sc1_scr[...] = jnp.zeros_like(sc1_scr)
        g1 = gain_ref[1:2, :]
        zeros = jnp.zeros((wb, hd), BF16)
        for g in range(groups):
            cols = slice(g * hd, (g + 1) * hd)
            kpad_scr[g, 0:wb, :] = zeros
            kpad_scr[g, seq + wb:seq + 2 * wb, :] = zeros
            vpad_scr[g, 0:wb, :] = zeros
            vpad_scr[g, seq + wb:seq + 2 * wb, :] = zeros
            kpad_scr[g, wb:seq + wb, :] = _rope(_rms(k_ref[:, cols], g1), ck_ref[...], sk_ref[...]).astype(BF16)
            vpad_scr[g, wb:seq + wb, :] = v_ref[:, cols].astype(BF16)
            kcn_scr[g] = _rms(kc_ref[:, cols], g1).astype(BF16)
            vc_scr[g] = vc_ref[:, cols].astype(BF16)

    scale = hd ** -0.5

    def stages(write_loc, write_ctx, read_loc, read_ctx):
        prev = jnp.maximum(step - 1, 0)
        cur = jnp.minimum(step, nb - 1)
        g0 = gain_ref[0:1, :]
        cq = cq_ref[...]
        sq = sq_ref[...]
        shp = (2 * wb, 3 * wb)
        i = _iota(shp, 0) % wb
        j = _iota(shp, 1)
        kpos = (cur - 1) * wb + j
        ok = (j - i >= 0) & (j - i <= 2 * wb) & (kpos >= 0) & (kpos < seq)
        for g in range(groups):
            s_loc = read_loc[g]
            s_ctx = read_ctx[g]
            sink = jnp.where(_iota((2 * wb, 1), 0) < wb, sink_ref[2 * g], sink_ref[2 * g + 1])
            m = jnp.maximum(jnp.maximum(jnp.max(s_loc, axis=-1, keepdims=True),
                                        jnp.max(s_ctx, axis=-1, keepdims=True)), sink)
            p_loc = jnp.exp(s_loc - m)
            p_ctx = jnp.exp(s_ctx - m)
            den = jnp.sum(p_loc, axis=-1, keepdims=True) + jnp.sum(p_ctx, axis=-1, keepdims=True) + jnp.exp(sink - m)
            vw = vpad_scr[g, pl.ds(pl.multiple_of(prev * wb, wb), 3 * wb), :]
            o = (_dot(p_loc.astype(BF16), vw) + _dot(p_ctx.astype(BF16), vc_scr[g])) / den
            o_ref[:, 2 * g * hd:(2 * g + 1) * hd] = o[0:wb].astype(BF16)
            o_ref[:, (2 * g + 1) * hd:(2 * g + 2) * hd] = o[wb:2 * wb].astype(BF16)
            q2 = jnp.concatenate([_rope(_rms(q_ref[:, 2 * g * hd:(2 * g + 1) * hd], g0), cq, sq),
                                  _rope(_rms(q_ref[:, (2 * g + 1) * hd:(2 * g + 2) * hd], g0), cq, sq)],
                                 axis=0).astype(BF16)
            kw = kpad_scr[g, pl.ds(pl.multiple_of(cur * wb, wb), 3 * wb), :]
            write_loc[g] = jnp.where(ok, _dot_nt(q2, kw) * scale, NEG_INF)
            write_ctx[g] = _dot_nt(q2, kcn_scr[g]) * scale

    @pl.when(step % 2 == 0)
    def _():
        stages(sl0_scr, sc0_scr, sl1_scr, sc1_scr)

    @pl.when(step % 2 == 1)
    def _():
        stages(sl1_scr, sc1_scr, sl0_scr, sc0_scr)


def _wa_call(p_lat, p_ctx, gain, sink, cos_t, sin_t, batch, seq, lc):
    wb = WA_BLOCK
    hd = HEAD_DIM
    nb = seq // wb
    kvh = 2
    qw = 2 * kvh * hd
    kw_ = kvh * hd
    assert (WA_Q * hd) % qw == 0 and (WA_K * hd) % kw_ == 0 and (WA_V * hd) % kw_ == 0
    return pl.pallas_call(
        functools.partial(_wa_kernel, seq=seq),
        grid=(batch, nb + 1),
        in_specs=[pl.BlockSpec(memory_space=pltpu.SMEM),
                  pl.BlockSpec((wb, qw), lambda b, n: (b * nb + jnp.minimum(n, nb - 1), WA_Q * hd // qw)),
                  pl.BlockSpec((seq, kw_), lambda b, n: (b, WA_K * hd // kw_)),
                  pl.BlockSpec((seq, kw_), lambda b, n: (b, WA_V * hd // kw_)),
                  pl.BlockSpec((lc, kw_), lambda b, n: (b, WA_K * hd // kw_)),
                  pl.BlockSpec((lc, kw_), lambda b, n: (b, WA_V * hd // kw_)),
                  pl.BlockSpec((2, hd), lambda b, n: (0, 0)),
                  pl.BlockSpec((wb, hd), lambda b, n: (jnp.minimum(n, nb - 1), 0)),
                  pl.BlockSpec((wb, hd), lambda b, n: (jnp.minimum(n, nb - 1), 0)),
                  pl.BlockSpec((seq, hd), lambda b, n: (0, 0)),
                  pl.BlockSpec((seq, hd), lambda b, n: (0, 0))],
        out_specs=pl.BlockSpec((wb, qw), lambda b, n: (b * nb + jnp.maximum(n - 1, 0), 0)),
        out_shape=jax.ShapeDtypeStruct((batch * seq, qw), BF16),
        scratch_shapes=[pltpu.VMEM((kvh, seq + 2 * wb, hd), BF16), pltpu.VMEM((kvh, seq + 2 * wb, hd), BF16),
                        pltpu.VMEM((kvh, lc, hd), BF16), pltpu.VMEM((kvh, lc, hd), BF16),
                        pltpu.VMEM((kvh, 2 * wb, 3 * wb), F32), pltpu.VMEM((kvh, 2 * wb, lc), F32),
                        pltpu.VMEM((kvh, 2 * wb, 3 * wb), F32), pltpu.VMEM((kvh, 2 * wb, lc), F32)],
        compiler_params=_cparams(("arbitrary", "arbitrary")),
        name="wa_attn",
    )(sink, p_lat, p_lat, p_lat, p_ctx, p_ctx, gain, cos_t, sin_t, cos_t, sin_t)


def _dn_prep_kernel(x_ref, prev_ref, next_ref, w_ref, o_ref, *, n_tiles):
    t = pl.program_id(1)
    tl = x_ref.shape[0]
    x = x_ref[...]
    prev = jnp.where(t > 0, prev_ref[...], 0.0)
    nxt = jnp.where(t < n_tiles - 1, next_ref[...], 0.0)
    xe = jnp.concatenate([prev, x, nxt], axis=0)
    n = tl + 16
    taps = w_ref.shape[0]
    acc = jnp.zeros_like(x)
    for j in range(taps):
        off = 8 - taps // 2 + j
        shifted = xe if off % n == 0 else pltpu.roll(xe, n - off, axis=0)
        acc += shifted[0:tl] * w_ref[j:j + 1, :]
    y = _silu(acc)
    hd = HEAD_DIM
    nh = y.shape[1] // (3 * hd)
    for blk in range(3 * nh):
        yb = y[:, blk * hd:(blk + 1) * hd]
        if blk < 2 * nh:
            yb = yb * lax.rsqrt(jnp.sum(yb * yb, axis=-1, keepdims=True) + EPS)
            if blk < nh:
                yb = yb * hd ** -0.5
        o_ref[:, blk * hd:(blk + 1) * hd] = yb


def _dn_prep_call(p, conv_w, batch, length, tl):
    width = conv_w.shape[1]
    n_tiles = length // tl
    per8 = tl // 8
    last8 = batch * length // 8 - 1
    return pl.pallas_call(
        functools.partial(_dn_prep_kernel, n_tiles=n_tiles),
        grid=(batch, n_tiles),
        in_specs=[pl.BlockSpec((tl, width), lambda b, t: (b * n_tiles + t, 1)),
                  pl.BlockSpec((8, width), lambda b, t: (jnp.maximum((b * n_tiles + t) * per8 - 1, 0), 1)),
                  pl.BlockSpec((8, width), lambda b, t: (jnp.minimum((b * n_tiles + t + 1) * per8, last8), 1)),
                  pl.BlockSpec((conv_w.shape[0], width), lambda b, t: (0, 0))],
        out_specs=pl.BlockSpec((tl, width), lambda b, t: (b * n_tiles + t, 0)),
        out_shape=jax.ShapeDtypeStruct((batch * length, width), F32),
        compiler_params=_cparams(("arbitrary", "arbitrary")),
        name="dn_prep",
    )(p, p, p, conv_w)


def _solve_dot(a, b):
    return _dot(a.astype(BF16), b.astype(BF16))


def _dn_pass(q_ref, k_ref, v_ref, og_ref, gates_ref, gadd, gmul, norm_g, y_ref, length, h, state,
             lin_scr, add_scr, egl_scr, acc_scr):
    assert CHUNK == HEAD_DIM
    n_chunks = length // CHUNK

    acc_scr[0:length, :] = jnp.zeros((length, HEAD_DIM), F32)

    shp = (CHUNK, CHUNK)
    ri = _iota(shp, 0)
    ci = _iota(shp, 1)
    masks = ((ci <= ri, ci < ri), (ci >= ri, ci > ri))
    couple = ([], [])
    s = 1
    while s < CHUNK:
        same = (ri // (2 * s)) == (ci // (2 * s))
        couple[0].append(same & ((ri // s) % 2 == 1) & ((ci // s) % 2 == 0))
        couple[1].append(same & ((ri // s) % 2 == 0) & ((ci // s) % 2 == 1))
        s *= 2
    group = min(n_chunks, PREP_GROUP)

    def prep(gi, carry):
        chains = []
        for g in range(group):
            c = gi * group + g
            r0 = pl.multiple_of(c * CHUNK, CHUNK)
            q = q_ref[pl.ds(r0, CHUNK), :]
            k = k_ref[pl.ds(r0, CHUNK), :]
            raw = gates_ref[pl.ds(r0, CHUNK), :]
            gb = jnp.where(_iota(raw.shape, 1) < LANE_ALPHA, jax.nn.sigmoid(raw), -gmul * _softplus(raw + gadd))
            k16 = k.astype(BF16)
            kk = _dot_nt(k16, k16)
            qk = _dot_nt(q.astype(BF16), k16)
            for d in range(2):
                incl, strict = masks[d]
                beta_col = _lane_pick(gb, LANE_BETA + 4 * d + h)
                g_col = _lane_pick(gb, LANE_ALPHA + 4 * d + h)
                beta_row = _to_row(beta_col)
                g_row = _to_row(g_col)
                cum_col = jnp.sum(jnp.where(incl, g_row, 0.0), axis=-1, keepdims=True)
                cum_row = _to_row(cum_col)
                tot = jnp.sum(g_col, axis=0, keepdims=True)
                decay = jnp.exp(jnp.where(incl, cum_col - cum_row, NEG_INF))
                lm = jnp.where(strict, kk * beta_col * decay, 0.0)
                egl_scr[d, pl.ds(pl.multiple_of(c * 8, 8), 8), :] = jnp.broadcast_to(jnp.exp(tot), (8, HEAD_DIM))
                chains.append((d, c, r0, lm, beta_row, beta_row * jnp.exp(cum_row), (qk * decay).astype(BF16),
                               q * jnp.exp(cum_col), k * jnp.exp(tot - cum_col)))
        eye = (ri == ci).astype(F32)
        xs = [eye - jnp.where(couple[ch[0]][0], ch[3], 0.0) for ch in chains]
        for lvl in range(1, len(couple[0])):
            cs = [jnp.where(couple[ch[0]][lvl], ch[3], 0.0) for ch in chains]
            xc = [_solve_dot(x, c) for x, c in zip(xs, cs)]
            xs = [x - _solve_dot(t, x) for x, t in zip(xs, xc)]
        u16 = [_solve_dot(ainv * ch[4], v_ref[pl.ds(ch[2], CHUNK), :]).astype(BF16) for ch, ainv in zip(chains, xs)]
        w16 = [_solve_dot(ainv * ch[5], k_ref[pl.ds(ch[2], CHUNK), :]).astype(BF16) for ch, ainv in zip(chains, xs)]
        ket = [ch[8].T.astype(BF16) for ch in chains]
        s_lin = [_dot(kt, w) for kt, w in zip(ket, w16)]
        s_add = [_dot(kt, u) for kt, u in zip(ket, u16)]
        o_lin = [ch[7] - _dot(ch[6], w) for ch, w in zip(chains, w16)]
        o_add = [_dot(ch[6], u) for ch, u in zip(chains, u16)]
        for ch, sl, sa, ol, oa in zip(chains, s_lin, s_add, o_lin, o_add):
            d = ch[0]
            base = pl.multiple_of(ch[1] * 2 * CHUNK, 2 * CHUNK)
            lin_scr[d, pl.ds(base, CHUNK), :] = sl.astype(BF16)
            lin_scr[d, pl.ds(base + CHUNK, CHUNK), :] = ol.astype(BF16)
            add_scr[d, pl.ds(base, CHUNK), :] = sa
            add_scr[d, pl.ds(base + CHUNK, CHUNK), :] = oa
        return carry

    lax.fori_loop(0, n_chunks // group, prep, 0)

    def scan(s, st):
        new = []
        for d in range(2):
            c = s if d == 0 else n_chunks - 1 - s
            base = pl.multiple_of(c * 2 * CHUNK, 2 * CHUNK)
            sm = st[d]
            prod = _dot(lin_scr[d, pl.ds(base, 2 * CHUNK), :], sm.astype(BF16))
            add = add_scr[d, pl.ds(base, 2 * CHUNK), :]
            acc_scr[pl.ds(pl.multiple_of(c * CHUNK, CHUNK), CHUNK), :] += prod[CHUNK:] + add[CHUNK:]
            egl = egl_scr[d, pl.ds(pl.multiple_of(c * 8, 8), 8), :][0:1, :]
            new.append(sm * egl - prod[:CHUNK] + add[:CHUNK])
        return tuple(new)

    state = lax.fori_loop(0, n_chunks, scan, state, unroll=SCAN_UNROLL)

    y = _rms(acc_scr[0:length, :], norm_g) * _silu(og_ref[...])
    y_ref[...] = y.astype(BF16)
    return state


def _dn_kernel(q_ref, k_ref, v_ref, og_ref, gates_ref, qc_ref, kc_ref, vc_ref, ogc_ref, gatesc_ref,
               gadd_ref, gmul_ref, norm_ref, y_ref, yc_ref,
               lin_scr, add_scr, egl_scr, acc_scr, *, seq, lc):
    h = pl.program_id(1)
    gadd = gadd_ref[...]
    gmul = gmul_ref[...]
    norm_g = norm_ref[...]
    scr = (lin_scr, add_scr, egl_scr, acc_scr)
    zero = jnp.zeros((HEAD_DIM, HEAD_DIM), F32)
    state = _dn_pass(qc_ref, kc_ref, vc_ref, ogc_ref, gatesc_ref, gadd, gmul, norm_g, yc_ref, lc, h, (zero, zero), *scr)
    _dn_pass(q_ref, k_ref, v_ref, og_ref, gates_ref, gadd, gmul, norm_g, y_ref, seq, h, state, *scr)


def _dn_call(dq_lat, dq_ctx, p_lat, p_ctx, pg_lat, pg_ctx, gadd, gmul, norm_g, batch, seq, lc):
    hd = HEAD_DIM
    heads = 4

    def blk(n, col):
        return pl.BlockSpec((n, hd), lambda b, h: (b, col + h))

    def gate_blk(n):
        return pl.BlockSpec((n, hd), lambda b, h: (b, 0))

    vec = pl.BlockSpec((1, hd), lambda b, h: (0, 0))
    return pl.pallas_call(
        functools.partial(_dn_kernel, seq=seq, lc=lc),
        grid=(batch, heads),
        in_specs=[blk(seq, 0), blk(seq, heads), blk(seq, 2 * heads), blk(seq, DN_OG), gate_blk(seq),
                  blk(lc, 0), blk(lc, heads), blk(lc, 2 * heads), blk(lc, DN_OG), gate_blk(lc),
                  vec, vec, vec],
        out_specs=[pl.BlockSpec((seq, hd), lambda b, h: (b, h)),
                   pl.BlockSpec((lc, hd), lambda b, h: (b, h))],
        out_shape=[jax.ShapeDtypeStruct((batch * seq, heads * hd), BF16),
                   jax.ShapeDtypeStruct((batch * lc, heads * hd), BF16)],
        scratch_shapes=[pltpu.VMEM((2, 2 * seq, hd), BF16),
                        pltpu.VMEM((2, 2 * seq, hd), F32),
                        pltpu.VMEM((2, seq // CHUNK * 8, hd), F32),
                        pltpu.VMEM((seq, hd), F32)],
        compiler_params=_cparams(("arbitrary", "arbitrary"), DN_VMEM_LIMIT),
        name="deltanet",
    )(dq_lat, dq_lat, dq_lat, p_lat, pg_lat, dq_ctx, dq_ctx, dq_ctx, p_ctx, pg_ctx, gadd, gmul, norm_g)


def _ml_pass(q_ref, k_ref, v_ref, og_ref, gates_ref, gadd, norm_g, y_ref, length, h, state, acc_scr):
    n_chunks = length // CHUNK
    acc_scr[0:length, :] = jnp.zeros((length, HEAD_DIM), F32)
    shp = (CHUNK, CHUNK)
    ri = _iota(shp, 0)
    ci = _iota(shp, 1)
    masks = (ci <= ri, ci >= ri)
    head_lanes = (_iota((1, HEAD_DIM), 1) // ML_QK) == (h % 2)

    group = min(n_chunks, ML_GROUP)

    def each(f, *lists):
        return [f(*a) for a in zip(*lists)]

    def step(gi, st):
        dirs, r0s = [], []
        for j in range(group):
            s = gi * group + j
            for d in range(2):
                c = s if d == 0 else n_chunks - 1 - s
                dirs.append(d)
                r0s.append(pl.multiple_of(c * CHUNK, CHUNK))
        incl = [masks[d] for d in dirs]
        q = [jnp.where(head_lanes, q_ref[pl.ds(r0, CHUNK), :], 0.0) for r0 in r0s]
        k = [jnp.where(head_lanes, k_ref[pl.ds(r0, CHUNK), :], 0.0) * ML_QK ** -0.5 for r0 in r0s]
        v16 = [v_ref[pl.ds(r0, CHUNK), :].astype(BF16) for r0 in r0s]
        gb = [gates_ref[pl.ds(r0, CHUNK), :] + gadd for r0 in r0s]
        ig_col = [_lane_pick(g, LANE_IG + 4 * d + h) for g, d in zip(gb, dirs)]
        lf_col = [_log_sigmoid(_lane_pick(g, LANE_FG + 4 * d + h)) for g, d in zip(gb, dirs)]
        ig_row = each(_to_row, ig_col)
        lf_row = each(_to_row, lf_col)
        b_col = each(lambda m, r: jnp.sum(jnp.where(m, r, 0.0), axis=-1, keepdims=True), incl, lf_row)
        b_row = each(_to_row, b_col)
        b_last = each(lambda x: jnp.sum(x, axis=0, keepdims=True), lf_col)
        g_end = each(lambda bl, b, i: bl - b + i, b_last, b_col, ig_col)
        m_loc = each(lambda x: jnp.max(x, axis=0, keepdims=True), g_end)
        kw = each(lambda kk, g, m: kk * jnp.exp(g - m), k, g_end, m_loc)
        q16 = each(lambda x: x.astype(BF16), q)
        k16 = each(lambda x: x.astype(BF16), k)
        c_loc = each(lambda a, b: _dot_tn(a.astype(BF16), b), kw, v16)
        n_loc = each(lambda x: jnp.sum(x, axis=0, keepdims=True), kw)
        log_d = each(lambda m, bc, br, ir: jnp.where(m, bc - br + ir, NEG_INF), incl, b_col, b_row, ig_row)
        mx_col = each(lambda x: jnp.max(x, axis=-1, keepdims=True), log_d)
        s0 = each(lambda a, b, ld, mx: _dot_nt(a, b) * jnp.exp(ld - mx), q16, k16, log_d, mx_col)
        sv0 = each(lambda s, vv: _dot(s.astype(BF16), vv), s0, v16)
        rs0 = each(lambda s: jnp.sum(s, axis=-1, keepdims=True), s0)
        st = list(st)
        for i, (d, r0) in enumerate(zip(dirs, r0s)):
            c_st, n_st, m_st = st[d]
            m_inter = b_col[i] + m_st
            m_t = jnp.maximum(mx_col[i], m_inter)
            intra = jnp.exp(mx_col[i] - m_t)
            inter = jnp.exp(m_inter - m_t)
            num = intra * sv0[i] + inter * _dot(q16[i], c_st.astype(BF16))
            den = intra * rs0[i] + inter * jnp.sum(q[i] * n_st, axis=-1, keepdims=True)
            acc_scr[pl.ds(r0, CHUNK), :] += num / jnp.maximum(jnp.abs(den), jnp.exp(-m_t))
            m_new = jnp.maximum(b_last[i] + m_st, m_loc[i])
            a = jnp.exp(b_last[i] + m_st - m_new)
            e = jnp.exp(m_loc[i] - m_new)
            st[d] = (a * c_st + e * c_loc[i], a * n_st + e * n_loc[i], m_new)
        return tuple(st)

    state = lax.fori_loop(0, n_chunks // group, step, state)
    y = jax.nn.sigmoid(og_ref[...]) * _rms(acc_scr[0:length, :], norm_g)
    y_ref[...] = y.astype(BF16)
    return state


def _ml_kernel(q_ref, k_ref, v_ref, og_ref, gates_ref, qc_ref, kc_ref, vc_ref, ogc_ref, gatesc_ref,
               gadd_ref, norm_ref, y_ref, yc_ref, acc_scr, *, seq, lc):
    h = pl.program_id(1)
    gadd = gadd_ref[...]
    norm_g = norm_ref[0]
    zero = (jnp.zeros((HEAD_DIM, HEAD_DIM), F32), jnp.zeros((1, HEAD_DIM), F32), jnp.zeros((1, 1), F32))
    state = _ml_pass(qc_ref, kc_ref, vc_ref, ogc_ref, gatesc_ref, gadd, norm_g, yc_ref, lc, h, (zero, zero), acc_scr)
    _ml_pass(q_ref, k_ref, v_ref, og_ref, gates_ref, gadd, norm_g, y_ref, seq, h, state, acc_scr)


def _ml_call(p_lat, p_ctx, pg_lat, pg_ctx, gadd, norm_g, batch, seq, lc):
    hd = HEAD_DIM
    heads = 4

    def specs(n):
        return [pl.BlockSpec((n, hd), lambda b, h: (b, ML_Q + h // 2)),
                pl.BlockSpec((n, hd), lambda b, h: (b, ML_K + h // 2)),
                pl.BlockSpec((n, hd), lambda b, h: (b, ML_V + h)),
                pl.BlockSpec((n, hd), lambda b, h: (b, ML_OG + h)),
                pl.BlockSpec((n, hd), lambda b, h: (b, 0))]

    return pl.pallas_call(
        functools.partial(_ml_kernel, seq=seq, lc=lc),
        grid=(batch, heads),
        in_specs=specs(seq) + specs(lc) + [pl.BlockSpec((1, hd), lambda b, h: (0, 0)),
                                            pl.BlockSpec((1, 1, hd), lambda b, h: (h, 0, 0))],
        out_specs=[pl.BlockSpec((seq, hd), lambda b, h: (b, h)),
                   pl.BlockSpec((lc, hd), lambda b, h: (b, h))],
        out_shape=[jax.ShapeDtypeStruct((batch * seq, heads * hd), BF16),
                   jax.ShapeDtypeStruct((batch * lc, heads * hd), BF16)],
        scratch_shapes=[pltpu.VMEM((seq, hd), F32)],
        compiler_params=_cparams(("arbitrary", "arbitrary")),
        name="mlstm",
    )(p_lat, p_lat, p_lat, p_lat, pg_lat, p_ctx, p_ctx, p_ctx, p_ctx, pg_ctx, gadd, norm_g.reshape(heads, 1, hd))


def _regroup_kernel(wt_ref, main_ref, gate_ref, *, cuts):
    (a0, a1), (b0, b1), (c0, c1), (g0, g1), (h0, h1) = cuts
    main_ref[0, 0:a1 - a0, :] = wt_ref[0, a0:a1, :].astype(BF16)
    main_ref[0, a1 - a0:a1 - a0 + b1 - b0, :] = wt_ref[0, b0:b1, :].astype(BF16)
    main_ref[0, a1 - a0 + b1 - b0:, :] = wt_ref[0, c0:c1, :].astype(BF16)
    ng = (g1 - g0) + (h1 - h0)
    gate_ref[0, 0:g1 - g0, :] = wt_ref[0, g0:g1, :].astype(BF16)
    gate_ref[0, g1 - g0:ng, :] = wt_ref[0, h0:h1, :].astype(BF16)
    gate_ref[0, ng:, :] = jnp.zeros((gate_ref.shape[1] - ng, gate_ref.shape[2]), BF16)


def _regroup_call(w_in):
    depth, d, n = w_in.shape
    gw = 512
    dn_gate0 = 3 * gw + 4 * gw
    ml0 = dn_gate0 + 16
    ml_gate0 = ml0 + 3 * gw
    wa0 = ml_gate0 + 16
    cuts = ((0, dn_gate0), (ml0, ml_gate0), (wa0, n), (dn_gate0, ml0), (ml_gate0, wa0))
    tc = 256
    return pl.pallas_call(
        functools.partial(_regroup_kernel, cuts=cuts),
        grid=(depth, d // tc),
        in_specs=[pl.BlockSpec((1, n, tc), lambda l, i: (l, 0, i))],
        out_specs=[pl.BlockSpec((1, MAIN_COLS, tc), lambda l, i: (l, 0, i)),
                   pl.BlockSpec((1, 128, tc), lambda l, i: (l, 0, i))],
        out_shape=[jax.ShapeDtypeStruct((depth, MAIN_COLS, d), BF16), jax.ShapeDtypeStruct((depth, 128, d), BF16)],
        compiler_params=_cparams(("arbitrary", "arbitrary")),
        name="regroup_w_in",
    )(jnp.swapaxes(w_in, 1, 2))


def _rope_tables(seq):
    t = jnp.arange(seq)
    n_freq = HEAD_DIM // 4
    inv_freq = ROPE_THETA ** (-jnp.arange(n_freq, dtype=F32) / n_freq)
    pos = jnp.stack([t // GRID_W, t % GRID_W], axis=-1).astype(F32)
    ang = pos[:, :, None] * inv_freq
    cos, sin = jnp.cos(ang), jnp.sin(ang)
    cos_t = jnp.concatenate([cos, cos], axis=-1).reshape(seq, HEAD_DIM)
    sin_t = jnp.concatenate([-sin, sin], axis=-1).reshape(seq, HEAD_DIM)
    return cos_t, sin_t


def _gate_vectors(dt_bias, a_log, i_bias, f_bias):
    zeros8 = jnp.zeros((8,), F32)
    gadd = jnp.concatenate([zeros8, dt_bias.reshape(-1), i_bias.reshape(-1), f_bias.reshape(-1),
                            jnp.zeros((96,), F32)]).astype(F32)
    gmul = jnp.concatenate([zeros8, jnp.exp(a_log.astype(F32)).reshape(-1), jnp.zeros((112,), F32)])
    return gadd.reshape(1, 128), gmul.reshape(1, 128)


def _forward(x, c, ctx, c_ctx, w_ada, b_ada, norm_mix, norm_ffn, w_in, w_out, na_qk_gain, na_rpb, dn_conv,
             dn_a_log, dn_dt_bias, dn_norm, ml_i_bias, ml_f_bias, ml_norm, wa_qk_gain, wa_sink, w_ffn_in,
             w_ffn_out):
    batch, seq, d = x.shape
    lc = ctx.shape[1]
    depth = w_ada.shape[0]
    tm = 512
    tn_ffn = 512

    cos_t, sin_t = _rope_tables(seq)
    na_bias = _na_col_tables(na_rpb)
    cs = jnp.concatenate([c, c_ctx[None, :], jnp.zeros((8 - batch - 1, d), F32)], axis=0)
    mods = _ada_call(cs, w_ada, b_ada).reshape(depth, 8, 6, d)
    mods = jnp.pad(mods, ((0, 0), (0, 0), (0, 2), (0, 0)))

    w_main, w_gate = _regroup_call(w_in)
    w_o = w_out.astype(BF16)

    xl = x.reshape(batch * seq, d)
    xc = ctx.reshape(batch * lc, d)
    hl = _norm_call(xl, mods[0, :batch], norm_mix[0:1], seq, tm)
    hc = _norm_call(xc, mods[0, batch:batch + 1], norm_mix[0:1], batch * lc, tm)

    for l in range(depth):
        need_ctx = l < depth - 1
        mod_l, mod_c = mods[l, :batch], mods[l, batch:batch + 1]
        if l == 0:
            p_lat, w_f1, w_f2 = _inproj_cast_call(hl, w_main, l, w_ffn_in, w_ffn_out, 2 * tm, 1024)
        else:
            p_lat = _matmul_call(hl, w_main, l, 2 * tm, 2048, "inproj")
            w_f1, w_f2 = w_f1_next, w_f2_next
        w_f1 = w_f1[None]
        w_f2 = w_f2[None]
        pg_lat = _matmul_call(hl, w_gate, l, 2 * tm, 128, "inproj_gates")
        p_ctx = _matmul_call(hc, w_main, l, tm, 2048, "inproj_ctx")
        pg_ctx = _matmul_call(hc, w_gate, l, tm, 128, "inproj_gates_ctx")

        ya = _na_call(p_lat, p_ctx, na_qk_gain[l], na_bias, l, batch, seq, lc)
        gadd, gmul = _gate_vectors(dn_dt_bias[l], dn_a_log[l], ml_i_bias[l], ml_f_bias[l])
        dq_lat = _dn_prep_call(p_lat, dn_conv[l], batch, seq, 512)
        dq_ctx = _dn_prep_call(p_ctx, dn_conv[l], batch, lc, 256)
        yb, ycb = _dn_call(dq_lat, dq_ctx, p_lat, p_ctx, pg_lat, pg_ctx, gadd, gmul, dn_norm[l].reshape(1, -1),
                           batch, seq, lc)
        ym, ycm = _ml_call(p_lat, p_ctx, pg_lat, pg_ctx, gadd, ml_norm[l], batch, seq, lc)
        yw = _wa_call(p_lat, p_ctx, wa_qk_gain[l], wa_sink[l], cos_t, sin_t, batch, seq, lc)

        nl = min(l + 1, depth - 1)
        xl, h2 = _outproj_call((ya, yb, ym, yw), w_o, l, xl, mod_l, norm_ffn[l:l + 1], seq, tm)
        if need_ctx:
            xl, hl, w_f1_next, w_f2_next = _ffn_call(h2, w_f1, w_f2, 0, xl, mod_l, mods[nl, :batch],
                                                     norm_mix[nl:nl + 1], seq, tm, tn_ffn, True,
                                                     cast_next=(w_ffn_in, w_ffn_out, l + 1))
        else:
            xl, hl = _ffn_call(h2, w_f1, w_f2, 0, xl, mod_l, mods[nl, :batch], norm_mix[nl:nl + 1], seq, tm, tn_ffn,
                               False)
        if need_ctx:
            yca = _ctx_attn_call(p_ctx, na_qk_gain[l], wa_sink[l], batch, lc, 4, 1, NA_Q, NA_K, NA_V, False,
                                 "na_ctx_attn")
            ycw = _ctx_attn_call(p_ctx, wa_qk_gain[l], wa_sink[l], batch, lc, 4, 2, WA_Q, WA_K, WA_V, True,
                                 "wa_ctx_attn")
            xc, h2c = _outproj_call((yca, ycb, ycm, ycw), w_o, l, xc, mod_c, norm_ffn[l:l + 1], batch * lc, tm)
            xc, hc = _ffn_call(h2c, w_f1, w_f2, 0, xc, mod_c, mods[nl, batch:batch + 1], norm_mix[nl:nl + 1],
                               batch * lc, tm, tn_ffn, True)
    return xl.reshape(batch, seq, d)


def kernel(x, c, ctx, c_ctx, w_ada, b_ada, norm_mix, norm_ffn, w_in, w_out, na_qk_gain, na_rpb, dn_conv, dn_a_log,
           dn_dt_bias, dn_norm, ml_i_bias, ml_f_bias, ml_norm, wa_qk_gain, wa_sink, w_ffn_in, w_ffn_out):
    return _forward(x, c, ctx, c_ctx, w_ada, b_ada, norm_mix, norm_ffn, w_in, w_out, na_qk_gain, na_rpb, dn_conv,
                    dn_a_log, dn_dt_bias, dn_norm, ml_i_bias, ml_f_bias, ml_norm, wa_qk_gain, wa_sink, w_ffn_in,
                    w_ffn_out)
```

```python
import functools

import jax
import jax.numpy as jnp
from jax import lax
from jax.experimental import pallas as pl
from jax.experimental.pallas import tpu as pltpu

F32 = jnp.float32
BF16 = jnp.bfloat16

HEAD_DIM = 128
GRID_W = 64
CHUNK = 128
FFN_CAST_SLABS = 44
PREP_GROUP = 8
ML_GROUP = 4
SCAN_UNROLL = 2
NA_WIN_ROWS = 8
NA_WIN_COLS = 16
NA_QROWS = 4
NA_KROWS = NA_QROWS + NA_WIN_ROWS
NA_HEADS_PER_STEP = 2
WA_BLOCK = 128
ML_QK = 64
ROPE_THETA = 10000.0
EPS = 1e-6
NEG_INF = float("-inf")
VMEM_LIMIT = 48 * 1024 * 1024
DN_VMEM_LIMIT = 56 * 1024 * 1024

NA_Q, NA_K, NA_V = 0, 4, 8
DN_OG = 24
ML_Q, ML_K, ML_V, ML_OG = 28, 30, 32, 36
WA_Q, WA_K, WA_V = 40, 44, 46
MAIN_COLS = 48 * 128
LANE_BETA, LANE_ALPHA, LANE_IG, LANE_FG = 0, 8, 16, 24


def _cparams(sem, vmem_limit=VMEM_LIMIT):
    return pltpu.CompilerParams(dimension_semantics=sem, vmem_limit_bytes=vmem_limit)


def _rms(x, gain):
    return x * lax.rsqrt(jnp.mean(x * x, axis=-1, keepdims=True) + EPS) * gain


def _silu(x):
    return x * jax.nn.sigmoid(x)


def _softplus(x):
    return jnp.maximum(x, 0.0) + jnp.log1p(jnp.exp(-jnp.abs(x)))


def _log_sigmoid(x):
    return jnp.minimum(x, 0.0) - jnp.log1p(jnp.exp(-jnp.abs(x)))


def _dot(a, b):
    return jnp.dot(a, b, preferred_element_type=F32)


def _dot_nt(a, b):
    return lax.dot_general(a, b, (((1,), (1,)), ((), ())), preferred_element_type=F32)


def _dot_tn(a, b):
    return lax.dot_general(a, b, (((0,), (0,)), ((), ())), preferred_element_type=F32)


def _iota(shape, dim):
    return lax.broadcasted_iota(jnp.int32, shape, dim)


def _lane_pick(x, lane):
    return jnp.sum(jnp.where(_iota(x.shape, 1) == lane, x, 0.0), axis=-1, keepdims=True)


def _to_row(col):
    n = col.shape[0]
    eye = _iota((n, n), 0) == _iota((n, n), 1)
    return jnp.sum(jnp.where(eye, col, 0.0), axis=0, keepdims=True)


def _ada_kernel(c_ref, w_ref, b_ref, o_ref):
    a = _silu(c_ref[...]).astype(BF16)
    o_ref[0] = _dot(a, w_ref[0].astype(BF16)) + b_ref[0]


def _ada_call(cs, w_ada, b_ada):
    depth, d, n = w_ada.shape
    tn = 1024
    return pl.pallas_call(
        _ada_kernel,
        grid=(depth, n // tn),
        in_specs=[pl.BlockSpec((8, d), lambda l, j: (0, 0)),
                  pl.BlockSpec((1, d, tn), lambda l, j: (l, 0, j)),
                  pl.BlockSpec((1, 1, tn), lambda l, j: (l, 0, j))],
        out_specs=pl.BlockSpec((1, 8, tn), lambda l, j: (l, 0, j)),
        out_shape=jax.ShapeDtypeStruct((depth, 8, n), F32),
        compiler_params=_cparams(("arbitrary", "arbitrary")),
        name="ada",
    )(cs, w_ada, b_ada.reshape(depth, 1, n))


def _norm_kernel(x_ref, mod_ref, g_ref, h_ref):
    mod = mod_ref[0]
    h = _rms(x_ref[...], g_ref[...])
    h_ref[...] = (h * (1.0 + mod[1:2]) + mod[0:1]).astype(BF16)


def _norm_call(x, mod, gain, rows_per_mod, tm):
    m, d = x.shape
    per = rows_per_mod // tm
    return pl.pallas_call(
        _norm_kernel,
        grid=(m // tm,),
        in_specs=[pl.BlockSpec((tm, d), lambda i: (i, 0)),
                  pl.BlockSpec((1, 8, d), lambda i: (i // per, 0, 0)),
                  pl.BlockSpec((1, d), lambda i: (0, 0))],
        out_specs=pl.BlockSpec((tm, d), lambda i: (i, 0)),
        out_shape=jax.ShapeDtypeStruct((m, d), BF16),
        compiler_params=_cparams(("arbitrary",)),
        name="norm_mod",
    )(x, mod, gain)


def _inproj_kernel(a_ref, wt_ref, wg_ref, *rest, with_cast):
    if with_cast:
        f1_ref, f2_ref, o_ref, g_ref, f1o_ref, f2o_ref = rest
        f1o_ref[...] = f1_ref[0].astype(BF16)
        f2o_ref[...] = f2_ref[0].astype(BF16)
    else:
        o_ref, g_ref = rest
    a = a_ref[...]
    o_ref[...] = _dot_nt(a, wt_ref[0])

    @pl.when(pl.program_id(1) == 0)
    def _():
        g_ref[...] = _dot_nt(a, wg_ref[0])


def _inproj_call(a, wt, wgt, layer, tm, tn, name, cast=None):
    m, k = a.shape
    n = wt.shape[1]
    ng = wgt.shape[1]
    n_i, n_j = m // tm, n // tn
    in_specs = [pl.BlockSpec((tm, k), lambda i, j: (i, 0)),
                pl.BlockSpec((1, tn, k), lambda i, j: (layer, j, 0)),
                pl.BlockSpec((1, ng, k), lambda i, j: (layer, 0, 0))]
    out_specs = [pl.BlockSpec((tm, tn), lambda i, j: (i, j)),
                 pl.BlockSpec((tm, ng), lambda i, j: (i, 0))]
    out_shape = [jax.ShapeDtypeStruct((m, n), F32), jax.ShapeDtypeStruct((m, ng), F32)]
    args = [a, wt, wgt]
    if cast is not None:
        w_ffn_in, w_ffn_out = cast
        _, d, two_hidden = w_ffn_in.shape
        hidden = w_ffn_out.shape[1]
        slabs = FFN_CAST_SLABS
        assert n_i * n_j >= slabs and two_hidden % (slabs * 128) == 0 and hidden % (slabs * 16) == 0
        c1 = two_hidden // slabs
        r2 = hidden // slabs

        def slab(i, j):
            return jnp.minimum(i * n_j + j, slabs - 1)

        in_specs += [pl.BlockSpec((1, d, c1), lambda i, j: (layer, 0, slab(i, j))),
                     pl.BlockSpec((1, r2, d), lambda i, j: (layer, slab(i, j), 0))]
        out_specs += [pl.BlockSpec((d, c1), lambda i, j: (0, slab(i, j))),
                      pl.BlockSpec((r2, d), lambda i, j: (slab(i, j), 0))]
        out_shape += [jax.ShapeDtypeStruct((d, two_hidden), BF16), jax.ShapeDtypeStruct((hidden, d), BF16)]
        args += [w_ffn_in, w_ffn_out]
    return pl.pallas_call(
        functools.partial(_inproj_kernel, with_cast=cast is not None),
        grid=(n_i, n_j),
        in_specs=in_specs,
        out_specs=out_specs,
        out_shape=out_shape,
        compiler_params=_cparams(("arbitrary", "arbitrary")),
        name=name,
    )(*args)


def _outproj_kernel(ya_ref, yb_ref, ym_ref, yw_ref, w_ref, x_ref, mod_ref, g_ref, xo_ref, h_ref):
    gw = ya_ref.shape[1]
    acc = _dot(ya_ref[...], w_ref[0, 0:gw, :])
    acc += _dot(yb_ref[...], w_ref[0, gw:2 * gw, :])
    acc += _dot(ym_ref[...], w_ref[0, 2 * gw:3 * gw, :])
    acc += _dot(yw_ref[...], w_ref[0, 3 * gw:4 * gw, :])
    mod = mod_ref[0]
    xn = x_ref[...] + mod[2:3] * acc
    xo_ref[...] = xn
    h = _rms(xn, g_ref[...])
    h_ref[...] = (h * (1.0 + mod[4:5]) + mod[3:4]).astype(BF16)


def _outproj_call(ys, w, layer, x, mod, gain, rows_per_mod, tm):
    m, d = x.shape
    gw = ys[0].shape[1]
    per = rows_per_mod // tm
    yspec = pl.BlockSpec((tm, gw), lambda i: (i, 0))
    return pl.pallas_call(
        _outproj_kernel,
        grid=(m // tm,),
        in_specs=[yspec, yspec, yspec, yspec,
                  pl.BlockSpec((1, 4 * gw, d), lambda i: (layer, 0, 0)),
                  pl.BlockSpec((tm, d), lambda i: (i, 0)),
                  pl.BlockSpec((1, 8, d), lambda i: (i // per, 0, 0)),
                  pl.BlockSpec((1, d), lambda i: (0, 0))],
        out_specs=[pl.BlockSpec((tm, d), lambda i: (i, 0)),
                   pl.BlockSpec((tm, d), lambda i: (i, 0))],
        out_shape=[jax.ShapeDtypeStruct((m, d), F32), jax.ShapeDtypeStruct((m, d), BF16)],
        compiler_params=_cparams(("arbitrary",)),
        name="outproj",
    )(*ys, w, x, mod, gain)


def _ffn_kernel(h_ref, wg_ref, wu_ref, w2_ref, x_ref, mod_ref, modn_ref, gn_ref, xo_ref, hn_ref, acc_ref,
                *, with_next):
    j = pl.program_id(1)

    @pl.when(j == 0)
    def _():
        acc_ref[...] = jnp.zeros_like(acc_ref)

    h = h_ref[...]
    act = _silu(_dot(h, wg_ref[0])) * _dot(h, wu_ref[0])
    acc_ref[...] += _dot(act.astype(BF16), w2_ref[0])

    @pl.when(j == pl.num_programs(1) - 1)
    def _():
        xn = x_ref[...] + mod_ref[0][5:6] * acc_ref[...]
        xo_ref[...] = xn
        if with_next:
            modn = modn_ref[0]
            hn = _rms(xn, gn_ref[...])
            hn_ref[...] = (hn * (1.0 + modn[1:2]) + modn[0:1]).astype(BF16)
        else:
            hn_ref[...] = jnp.zeros_like(hn_ref)


def _ffn_call(h, w_in, w_out, layer, x, mod, mod_next, gain_next, rows_per_mod, tm, tn, with_next):
    m, d = x.shape
    hidden = w_out.shape[1]
    nj = hidden // tn
    per = rows_per_mod // tm
    return pl.pallas_call(
        functools.partial(_ffn_kernel, with_next=with_next),
        grid=(m // tm, nj),
        in_specs=[pl.BlockSpec((tm, d), lambda i, j: (i, 0)),
                  pl.BlockSpec((1, d, tn), lambda i, j: (layer, 0, j)),
                  pl.BlockSpec((1, d, tn), lambda i, j: (layer, 0, nj + j)),
                  pl.BlockSpec((1, tn, d), lambda i, j: (layer, j, 0)),
                  pl.BlockSpec((tm, d), lambda i, j: (i, 0)),
                  pl.BlockSpec((1, 8, d), lambda i, j: (i // per, 0, 0)),
                  pl.BlockSpec((1, 8, d), lambda i, j: (i // per, 0, 0)),
                  pl.BlockSpec((1, d), lambda i, j: (0, 0))],
        out_specs=[pl.BlockSpec((tm, d), lambda i, j: (i, 0)),
                   pl.BlockSpec((tm, d), lambda i, j: (i, 0))],
        out_shape=[jax.ShapeDtypeStruct((m, d), F32), jax.ShapeDtypeStruct((m, d), BF16)],
        scratch_shapes=[pltpu.VMEM((tm, d), F32)],
        compiler_params=_cparams(("arbitrary", "arbitrary")),
        name="ffn",
    )(h, w_in, w_in, w_out, x, mod, mod_next, gain_next)


def _na_kernel(q_ref, k_ref, v_ref, kc_ref, vc_ref, gain_ref, colbias_ref, o_ref,
               kn_scr, v_scr, kcn_scr, vc_scr, bias_scr, sl0_scr, sc0_scr, sl1_scr, sc1_scr, *, rows):
    step = pl.program_id(2)
    n_rb = pl.num_programs(2) - 1
    hd = HEAD_DIM
    heads = q_ref.shape[1] // hd

    @pl.when(step == 0)
    def _():
        sl1_scr[...] = jnp.zeros_like(sl1_scr)
        sc1_scr[...] = jnp.zeros_like(sc1_scr)
        g1 = gain_ref[1:2, :]
        closed = jnp.full((GRID_W, GRID_W), NEG_INF, F32)
        for hh in range(heads):
            cols = slice(hh * hd, (hh + 1) * hd)
            kn_scr[hh] = _rms(k_ref[:, cols], g1).astype(BF16)
            v_scr[hh] = v_ref[:, cols].astype(BF16)
            kcn_scr[hh] = _rms(kc_ref[:, cols], g1).astype(BF16)
            vc_scr[hh] = vc_ref[:, cols].astype(BF16)
            for typ, (offs, inside) in enumerate(_na_row_plan(rows)):
                for i in range(NA_QROWS):
                    for jp in range(NA_KROWS // 2):
                        halves = [colbias_ref[0, hh, offs[i][j]] if inside[i][j] else closed
                                  for j in (2 * jp, 2 * jp + 1)]
                        bias_scr[hh, typ, i * GRID_W:(i + 1) * GRID_W, 2 * jp * GRID_W:(2 * jp + 2) * GRID_W] = (
                            jnp.concatenate(halves, axis=1))

    scale = hd ** -0.5
    nk = NA_KROWS * GRID_W

    def window_start(blk):
        ws = jnp.clip(NA_QROWS * blk - NA_WIN_ROWS // 2, 0, rows - NA_KROWS)
        return pl.multiple_of(ws * GRID_W, GRID_W)

    def stages(write_loc, write_ctx, read_loc, read_ctx):
        prev = jnp.maximum(step - 1, 0)
        cur = jnp.minimum(step, n_rb - 1)
        typ = jnp.where(cur == 0, 0, jnp.where(cur == n_rb - 1, 2, 1))
        for hh in range(heads):
            cols = slice(hh * hd, (hh + 1) * hd)
            s_loc = read_loc[hh]
            s_ctx = read_ctx[hh]
            m = jnp.maximum(jnp.max(s_loc, axis=-1, keepdims=True), jnp.max(s_ctx, axis=-1, keepdims=True))
            p_loc = jnp.exp(s_loc - m)
            p_ctx = jnp.exp(s_ctx - m)
            den = jnp.sum(p_loc, axis=-1, keepdims=True) + jnp.sum(p_ctx, axis=-1, keepdims=True)
            vw = v_scr[hh, pl.ds(window_start(prev), nk), :]
            o = _dot(p_loc.astype(BF16), vw) + _dot(p_ctx.astype(BF16), vc_scr[hh])
            o_ref[:, cols] = (o / den).astype(BF16)
            qn = _rms(q_ref[:, cols], gain_ref[0:1, :]).astype(BF16)
            kw = kn_scr[hh, pl.ds(window_start(cur), nk), :]
            write_loc[hh] = _dot_nt(qn, kw) * scale + bias_scr[hh, pl.ds(typ, 1)][0]
            write_ctx[hh] = _dot_nt(qn, kcn_scr[hh]) * scale

    @pl.when(step % 2 == 0)
    def _():
        stages(sl0_scr, sc0_scr, sl1_scr, sc1_scr)

    @pl.when(step % 2 == 1)
    def _():
        stages(sl1_scr, sc1_scr, sl0_scr, sc0_scr)


def _na_row_plan(rows):
    n_rb = rows // NA_QROWS
    kh = NA_WIN_ROWS
    plan = []
    for rb in (0, 1, n_rb - 1):
        ws = min(max(NA_QROWS * rb - kh // 2, 0), rows - NA_KROWS)
        offs, inside = [], []
        for i in range(NA_QROWS):
            r = NA_QROWS * rb + i
            rs = min(max(r - kh // 2, 0), rows - kh)
            offs.append([min(max(ws + j - r, 1 - kh), kh - 1) + kh - 1 for j in range(NA_KROWS)])
            inside.append([rs <= ws + j < rs + kh for j in range(NA_KROWS)])
        plan.append((offs, inside))
    return plan


def _na_col_tables(rpb):
    kw = NA_WIN_COLS
    c = jnp.arange(GRID_W)
    cs = jnp.clip(c - kw // 2, 0, GRID_W - kw)
    in_col = (c[None, :] >= cs[:, None]) & (c[None, :] < cs[:, None] + kw)
    dc = jnp.clip(c[None, :] - c[:, None], 1 - kw, kw - 1) + kw - 1
    dc_hot = jax.nn.one_hot(dc, 2 * kw - 1, dtype=F32)
    val = jnp.einsum("lhab,ckb->lhack", rpb.astype(F32), dc_hot, precision=lax.Precision.HIGHEST)
    return jnp.where(in_col, val, NEG_INF)


def _na_call(p_lat, p_ctx, gain, colbias, layer, batch, seq, lc):
    rows = seq // GRID_W
    n_rb = rows // NA_QROWS
    tq = NA_QROWS * GRID_W
    nk = NA_KROWS * GRID_W
    heads = colbias.shape[1]
    n_off = colbias.shape[2]
    hd = HEAD_DIM
    hp = NA_HEADS_PER_STEP
    wide = hp * hd
    assert heads % hp == 0 and NA_Q % hp == 0 and NA_K % hp == 0 and NA_V % hp == 0
    return pl.pallas_call(
        functools.partial(_na_kernel, rows=rows),
        grid=(batch, heads // hp, n_rb + 1),
        in_specs=[pl.BlockSpec((tq, wide), lambda b, h, r: (b * n_rb + jnp.minimum(r, n_rb - 1), NA_Q // hp + h)),
                  pl.BlockSpec((seq, wide), lambda b, h, r: (b, NA_K // hp + h)),
                  pl.BlockSpec((seq, wide), lambda b, h, r: (b, NA_V // hp + h)),
                  pl.BlockSpec((lc, wide), lambda b, h, r: (b, NA_K // hp + h)),
                  pl.BlockSpec((lc, wide), lambda b, h, r: (b, NA_V // hp + h)),
                  pl.BlockSpec((2, hd), lambda b, h, r: (0, 0)),
                  pl.BlockSpec((1, hp, n_off, GRID_W, GRID_W), lambda b, h, r: (layer, h, 0, 0, 0))],
        out_specs=pl.BlockSpec((tq, wide), lambda b, h, r: (b * n_rb + jnp.maximum(r - 1, 0), h)),
        out_shape=jax.ShapeDtypeStruct((batch * seq, heads * hd), BF16),
        scratch_shapes=[pltpu.VMEM((hp, seq, hd), BF16), pltpu.VMEM((hp, seq, hd), BF16),
                        pltpu.VMEM((hp, lc, hd), BF16), pltpu.VMEM((hp, lc, hd), BF16),
                        pltpu.VMEM((hp, 3, tq, nk), F32),
                        pltpu.VMEM((hp, tq, nk), F32), pltpu.VMEM((hp, tq, lc), F32),
                        pltpu.VMEM((hp, tq, nk), F32), pltpu.VMEM((hp, tq, lc), F32)],
        compiler_params=_cparams(("arbitrary", "arbitrary", "arbitrary")),
        name="na_attn",
    )(p_lat, p_lat, p_lat, p_ctx, p_ctx, gain, colbias)


def _ctx_attn_kernel(sink_ref, q_ref, k_ref, v_ref, gain_ref, o_ref, *, use_sink):
    scale = HEAD_DIM ** -0.5
    qn = _rms(q_ref[...], gain_ref[0:1, :]).astype(BF16)
    kn = _rms(k_ref[...], gain_ref[1:2, :]).astype(BF16)
    s = _dot_nt(qn, kn) * scale
    m = jnp.max(s, axis=-1, keepdims=True)
    if use_sink:
        sk = sink_ref[pl.program_id(1)]
        m = jnp.maximum(m, sk)
    p = jnp.exp(s - m)
    den = jnp.sum(p, axis=-1, keepdims=True)
    if use_sink:
        den = den + jnp.exp(sk - m)
    o = _dot(p.astype(BF16), v_ref[...].astype(BF16))
    o_ref[...] = (o / den).astype(BF16)


def _ctx_attn_call(p_ctx, gain, sink, batch, lc, heads, rep, qoff, koff, voff, use_sink, name):
    hd = HEAD_DIM
    return pl.pallas_call(
        functools.partial(_ctx_attn_kernel, use_sink=use_sink),
        grid=(batch, heads),
        in_specs=[pl.BlockSpec(memory_space=pltpu.SMEM),
                  pl.BlockSpec((lc, hd), lambda b, h: (b, qoff + h)),
                  pl.BlockSpec((lc, hd), lambda b, h: (b, koff + h // rep)),
                  pl.BlockSpec((lc, hd), lambda b, h: (b, voff + h // rep)),
                  pl.BlockSpec((2, hd), lambda b, h: (0, 0))],
        out_specs=pl.BlockSpec((lc, hd), lambda b, h: (b, h)),
        out_shape=jax.ShapeDtypeStruct((batch * lc, heads * hd), BF16),
        compiler_params=_cparams(("arbitrary", "arbitrary")),
        name=name,
    )(sink, p_ctx, p_ctx, p_ctx, gain)


def _rope(x, cos, sin_signed):
    lane = _iota(x.shape, 1)
    swapped = jnp.where(lane % 64 < 32, pltpu.roll(x, 96, axis=1), pltpu.roll(x, 32, axis=1))
    return x * cos + swapped * sin_signed


def _wa_kernel(sink_ref, q_ref, k_ref, v_ref, kc_ref, vc_ref, gain_ref, cq_ref, sq_ref, ck_ref, sk_ref, o_ref,
               kpad_scr, vpad_scr, kcn_scr, vc_scr, sl0_scr, sc0_scr, sl1_scr, sc1_scr, *, seq):
    step = pl.program_id(1)
    nb = pl.num_programs(1) - 1
    wb = WA_BLOCK
    hd = HEAD_DIM
    groups = k_ref.shape[1] // hd

    @pl.when(step == 0)
    def _():
        sl1_scr[...] = jnp.zeros_like(sl1_scr)
        sc1_scr[...] = jnp.zeros_like(sc1_scr)
        g1 = gain_ref[1:2, :]
        zeros = jnp.zeros((wb, hd), BF16)
        for g in range(groups):
            cols = slice(g * hd, (g + 1) * hd)
            kpad_scr[g, 0:wb, :] = zeros
            kpad_scr[g, seq + wb:seq + 2 * wb, :] = zeros
            vpad_scr[g, 0:wb, :] = zeros
            vpad_scr[g, seq + wb:seq + 2 * wb, :] = zeros
            kpad_scr[g, wb:seq + wb, :] = _rope(_rms(k_ref[:, cols], g1), ck_ref[...], sk_ref[...]).astype(BF16)
            vpad_scr[g, wb:seq + wb, :] = v_ref[:, cols].astype(BF16)
            kcn_scr[g] = _rms(kc_ref[:, cols], g1).astype(BF16)
            vc_scr[g] = vc_ref[:, cols].astype(BF16)

    scale = hd ** -0.5

    def stages(write_loc, write_ctx, read_loc, read_ctx):
        prev = jnp.maximum(step - 1, 0)
        cur = jnp.minimum(step, nb - 1)
        g0 = gain_ref[0:1, :]
        cq = cq_ref[...]
        sq = sq_ref[...]
        shp = (2 * wb, 3 * wb)
        i = _iota(shp, 0) % wb
        j = _iota(shp, 1)
        kpos = (cur - 1) * wb + j
        ok = (j - i >= 0) & (j - i <= 2 * wb) & (kpos >= 0) & (kpos < seq)
        for g in range(groups):
            s_loc = read_loc[g]
            s_ctx = read_ctx[g]
            sink = jnp.where(_iota((2 * wb, 1), 0) < wb, sink_ref[2 * g], sink_ref[2 * g + 1])
            m = jnp.maximum(jnp.maximum(jnp.max(s_loc, axis=-1, keepdims=True),
                                        jnp.max(s_ctx, axis=-1, keepdims=True)), sink)
            p_loc = jnp.exp(s_loc - m)
            p_ctx = jnp.exp(s_ctx - m)
            den = jnp.sum(p_loc, axis=-1, keepdims=True) + jnp.sum(p_ctx, axis=-1, keepdims=True) + jnp.exp(sink - m)
            vw = vpad_scr[g, pl.ds(pl.multiple_of(prev * wb, wb), 3 * wb), :]
            o = (_dot(p_loc.astype(BF16), vw) + _dot(p_ctx.astype(BF16), vc_scr[g])) / den
            o_ref[:, 2 * g * hd:(2 * g + 1) * hd] = o[0:wb].astype(BF16)
            o_ref[:, (2 * g + 1) * hd:(2 * g + 2) * hd] = o[wb:2 * wb].astype(BF16)
            q2 = jnp.concatenate([_rope(_rms(q_ref[:, 2 * g * hd:(2 * g + 1) * hd], g0), cq, sq),
                                  _rope(_rms(q_ref[:, (2 * g + 1) * hd:(2 * g + 2) * hd], g0), cq, sq)],
                                 axis=0).astype(BF16)
            kw = kpad_scr[g, pl.ds(pl.multiple_of(cur * wb, wb), 3 * wb), :]
            write_loc[g] = jnp.where(ok, _dot_nt(q2, kw) * scale, NEG_INF)
            write_ctx[g] = _dot_nt(q2, kcn_scr[g]) * scale

    @pl.when(step % 2 == 0)
    def _():
        stages(sl0_scr, sc0_scr, sl1_scr, sc1_scr)

    @pl.when(step % 2 == 1)
    def _():
        stages(sl1_scr, sc1_scr, sl0_scr, sc0_scr)


def _wa_call(p_lat, p_ctx, gain, sink, cos_t, sin_t, batch, seq, lc):
    wb = WA_BLOCK
    hd = HEAD_DIM
    nb = seq // wb
    kvh = 2
    qw = 2 * kvh * hd
    kw_ = kvh * hd
    assert (WA_Q * hd) % qw == 0 and (WA_K * hd) % kw_ == 0 and (WA_V * hd) % kw_ == 0
    return pl.pallas_call(
        functools.partial(_wa_kernel, seq=seq),
        grid=(batch, nb + 1),
        in_specs=[pl.BlockSpec(memory_space=pltpu.SMEM),
                  pl.BlockSpec((wb, qw), lambda b, n: (b * nb + jnp.minimum(n, nb - 1), WA_Q * hd // qw)),
                  pl.BlockSpec((seq, kw_), lambda b, n: (b, WA_K * hd // kw_)),
                  pl.BlockSpec((seq, kw_), lambda b, n: (b, WA_V * hd // kw_)),
                  pl.BlockSpec((lc, kw_), lambda b, n: (b, WA_K * hd // kw_)),
                  pl.BlockSpec((lc, kw_), lambda b, n: (b, WA_V * hd // kw_)),
                  pl.BlockSpec((2, hd), lambda b, n: (0, 0)),
                  pl.BlockSpec((wb, hd), lambda b, n: (jnp.minimum(n, nb - 1), 0)),
                  pl.BlockSpec((wb, hd), lambda b, n: (jnp.minimum(n, nb - 1), 0)),
                  pl.BlockSpec((seq, hd), lambda b, n: (0, 0)),
                  pl.BlockSpec((seq, hd), lambda b, n: (0, 0))],
        out_specs=pl.BlockSpec((wb, qw), lambda b, n: (b * nb + jnp.maximum(n - 1, 0), 0)),
        out_shape=jax.ShapeDtypeStruct((batch * seq, qw), BF16),
        scratch_shapes=[pltpu.VMEM((kvh, seq + 2 * wb, hd), BF16), pltpu.VMEM((kvh, seq + 2 * wb, hd), BF16),
                        pltpu.VMEM((kvh, lc, hd), BF16), pltpu.VMEM((kvh, lc, hd), BF16),
                        pltpu.VMEM((kvh, 2 * wb, 3 * wb), F32), pltpu.VMEM((kvh, 2 * wb, lc), F32),
                        pltpu.VMEM((kvh, 2 * wb, 3 * wb), F32), pltpu.VMEM((kvh, 2 * wb, lc), F32)],
        compiler_params=_cparams(("arbitrary", "arbitrary")),
        name="wa_attn",
    )(sink, p_lat, p_lat, p_lat, p_ctx, p_ctx, gain, cos_t, sin_t, cos_t, sin_t)


def _dn_prep_kernel(x_ref, prev_ref, next_ref, w_ref, o_ref, *, n_tiles):
    t = pl.program_id(1)
    tl = x_ref.shape[0]
    x = x_ref[...]
    prev = jnp.where(t > 0, prev_ref[...], 0.0)
    nxt = jnp.where(t < n_tiles - 1, next_ref[...], 0.0)
    xe = jnp.concatenate([prev, x, nxt], axis=0)
    n = tl + 16
    taps = w_ref.shape[0]
    acc = jnp.zeros_like(x)
    for j in range(taps):
        off = 8 - taps // 2 + j
        shifted = xe if off % n == 0 else pltpu.roll(xe, n - off, axis=0)
        acc += shifted[0:tl] * w_ref[j:j + 1, :]
    y = _silu(acc)
    hd = HEAD_DIM
    nh = y.shape[1] // (3 * hd)
    for blk in range(3 * nh):
        yb = y[:, blk * hd:(blk + 1) * hd]
        if blk < 2 * nh:
            yb = yb * lax.rsqrt(jnp.sum(yb * yb, axis=-1, keepdims=True) + EPS)
            if blk < nh:
                yb = yb * hd ** -0.5
        o_ref[:, blk * hd:(blk + 1) * hd] = yb


def _dn_prep_call(p, conv_w, batch, length, tl):
    width = conv_w.shape[1]
    n_tiles = length // tl
    per8 = tl // 8
    last8 = batch * length // 8 - 1
    return pl.pallas_call(
        functools.partial(_dn_prep_kernel, n_tiles=n_tiles),
        grid=(batch, n_tiles),
        in_specs=[pl.BlockSpec((tl, width), lambda b, t: (b * n_tiles + t, 1)),
                  pl.BlockSpec((8, width), lambda b, t: (jnp.maximum((b * n_tiles + t) * per8 - 1, 0), 1)),
                  pl.BlockSpec((8, width), lambda b, t: (jnp.minimum((b * n_tiles + t + 1) * per8, last8), 1)),
                  pl.BlockSpec((conv_w.shape[0], width), lambda b, t: (0, 0))],
        out_specs=pl.BlockSpec((tl, width), lambda b, t: (b * n_tiles + t, 0)),
        out_shape=jax.ShapeDtypeStruct((batch * length, width), F32),
        compiler_params=_cparams(("arbitrary", "arbitrary")),
        name="dn_prep",
    )(p, p, p, conv_w)


def _solve_dot(a, b):
    return _dot(a.astype(BF16), b.astype(BF16))


def _dn_pass(q_ref, k_ref, v_ref, og_ref, gates_ref, gadd, gmul, norm_g, y_ref, length, h, state,
             lin_scr, add_scr, egl_scr, acc_scr):
    assert CHUNK == HEAD_DIM
    n_chunks = length // CHUNK

    acc_scr[0:length, :] = jnp.zeros((length, HEAD_DIM), F32)

    shp = (CHUNK, CHUNK)
    ri = _iota(shp, 0)
    ci = _iota(shp, 1)
    masks = ((ci <= ri, ci < ri), (ci >= ri, ci > ri))
    couple = ([], [])
    s = 1
    while s < CHUNK:
        same = (ri // (2 * s)) == (ci // (2 * s))
        couple[0].append(same & ((ri // s) % 2 == 1) & ((ci // s) % 2 == 0))
        couple[1].append(same & ((ri // s) % 2 == 0) & ((ci // s) % 2 == 1))
        s *= 2
    group = min(n_chunks, PREP_GROUP)

    def prep(gi, carry):
        chains = []
        for g in range(group):
            c = gi * group + g
            r0 = pl.multiple_of(c * CHUNK, CHUNK)
            q = q_ref[pl.ds(r0, CHUNK), :]
            k = k_ref[pl.ds(r0, CHUNK), :]
            raw = gates_ref[pl.ds(r0, CHUNK), :]
            gb = jnp.where(_iota(raw.shape, 1) < LANE_ALPHA, jax.nn.sigmoid(raw), -gmul * _softplus(raw + gadd))
            k16 = k.astype(BF16)
            kk = _dot_nt(k16, k16)
            qk = _dot_nt(q.astype(BF16), k16)
            for d in range(2):
                incl, strict = masks[d]
                beta_col = _lane_pick(gb, LANE_BETA + 4 * d + h)
                g_col = _lane_pick(gb, LANE_ALPHA + 4 * d + h)
                beta_row = _to_row(beta_col)
                g_row = _to_row(g_col)
                cum_col = jnp.sum(jnp.where(incl, g_row, 0.0), axis=-1, keepdims=True)
                cum_row = _to_row(cum_col)
                tot = jnp.sum(g_col, axis=0, keepdims=True)
                decay = jnp.exp(jnp.where(incl, cum_col - cum_row, NEG_INF))
                lm = jnp.where(strict, kk * beta_col * decay, 0.0)
                egl_scr[d, pl.ds(pl.multiple_of(c * 8, 8), 8), :] = jnp.broadcast_to(jnp.exp(tot), (8, HEAD_DIM))
                chains.append((d, c, r0, lm, beta_row, beta_row * jnp.exp(cum_row), (qk * decay).astype(BF16),
                               q * jnp.exp(cum_col), k * jnp.exp(tot - cum_col)))
        eye = (ri == ci).astype(F32)
        xs = [eye - jnp.where(couple[ch[0]][0], ch[3], 0.0) for ch in chains]
        for lvl in range(1, len(couple[0])):
            cs = [jnp.where(couple[ch[0]][lvl], ch[3], 0.0) for ch in chains]
            xc = [_solve_dot(x, c) for x, c in zip(xs, cs)]
            xs = [x - _solve_dot(t, x) for x, t in zip(xs, xc)]
        u16 = [_solve_dot(ainv * ch[4], v_ref[pl.ds(ch[2], CHUNK), :]).astype(BF16) for ch, ainv in zip(chains, xs)]
        w16 = [_solve_dot(ainv * ch[5], k_ref[pl.ds(ch[2], CHUNK), :]).astype(BF16) for ch, ainv in zip(chains, xs)]
        ket = [ch[8].T.astype(BF16) for ch in chains]
        s_lin = [_dot(kt, w) for kt, w in zip(ket, w16)]
        s_add = [_dot(kt, u) for kt, u in zip(ket, u16)]
        o_lin = [ch[7] - _dot(ch[6], w) for ch, w in zip(chains, w16)]
        o_add = [_dot(ch[6], u) for ch, u in zip(chains, u16)]
        for ch, sl, sa, ol, oa in zip(chains, s_lin, s_add, o_lin, o_add):
            d = ch[0]
            base = pl.multiple_of(ch[1] * 2 * CHUNK, 2 * CHUNK)
            lin_scr[d, pl.ds(base, CHUNK), :] = sl.astype(BF16)
            lin_scr[d, pl.ds(base + CHUNK, CHUNK), :] = ol.astype(BF16)
            add_scr[d, pl.ds(base, CHUNK), :] = sa
            add_scr[d, pl.ds(base + CHUNK, CHUNK), :] = oa
        return carry

    lax.fori_loop(0, n_chunks // group, prep, 0)

    def scan(s, st):
        new = []
        for d in range(2):
            c = s if d == 0 else n_chunks - 1 - s
            base = pl.multiple_of(c * 2 * CHUNK, 2 * CHUNK)
            sm = st[d]
            prod = _dot(lin_scr[d, pl.ds(base, 2 * CHUNK), :], sm.astype(BF16))
            add = add_scr[d, pl.ds(base, 2 * CHUNK), :]
            acc_scr[pl.ds(pl.multiple_of(c * CHUNK, CHUNK), CHUNK), :] += prod[CHUNK:] + add[CHUNK:]
            egl = egl_scr[d, pl.ds(pl.multiple_of(c * 8, 8), 8), :][0:1, :]
            new.append(sm * egl - prod[:CHUNK] + add[:CHUNK])
        return tuple(new)

    state = lax.fori_loop(0, n_chunks, scan, state, unroll=SCAN_UNROLL)

    y = _rms(acc_scr[0:length, :], norm_g) * _silu(og_ref[...])
    y_ref[...] = y.astype(BF16)
    return state


def _dn_kernel(q_ref, k_ref, v_ref, og_ref, gates_ref, qc_ref, kc_ref, vc_ref, ogc_ref, gatesc_ref,
               gadd_ref, gmul_ref, norm_ref, y_ref, yc_ref,
               lin_scr, add_scr, egl_scr, acc_scr, *, seq, lc):
    h = pl.program_id(1)
    gadd = gadd_ref[...]
    gmul = gmul_ref[...]
    norm_g = norm_ref[...]
    scr = (lin_scr, add_scr, egl_scr, acc_scr)
    zero = jnp.zeros((HEAD_DIM, HEAD_DIM), F32)
    state = _dn_pass(qc_ref, kc_ref, vc_ref, ogc_ref, gatesc_ref, gadd, gmul, norm_g, yc_ref, lc, h, (zero, zero), *scr)
    _dn_pass(q_ref, k_ref, v_ref, og_ref, gates_ref, gadd, gmul, norm_g, y_ref, seq, h, state, *scr)


def _dn_call(dq_lat, dq_ctx, p_lat, p_ctx, pg_lat, pg_ctx, gadd, gmul, norm_g, batch, seq, lc):
    hd = HEAD_DIM
    heads = 4

    def blk(n, col):
        return pl.BlockSpec((n, hd), lambda b, h: (b, col + h))

    def gate_blk(n):
        return pl.BlockSpec((n, hd), lambda b, h: (b, 0))

    vec = pl.BlockSpec((1, hd), lambda b, h: (0, 0))
    return pl.pallas_call(
        functools.partial(_dn_kernel, seq=seq, lc=lc),
        grid=(batch, heads),
        in_specs=[blk(seq, 0), blk(seq, heads), blk(seq, 2 * heads), blk(seq, DN_OG), gate_blk(seq),
                  blk(lc, 0), blk(lc, heads), blk(lc, 2 * heads), blk(lc, DN_OG), gate_blk(lc),
                  vec, vec, vec],
        out_specs=[pl.BlockSpec((seq, hd), lambda b, h: (b, h)),
                   pl.BlockSpec((lc, hd), lambda b, h: (b, h))],
        out_shape=[jax.ShapeDtypeStruct((batch * seq, heads * hd), BF16),
                   jax.ShapeDtypeStruct((batch * lc, heads * hd), BF16)],
        scratch_shapes=[pltpu.VMEM((2, 2 * seq, hd), BF16),
                        pltpu.VMEM((2, 2 * seq, hd), F32),
                        pltpu.VMEM((2, seq // CHUNK * 8, hd), F32),
                        pltpu.VMEM((seq, hd), F32)],
        compiler_params=_cparams(("arbitrary", "arbitrary"), DN_VMEM_LIMIT),
        name="deltanet",
    )(dq_lat, dq_lat, dq_lat, p_lat, pg_lat, dq_ctx, dq_ctx, dq_ctx, p_ctx, pg_ctx, gadd, gmul, norm_g)


def _ml_pass(q_ref, k_ref, v_ref, og_ref, gates_ref, gadd, norm_g, y_ref, length, h, state, acc_scr):
    n_chunks = length // CHUNK
    acc_scr[0:length, :] = jnp.zeros((length, HEAD_DIM), F32)
    shp = (CHUNK, CHUNK)
    ri = _iota(shp, 0)
    ci = _iota(shp, 1)
    masks = (ci <= ri, ci >= ri)
    head_lanes = (_iota((1, HEAD_DIM), 1) // ML_QK) == (h % 2)

    group = min(n_chunks, ML_GROUP)

    def each(f, *lists):
        return [f(*a) for a in zip(*lists)]

    def step(gi, st):
        dirs, r0s = [], []
        for j in range(group):
            s = gi * group + j
            for d in range(2):
                c = s if d == 0 else n_chunks - 1 - s
                dirs.append(d)
                r0s.append(pl.multiple_of(c * CHUNK, CHUNK))
        incl = [masks[d] for d in dirs]
        q = [jnp.where(head_lanes, q_ref[pl.ds(r0, CHUNK), :], 0.0) for r0 in r0s]
        k = [jnp.where(head_lanes, k_ref[pl.ds(r0, CHUNK), :], 0.0) * ML_QK ** -0.5 for r0 in r0s]
        v16 = [v_ref[pl.ds(r0, CHUNK), :].astype(BF16) for r0 in r0s]
        gb = [gates_ref[pl.ds(r0, CHUNK), :] + gadd for r0 in r0s]
        ig_col = [_lane_pick(g, LANE_IG + 4 * d + h) for g, d in zip(gb, dirs)]
        lf_col = [_log_sigmoid(_lane_pick(g, LANE_FG + 4 * d + h)) for g, d in zip(gb, dirs)]
        ig_row = each(_to_row, ig_col)
        lf_row = each(_to_row, lf_col)
        b_col = each(lambda m, r: jnp.sum(jnp.where(m, r, 0.0), axis=-1, keepdims=True), incl, lf_row)
        b_row = each(_to_row, b_col)
        b_last = each(lambda x: jnp.sum(x, axis=0, keepdims=True), lf_col)
        g_end = each(lambda bl, b, i: bl - b + i, b_last, b_col, ig_col)
        m_loc = each(lambda x: jnp.max(x, axis=0, keepdims=True), g_end)
        kw = each(lambda kk, g, m: kk * jnp.exp(g - m), k, g_end, m_loc)
        q16 = each(lambda x: x.astype(BF16), q)
        k16 = each(lambda x: x.astype(BF16), k)
        c_loc = each(lambda a, b: _dot_tn(a.astype(BF16), b), kw, v16)
        n_loc = each(lambda x: jnp.sum(x, axis=0, keepdims=True), kw)
        log_d = each(lambda m, bc, br, ir: jnp.where(m, bc - br + ir, NEG_INF), incl, b_col, b_row, ig_row)
        mx_col = each(lambda x: jnp.max(x, axis=-1, keepdims=True), log_d)
        s0 = each(lambda a, b, ld, mx: _dot_nt(a, b) * jnp.exp(ld - mx), q16, k16, log_d, mx_col)
        sv0 = each(lambda s, vv: _dot(s.astype(BF16), vv), s0, v16)
        rs0 = each(lambda s: jnp.sum(s, axis=-1, keepdims=True), s0)
        st = list(st)
        for i, (d, r0) in enumerate(zip(dirs, r0s)):
            c_st, n_st, m_st = st[d]
            m_inter = b_col[i] + m_st
            m_t = jnp.maximum(mx_col[i], m_inter)
            intra = jnp.exp(mx_col[i] - m_t)
            inter = jnp.exp(m_inter - m_t)
            num = intra * sv0[i] + inter * _dot(q16[i], c_st.astype(BF16))
            den = intra * rs0[i] + inter * jnp.sum(q[i] * n_st, axis=-1, keepdims=True)
            acc_scr[pl.ds(r0, CHUNK), :] += num / jnp.maximum(jnp.abs(den), jnp.exp(-m_t))
            m_new = jnp.maximum(b_last[i] + m_st, m_loc[i])
            a = jnp.exp(b_last[i] + m_st - m_new)
            e = jnp.exp(m_loc[i] - m_new)
            st[d] = (a * c_st + e * c_loc[i], a * n_st + e * n_loc[i], m_new)
        return tuple(st)

    state = lax.fori_loop(0, n_chunks // group, step, state)
    y = jax.nn.sigmoid(og_ref[...]) * _rms(acc_scr[0:length, :], norm_g)
    y_ref[...] = y.astype(BF16)
    return state


def _ml_kernel(q_ref, k_ref, v_ref, og_ref, gates_ref, qc_ref, kc_ref, vc_ref, ogc_ref, gatesc_ref,
               gadd_ref, norm_ref, y_ref, yc_ref, acc_scr, *, seq, lc):
    h = pl.program_id(1)
    gadd = gadd_ref[...]
    norm_g = norm_ref[0]
    zero = (jnp.zeros((HEAD_DIM, HEAD_DIM), F32), jnp.zeros((1, HEAD_DIM), F32), jnp.zeros((1, 1), F32))
    state = _ml_pass(qc_ref, kc_ref, vc_ref, ogc_ref, gatesc_ref, gadd, norm_g, yc_ref, lc, h, (zero, zero), acc_scr)
    _ml_pass(q_ref, k_ref, v_ref, og_ref, gates_ref, gadd, norm_g, y_ref, seq, h, state, acc_scr)


def _ml_call(p_lat, p_ctx, pg_lat, pg_ctx, gadd, norm_g, batch, seq, lc):
    hd = HEAD_DIM
    heads = 4

    def specs(n):
        return [pl.BlockSpec((n, hd), lambda b, h: (b, ML_Q + h // 2)),
                pl.BlockSpec((n, hd), lambda b, h: (b, ML_K + h // 2)),
                pl.BlockSpec((n, hd), lambda b, h: (b, ML_V + h)),
                pl.BlockSpec((n, hd), lambda b, h: (b, ML_OG + h)),
                pl.BlockSpec((n, hd), lambda b, h: (b, 0))]

    return pl.pallas_call(
        functools.partial(_ml_kernel, seq=seq, lc=lc),
        grid=(batch, heads),
        in_specs=specs(seq) + specs(lc) + [pl.BlockSpec((1, hd), lambda b, h: (0, 0)),
                                            pl.BlockSpec((1, 1, hd), lambda b, h: (h, 0, 0))],
        out_specs=[pl.BlockSpec((seq, hd), lambda b, h: (b, h)),
                   pl.BlockSpec((lc, hd), lambda b, h: (b, h))],
        out_shape=[jax.ShapeDtypeStruct((batch * seq, heads * hd), BF16),
                   jax.ShapeDtypeStruct((batch * lc, heads * hd), BF16)],
        scratch_shapes=[pltpu.VMEM((seq, hd), F32)],
        compiler_params=_cparams(("arbitrary", "arbitrary")),
        name="mlstm",
    )(p_lat, p_lat, p_lat, p_lat, pg_lat, p_ctx, p_ctx, p_ctx, p_ctx, pg_ctx, gadd, norm_g.reshape(heads, 1, hd))


def _regroup_kernel(wt_ref, main_ref, gate_ref, *, cuts):
    (a0, a1), (b0, b1), (c0, c1), (g0, g1), (h0, h1) = cuts
    main_ref[0, 0:a1 - a0, :] = wt_ref[0, a0:a1, :].astype(BF16)
    main_ref[0, a1 - a0:a1 - a0 + b1 - b0, :] = wt_ref[0, b0:b1, :].astype(BF16)
    main_ref[0, a1 - a0 + b1 - b0:, :] = wt_ref[0, c0:c1, :].astype(BF16)
    ng = (g1 - g0) + (h1 - h0)
    gate_ref[0, 0:g1 - g0, :] = wt_ref[0, g0:g1, :].astype(BF16)
    gate_ref[0, g1 - g0:ng, :] = wt_ref[0, h0:h1, :].astype(BF16)
    gate_ref[0, ng:, :] = jnp.zeros((gate_ref.shape[1] - ng, gate_ref.shape[2]), BF16)


def _regroup_call(w_in):
    depth, d, n = w_in.shape
    gw = 512
    dn_gate0 = 3 * gw + 4 * gw
    ml0 = dn_gate0 + 16
    ml_gate0 = ml0 + 3 * gw
    wa0 = ml_gate0 + 16
    cuts = ((0, dn_gate0), (ml0, ml_gate0), (wa0, n), (dn_gate0, ml0), (ml_gate0, wa0))
    tc = 256
    return pl.pallas_call(
        functools.partial(_regroup_kernel, cuts=cuts),
        grid=(depth, d // tc),
        in_specs=[pl.BlockSpec((1, n, tc), lambda l, i: (l, 0, i))],
        out_specs=[pl.BlockSpec((1, MAIN_COLS, tc), lambda l, i: (l, 0, i)),
                   pl.BlockSpec((1, 128, tc), lambda l, i: (l, 0, i))],
        out_shape=[jax.ShapeDtypeStruct((depth, MAIN_COLS, d), BF16), jax.ShapeDtypeStruct((depth, 128, d), BF16)],
        compiler_params=_cparams(("arbitrary", "arbitrary")),
        name="regroup_w_in",
    )(jnp.swapaxes(w_in, 1, 2))


def _rope_tables(seq):
    t = jnp.arange(seq)
    n_freq = HEAD_DIM // 4
    inv_freq = ROPE_THETA ** (-jnp.arange(n_freq, dtype=F32) / n_freq)
    pos = jnp.stack([t // GRID_W, t % GRID_W], axis=-1).astype(F32)
    ang = pos[:, :, None] * inv_freq
    cos, sin = jnp.cos(ang), jnp.sin(ang)
    cos_t = jnp.concatenate([cos, cos], axis=-1).reshape(seq, HEAD_DIM)
    sin_t = jnp.concatenate([-sin, sin], axis=-1).reshape(seq, HEAD_DIM)
    return cos_t, sin_t


def _gate_vectors(dt_bias, a_log, i_bias, f_bias):
    zeros8 = jnp.zeros((8,), F32)
    gadd = jnp.concatenate([zeros8, dt_bias.reshape(-1), i_bias.reshape(-1), f_bias.reshape(-1),
                            jnp.zeros((96,), F32)]).astype(F32)
    gmul = jnp.concatenate([zeros8, jnp.exp(a_log.astype(F32)).reshape(-1), jnp.zeros((112,), F32)])
    return gadd.reshape(1, 128), gmul.reshape(1, 128)


def _forward(x, c, ctx, c_ctx, w_ada, b_ada, norm_mix, norm_ffn, w_in, w_out, na_qk_gain, na_rpb, dn_conv,
             dn_a_log, dn_dt_bias, dn_norm, ml_i_bias, ml_f_bias, ml_norm, wa_qk_gain, wa_sink, w_ffn_in,
             w_ffn_out):
    batch, seq, d = x.shape
    lc = ctx.shape[1]
    depth = w_ada.shape[0]
    tm = 512
    tn_ffn = 512

    cos_t, sin_t = _rope_tables(seq)
    na_bias = _na_col_tables(na_rpb)
    cs = jnp.concatenate([c, c_ctx[None, :], jnp.zeros((8 - batch - 1, d), F32)], axis=0)
    mods = _ada_call(cs, w_ada, b_ada).reshape(depth, 8, 6, d)
    mods = jnp.pad(mods, ((0, 0), (0, 0), (0, 2), (0, 0)))

    w_main, w_gate = _regroup_call(w_in)
    w_o = w_out.astype(BF16)

    xl = x.reshape(batch * seq, d)
    xc = ctx.reshape(batch * lc, d)
    hl = _norm_call(xl, mods[0, :batch], norm_mix[0:1], seq, tm)
    hc = _norm_call(xc, mods[0, batch:batch + 1], norm_mix[0:1], batch * lc, tm)

    for l in range(depth):
        need_ctx = l < depth - 1
        mod_l, mod_c = mods[l, :batch], mods[l, batch:batch + 1]
        p_lat, pg_lat, w_f1, w_f2 = _inproj_call(hl, w_main, w_gate, l, 2 * tm, 1024, "inproj",
                                                 cast=(w_ffn_in, w_ffn_out))
        w_f1 = w_f1[None]
        w_f2 = w_f2[None]
        p_ctx, pg_ctx = _inproj_call(hc, w_main, w_gate, l, tm, 2048, "inproj_ctx")

        ya = _na_call(p_lat, p_ctx, na_qk_gain[l], na_bias, l, batch, seq, lc)
        gadd, gmul = _gate_vectors(dn_dt_bias[l], dn_a_log[l], ml_i_bias[l], ml_f_bias[l])
        dq_lat = _dn_prep_call(p_lat, dn_conv[l], batch, seq, 512)
        dq_ctx = _dn_prep_call(p_ctx, dn_conv[l], batch, lc, 256)
        yb, ycb = _dn_call(dq_lat, dq_ctx, p_lat, p_ctx, pg_lat, pg_ctx, gadd, gmul, dn_norm[l].reshape(1, -1),
                           batch, seq, lc)
        ym, ycm = _ml_call(p_lat, p_ctx, pg_lat, pg_ctx, gadd, ml_norm[l], batch, seq, lc)
        yw = _wa_call(p_lat, p_ctx, wa_qk_gain[l], wa_sink[l], cos_t, sin_t, batch, seq, lc)

        nl = min(l + 1, depth - 1)
        xl, h2 = _outproj_call((ya, yb, ym, yw), w_o, l, xl, mod_l, norm_ffn[l:l + 1], seq, tm)
        xl, hl = _ffn_call(h2, w_f1, w_f2, 0, xl, mod_l, mods[nl, :batch], norm_mix[nl:nl + 1], seq, tm, tn_ffn,
                           need_ctx)
        if need_ctx:
            yca = _ctx_attn_call(p_ctx, na_qk_gain[l], wa_sink[l], batch, lc, 4, 1, NA_Q, NA_K, NA_V, False,
                                 "na_ctx_attn")
            ycw = _ctx_attn_call(p_ctx, wa_qk_gain[l], wa_sink[l], batch, lc, 4, 2, WA_Q, WA_K, WA_V, True,
                                 "wa_ctx_attn")
            xc, h2c = _outproj_call((yca, ycb, ycm, ycw), w_o, l, xc, mod_c, norm_ffn[l:l + 1], batch * lc, tm)
            xc, hc = _ffn_call(h2c, w_f1, w_f2, 0, xc, mod_c, mods[nl, batch:batch + 1], norm_mix[nl:nl + 1],
                               batch * lc, tm, tn_ffn, True)
    return xl.reshape(batch, seq, d)


def kernel(x, c, ctx, c_ctx, w_ada, b_ada, norm_mix, norm_ffn, w_in, w_out, na_qk_gain, na_rpb, dn_conv, dn_a_log,
           dn_dt_bias, dn_norm, ml_i_bias, ml_f_bias, ml_norm, wa_qk_gain, wa_sink, w_ffn_in, w_ffn_out):
    return _forward(x, c, ctx, c_ctx, w_ada, b_ada, norm_mix, norm_ffn, w_in, w_out, na_qk_gain, na_rpb, dn_conv,
                    dn_a_log, dn_dt_bias, dn_norm, ml_i_bias, ml_f_bias, ml_norm, wa_qk_gain, wa_sink, w_ffn_in,
                    w_ffn_out)
```

```python
import functools

import jax
import jax.numpy as jnp
from jax import lax
from jax.experimental import pallas as pl
from jax.experimental.pallas import tpu as pltpu

F32 = jnp.float32
BF16 = jnp.bfloat16

HEAD_DIM = 128
GRID_W = 64
CHUNK = 128
FFN_CAST_SLABS = 44
PREP_GROUP = 8
ML_GROUP = 4
SCAN_UNROLL = 2
NA_WIN_ROWS = 8
NA_WIN_COLS = 16
NA_QROWS = 4
NA_KROWS = NA_QROWS + NA_WIN_ROWS
NA_HEADS_PER_STEP = 2
WA_BLOCK = 128
ML_QK = 64
ROPE_THETA = 10000.0
EPS = 1e-6
NEG_INF = float("-inf")
VMEM_LIMIT = 48 * 1024 * 1024
DN_VMEM_LIMIT = 56 * 1024 * 1024

NA_Q, NA_K, NA_V = 0, 4, 8
DN_Q, DN_OG = 12, 24
ML_Q, ML_K, ML_V, ML_OG = 28, 30, 32, 36
WA_Q, WA_K, WA_V = 40, 44, 46
MAIN_COLS = 48 * 128
LANE_BETA, LANE_ALPHA, LANE_IG, LANE_FG = 0, 8, 16, 24


def _cparams(sem, vmem_limit=VMEM_LIMIT):
    return pltpu.CompilerParams(dimension_semantics=sem, vmem_limit_bytes=vmem_limit)


def _rms(x, gain):
    return x * lax.rsqrt(jnp.mean(x * x, axis=-1, keepdims=True) + EPS) * gain


def _silu(x):
    return x * jax.nn.sigmoid(x)


def _softplus(x):
    return jnp.maximum(x, 0.0) + jnp.log1p(jnp.exp(-jnp.abs(x)))


def _log_sigmoid(x):
    return jnp.minimum(x, 0.0) - jnp.log1p(jnp.exp(-jnp.abs(x)))


def _dot(a, b):
    return jnp.dot(a, b, preferred_element_type=F32)


def _dot_nt(a, b):
    return lax.dot_general(a, b, (((1,), (1,)), ((), ())), preferred_element_type=F32)


def _dot_tn(a, b):
    return lax.dot_general(a, b, (((0,), (0,)), ((), ())), preferred_element_type=F32)


def _iota(shape, dim):
    return lax.broadcasted_iota(jnp.int32, shape, dim)


def _lane_pick(x, lane):
    return jnp.sum(jnp.where(_iota(x.shape, 1) == lane, x, 0.0), axis=-1, keepdims=True)


def _to_row(col):
    n = col.shape[0]
    eye = _iota((n, n), 0) == _iota((n, n), 1)
    return jnp.sum(jnp.where(eye, col, 0.0), axis=0, keepdims=True)


def _ada_kernel(c_ref, w_ref, b_ref, o_ref):
    a = _silu(c_ref[...]).astype(BF16)
    o_ref[0] = _dot(a, w_ref[0].astype(BF16)) + b_ref[0]


def _ada_call(cs, w_ada, b_ada):
    depth, d, n = w_ada.shape
    tn = 1024
    return pl.pallas_call(
        _ada_kernel,
        grid=(depth, n // tn),
        in_specs=[pl.BlockSpec((8, d), lambda l, j: (0, 0)),
                  pl.BlockSpec((1, d, tn), lambda l, j: (l, 0, j)),
                  pl.BlockSpec((1, 1, tn), lambda l, j: (l, 0, j))],
        out_specs=pl.BlockSpec((1, 8, tn), lambda l, j: (l, 0, j)),
        out_shape=jax.ShapeDtypeStruct((depth, 8, n), F32),
        compiler_params=_cparams(("arbitrary", "arbitrary")),
        name="ada",
    )(cs, w_ada, b_ada.reshape(depth, 1, n))


def _norm_kernel(x_ref, mod_ref, g_ref, h_ref):
    mod = mod_ref[0]
    h = _rms(x_ref[...], g_ref[...])
    h_ref[...] = (h * (1.0 + mod[1:2]) + mod[0:1]).astype(BF16)


def _norm_call(x, mod, gain, rows_per_mod, tm):
    m, d = x.shape
    per = rows_per_mod // tm
    return pl.pallas_call(
        _norm_kernel,
        grid=(m // tm,),
        in_specs=[pl.BlockSpec((tm, d), lambda i: (i, 0)),
                  pl.BlockSpec((1, 8, d), lambda i: (i // per, 0, 0)),
                  pl.BlockSpec((1, d), lambda i: (0, 0))],
        out_specs=pl.BlockSpec((tm, d), lambda i: (i, 0)),
        out_shape=jax.ShapeDtypeStruct((m, d), BF16),
        compiler_params=_cparams(("arbitrary",)),
        name="norm_mod",
    )(x, mod, gain)


def _inproj_kernel(a_ref, wt_ref, wg_ref, *rest, with_cast):
    if with_cast:
        f1_ref, f2_ref, o_ref, g_ref, f1o_ref, f2o_ref = rest
        f1o_ref[...] = f1_ref[0].astype(BF16)
        f2o_ref[...] = f2_ref[0].astype(BF16)
    else:
        o_ref, g_ref = rest
    a = a_ref[...]
    o_ref[...] = _dot_nt(a, wt_ref[0])

    @pl.when(pl.program_id(1) == 0)
    def _():
        g_ref[...] = _dot_nt(a, wg_ref[0])


def _inproj_call(a, wt, wgt, layer, tm, tn, name, cast=None):
    m, k = a.shape
    n = wt.shape[1]
    ng = wgt.shape[1]
    n_i, n_j = m // tm, n // tn
    in_specs = [pl.BlockSpec((tm, k), lambda i, j: (i, 0)),
                pl.BlockSpec((1, tn, k), lambda i, j: (layer, j, 0)),
                pl.BlockSpec((1, ng, k), lambda i, j: (layer, 0, 0))]
    out_specs = [pl.BlockSpec((tm, tn), lambda i, j: (i, j)),
                 pl.BlockSpec((tm, ng), lambda i, j: (i, 0))]
    out_shape = [jax.ShapeDtypeStruct((m, n), F32), jax.ShapeDtypeStruct((m, ng), F32)]
    args = [a, wt, wgt]
    if cast is not None:
        w_ffn_in, w_ffn_out = cast
        _, d, two_hidden = w_ffn_in.shape
        hidden = w_ffn_out.shape[1]
        slabs = FFN_CAST_SLABS
        assert n_i * n_j >= slabs and two_hidden % (slabs * 128) == 0 and hidden % (slabs * 16) == 0
        c1 = two_hidden // slabs
        r2 = hidden // slabs

        def slab(i, j):
            return jnp.minimum(i * n_j + j, slabs - 1)

        in_specs += [pl.BlockSpec((1, d, c1), lambda i, j: (layer, 0, slab(i, j))),
                     pl.BlockSpec((1, r2, d), lambda i, j: (layer, slab(i, j), 0))]
        out_specs += [pl.BlockSpec((d, c1), lambda i, j: (0, slab(i, j))),
                      pl.BlockSpec((r2, d), lambda i, j: (slab(i, j), 0))]
        out_shape += [jax.ShapeDtypeStruct((d, two_hidden), BF16), jax.ShapeDtypeStruct((hidden, d), BF16)]
        args += [w_ffn_in, w_ffn_out]
    return pl.pallas_call(
        functools.partial(_inproj_kernel, with_cast=cast is not None),
        grid=(n_i, n_j),
        in_specs=in_specs,
        out_specs=out_specs,
        out_shape=out_shape,
        compiler_params=_cparams(("arbitrary", "arbitrary")),
        name=name,
    )(*args)


def _outproj_kernel(ya_ref, yb_ref, ym_ref, yw_ref, w_ref, x_ref, mod_ref, g_ref, xo_ref, h_ref):
    gw = ya_ref.shape[1]
    acc = _dot(ya_ref[...], w_ref[0, 0:gw, :])
    acc += _dot(yb_ref[...], w_ref[0, gw:2 * gw, :])
    acc += _dot(ym_ref[...], w_ref[0, 2 * gw:3 * gw, :])
    acc += _dot(yw_ref[...], w_ref[0, 3 * gw:4 * gw, :])
    mod = mod_ref[0]
    xn = x_ref[...] + mod[2:3] * acc
    xo_ref[...] = xn
    h = _rms(xn, g_ref[...])
    h_ref[...] = (h * (1.0 + mod[4:5]) + mod[3:4]).astype(BF16)


def _outproj_call(ys, w, layer, x, mod, gain, rows_per_mod, tm):
    m, d = x.shape
    gw = ys[0].shape[1]
    per = rows_per_mod // tm
    yspec = pl.BlockSpec((tm, gw), lambda i: (i, 0))
    return pl.pallas_call(
        _outproj_kernel,
        grid=(m // tm,),
        in_specs=[yspec, yspec, yspec, yspec,
                  pl.BlockSpec((1, 4 * gw, d), lambda i: (layer, 0, 0)),
                  pl.BlockSpec((tm, d), lambda i: (i, 0)),
                  pl.BlockSpec((1, 8, d), lambda i: (i // per, 0, 0)),
                  pl.BlockSpec((1, d), lambda i: (0, 0))],
        out_specs=[pl.BlockSpec((tm, d), lambda i: (i, 0)),
                   pl.BlockSpec((tm, d), lambda i: (i, 0))],
        out_shape=[jax.ShapeDtypeStruct((m, d), F32), jax.ShapeDtypeStruct((m, d), BF16)],
        compiler_params=_cparams(("arbitrary",)),
        name="outproj",
    )(*ys, w, x, mod, gain)


def _ffn_kernel(h_ref, wg_ref, wu_ref, w2_ref, x_ref, mod_ref, modn_ref, gn_ref, xo_ref, hn_ref, acc_ref,
                *, with_next):
    j = pl.program_id(1)

    @pl.when(j == 0)
    def _():
        acc_ref[...] = jnp.zeros_like(acc_ref)

    h = h_ref[...]
    act = _silu(_dot(h, wg_ref[0])) * _dot(h, wu_ref[0])
    acc_ref[...] += _dot(act.astype(BF16), w2_ref[0])

    @pl.when(j == pl.num_programs(1) - 1)
    def _():
        xn = x_ref[...] + mod_ref[0][5:6] * acc_ref[...]
        xo_ref[...] = xn
        if with_next:
            modn = modn_ref[0]
            hn = _rms(xn, gn_ref[...])
            hn_ref[...] = (hn * (1.0 + modn[1:2]) + modn[0:1]).astype(BF16)
        else:
            hn_ref[...] = jnp.zeros_like(hn_ref)


def _ffn_call(h, w_in, w_out, layer, x, mod, mod_next, gain_next, rows_per_mod, tm, tn, with_next):
    m, d = x.shape
    hidden = w_out.shape[1]
    nj = hidden // tn
    per = rows_per_mod // tm
    return pl.pallas_call(
        functools.partial(_ffn_kernel, with_next=with_next),
        grid=(m // tm, nj),
        in_specs=[pl.BlockSpec((tm, d), lambda i, j: (i, 0)),
                  pl.BlockSpec((1, d, tn), lambda i, j: (layer, 0, j)),
                  pl.BlockSpec((1, d, tn), lambda i, j: (layer, 0, nj + j)),
                  pl.BlockSpec((1, tn, d), lambda i, j: (layer, j, 0)),
                  pl.BlockSpec((tm, d), lambda i, j: (i, 0)),
                  pl.BlockSpec((1, 8, d), lambda i, j: (i // per, 0, 0)),
                  pl.BlockSpec((1, 8, d), lambda i, j: (i // per, 0, 0)),
                  pl.BlockSpec((1, d), lambda i, j: (0, 0))],
        out_specs=[pl.BlockSpec((tm, d), lambda i, j: (i, 0)),
                   pl.BlockSpec((tm, d), lambda i, j: (i, 0))],
        out_shape=[jax.ShapeDtypeStruct((m, d), F32), jax.ShapeDtypeStruct((m, d), BF16)],
        scratch_shapes=[pltpu.VMEM((tm, d), F32)],
        compiler_params=_cparams(("arbitrary", "arbitrary")),
        name="ffn",
    )(h, w_in, w_in, w_out, x, mod, mod_next, gain_next)


def _na_kernel(q_ref, k_ref, v_ref, kc_ref, vc_ref, gain_ref, colbias_ref, o_ref,
               kn_scr, v_scr, kcn_scr, vc_scr, bias_scr, sl0_scr, sc0_scr, sl1_scr, sc1_scr, *, rows):
    step = pl.program_id(2)
    n_rb = pl.num_programs(2) - 1
    hd = HEAD_DIM
    heads = q_ref.shape[1] // hd

    @pl.when(step == 0)
    def _():
        sl1_scr[...] = jnp.zeros_like(sl1_scr)
        sc1_scr[...] = jnp.zeros_like(sc1_scr)
        g1 = gain_ref[1:2, :]
        closed = jnp.full((GRID_W, GRID_W), NEG_INF, F32)
        for hh in range(heads):
            cols = slice(hh * hd, (hh + 1) * hd)
            kn_scr[hh] = _rms(k_ref[:, cols], g1).astype(BF16)
            v_scr[hh] = v_ref[:, cols].astype(BF16)
            kcn_scr[hh] = _rms(kc_ref[:, cols], g1).astype(BF16)
            vc_scr[hh] = vc_ref[:, cols].astype(BF16)
            for typ, (offs, inside) in enumerate(_na_row_plan(rows)):
                for i in range(NA_QROWS):
                    for jp in range(NA_KROWS // 2):
                        halves = [colbias_ref[0, hh, offs[i][j]] if inside[i][j] else closed
                                  for j in (2 * jp, 2 * jp + 1)]
                        bias_scr[hh, typ, i * GRID_W:(i + 1) * GRID_W, 2 * jp * GRID_W:(2 * jp + 2) * GRID_W] = (
                            jnp.concatenate(halves, axis=1))

    scale = hd ** -0.5
    nk = NA_KROWS * GRID_W

    def window_start(blk):
        ws = jnp.clip(NA_QROWS * blk - NA_WIN_ROWS // 2, 0, rows - NA_KROWS)
        return pl.multiple_of(ws * GRID_W, GRID_W)

    def stages(write_loc, write_ctx, read_loc, read_ctx):
        prev = jnp.maximum(step - 1, 0)
        cur = jnp.minimum(step, n_rb - 1)
        typ = jnp.where(cur == 0, 0, jnp.where(cur == n_rb - 1, 2, 1))
        for hh in range(heads):
            cols = slice(hh * hd, (hh + 1) * hd)
            s_loc = read_loc[hh]
            s_ctx = read_ctx[hh]
            m = jnp.maximum(jnp.max(s_loc, axis=-1, keepdims=True), jnp.max(s_ctx, axis=-1, keepdims=True))
            p_loc = jnp.exp(s_loc - m)
            p_ctx = jnp.exp(s_ctx - m)
            den = jnp.sum(p_loc, axis=-1, keepdims=True) + jnp.sum(p_ctx, axis=-1, keepdims=True)
            vw = v_scr[hh, pl.ds(window_start(prev), nk), :]
            o = _dot(p_loc.astype(BF16), vw) + _dot(p_ctx.astype(BF16), vc_scr[hh])
            o_ref[:, cols] = (o / den).astype(BF16)
            qn = _rms(q_ref[:, cols], gain_ref[0:1, :]).astype(BF16)
            kw = kn_scr[hh, pl.ds(window_start(cur), nk), :]
            write_loc[hh] = _dot_nt(qn, kw) * scale + bias_scr[hh, pl.ds(typ, 1)][0]
            write_ctx[hh] = _dot_nt(qn, kcn_scr[hh]) * scale

    @pl.when(step % 2 == 0)
    def _():
        stages(sl0_scr, sc0_scr, sl1_scr, sc1_scr)

    @pl.when(step % 2 == 1)
    def _():
        stages(sl1_scr, sc1_scr, sl0_scr, sc0_scr)


def _na_row_plan(rows):
    n_rb = rows // NA_QROWS
    kh = NA_WIN_ROWS
    plan = []
    for rb in (0, 1, n_rb - 1):
        ws = min(max(NA_QROWS * rb - kh // 2, 0), rows - NA_KROWS)
        offs, inside = [], []
        for i in range(NA_QROWS):
            r = NA_QROWS * rb + i
            rs = min(max(r - kh // 2, 0), rows - kh)
            offs.append([min(max(ws + j - r, 1 - kh), kh - 1) + kh - 1 for j in range(NA_KROWS)])
            inside.append([rs <= ws + j < rs + kh for j in range(NA_KROWS)])
        plan.append((offs, inside))
    return plan


def _na_col_tables(rpb):
    kw = NA_WIN_COLS
    c = jnp.arange(GRID_W)
    cs = jnp.clip(c - kw // 2, 0, GRID_W - kw)
    in_col = (c[None, :] >= cs[:, None]) & (c[None, :] < cs[:, None] + kw)
    dc = jnp.clip(c[None, :] - c[:, None], 1 - kw, kw - 1) + kw - 1
    dc_hot = jax.nn.one_hot(dc, 2 * kw - 1, dtype=F32)
    val = jnp.einsum("lhab,ckb->lhack", rpb.astype(F32), dc_hot, precision=lax.Precision.HIGHEST)
    return jnp.where(in_col, val, NEG_INF)


def _na_call(p_lat, p_ctx, gain, colbias, layer, batch, seq, lc):
    rows = seq // GRID_W
    n_rb = rows // NA_QROWS
    tq = NA_QROWS * GRID_W
    nk = NA_KROWS * GRID_W
    heads = colbias.shape[1]
    n_off = colbias.shape[2]
    hd = HEAD_DIM
    hp = NA_HEADS_PER_STEP
    wide = hp * hd
    assert heads % hp == 0 and NA_Q % hp == 0 and NA_K % hp == 0 and NA_V % hp == 0
    return pl.pallas_call(
        functools.partial(_na_kernel, rows=rows),
        grid=(batch, heads // hp, n_rb + 1),
        in_specs=[pl.BlockSpec((tq, wide), lambda b, h, r: (b * n_rb + jnp.minimum(r, n_rb - 1), NA_Q // hp + h)),
                  pl.BlockSpec((seq, wide), lambda b, h, r: (b, NA_K // hp + h)),
                  pl.BlockSpec((seq, wide), lambda b, h, r: (b, NA_V // hp + h)),
                  pl.BlockSpec((lc, wide), lambda b, h, r: (b, NA_K // hp + h)),
                  pl.BlockSpec((lc, wide), lambda b, h, r: (b, NA_V // hp + h)),
                  pl.BlockSpec((2, hd), lambda b, h, r: (0, 0)),
                  pl.BlockSpec((1, hp, n_off, GRID_W, GRID_W), lambda b, h, r: (layer, h, 0, 0, 0))],
        out_specs=pl.BlockSpec((tq, wide), lambda b, h, r: (b * n_rb + jnp.maximum(r - 1, 0), h)),
        out_shape=jax.ShapeDtypeStruct((batch * seq, heads * hd), BF16),
        scratch_shapes=[pltpu.VMEM((hp, seq, hd), BF16), pltpu.VMEM((hp, seq, hd), BF16),
                        pltpu.VMEM((hp, lc, hd), BF16), pltpu.VMEM((hp, lc, hd), BF16),
                        pltpu.VMEM((hp, 3, tq, nk), F32),
                        pltpu.VMEM((hp, tq, nk), F32), pltpu.VMEM((hp, tq, lc), F32),
                        pltpu.VMEM((hp, tq, nk), F32), pltpu.VMEM((hp, tq, lc), F32)],
        compiler_params=_cparams(("arbitrary", "arbitrary", "arbitrary")),
        name="na_attn",
    )(p_lat, p_lat, p_lat, p_ctx, p_ctx, gain, colbias)


def _ctx_attn_kernel(sink_ref, q_ref, k_ref, v_ref, gain_ref, o_ref, *, use_sink):
    scale = HEAD_DIM ** -0.5
    qn = _rms(q_ref[...], gain_ref[0:1, :]).astype(BF16)
    kn = _rms(k_ref[...], gain_ref[1:2, :]).astype(BF16)
    s = _dot_nt(qn, kn) * scale
    m = jnp.max(s, axis=-1, keepdims=True)
    if use_sink:
        sk = sink_ref[pl.program_id(1)]
        m = jnp.maximum(m, sk)
    p = jnp.exp(s - m)
    den = jnp.sum(p, axis=-1, keepdims=True)
    if use_sink:
        den = den + jnp.exp(sk - m)
    o = _dot(p.astype(BF16), v_ref[...].astype(BF16))
    o_ref[...] = (o / den).astype(BF16)


def _ctx_attn_call(p_ctx, gain, sink, batch, lc, heads, rep, qoff, koff, voff, use_sink, name):
    hd = HEAD_DIM
    return pl.pallas_call(
        functools.partial(_ctx_attn_kernel, use_sink=use_sink),
        grid=(batch, heads),
        in_specs=[pl.BlockSpec(memory_space=pltpu.SMEM),
                  pl.BlockSpec((lc, hd), lambda b, h: (b, qoff + h)),
                  pl.BlockSpec((lc, hd), lambda b, h: (b, koff + h // rep)),
                  pl.BlockSpec((lc, hd), lambda b, h: (b, voff + h // rep)),
                  pl.BlockSpec((2, hd), lambda b, h: (0, 0))],
        out_specs=pl.BlockSpec((lc, hd), lambda b, h: (b, h)),
        out_shape=jax.ShapeDtypeStruct((batch * lc, heads * hd), BF16),
        compiler_params=_cparams(("arbitrary", "arbitrary")),
        name=name,
    )(sink, p_ctx, p_ctx, p_ctx, gain)


def _rope(x, cos, sin_signed):
    lane = _iota(x.shape, 1)
    swapped = jnp.where(lane % 64 < 32, pltpu.roll(x, 96, axis=1), pltpu.roll(x, 32, axis=1))
    return x * cos + swapped * sin_signed


def _wa_kernel(sink_ref, q_ref, k_ref, v_ref, kc_ref, vc_ref, gain_ref, cq_ref, sq_ref, ck_ref, sk_ref, o_ref,
               kpad_scr, vpad_scr, kcn_scr, vc_scr, sl0_scr, sc0_scr, sl1_scr, sc1_scr, *, seq):
    step = pl.program_id(1)
    nb = pl.num_programs(1) - 1
    wb = WA_BLOCK
    hd = HEAD_DIM
    groups = k_ref.shape[1] // hd

    @pl.when(step == 0)
    def _():
        sl1_scr[...] = jnp.zeros_like(sl1_scr)
        sc1_scr[...] = jnp.zeros_like(sc1_scr)
        g1 = gain_ref[1:2, :]
        zeros = jnp.zeros((wb, hd), BF16)
        for g in range(groups):
            cols = slice(g * hd, (g + 1) * hd)
            kpad_scr[g, 0:wb, :] = zeros
            kpad_scr[g, seq + wb:seq + 2 * wb, :] = zeros
            vpad_scr[g, 0:wb, :] = zeros
            vpad_scr[g, seq + wb:seq + 2 * wb, :] = zeros
            kpad_scr[g, wb:seq + wb, :] = _rope(_rms(k_ref[:, cols], g1), ck_ref[...], sk_ref[...]).astype(BF16)
            vpad_scr[g, wb:seq + wb, :] = v_ref[:, cols].astype(BF16)
            kcn_scr[g] = _rms(kc_ref[:, cols], g1).astype(BF16)
            vc_scr[g] = vc_ref[:, cols].astype(BF16)

    scale = hd ** -0.5

    def stages(write_loc, write_ctx, read_loc, read_ctx):
        prev = jnp.maximum(step - 1, 0)
        cur = jnp.minimum(step, nb - 1)
        g0 = gain_ref[0:1, :]
        cq = cq_ref[...]
        sq = sq_ref[...]
        shp = (2 * wb, 3 * wb)
        i = _iota(shp, 0) % wb
        j = _iota(shp, 1)
        kpos = (cur - 1) * wb + j
        ok = (j - i >= 0) & (j - i <= 2 * wb) & (kpos >= 0) & (kpos < seq)
        for g in range(groups):
            s_loc = read_loc[g]
            s_ctx = read_ctx[g]
            sink = jnp.where(_iota((2 * wb, 1), 0) < wb, sink_ref[2 * g], sink_ref[2 * g + 1])
            m = jnp.maximum(jnp.maximum(jnp.max(s_loc, axis=-1, keepdims=True),
                                        jnp.max(s_ctx, axis=-1, keepdims=True)), sink)
            p_loc = jnp.exp(s_loc - m)
            p_ctx = jnp.exp(s_ctx - m)
            den = jnp.sum(p_loc, axis=-1, keepdims=True) + jnp.sum(p_ctx, axis=-1, keepdims=True) + jnp.exp(sink - m)
            vw = vpad_scr[g, pl.ds(pl.multiple_of(prev * wb, wb), 3 * wb), :]
            o = (_dot(p_loc.astype(BF16), vw) + _dot(p_ctx.astype(BF16), vc_scr[g])) / den
            o_ref[:, 2 * g * hd:(2 * g + 1) * hd] = o[0:wb].astype(BF16)
            o_ref[:, (2 * g + 1) * hd:(2 * g + 2) * hd] = o[wb:2 * wb].astype(BF16)
            q2 = jnp.concatenate([_rope(_rms(q_ref[:, 2 * g * hd:(2 * g + 1) * hd], g0), cq, sq),
                                  _rope(_rms(q_ref[:, (2 * g + 1) * hd:(2 * g + 2) * hd], g0), cq, sq)],
                                 axis=0).astype(BF16)
            kw = kpad_scr[g, pl.ds(pl.multiple_of(cur * wb, wb), 3 * wb), :]
            write_loc[g] = jnp.where(ok, _dot_nt(q2, kw) * scale, NEG_INF)
            write_ctx[g] = _dot_nt(q2, kcn_scr[g]) * scale

    @pl.when(step % 2 == 0)
    def _():
        stages(sl0_scr, sc0_scr, sl1_scr, sc1_scr)

    @pl.when(step % 2 == 1)
    def _():
        stages(sl1_scr, sc1_scr, sl0_scr, sc0_scr)


def _wa_call(p_lat, p_ctx, gain, sink, cos_t, sin_t, batch, seq, lc):
    wb = WA_BLOCK
    hd = HEAD_DIM
    nb = seq // wb
    kvh = 2
    qw = 2 * kvh * hd
    kw_ = kvh * hd
    assert (WA_Q * hd) % qw == 0 and (WA_K * hd) % kw_ == 0 and (WA_V * hd) % kw_ == 0
    return pl.pallas_call(
        functools.partial(_wa_kernel, seq=seq),
        grid=(batch, nb + 1),
        in_specs=[pl.BlockSpec(memory_space=pltpu.SMEM),
                  pl.BlockSpec((wb, qw), lambda b, n: (b * nb + jnp.minimum(n, nb - 1), WA_Q * hd // qw)),
                  pl.BlockSpec((seq, kw_), lambda b, n: (b, WA_K * hd // kw_)),
                  pl.BlockSpec((seq, kw_), lambda b, n: (b, WA_V * hd // kw_)),
                  pl.BlockSpec((lc, kw_), lambda b, n: (b, WA_K * hd // kw_)),
                  pl.BlockSpec((lc, kw_), lambda b, n: (b, WA_V * hd // kw_)),
                  pl.BlockSpec((2, hd), lambda b, n: (0, 0)),
                  pl.BlockSpec((wb, hd), lambda b, n: (jnp.minimum(n, nb - 1), 0)),
                  pl.BlockSpec((wb, hd), lambda b, n: (jnp.minimum(n, nb - 1), 0)),
                  pl.BlockSpec((seq, hd), lambda b, n: (0, 0)),
                  pl.BlockSpec((seq, hd), lambda b, n: (0, 0))],
        out_specs=pl.BlockSpec((wb, qw), lambda b, n: (b * nb + jnp.maximum(n - 1, 0), 0)),
        out_shape=jax.ShapeDtypeStruct((batch * seq, qw), BF16),
        scratch_shapes=[pltpu.VMEM((kvh, seq + 2 * wb, hd), BF16), pltpu.VMEM((kvh, seq + 2 * wb, hd), BF16),
                        pltpu.VMEM((kvh, lc, hd), BF16), pltpu.VMEM((kvh, lc, hd), BF16),
                        pltpu.VMEM((kvh, 2 * wb, 3 * wb), F32), pltpu.VMEM((kvh, 2 * wb, lc), F32),
                        pltpu.VMEM((kvh, 2 * wb, 3 * wb), F32), pltpu.VMEM((kvh, 2 * wb, lc), F32)],
        compiler_params=_cparams(("arbitrary", "arbitrary")),
        name="wa_attn",
    )(sink, p_lat, p_lat, p_lat, p_ctx, p_ctx, gain, cos_t, sin_t, cos_t, sin_t)


def _dn_prep_kernel(x_ref, prev_ref, next_ref, w_ref, o_ref, *, n_tiles):
    t = pl.program_id(1)
    tl = x_ref.shape[0]
    x = x_ref[...]
    prev = jnp.where(t > 0, prev_ref[...], 0.0)
    nxt = jnp.where(t < n_tiles - 1, next_ref[...], 0.0)
    xe = jnp.concatenate([prev, x, nxt], axis=0)
    n = tl + 16
    taps = w_ref.shape[0]
    acc = jnp.zeros_like(x)
    for j in range(taps):
        off = 8 - taps // 2 + j
        shifted = xe if off % n == 0 else pltpu.roll(xe, n - off, axis=0)
        acc += shifted[0:tl] * w_ref[j:j + 1, :]
    y = _silu(acc)
    hd = HEAD_DIM
    nh = y.shape[1] // (3 * hd)
    for blk in range(3 * nh):
        yb = y[:, blk * hd:(blk + 1) * hd]
        if blk < 2 * nh:
            yb = yb * lax.rsqrt(jnp.sum(yb * yb, axis=-1, keepdims=True) + EPS)
            if blk < nh:
                yb = yb * hd ** -0.5
        o_ref[:, blk * hd:(blk + 1) * hd] = yb


def _dn_prep_call(p, conv_w, batch, length, tl):
    width = conv_w.shape[1]
    n_tiles = length // tl
    per8 = tl // 8
    last8 = batch * length // 8 - 1
    return pl.pallas_call(
        functools.partial(_dn_prep_kernel, n_tiles=n_tiles),
        grid=(batch, n_tiles),
        in_specs=[pl.BlockSpec((tl, width), lambda b, t: (b * n_tiles + t, 1)),
                  pl.BlockSpec((8, width), lambda b, t: (jnp.maximum((b * n_tiles + t) * per8 - 1, 0), 1)),
                  pl.BlockSpec((8, width), lambda b, t: (jnp.minimum((b * n_tiles + t + 1) * per8, last8), 1)),
                  pl.BlockSpec((conv_w.shape[0], width), lambda b, t: (0, 0))],
        out_specs=pl.BlockSpec((tl, width), lambda b, t: (b * n_tiles + t, 0)),
        out_shape=jax.ShapeDtypeStruct((batch * length, width), F32),
        compiler_params=_cparams(("arbitrary", "arbitrary")),
        name="dn_prep",
    )(p, p, p, conv_w)


def _solve_dot(a, b):
    return _dot(a.astype(BF16), b.astype(BF16))


def _dn_pass(q_ref, k_ref, v_ref, og_ref, gates_ref, conv_w, gadd, gmul, norm_g, y_ref, length, h, state,
             lin_scr, add_scr, egl_scr, acc_scr):
    assert CHUNK == HEAD_DIM
    n_chunks = length // CHUNK

    def conv_chunk(ref, w, c, r0, l2, scale):
        lo = pl.multiple_of(jnp.maximum(r0 - 8, 0), 8)
        hi = pl.multiple_of(jnp.minimum(r0 + CHUNK, length - 8), 8)
        prev = jnp.where(c > 0, ref[pl.ds(lo, 8), :], 0.0)
        nxt = jnp.where(c < n_chunks - 1, ref[pl.ds(hi, 8), :], 0.0)
        xe = jnp.concatenate([prev, ref[pl.ds(r0, CHUNK), :], nxt], axis=0)
        n = CHUNK + 16
        taps = w.shape[0]
        acc = jnp.zeros((CHUNK, HEAD_DIM), F32)
        for j in range(taps):
            off = 8 - taps // 2 + j
            acc += pltpu.roll(xe, n - off, axis=0)[0:CHUNK] * w[j:j + 1, :]
        y = _silu(acc)
        if l2:
            y = y * lax.rsqrt(jnp.sum(y * y, axis=-1, keepdims=True) + EPS)
        return y * scale if scale != 1.0 else y

    acc_scr[0:length, :] = jnp.zeros((length, HEAD_DIM), F32)

    shp = (CHUNK, CHUNK)
    ri = _iota(shp, 0)
    ci = _iota(shp, 1)
    masks = ((ci <= ri, ci < ri), (ci >= ri, ci > ri))
    couple = ([], [])
    s = 1
    while s < CHUNK:
        same = (ri // (2 * s)) == (ci // (2 * s))
        couple[0].append(same & ((ri // s) % 2 == 1) & ((ci // s) % 2 == 0))
        couple[1].append(same & ((ri // s) % 2 == 0) & ((ci // s) % 2 == 1))
        s *= 2
    group = min(n_chunks, PREP_GROUP)

    def prep(gi, carry):
        chains = []
        for g in range(group):
            c = gi * group + g
            r0 = pl.multiple_of(c * CHUNK, CHUNK)
            q = conv_chunk(q_ref, conv_w[0], c, r0, True, HEAD_DIM ** -0.5)
            k = conv_chunk(k_ref, conv_w[1], c, r0, True, 1.0)
            v16 = conv_chunk(v_ref, conv_w[2], c, r0, False, 1.0).astype(BF16)
            raw = gates_ref[pl.ds(r0, CHUNK), :]
            gb = jnp.where(_iota(raw.shape, 1) < LANE_ALPHA, jax.nn.sigmoid(raw), -gmul * _softplus(raw + gadd))
            k16 = k.astype(BF16)
            kk = _dot_nt(k16, k16)
            qk = _dot_nt(q.astype(BF16), k16)
            for d in range(2):
                incl, strict = masks[d]
                beta_col = _lane_pick(gb, LANE_BETA + 4 * d + h)
                g_col = _lane_pick(gb, LANE_ALPHA + 4 * d + h)
                beta_row = _to_row(beta_col)
                g_row = _to_row(g_col)
                cum_col = jnp.sum(jnp.where(incl, g_row, 0.0), axis=-1, keepdims=True)
                cum_row = _to_row(cum_col)
                tot = jnp.sum(g_col, axis=0, keepdims=True)
                decay = jnp.exp(jnp.where(incl, cum_col - cum_row, NEG_INF))
                lm = jnp.where(strict, kk * beta_col * decay, 0.0)
                egl_scr[d, pl.ds(pl.multiple_of(c * 8, 8), 8), :] = jnp.broadcast_to(jnp.exp(tot), (8, HEAD_DIM))
                chains.append((d, c, r0, lm, beta_row, beta_row * jnp.exp(cum_row), (qk * decay).astype(BF16),
                               q * jnp.exp(cum_col), k * jnp.exp(tot - cum_col), k16, v16))
        eye = (ri == ci).astype(F32)
        xs = [eye - jnp.where(couple[ch[0]][0], ch[3], 0.0) for ch in chains]
        for lvl in range(1, len(couple[0])):
            cs = [jnp.where(couple[ch[0]][lvl], ch[3], 0.0) for ch in chains]
            xc = [_solve_dot(x, c) for x, c in zip(xs, cs)]
            xs = [x - _solve_dot(t, x) for x, t in zip(xs, xc)]
        u16 = [_solve_dot(ainv * ch[4], ch[10]).astype(BF16) for ch, ainv in zip(chains, xs)]
        w16 = [_solve_dot(ainv * ch[5], ch[9]).astype(BF16) for ch, ainv in zip(chains, xs)]
        ket = [ch[8].T.astype(BF16) for ch in chains]
        s_lin = [_dot(kt, w) for kt, w in zip(ket, w16)]
        s_add = [_dot(kt, u) for kt, u in zip(ket, u16)]
        o_lin = [ch[7] - _dot(ch[6], w) for ch, w in zip(chains, w16)]
        o_add = [_dot(ch[6], u) for ch, u in zip(chains, u16)]
        for ch, sl, sa, ol, oa in zip(chains, s_lin, s_add, o_lin, o_add):
            d = ch[0]
            base = pl.multiple_of(ch[1] * 2 * CHUNK, 2 * CHUNK)
            lin_scr[d, pl.ds(base, CHUNK), :] = sl.astype(BF16)
            lin_scr[d, pl.ds(base + CHUNK, CHUNK), :] = ol.astype(BF16)
            add_scr[d, pl.ds(base, CHUNK), :] = sa
            add_scr[d, pl.ds(base + CHUNK, CHUNK), :] = oa
        return carry

    lax.fori_loop(0, n_chunks // group, prep, 0)

    def scan(s, st):
        new = []
        for d in range(2):
            c = s if d == 0 else n_chunks - 1 - s
            base = pl.multiple_of(c * 2 * CHUNK, 2 * CHUNK)
            sm = st[d]
            prod = _dot(lin_scr[d, pl.ds(base, 2 * CHUNK), :], sm.astype(BF16))
            add = add_scr[d, pl.ds(base, 2 * CHUNK), :]
            acc_scr[pl.ds(pl.multiple_of(c * CHUNK, CHUNK), CHUNK), :] += prod[CHUNK:] + add[CHUNK:]
            egl = egl_scr[d, pl.ds(pl.multiple_of(c * 8, 8), 8), :][0:1, :]
            new.append(sm * egl - prod[:CHUNK] + add[:CHUNK])
        return tuple(new)

    state = lax.fori_loop(0, n_chunks, scan, state, unroll=SCAN_UNROLL)

    y = _rms(acc_scr[0:length, :], norm_g) * _silu(og_ref[...])
    y_ref[...] = y.astype(BF16)
    return state


def _dn_kernel(q_ref, k_ref, v_ref, og_ref, gates_ref, qc_ref, kc_ref, vc_ref, ogc_ref, gatesc_ref,
               cwq_ref, cwk_ref, cwv_ref, gadd_ref, gmul_ref, norm_ref, y_ref, yc_ref,
               lin_scr, add_scr, egl_scr, acc_scr, *, seq, lc):
    h = pl.program_id(1)
    gadd = gadd_ref[...]
    gmul = gmul_ref[...]
    norm_g = norm_ref[...]
    conv_w = (cwq_ref[...], cwk_ref[...], cwv_ref[...])
    scr = (lin_scr, add_scr, egl_scr, acc_scr)
    zero = jnp.zeros((HEAD_DIM, HEAD_DIM), F32)
    state = _dn_pass(qc_ref, kc_ref, vc_ref, ogc_ref, gatesc_ref, conv_w, gadd, gmul, norm_g, yc_ref, lc, h,
                     (zero, zero), *scr)
    _dn_pass(q_ref, k_ref, v_ref, og_ref, gates_ref, conv_w, gadd, gmul, norm_g, y_ref, seq, h, state, *scr)


def _dn_call(p_lat, p_ctx, pg_lat, pg_ctx, conv_w, gadd, gmul, norm_g, batch, seq, lc):
    hd = HEAD_DIM
    heads = 4
    taps = conv_w.shape[0]

    def blk(n, col):
        return pl.BlockSpec((n, hd), lambda b, h: (b, col + h))

    def gate_blk(n):
        return pl.BlockSpec((n, hd), lambda b, h: (b, 0))

    def conv_blk(col):
        return pl.BlockSpec((taps, hd), lambda b, h: (0, col + h))

    vec = pl.BlockSpec((1, hd), lambda b, h: (0, 0))
    dq, dk, dv = DN_Q, DN_Q + heads, DN_Q + 2 * heads
    return pl.pallas_call(
        functools.partial(_dn_kernel, seq=seq, lc=lc),
        grid=(batch, heads),
        in_specs=[blk(seq, dq), blk(seq, dk), blk(seq, dv), blk(seq, DN_OG), gate_blk(seq),
                  blk(lc, dq), blk(lc, dk), blk(lc, dv), blk(lc, DN_OG), gate_blk(lc),
                  conv_blk(0), conv_blk(heads), conv_blk(2 * heads), vec, vec, vec],
        out_specs=[pl.BlockSpec((seq, hd), lambda b, h: (b, h)),
                   pl.BlockSpec((lc, hd), lambda b, h: (b, h))],
        out_shape=[jax.ShapeDtypeStruct((batch * seq, heads * hd), BF16),
                   jax.ShapeDtypeStruct((batch * lc, heads * hd), BF16)],
        scratch_shapes=[pltpu.VMEM((2, 2 * seq, hd), BF16),
                        pltpu.VMEM((2, 2 * seq, hd), F32),
                        pltpu.VMEM((2, seq // CHUNK * 8, hd), F32),
                        pltpu.VMEM((seq, hd), F32)],
        compiler_params=_cparams(("arbitrary", "arbitrary"), DN_VMEM_LIMIT),
        name="deltanet",
    )(p_lat, p_lat, p_lat, p_lat, pg_lat, p_ctx, p_ctx, p_ctx, p_ctx, pg_ctx, conv_w, conv_w, conv_w,
      gadd, gmul, norm_g)


def _ml_pass(q_ref, k_ref, v_ref, og_ref, gates_ref, gadd, norm_g, y_ref, length, h, state, acc_scr):
    n_chunks = length // CHUNK
    acc_scr[0:length, :] = jnp.zeros((length, HEAD_DIM), F32)
    shp = (CHUNK, CHUNK)
    ri = _iota(shp, 0)
    ci = _iota(shp, 1)
    masks = (ci <= ri, ci >= ri)
    head_lanes = (_iota((1, HEAD_DIM), 1) // ML_QK) == (h % 2)

    group = min(n_chunks, ML_GROUP)

    def each(f, *lists):
        return [f(*a) for a in zip(*lists)]

    def step(gi, st):
        dirs, r0s = [], []
        for j in range(group):
            s = gi * group + j
            for d in range(2):
                c = s if d == 0 else n_chunks - 1 - s
                dirs.append(d)
                r0s.append(pl.multiple_of(c * CHUNK, CHUNK))
        incl = [masks[d] for d in dirs]
        q = [jnp.where(head_lanes, q_ref[pl.ds(r0, CHUNK), :], 0.0) for r0 in r0s]
        k = [jnp.where(head_lanes, k_ref[pl.ds(r0, CHUNK), :], 0.0) * ML_QK ** -0.5 for r0 in r0s]
        v16 = [v_ref[pl.ds(r0, CHUNK), :].astype(BF16) for r0 in r0s]
        gb = [gates_ref[pl.ds(r0, CHUNK), :] + gadd for r0 in r0s]
        ig_col = [_lane_pick(g, LANE_IG + 4 * d + h) for g, d in zip(gb, dirs)]
        lf_col = [_log_sigmoid(_lane_pick(g, LANE_FG + 4 * d + h)) for g, d in zip(gb, dirs)]
        ig_row = each(_to_row, ig_col)
        lf_row = each(_to_row, lf_col)
        b_col = each(lambda m, r: jnp.sum(jnp.where(m, r, 0.0), axis=-1, keepdims=True), incl, lf_row)
        b_row = each(_to_row, b_col)
        b_last = each(lambda x: jnp.sum(x, axis=0, keepdims=True), lf_col)
        g_end = each(lambda bl, b, i: bl - b + i, b_last, b_col, ig_col)
        m_loc = each(lambda x: jnp.max(x, axis=0, keepdims=True), g_end)
        kw = each(lambda kk, g, m: kk * jnp.exp(g - m), k, g_end, m_loc)
        q16 = each(lambda x: x.astype(BF16), q)
        k16 = each(lambda x: x.astype(BF16), k)
        c_loc = each(lambda a, b: _dot_tn(a.astype(BF16), b), kw, v16)
        n_loc = each(lambda x: jnp.sum(x, axis=0, keepdims=True), kw)
        log_d = each(lambda m, bc, br, ir: jnp.where(m, bc - br + ir, NEG_INF), incl, b_col, b_row, ig_row)
        mx_col = each(lambda x: jnp.max(x, axis=-1, keepdims=True), log_d)
        s0 = each(lambda a, b, ld, mx: _dot_nt(a, b) * jnp.exp(ld - mx), q16, k16, log_d, mx_col)
        sv0 = each(lambda s, vv: _dot(s.astype(BF16), vv), s0, v16)
        rs0 = each(lambda s: jnp.sum(s, axis=-1, keepdims=True), s0)
        st = list(st)
        for i, (d, r0) in enumerate(zip(dirs, r0s)):
            c_st, n_st, m_st = st[d]
            m_inter = b_col[i] + m_st
            m_t = jnp.maximum(mx_col[i], m_inter)
            intra = jnp.exp(mx_col[i] - m_t)
            inter = jnp.exp(m_inter - m_t)
            num = intra * sv0[i] + inter * _dot(q16[i], c_st.astype(BF16))
            den = intra * rs0[i] + inter * jnp.sum(q[i] * n_st, axis=-1, keepdims=True)
            acc_scr[pl.ds(r0, CHUNK), :] += num / jnp.maximum(jnp.abs(den), jnp.exp(-m_t))
            m_new = jnp.maximum(b_last[i] + m_st, m_loc[i])
            a = jnp.exp(b_last[i] + m_st - m_new)
            e = jnp.exp(m_loc[i] - m_new)
            st[d] = (a * c_st + e * c_loc[i], a * n_st + e * n_loc[i], m_new)
        return tuple(st)

    state = lax.fori_loop(0, n_chunks // group, step, state)
    y = jax.nn.sigmoid(og_ref[...]) * _rms(acc_scr[0:length, :], norm_g)
    y_ref[...] = y.astype(BF16)
    return state


def _ml_kernel(q_ref, k_ref, v_ref, og_ref, gates_ref, qc_ref, kc_ref, vc_ref, ogc_ref, gatesc_ref,
               gadd_ref, norm_ref, y_ref, yc_ref, acc_scr, *, seq, lc):
    h = pl.program_id(1)
    gadd = gadd_ref[...]
    norm_g = norm_ref[0]
    zero = (jnp.zeros((HEAD_DIM, HEAD_DIM), F32), jnp.zeros((1, HEAD_DIM), F32), jnp.zeros((1, 1), F32))
    state = _ml_pass(qc_ref, kc_ref, vc_ref, ogc_ref, gatesc_ref, gadd, norm_g, yc_ref, lc, h, (zero, zero), acc_scr)
    _ml_pass(q_ref, k_ref, v_ref, og_ref, gates_ref, gadd, norm_g, y_ref, seq, h, state, acc_scr)


def _ml_call(p_lat, p_ctx, pg_lat, pg_ctx, gadd, norm_g, batch, seq, lc):
    hd = HEAD_DIM
    heads = 4

    def specs(n):
        return [pl.BlockSpec((n, hd), lambda b, h: (b, ML_Q + h // 2)),
                pl.BlockSpec((n, hd), lambda b, h: (b, ML_K + h // 2)),
                pl.BlockSpec((n, hd), lambda b, h: (b, ML_V + h)),
                pl.BlockSpec((n, hd), lambda b, h: (b, ML_OG + h)),
                pl.BlockSpec((n, hd), lambda b, h: (b, 0))]

    return pl.pallas_call(
        functools.partial(_ml_kernel, seq=seq, lc=lc),
        grid=(batch, heads),
        in_specs=specs(seq) + specs(lc) + [pl.BlockSpec((1, hd), lambda b, h: (0, 0)),
                                            pl.BlockSpec((1, 1, hd), lambda b, h: (h, 0, 0))],
        out_specs=[pl.BlockSpec((seq, hd), lambda b, h: (b, h)),
                   pl.BlockSpec((lc, hd), lambda b, h: (b, h))],
        out_shape=[jax.ShapeDtypeStruct((batch * seq, heads * hd), BF16),
                   jax.ShapeDtypeStruct((batch * lc, heads * hd), BF16)],
        scratch_shapes=[pltpu.VMEM((seq, hd), F32)],
        compiler_params=_cparams(("arbitrary", "arbitrary")),
        name="mlstm",
    )(p_lat, p_lat, p_lat, p_lat, pg_lat, p_ctx, p_ctx, p_ctx, p_ctx, pg_ctx, gadd, norm_g.reshape(heads, 1, hd))


def _regroup_kernel(wt_ref, main_ref, gate_ref, *, cuts):
    (a0, a1), (b0, b1), (c0, c1), (g0, g1), (h0, h1) = cuts
    main_ref[0, 0:a1 - a0, :] = wt_ref[0, a0:a1, :].astype(BF16)
    main_ref[0, a1 - a0:a1 - a0 + b1 - b0, :] = wt_ref[0, b0:b1, :].astype(BF16)
    main_ref[0, a1 - a0 + b1 - b0:, :] = wt_ref[0, c0:c1, :].astype(BF16)
    ng = (g1 - g0) + (h1 - h0)
    gate_ref[0, 0:g1 - g0, :] = wt_ref[0, g0:g1, :].astype(BF16)
    gate_ref[0, g1 - g0:ng, :] = wt_ref[0, h0:h1, :].astype(BF16)
    gate_ref[0, ng:, :] = jnp.zeros((gate_ref.shape[1] - ng, gate_ref.shape[2]), BF16)


def _regroup_call(w_in):
    depth, d, n = w_in.shape
    gw = 512
    dn_gate0 = 3 * gw + 4 * gw
    ml0 = dn_gate0 + 16
    ml_gate0 = ml0 + 3 * gw
    wa0 = ml_gate0 + 16
    cuts = ((0, dn_gate0), (ml0, ml_gate0), (wa0, n), (dn_gate0, ml0), (ml_gate0, wa0))
    tc = 256
    return pl.pallas_call(
        functools.partial(_regroup_kernel, cuts=cuts),
        grid=(depth, d // tc),
        in_specs=[pl.BlockSpec((1, n, tc), lambda l, i: (l, 0, i))],
        out_specs=[pl.BlockSpec((1, MAIN_COLS, tc), lambda l, i: (l, 0, i)),
                   pl.BlockSpec((1, 128, tc), lambda l, i: (l, 0, i))],
        out_shape=[jax.ShapeDtypeStruct((depth, MAIN_COLS, d), BF16), jax.ShapeDtypeStruct((depth, 128, d), BF16)],
        compiler_params=_cparams(("arbitrary", "arbitrary")),
        name="regroup_w_in",
    )(jnp.swapaxes(w_in, 1, 2))


def _rope_tables(seq):
    t = jnp.arange(seq)
    n_freq = HEAD_DIM // 4
    inv_freq = ROPE_THETA ** (-jnp.arange(n_freq, dtype=F32) / n_freq)
    pos = jnp.stack([t // GRID_W, t % GRID_W], axis=-1).astype(F32)
    ang = pos[:, :, None] * inv_freq
    cos, sin = jnp.cos(ang), jnp.sin(ang)
    cos_t = jnp.concatenate([cos, cos], axis=-1).reshape(seq, HEAD_DIM)
    sin_t = jnp.concatenate([-sin, sin], axis=-1).reshape(seq, HEAD_DIM)
    return cos_t, sin_t


def _gate_vectors(dt_bias, a_log, i_bias, f_bias):
    zeros8 = jnp.zeros((8,), F32)
    gadd = jnp.concatenate([zeros8, dt_bias.reshape(-1), i_bias.reshape(-1), f_bias.reshape(-1),
                            jnp.zeros((96,), F32)]).astype(F32)
    gmul = jnp.concatenate([zeros8, jnp.exp(a_log.astype(F32)).reshape(-1), jnp.zeros((112,), F32)])
    return gadd.reshape(1, 128), gmul.reshape(1, 128)


def _forward(x, c, ctx, c_ctx, w_ada, b_ada, norm_mix, norm_ffn, w_in, w_out, na_qk_gain, na_rpb, dn_conv,
             dn_a_log, dn_dt_bias, dn_norm, ml_i_bias, ml_f_bias, ml_norm, wa_qk_gain, wa_sink, w_ffn_in,
             w_ffn_out):
    batch, seq, d = x.shape
    lc = ctx.shape[1]
    depth = w_ada.shape[0]
    tm = 512
    tn_ffn = 512

    cos_t, sin_t = _rope_tables(seq)
    na_bias = _na_col_tables(na_rpb)
    cs = jnp.concatenate([c, c_ctx[None, :], jnp.zeros((8 - batch - 1, d), F32)], axis=0)
    mods = _ada_call(cs, w_ada, b_ada).reshape(depth, 8, 6, d)
    mods = jnp.pad(mods, ((0, 0), (0, 0), (0, 2), (0, 0)))

    w_main, w_gate = _regroup_call(w_in)
    w_o = w_out.astype(BF16)

    xl = x.reshape(batch * seq, d)
    xc = ctx.reshape(batch * lc, d)
    hl = _norm_call(xl, mods[0, :batch], norm_mix[0:1], seq, tm)
    hc = _norm_call(xc, mods[0, batch:batch + 1], norm_mix[0:1], batch * lc, tm)

    for l in range(depth):
        need_ctx = l < depth - 1
        mod_l, mod_c = mods[l, :batch], mods[l, batch:batch + 1]
        p_lat, pg_lat, w_f1, w_f2 = _inproj_call(hl, w_main, w_gate, l, 2 * tm, 1024, "inproj",
                                                 cast=(w_ffn_in, w_ffn_out))
        w_f1 = w_f1[None]
        w_f2 = w_f2[None]
        p_ctx, pg_ctx = _inproj_call(hc, w_main, w_gate, l, tm, 2048, "inproj_ctx")

        ya = _na_call(p_lat, p_ctx, na_qk_gain[l], na_bias, l, batch, seq, lc)
        gadd, gmul = _gate_vectors(dn_dt_bias[l], dn_a_log[l], ml_i_bias[l], ml_f_bias[l])
        yb, ycb = _dn_call(p_lat, p_ctx, pg_lat, pg_ctx, dn_conv[l], gadd, gmul, dn_norm[l].reshape(1, -1),
                           batch, seq, lc)
        ym, ycm = _ml_call(p_lat, p_ctx, pg_lat, pg_ctx, gadd, ml_norm[l], batch, seq, lc)
        yw = _wa_call(p_lat, p_ctx, wa_qk_gain[l], wa_sink[l], cos_t, sin_t, batch, seq, lc)

        nl = min(l + 1, depth - 1)
        xl, h2 = _outproj_call((ya, yb, ym, yw), w_o, l, xl, mod_l, norm_ffn[l:l + 1], seq, tm)
        xl, hl = _ffn_call(h2, w_f1, w_f2, 0, xl, mod_l, mods[nl, :batch], norm_mix[nl:nl + 1], seq, tm, tn_ffn,
                           need_ctx)
        if need_ctx:
            yca = _ctx_attn_call(p_ctx, na_qk_gain[l], wa_sink[l], batch, lc, 4, 1, NA_Q, NA_K, NA_V, False,
                                 "na_ctx_attn")
            ycw = _ctx_attn_call(p_ctx, wa_qk_gain[l], wa_sink[l], batch, lc, 4, 2, WA_Q, WA_K, WA_V, True,
                                 "wa_ctx_attn")
            xc, h2c = _outproj_call((yca, ycb, ycm, ycw), w_o, l, xc, mod_c, norm_ffn[l:l + 1], batch * lc, tm)
            xc, hc = _ffn_call(h2c, w_f1, w_f2, 0, xc, mod_c, mods[nl, batch:batch + 1], norm_mix[nl:nl + 1],
                               batch * lc, tm, tn_ffn, True)
    return xl.reshape(batch, seq, d)


def kernel(x, c, ctx, c_ctx, w_ada, b_ada, norm_mix, norm_ffn, w_in, w_out, na_qk_gain, na_rpb, dn_conv, dn_a_log,
           dn_dt_bias, dn_norm, ml_i_bias, ml_f_bias, ml_norm, wa_qk_gain, wa_sink, w_ffn_in, w_ffn_out):
    return _forward(x, c, ctx, c_ctx, w_ada, b_ada, norm_mix, norm_ffn, w_in, w_out, na_qk_gain, na_rpb, dn_conv,
                    dn_a_log, dn_dt_bias, dn_norm, ml_i_bias, ml_f_bias, ml_norm, wa_qk_gain, wa_sink, w_ffn_in,
                    w_ffn_out)
```
